```python
import math
import jax, jax.numpy as jnp
from jax import lax
import numpy as np

D_MODEL = 1024
BATCH = 8
SEQ = 2048
DEPTH = 2

CHUNK = 64

SSD_EXPAND = 2
SSD_D_INNER = SSD_EXPAND * D_MODEL
SSD_HEAD_DIM = 64
SSD_HEADS = SSD_D_INNER // SSD_HEAD_DIM
SSD_GROUPS = 4
SSD_HEADS_PER_GROUP = SSD_HEADS // SSD_GROUPS
SSD_STATE = 128
SSD_CONV = 4
SSD_CONV_DIM = SSD_D_INNER + 2 * SSD_GROUPS * SSD_STATE

DIFF_HEAD_DIM = 64
DIFF_HEADS = D_MODEL // (2 * DIFF_HEAD_DIM)
DIFF_WIDTH = DIFF_HEADS * 2 * DIFF_HEAD_DIM
Q_BLOCK = 128

REL_BUCKETS = 32
REL_MAX_DIST = 128

N_EXPERTS = 32
TOP_K = 4
D_FF = D_MODEL
SWIGLU_ALPHA = 1.702
SWIGLU_LIMIT = 7.0
MOE_BLOCK = 128

DN_ALPHA = (2 * DEPTH) ** 0.25
DN_BETA = (8 * DEPTH) ** -0.25
LN_EPS = 1e-5
RMS_EPS = 1e-5

N_BRANCH = 2
OFF_Z = 0
OFF_XBC = OFF_Z + SSD_D_INNER
OFF_DT = OFF_XBC + SSD_CONV_DIM
OFF_Q = OFF_DT + SSD_HEADS
OFF_K = OFF_Q + DIFF_WIDTH
OFF_V = OFF_K + DIFF_WIDTH
OFF_G = OFF_V + DIFF_WIDTH
IN_COLS = OFF_G + N_BRANCH * D_MODEL

kernel_name = "hybrid_ssd_diffattn_moe_deepnorm"


def layer_norm(x, g, b):
    xf = x.astype(jnp.float32)
    mu = jnp.mean(xf, -1, keepdims=True)
    var = jnp.mean(jnp.square(xf - mu), -1, keepdims=True)
    return ((xf - mu) * lax.rsqrt(var + LN_EPS) * g.astype(jnp.float32) + b.astype(jnp.float32)).astype(x.dtype)


def rms_norm(x, g):
    xf = x.astype(jnp.float32)
    ms = jnp.mean(jnp.square(xf), -1, keepdims=True)
    return (xf * lax.rsqrt(ms + RMS_EPS) * g.astype(jnp.float32)).astype(x.dtype)


def causal_depthwise_conv(u, w, b):
    out = lax.conv_general_dilated(
        u, w[:, None, :].astype(u.dtype), window_strides=(1,),
        padding=[(SSD_CONV - 1, 0)], dimension_numbers=("NWC", "WIO", "NWC"),
        feature_group_count=u.shape[-1])
    return out + b.astype(u.dtype)


def segsum_exp(a):
    cs = jnp.cumsum(a, -1)
    diff = cs[..., :, None] - cs[..., None, :]
    l = a.shape[-1]
    mask = jnp.tril(jnp.ones((l, l), dtype=bool))
    return jnp.exp(jnp.where(mask, diff, -jnp.inf))


def ssd_scan(xs, dt, A, Bm, Cm):
    b_, s_ = xs.shape[:2]
    c = s_ // CHUNK
    G, R, P, N = SSD_GROUPS, SSD_HEADS_PER_GROUP, SSD_HEAD_DIM, SSD_STATE
    f32 = jnp.float32
    xr = xs.astype(f32).reshape(b_, c, CHUNK, G, R, P)
    dtr = dt.reshape(b_, c, CHUNK, G, R)
    Br = Bm.astype(f32).reshape(b_, c, CHUNK, G, N)
    Cr = Cm.astype(f32).reshape(b_, c, CHUNK, G, N)
    a = (dtr * A.reshape(G, R)).transpose(0, 3, 4, 1, 2)
    a_cs = jnp.cumsum(a, -1)
    x_dt = xr * dtr[..., None]
    decay_in = segsum_exp(a)
    cb = jnp.einsum("bclgn,bcsgn->bgcls", Cr, Br)
    y_diag = jnp.einsum("bgcls,bgrcls,bcsgrp->bclgrp", cb, decay_in, x_dt)
    decay_to_end = jnp.exp(a_cs[..., -1:] - a_cs)
    chunk_states = jnp.einsum("bcsgn,bgrcs,bcsgrp->bcgrpn", Br, decay_to_end, x_dt)
    chunk_decay = jnp.exp(a_cs[..., -1])

    def step(h, inp):
        st, dec = inp
        return h * dec[..., None, None] + st, h

    h0 = jnp.zeros((b_, G, R, P, N), f32)
    _, prev = lax.scan(step, h0, (jnp.moveaxis(chunk_states, 1, 0), jnp.moveaxis(chunk_decay, -1, 0)))
    prev = jnp.moveaxis(prev, 0, 1)
    y_off = jnp.einsum("bclgn,bcgrpn,bgrcl->bclgrp", Cr, prev, jnp.exp(a_cs))
    return (y_diag + y_off).reshape(b_, s_, SSD_HEADS, P)


def ssd_branch(proj, conv_w, conv_b, dt_bias, a_log, d_skip, norm_g, w_out):
    b_, s_ = proj.shape[:2]
    z = proj[..., OFF_Z:OFF_XBC]
    xbc = jax.nn.silu(causal_depthwise_conv(proj[..., OFF_XBC:OFF_DT], conv_w, conv_b))
    dt_raw = proj[..., OFF_DT:OFF_Q]
    xs = xbc[..., :SSD_D_INNER].reshape(b_, s_, SSD_HEADS, SSD_HEAD_DIM)
    gn = SSD_GROUPS * SSD_STATE
    Bm = xbc[..., SSD_D_INNER:SSD_D_INNER + gn].reshape(b_, s_, SSD_GROUPS, SSD_STATE)
    Cm = xbc[..., SSD_D_INNER + gn:].reshape(b_, s_, SSD_GROUPS, SSD_STATE)
    dt = jax.nn.softplus(dt_raw.astype(jnp.float32) + dt_bias.astype(jnp.float32))
    A = -jnp.exp(a_log.astype(jnp.float32))
    y = ssd_scan(xs, dt, A, Bm, Cm) + d_skip.astype(jnp.float32)[:, None] * xs.astype(jnp.float32)
    y = y.reshape(b_, s_, SSD_D_INNER).astype(proj.dtype)
    y = rms_norm(y * jax.nn.silu(z), norm_g)
    return y @ w_out


def rel_bucket(rel):
    half = REL_BUCKETS // 2
    max_exact = half // 2
    ret = jnp.where(rel > 0, half, 0)
    n = jnp.abs(rel)
    nf = jnp.maximum(n, 1).astype(jnp.float32)
    large = max_exact + (jnp.log(nf / max_exact) / math.log(REL_MAX_DIST / max_exact)
                         * (half - max_exact)).astype(jnp.int32)
    large = jnp.minimum(large, half - 1)
    return ret + jnp.where(n < max_exact, n, large)


def diff_branch(proj, rel_bias, lam_vecs, norm_g, w_out, layer_idx):
    b_, s_ = proj.shape[:2]
    H, dh = DIFF_HEADS, DIFF_HEAD_DIM
    q = proj[..., OFF_Q:OFF_K].reshape(b_, s_, H, 2, dh)
    k = proj[..., OFF_K:OFF_V].reshape(b_, s_, H, 2, dh)
    v = proj[..., OFF_V:OFF_G].reshape(b_, s_, H, 2 * dh)
    lam_init = 0.8 - 0.6 * math.exp(-0.3 * layer_idx)
    lv = lam_vecs.astype(jnp.float32)
    lam = jnp.exp(jnp.dot(lv[0], lv[1])) - jnp.exp(jnp.dot(lv[2], lv[3])) + lam_init
    qh = q.transpose(0, 2, 3, 1, 4) * (dh ** -0.5)
    kh = k.transpose(0, 2, 3, 1, 4)
    vh = v.transpose(0, 2, 1, 3)
    table = rel_bias.astype(jnp.float32)
    outs = []
    for i in range(s_ // Q_BLOCK):
        q0, kend = i * Q_BLOCK, (i + 1) * Q_BLOCK
        logits = jnp.einsum("bhmqd,bhmkd->bhmqk", qh[:, :, :, q0:kend], kh[:, :, :, :kend]).astype(jnp.float32)
        qpos = jnp.arange(q0, kend)
        kpos = jnp.arange(kend)
        bias = table[rel_bucket(kpos[None, :] - qpos[:, None])].transpose(2, 0, 1)
        allowed = (kpos[None, :] // CHUNK) <= (qpos[:, None] // CHUNK)
        logits = jnp.where(allowed, logits + bias[None, :, None], -jnp.inf)
        p = jax.nn.softmax(logits, axis=-1)
        attn = p[:, :, 0] - lam * p[:, :, 1]
        outs.append(jnp.einsum("bhqk,bhkd->bhqd", attn.astype(vh.dtype), vh[:, :, :kend]))
    o = jnp.concatenate(outs, axis=2)
    o = rms_norm(o, norm_g) * (1.0 - lam_init)
    o = o.transpose(0, 2, 1, 3).reshape(b_, s_, DIFF_WIDTH)
    return o @ w_out


def moe(h, w_router, b_router, w_gate, b_gate, w_up, b_up, w_down, b_down):
    b_, s_, d = h.shape
    xf = h.reshape(-1, d)
    n_tok = xf.shape[0]
    logits = xf.astype(jnp.float32) @ w_router.astype(jnp.float32) + b_router.astype(jnp.float32)
    top_v, top_i = lax.top_k(logits, TOP_K)
    top_w = jax.nn.softmax(top_v, axis=-1)
    n_assign = n_tok * TOP_K
    e_flat = top_i.reshape(-1)
    w_flat = top_w.reshape(-1)
    tok_flat = jnp.repeat(jnp.arange(n_tok, dtype=jnp.int32), TOP_K)
    order = jnp.argsort(e_flat)
    e_sorted = e_flat[order]
    counts = jnp.bincount(e_flat, length=N_EXPERTS)
    padded = (counts + MOE_BLOCK - 1) // MOE_BLOCK * MOE_BLOCK
    start = jnp.cumsum(counts) - counts
    pad_start = jnp.cumsum(padded) - padded
    pad_end = pad_start + padded
    dest = pad_start[e_sorted] + (jnp.arange(n_assign) - start[e_sorted])
    n_blocks = (n_assign + N_EXPERTS * (MOE_BLOCK - 1) + MOE_BLOCK - 1) // MOE_BLOCK
    cap = n_blocks * MOE_BLOCK
    slot_tok = jnp.full((cap,), n_tok, jnp.int32).at[dest].set(tok_flat[order])
    slot_w = jnp.zeros((cap,), jnp.float32).at[dest].set(w_flat[order])
    blk_expert = jnp.minimum(
        jnp.searchsorted(pad_end, jnp.arange(n_blocks) * MOE_BLOCK, side="right"), N_EXPERTS - 1)
    x_pad = jnp.concatenate([xf, jnp.zeros((1, d), xf.dtype)], axis=0)
    xs = x_pad[slot_tok].reshape(n_blocks, MOE_BLOCK, d)

    def expert_block(args):
        xb, e = args
        g = xb @ w_gate[e] + b_gate[e]
        u = xb @ w_up[e] + b_up[e]
        g = jnp.minimum(g, SWIGLU_LIMIT)
        u = jnp.clip(u, -SWIGLU_LIMIT, SWIGLU_LIMIT)
        a = g * jax.nn.sigmoid(SWIGLU_ALPHA * g) * (u + 1.0)
        return a @ w_down[e] + b_down[e]

    ys = lax.map(expert_block, (xs, blk_expert)).reshape(cap, d)
    out = jnp.zeros((n_tok + 1, d), ys.dtype).at[slot_tok].add(ys * slot_w[:, None].astype(ys.dtype))
    return out[:n_tok].reshape(b_, s_, d)


def setup_inputs(seed: int = 0) -> dict:
    key = jax.random.key(seed)
    ks = iter(jax.random.split(key, 40))
    f32 = jnp.float32
    L = DEPTH

    def nrm(shape, scale):
        return jax.random.normal(next(ks), shape, f32) * scale

    x = nrm((BATCH, SEQ, D_MODEL), 1.0)
    rel_bias = nrm((REL_BUCKETS, DIFF_HEADS), 0.5)
    w_in = nrm((L, D_MODEL, IN_COLS), D_MODEL ** -0.5)
    conv_w = nrm((L, SSD_CONV, SSD_CONV_DIM), SSD_CONV ** -0.5)
    conv_b = nrm((L, SSD_CONV_DIM), 0.02)
    dt0 = jnp.exp(jax.random.uniform(next(ks), (L, SSD_HEADS), f32, math.log(1e-3), math.log(1e-1)))
    dt_bias = dt0 + jnp.log(-jnp.expm1(-dt0))
    a_log = jnp.log(jax.random.uniform(next(ks), (L, SSD_HEADS), f32, 1.0, 16.0))
    d_skip = 1.0 + nrm((L, SSD_HEADS), 0.1)
    ssd_norm_g = 1.0 + nrm((L, SSD_D_INNER), 0.1)
    w_ssd_out = nrm((L, SSD_D_INNER, D_MODEL), SSD_D_INNER ** -0.5)
    diff_lambda = nrm((L, 4, DIFF_HEAD_DIM), 0.1)
    diff_norm_g = 1.0 + nrm((L, 2 * DIFF_HEAD_DIM), 0.1)
    w_attn_out = nrm((L, DIFF_WIDTH, D_MODEL), DIFF_WIDTH ** -0.5)
    gate_b = nrm((L, N_BRANCH * D_MODEL), 0.1)
    w_o = nrm((L, D_MODEL, D_MODEL), DN_BETA * D_MODEL ** -0.5)
    ln1_g = 1.0 + nrm((L, D_MODEL), 0.05)
    ln1_b = nrm((L, D_MODEL), 0.02)
    w_router = nrm((L, D_MODEL, N_EXPERTS), D_MODEL ** -0.5)
    b_router = nrm((L, N_EXPERTS), 0.01)
    w_gate = nrm((L, N_EXPERTS, D_MODEL, D_FF), D_MODEL ** -0.5)
    b_gate = nrm((L, N_EXPERTS, D_FF), 0.02)
    w_up = nrm((L, N_EXPERTS, D_MODEL, D_FF), D_MODEL ** -0.5)
    b_up = nrm((L, N_EXPERTS, D_FF), 0.02)
    w_down = nrm((L, N_EXPERTS, D_FF, D_MODEL), DN_BETA * D_FF ** -0.5)
    b_down = nrm((L, N_EXPERTS, D_MODEL), 0.02)
    ln2_g = 1.0 + nrm((L, D_MODEL), 0.05)
    ln2_b = nrm((L, D_MODEL), 0.02)
    return {"x": x, "rel_bias": rel_bias, "w_in": w_in, "conv_w": conv_w, "conv_b": conv_b,
            "dt_bias": dt_bias, "a_log": a_log, "d_skip": d_skip, "ssd_norm_g": ssd_norm_g,
            "w_ssd_out": w_ssd_out, "diff_lambda": diff_lambda, "diff_norm_g": diff_norm_g,
            "w_attn_out": w_attn_out, "gate_b": gate_b, "w_o": w_o, "ln1_g": ln1_g, "ln1_b": ln1_b,
            "w_router": w_router, "b_router": b_router, "w_gate": w_gate, "b_gate": b_gate,
            "w_up": w_up, "b_up": b_up, "w_down": w_down, "b_down": b_down,
            "ln2_g": ln2_g, "ln2_b": ln2_b}


def reference(x, rel_bias, w_in, conv_w, conv_b, dt_bias, a_log, d_skip, ssd_norm_g,
              w_ssd_out, diff_lambda, diff_norm_g, w_attn_out, gate_b, w_o, ln1_g, ln1_b,
              w_router, b_router, w_gate, b_gate, w_up, b_up, w_down, b_down, ln2_g, ln2_b):
    h = x
    for l in range(DEPTH):
        b_, s_, d = h.shape
        proj = h @ w_in[l]
        y_ssd = ssd_branch(proj, conv_w[l], conv_b[l], dt_bias[l], a_log[l], d_skip[l],
                           ssd_norm_g[l], w_ssd_out[l])
        y_diff = diff_branch(proj, rel_bias, diff_lambda[l], diff_norm_g[l], w_attn_out[l], l)
        gates = jax.nn.sigmoid(proj[..., OFF_G:] + gate_b[l]).reshape(b_, s_, N_BRANCH, d)
        mix = (gates[:, :, 0] * y_ssd + gates[:, :, 1] * y_diff) @ w_o[l]
        h = layer_norm(DN_ALPHA * h + mix, ln1_g[l], ln1_b[l])
        ff = moe(h, w_router[l], b_router[l], w_gate[l], b_gate[l], w_up[l], b_up[l],
                 w_down[l], b_down[l])
        h = layer_norm(DN_ALPHA * h + ff, ln2_g[l], ln2_b[l])
    return h
```

```python
import functools
import math

import jax
import jax.numpy as jnp
from jax import lax
from jax.experimental import pallas as pl
from jax.experimental.pallas import tpu as pltpu

F32 = jnp.float32
BF16 = jnp.bfloat16
HIGHEST = lax.Precision.HIGHEST

D_MODEL = 1024
DEPTH = 2
CHUNK = 64

SSD_D_INNER = 2048
SSD_HEAD_DIM = 64
SSD_HEADS = 32
SSD_GROUPS = 4
SSD_HEADS_PER_GROUP = 8
SSD_STATE = 128
SSD_CONV = 4
SSD_CONV_DIM = 3072
SSD_GROUP_WIDTH = SSD_HEADS_PER_GROUP * SSD_HEAD_DIM

DIFF_HEAD_DIM = 64
DIFF_HEADS = 8
DIFF_WIDTH = 1024

REL_BUCKETS = 32
REL_MAX_DIST = 128

N_EXPERTS = 32
TOP_K = 4
D_FF = 1024
SWIGLU_ALPHA = 1.702
SWIGLU_LIMIT = 7.0

DN_ALPHA = (2 * DEPTH) ** 0.25
LN_EPS = 1e-5
RMS_EPS = 1e-5

OFF_XBC = 2048
OFF_DT = 5120
OFF_Q = 5152
OFF_V = 7200
OFF_G = 8224
IN_COLS = 10272

VMEM_LIMIT_BYTES = 56 * 1024 * 1024

ATTN_TILE = 256
SSD_TILE = 256
EXPERT_BLOCK = 256
TOKEN_TILE = 128
NEG_BIG = -1e30


def _params(semantics):
    return pltpu.CompilerParams(dimension_semantics=semantics,
                                vmem_limit_bytes=VMEM_LIMIT_BYTES)


def _sigmoid(x):
    return 1.0 / (1.0 + jnp.exp(-x))


def _split2(v):
    hi = v.astype(BF16)
    lo = (v - hi.astype(F32)).astype(BF16)
    return hi, lo


def _split3(v):
    hi = v.astype(BF16)
    r = v - hi.astype(F32)
    mid = r.astype(BF16)
    lo = (r - mid.astype(F32)).astype(BF16)
    return hi, mid, lo


def _mm_kernel(x_ref, w_ref, o_ref, xb_ref):
    @pl.when(pl.program_id(1) == 0)
    def _():
        xb_ref[...] = x_ref[...].astype(BF16)

    o_ref[...] = jnp.dot(xb_ref[...], w_ref[...],
                         preferred_element_type=F32).astype(o_ref.dtype)


def _matmul(x, w, out_dtype, bm, bn):
    m, k = x.shape
    n = w.shape[1]
    return pl.pallas_call(
        _mm_kernel,
        grid=(m // bm, n // bn),
        in_specs=[pl.BlockSpec((bm, k), lambda i, j: (i, 0)),
                  pl.BlockSpec((k, bn), lambda i, j: (0, j))],
        out_specs=pl.BlockSpec((bm, bn), lambda i, j: (i, j)),
        out_shape=jax.ShapeDtypeStruct((m, n), out_dtype),
        scratch_shapes=[pltpu.VMEM((bm, k), BF16)],
        compiler_params=_params(("parallel", "arbitrary")),
        name="in_proj",
    )(x, w)


def _softplus(x):
    return jnp.maximum(x, 0.0) + jnp.log(1.0 + jnp.exp(-jnp.abs(x)))


def _dt_kernel(x_ref, w_ref, wt_ref, b_ref, bt_ref, dt_ref, dtt_ref):
    x = x_ref[...]
    tm = x.shape[0]
    raw = jnp.dot(x, w_ref[...], precision=HIGHEST, preferred_element_type=F32)
    dt = _softplus(raw + b_ref[...])
    raw_t = lax.dot_general(wt_ref[...], x, (((1,), (1,)), ((), ())),
                            precision=HIGHEST, preferred_element_type=F32)
    dt_t = _softplus(raw_t + bt_ref[...])
    r = SSD_HEADS_PER_GROUP
    for g in range(SSD_GROUPS):
        dt_ref[g] = dt[:, g * r:(g + 1) * r]
        for j in range(tm // CHUNK):
            dtt_ref[g, j] = dt_t[g * r:(g + 1) * r, j * CHUNK:(j + 1) * CHUNK]


def _dt_call(h, w_dt, dt_bias, tm=512):
    t = h.shape[0]
    r = SSD_HEADS_PER_GROUP
    return pl.pallas_call(
        _dt_kernel,
        grid=(t // tm,),
        in_specs=[pl.BlockSpec((tm, D_MODEL), lambda i: (i, 0)),
                  pl.BlockSpec((D_MODEL, SSD_HEADS), lambda i: (0, 0)),
                  pl.BlockSpec((SSD_HEADS, D_MODEL), lambda i: (0, 0)),
                  pl.BlockSpec((1, SSD_HEADS), lambda i: (0, 0)),
                  pl.BlockSpec((SSD_HEADS, 1), lambda i: (0, 0))],
        out_specs=[pl.BlockSpec((SSD_GROUPS, tm, r), lambda i: (0, i, 0)),
                   pl.BlockSpec((SSD_GROUPS, tm // CHUNK, r, CHUNK), lambda i: (0, i, 0, 0))],
        out_shape=[jax.ShapeDtypeStruct((SSD_GROUPS, t, r), F32),
                   jax.ShapeDtypeStruct((SSD_GROUPS, t // CHUNK, r, CHUNK), F32)],
        compiler_params=_params(("parallel",)),
        name="dt_proj",
    )(h, w_dt, w_dt.T, dt_bias.reshape(1, SSD_HEADS), dt_bias.reshape(SSD_HEADS, 1))


def _ssd_kernel(x_ref, b_ref, c_ref, wx_ref, wb_ref, wc_ref, bx_ref, bb_ref, bc_ref,
                dt_ref, dtt_ref, alr_ref, alc_ref, dsk_ref, bd_ref, trit_ref, e_ref,
                y_ref,
                state_ref, ux_ref, ub_ref, uc_ref, xc_ref, bcv_ref, ccv_ref, xdt_ref,
                wst_ref, eacs_ref, acs_ref, acst_ref):
    s_idx = pl.program_id(2)
    ts = x_ref.shape[0]
    nc = ts // CHUNK
    r = SSD_HEADS_PER_GROUP

    @pl.when(s_idx == 0)
    def _():
        state_ref[...] = jnp.zeros_like(state_ref)
        ux_ref[0:8, :] = jnp.zeros((8, ux_ref.shape[1]), F32)
        ub_ref[0:8, :] = jnp.zeros((8, ub_ref.shape[1]), F32)
        uc_ref[0:8, :] = jnp.zeros((8, uc_ref.shape[1]), F32)

    def conv_silu(raw_ref, u_ref, w_ref, bias_ref):
        u_ref[8:8 + ts, :] = raw_ref[...]
        acc = bias_ref[...] + w_ref[0:1, :] * u_ref[5:5 + ts, :]
        for k in range(1, SSD_CONV):
            acc = acc + w_ref[k:k + 1, :] * u_ref[5 + k:5 + k + ts, :]
        u_ref[0:8, :] = u_ref[ts:ts + 8, :]
        return acc * _sigmoid(acc)

    xc = conv_silu(x_ref, ux_ref, wx_ref, bx_ref)
    xc_ref[...] = xc
    bcv_ref[...] = conv_silu(b_ref, ub_ref, wb_ref, bb_ref).astype(BF16)
    ccv_ref[...] = conv_silu(c_ref, uc_ref, wc_ref, bc_ref).astype(BF16)

    a_row = -jnp.exp(alr_ref[0])
    a_col = -jnp.exp(alc_ref[0])
    dt = dt_ref[0]
    a = dt * a_row
    bd = bd_ref[...]
    acs = jnp.zeros((ts, r), F32)
    for part in _split3(a):
        acs = acs + jnp.dot(bd, part, preferred_element_type=F32)
    acs_ref[...] = acs
    a_t = dtt_ref[0].reshape(nc * r, CHUNK) * jnp.concatenate([a_col] * nc, axis=0)
    acs_t = jnp.zeros((nc * r, CHUNK), F32)
    trit = trit_ref[...]
    for part in _split3(a_t):
        acs_t = acs_t + jnp.dot(part, trit, preferred_element_type=F32)
    acst_ref[...] = acs_t

    a_last = jnp.concatenate(
        [jnp.broadcast_to(acs[c * CHUNK + CHUNK - 1:c * CHUNK + CHUNK, :], (CHUNK, r))
         for c in range(nc)], axis=0)
    e_mat = e_ref[...]

    def expand(v):
        hi, lo = _split2(v)
        return (jnp.dot(hi, e_mat, preferred_element_type=F32)
                + jnp.dot(lo, e_mat, preferred_element_type=F32))

    xdt = xc * expand(dt)
    xdt_ref[...] = xdt.astype(BF16)
    wst_ref[...] = (xdt * expand(jnp.exp(a_last - acs))).astype(BF16)
    eacs_ref[...] = expand(jnp.exp(acs))

    row_i = lax.broadcasted_iota(jnp.int32, (CHUNK, CHUNK), 0)
    col_i = lax.broadcasted_iota(jnp.int32, (CHUNK, CHUNK), 1)
    tril = col_i <= row_i
    left_half = lax.broadcasted_iota(jnp.int32, (CHUNK, 128), 1) < SSD_HEAD_DIM
    dsk = dsk_ref[...]

    def chunk_body(c, carry):
        rows = pl.ds(pl.multiple_of(c * CHUNK, CHUNK), CHUNK)
        cc = ccv_ref[rows, :]
        bc = bcv_ref[rows, :]
        cb = lax.dot_general(cc, bc, (((1,), (1,)), ((), ())), preferred_element_type=F32)
        state = state_ref[...]
        eacs = eacs_ref[rows, :]
        y_off = jnp.dot(cc, state.astype(BF16), preferred_element_type=F32) * eacs
        acs_c = acs_ref[rows, :]
        acs_tc = acst_ref[pl.ds(pl.multiple_of(c * r, r), r), :]
        for p in range(r // 2):
            lanes = slice(p * 128, (p + 1) * 128)
            xp = xdt_ref[rows, lanes]
            halves = []
            for hh in (2 * p, 2 * p + 1):
                diff = acs_c[:, hh:hh + 1] - acs_tc[hh:hh + 1, :]
                decay = jnp.exp(jnp.where(tril, diff, -jnp.inf))
                halves.append(jnp.dot((cb * decay).astype(BF16), xp, preferred_element_type=F32))
            y_diag = jnp.where(left_half, halves[0], halves[1])
            y_ref[rows, lanes] = y_diag + y_off[:, lanes] + dsk[:, lanes] * xc_ref[rows, lanes]
        upd = lax.dot_general(bc, wst_ref[rows, :], (((0,), (0,)), ((), ())),
                              preferred_element_type=F32)
        state_ref[...] = state * eacs[CHUNK - 1:CHUNK, :] + upd
        return carry

    lax.fori_loop(0, nc, chunk_body, 0)


def _ssd_call(pa, conv_w, conv_b, dt, dt_t, a_log, d_skip, batch, seq):
    t = batch * seq
    ts = SSD_TILE
    ns = seq // ts
    nc = ts // CHUNK
    r = SSD_HEADS_PER_GROUP
    gw = SSD_GROUP_WIDTH
    n = SSD_STATE
    xcol0 = OFF_XBC // gw
    bcol0 = (OFF_XBC + SSD_D_INNER) // n
    ccol0 = bcol0 + SSD_GROUPS

    li = jnp.arange(ts)
    bd = ((li[None, :] <= li[:, None]) & (li[None, :] // CHUNK == li[:, None] // CHUNK)).astype(BF16)
    lc = jnp.arange(CHUNK)
    trit = (lc[:, None] <= lc[None, :]).astype(BF16)
    e_mat = (jnp.arange(gw)[None, :] // SSD_HEAD_DIM == jnp.arange(r)[:, None]).astype(BF16)
    conv_b2 = conv_b.reshape(1, SSD_CONV_DIM)
    alr = a_log.reshape(SSD_GROUPS, 1, r)
    alc = a_log.reshape(SSD_GROUPS, r, 1)
    dsk = jnp.repeat(d_skip, SSD_HEAD_DIM).reshape(1, SSD_D_INNER)

    row = lambda b, g, s: b * ns + s
    in_specs = [
        pl.BlockSpec((ts, gw), lambda b, g, s: (row(b, g, s), xcol0 + g)),
        pl.BlockSpec((ts, n), lambda b, g, s: (row(b, g, s), bcol0 + g)),
        pl.BlockSpec((ts, n), lambda b, g, s: (row(b, g, s), ccol0 + g)),
        pl.BlockSpec((SSD_CONV, gw), lambda b, g, s: (0, g)),
        pl.BlockSpec((SSD_CONV, n), lambda b, g, s: (0, SSD_D_INNER // n + g)),
        pl.BlockSpec((SSD_CONV, n), lambda b, g, s: (0, SSD_D_INNER // n + SSD_GROUPS + g)),
        pl.BlockSpec((1, gw), lambda b, g, s: (0, g)),
        pl.BlockSpec((1, n), lambda b, g, s: (0, SSD_D_INNER // n + g)),
        pl.BlockSpec((1, n), lambda b, g, s: (0, SSD_D_INNER // n + SSD_GROUPS + g)),
        pl.BlockSpec((1, ts, r), lambda b, g, s: (g, row(b, g, s), 0)),
        pl.BlockSpec((1, nc, r, CHUNK), lambda b, g, s: (g, row(b, g, s), 0, 0)),
        pl.BlockSpec((1, 1, r), lambda b, g, s: (g, 0, 0)),
        pl.BlockSpec((1, r, 1), lambda b, g, s: (g, 0, 0)),
        pl.BlockSpec((1, gw), lambda b, g, s: (0, g)),
        pl.BlockSpec((ts, ts), lambda b, g, s: (0, 0)),
        pl.BlockSpec((CHUNK, CHUNK), lambda b, g, s: (0, 0)),
        pl.BlockSpec((r, gw), lambda b, g, s: (0, 0)),
    ]
    scratch = [
        pltpu.VMEM((n, gw), F32),
        pltpu.VMEM((ts + 8, gw), F32),
        pltpu.VMEM((ts + 8, n), F32),
        pltpu.VMEM((ts + 8, n), F32),
        pltpu.VMEM((ts, gw), F32),
        pltpu.VMEM((ts, n), BF16),
        pltpu.VMEM((ts, n), BF16),
        pltpu.VMEM((ts, gw), BF16),
        pltpu.VMEM((ts, gw), BF16),
        pltpu.VMEM((ts, gw), F32),
        pltpu.VMEM((ts, r), F32),
        pltpu.VMEM((nc * r, CHUNK), F32),
    ]
    return pl.pallas_call(
        _ssd_kernel,
        grid=(batch, SSD_GROUPS, ns),
        in_specs=in_specs,
        out_specs=pl.BlockSpec((ts, gw), lambda b, g, s: (row(b, g, s), g)),
        out_shape=jax.ShapeDtypeStruct((t, SSD_D_INNER), F32),
        scratch_shapes=scratch,
        compiler_params=_params(("parallel", "parallel", "arbitrary")),
        name="ssd_scan",
    )(pa, pa, pa, conv_w, conv_w, conv_w, conv_b2, conv_b2, conv_b2,
      dt, dt_t, alr, alc, dsk, bd, trit, e_mat)


def _attn_kernel(q_ref, k_ref, v_ref, bias_ref, lam_ref, g_ref, o_ref, *, lam_init):
    i = pl.program_id(2)
    tq = q_ref.shape[0]
    dh = DIFF_HEAD_DIM
    lv = lam_ref[...]
    lam = (jnp.exp(jnp.sum(lv[0:1] * lv[1:2], axis=1, keepdims=True))
           - jnp.exp(jnp.sum(lv[2:3] * lv[3:4], axis=1, keepdims=True)) + lam_init)

    q = q_ref[...] * jnp.asarray(dh ** -0.5, BF16)
    qs = (q[:, :dh], q[:, dh:])
    nt = (((1,), (1,)), ((), ()))

    def scores(j, d):
        rows = pl.ds(pl.multiple_of(j * tq, tq), tq)
        kj = k_ref[rows, :]
        vj = v_ref[rows, :]
        bias = bias_ref[0, d]
        s = [lax.dot_general(qs[m], kj[:, m * dh:(m + 1) * dh], nt,
                             preferred_element_type=F32) + bias for m in range(2)]
        return s, vj

    s, vj = scores(i, 0)
    carry = []
    for m in range(2):
        mx = jnp.max(s[m], axis=1, keepdims=True)
        p = jnp.exp(s[m] - mx)
        carry += [mx, jnp.sum(p, axis=1, keepdims=True),
                  jnp.dot(p.astype(BF16), vj, preferred_element_type=F32)]

    def body(j, carry):
        s, vj = scores(j, i - j)
        out = []
        for m in range(2):
            mx, l, acc = carry[3 * m:3 * m + 3]
            mx_new = jnp.maximum(mx, jnp.max(s[m], axis=1, keepdims=True))
            scale = jnp.exp(mx - mx_new)
            p = jnp.exp(s[m] - mx_new)
            out += [mx_new, scale * l + jnp.sum(p, axis=1, keepdims=True),
                    scale * acc + jnp.dot(p.astype(BF16), vj, preferred_element_type=F32)]
        return tuple(out)

    _, l1, acc1, _, l2, acc2 = lax.fori_loop(0, i, body, tuple(carry))
    o = acc1 / l1 - lam * (acc2 / l2)
    ms = jnp.mean(o * o, axis=1, keepdims=True)
    o = o * lax.rsqrt(ms + RMS_EPS) * g_ref[...] * (1.0 - lam_init)
    o_ref[...] = o.astype(o_ref.dtype)


def _rel_bucket(rel):
    half = REL_BUCKETS // 2
    max_exact = half // 2
    ret = jnp.where(rel > 0, half, 0)
    n = jnp.abs(rel)
    nf = jnp.maximum(n, 1).astype(F32)
    large = max_exact + (jnp.log(nf / max_exact) / math.log(REL_MAX_DIST / max_exact)
                         * (half - max_exact)).astype(jnp.int32)
    large = jnp.minimum(large, half - 1)
    return ret + jnp.where(n < max_exact, n, large)


def _bias_tiles(rel_bias, seq):
    tq = ATTN_TILE
    nd = seq // tq
    qq = jnp.arange(tq)[None, :, None]
    kk = jnp.arange(tq)[None, None, :]
    d = jnp.arange(nd)[:, None, None]
    rel = kk - qq - d * tq
    bias = rel_bias.astype(F32)[_rel_bucket(rel)]
    allowed = (d > 0) | ((kk // CHUNK) <= (qq // CHUNK))
    bias = jnp.where(allowed[..., None], bias, NEG_BIG)
    return bias.transpose(3, 0, 1, 2)


def _attn_call(qkv, bias_tiles, lam_vecs, norm_g, layer_idx, batch, seq):
    t = batch * seq
    tq = ATTN_TILE
    nq = seq // tq
    w = 2 * DIFF_HEAD_DIM
    lam_init = 0.8 - 0.6 * math.exp(-0.3 * layer_idx)
    return pl.pallas_call(
        functools.partial(_attn_kernel, lam_init=lam_init),
        grid=(batch, DIFF_HEADS, nq),
        in_specs=[pl.BlockSpec((tq, w), lambda b, h, i: (b * nq + i, h)),
                  pl.BlockSpec((seq, w), lambda b, h, i: (b, DIFF_HEADS + h)),
                  pl.BlockSpec((seq, w), lambda b, h, i: (b, 2 * DIFF_HEADS + h)),
                  pl.BlockSpec((1, nq, tq, tq), lambda b, h, i: (h, 0, 0, 0)),
                  pl.BlockSpec((4, DIFF_HEAD_DIM), lambda b, h, i: (0, 0)),
                  pl.BlockSpec((1, w), lambda b, h, i: (0, 0))],
        out_specs=pl.BlockSpec((tq, w), lambda b, h, i: (b * nq + i, h)),
        out_shape=jax.ShapeDtypeStruct((t, DIFF_WIDTH), BF16),
        compiler_params=_params(("parallel", "parallel", "arbitrary")),
        name="diff_attn",
    )(qkv, qkv, qkv, bias_tiles, lam_vecs, norm_g.reshape(1, w))


def _layer_norm(x, g, b):
    mu = jnp.mean(x, axis=1, keepdims=True)
    xc = x - mu
    var = jnp.mean(xc * xc, axis=1, keepdims=True)
    return xc * lax.rsqrt(var + LN_EPS) * g + b


def _mix_kernel(y_ref, z_ref, ao_ref, g0_ref, g1_ref, h_ref, ng_ref, wso_ref, wao_ref,
                gb_ref, wo_ref, lg_ref, lb_ref, o_ref):
    z = z_ref[...]
    yg = y_ref[...] * (z * _sigmoid(z))
    ms = jnp.mean(yg * yg, axis=1, keepdims=True)
    yn = (yg * lax.rsqrt(ms + RMS_EPS) * ng_ref[...]).astype(BF16)
    y_ssd = jnp.dot(yn, wso_ref[...], preferred_element_type=F32)
    y_att = jnp.dot(ao_ref[...], wao_ref[...], preferred_element_type=F32)
    gb = gb_ref[...]
    gate0 = _sigmoid(g0_ref[...] + gb[:, :D_MODEL])
    gate1 = _sigmoid(g1_ref[...] + gb[:, D_MODEL:])
    mixed = (gate0 * y_ssd + gate1 * y_att).astype(BF16)
    mix = jnp.dot(mixed, wo_ref[...], preferred_element_type=F32)
    o_ref[...] = _layer_norm(DN_ALPHA * h_ref[...] + mix, lg_ref[...], lb_ref[...])


def _mix_call(y, pa, ao, h, norm_g, w_ssd_out, w_attn_out, gate_b, w_o, ln_g, ln_b, tm=256):
    t = h.shape[0]
    d = D_MODEL
    gcol0 = (OFF_DT) // d
    const = lambda i: (0, 0)
    return pl.pallas_call(
        _mix_kernel,
        grid=(t // tm,),
        in_specs=[pl.BlockSpec((tm, SSD_D_INNER), lambda i: (i, 0)),
                  pl.BlockSpec((tm, SSD_D_INNER), lambda i: (i, 0)),
                  pl.BlockSpec((tm, DIFF_WIDTH), lambda i: (i, 0)),
                  pl.BlockSpec((tm, d), lambda i: (i, gcol0)),
                  pl.BlockSpec((tm, d), lambda i: (i, gcol0 + 1)),
                  pl.BlockSpec((tm, d), lambda i: (i, 0)),
                  pl.BlockSpec((1, SSD_D_INNER), const),
                  pl.BlockSpec((SSD_D_INNER, d), const),
                  pl.BlockSpec((DIFF_WIDTH, d), const),
                  pl.BlockSpec((1, 2 * d), const),
                  pl.BlockSpec((d, d), const),
                  pl.BlockSpec((1, d), const),
                  pl.BlockSpec((1, d), const)],
        out_specs=pl.BlockSpec((tm, d), lambda i: (i, 0)),
        out_shape=jax.ShapeDtypeStruct((t, d), F32),
        compiler_params=_params(("parallel",)),
        name="mix_ln",
    )(y, pa, ao, pa, pa, h, norm_g.reshape(1, -1), w_ssd_out.astype(BF16),
      w_attn_out.astype(BF16), gate_b.reshape(1, -1), w_o.astype(BF16),
      ln_g.reshape(1, -1), ln_b.reshape(1, -1))


def _router_kernel(h_ref, w_ref, b_ref, tri_ref, idx_ref, wt_ref, rank_ref, cnt_ref, run_ref):
    @pl.when(pl.program_id(0) == 0)
    def _():
        run_ref[...] = jnp.zeros_like(run_ref)

    tm = h_ref.shape[0]
    ne = N_EXPERTS
    logits = jnp.dot(h_ref[...], w_ref[...], precision=HIGHEST,
                     preferred_element_type=F32) + b_ref[...]
    lane = lax.broadcasted_iota(jnp.int32, (tm, ne), 1).astype(F32)
    work = logits
    sel, vals = [], []
    for _ in range(TOP_K):
        mx = jnp.max(work, axis=1, keepdims=True)
        first = jnp.min(jnp.where(work == mx, lane, float(ne)), axis=1, keepdims=True)
        hit = lane == first
        sel.append((first, hit))
        vals.append(mx)
        work = jnp.where(hit, -jnp.inf, work)
    exps = [jnp.exp(v - vals[0]) for v in vals]
    denom = exps[0] + exps[1] + exps[2] + exps[3]

    onehot = jnp.zeros((tm, ne), F32)
    for _, hit in sel:
        onehot = onehot + hit.astype(F32)
    before = jnp.dot(tri_ref[...], onehot.astype(BF16), preferred_element_type=F32)
    before = before + run_ref[...]

    out_lane = lax.broadcasted_iota(jnp.int32, (tm, 128), 1)
    idx_out = jnp.zeros((tm, 128), jnp.int32)
    wt_out = jnp.zeros((tm, 128), F32)
    rank_out = jnp.zeros((tm, 128), jnp.int32)
    for k, (first, hit) in enumerate(sel):
        rank = jnp.sum(jnp.where(hit, before, 0.0), axis=1, keepdims=True)
        idx_out = jnp.where(out_lane == k, first.astype(jnp.int32), idx_out)
        wt_out = jnp.where(out_lane == k, exps[k] / denom, wt_out)
        rank_out = jnp.where(out_lane == k, rank.astype(jnp.int32), rank_out)
    idx_ref[...] = idx_out
    wt_ref[...] = wt_out
    rank_ref[...] = rank_out
    total = run_ref[...] + jnp.sum(onehot, axis=0, keepdims=True)
    run_ref[...] = total
    cnt_ref[...] = total


def _router_call(h, w_router, b_router, tm=512):
    t = h.shape[0]
    li = jnp.arange(tm)
    tri = (li[None, :] < li[:, None]).astype(BF16)
    const = lambda i: (0, 0)
    return pl.pallas_call(
        _router_kernel,
        grid=(t // tm,),
        in_specs=[pl.BlockSpec((tm, D_MODEL), lambda i: (i, 0)),
                  pl.BlockSpec((D_MODEL, N_EXPERTS), const),
                  pl.BlockSpec((1, N_EXPERTS), const),
                  pl.BlockSpec((tm, tm), const)],
        out_specs=[pl.BlockSpec((tm, 128), lambda i: (i, 0)),
                   pl.BlockSpec((tm, 128), lambda i: (i, 0)),
                   pl.BlockSpec((tm, 128), lambda i: (i, 0)),
                   pl.BlockSpec((1, N_EXPERTS), const)],
        out_shape=[jax.ShapeDtypeStruct((t, 128), jnp.int32),
                   jax.ShapeDtypeStruct((t, 128), F32),
                   jax.ShapeDtypeStruct((t, 128), jnp.int32),
                   jax.ShapeDtypeStruct((1, N_EXPERTS), F32)],
        scratch_shapes=[pltpu.VMEM((1, N_EXPERTS), F32)],
        compiler_params=_params(("arbitrary",)),
        name="router",
    )(h, w_router, b_router.reshape(1, N_EXPERTS), tri)


def _dispatch_kernel(slot_ref, x_hbm, xs_in, xs_out, sem):
    del xs_in
    i = pl.program_id(0)
    n = TOKEN_TILE * TOP_K

    def row_copy(a):
        tok = i * TOKEN_TILE + a // TOP_K
        return pltpu.make_async_copy(x_hbm.at[pl.ds(tok, 1)],
                                     xs_out.at[pl.ds(slot_ref[0, 0, a], 1)], sem)

    def start(a, c):
        row_copy(a).start()
        return c

    def wait(a, c):
        row_copy(a).wait()
        return c

    lax.fori_loop(0, n, start, 0)
    lax.fori_loop(0, n, wait, 0)


def _dispatch_call(x, slots, cap):
    t, d = x.shape
    nt = t // TOKEN_TILE
    n = TOKEN_TILE * TOP_K
    xs_init = jnp.zeros((cap, d), x.dtype)
    return pl.pallas_call(
        _dispatch_kernel,
        grid=(nt,),
        in_specs=[pl.BlockSpec((1, 1, n), lambda i: (i, 0, 0), memory_space=pltpu.SMEM),
                  pl.BlockSpec(memory_space=pl.ANY),
                  pl.BlockSpec(memory_space=pl.ANY)],
        out_specs=pl.BlockSpec(memory_space=pl.ANY),
        out_shape=jax.ShapeDtypeStruct((cap, d), x.dtype),
        scratch_shapes=[pltpu.SemaphoreType.DMA],
        input_output_aliases={2: 0},
        compiler_params=_params(("arbitrary",)),
        name="moe_dispatch",
    )(slots.reshape(nt, 1, n), x, xs_init)


def _expert_kernel(be_ref, nb_ref, x_ref, wg_ref, bg_ref, wu_ref, bu_ref, wd_ref, bd_ref,
                   o_ref, wgb_ref, wub_ref, wdb_ref):
    i = pl.program_id(0)
    prev = be_ref[jnp.maximum(i - 1, 0)]
    changed = jnp.logical_or(i == 0, be_ref[i] != prev)

    @pl.when(changed)
    def _():
        wgb_ref[...] = wg_ref[0].astype(BF16)
        wub_ref[...] = wu_ref[0].astype(BF16)
        wdb_ref[...] = wd_ref[0].astype(BF16)

    @pl.when(i < nb_ref[0])
    def _():
        xb = x_ref[...].astype(BF16)
        g = jnp.dot(xb, wgb_ref[...], preferred_element_type=F32) + bg_ref[0]
        u = jnp.dot(xb, wub_ref[...], preferred_element_type=F32) + bu_ref[0]
        g = jnp.minimum(g, SWIGLU_LIMIT)
        u = jnp.clip(u, -SWIGLU_LIMIT, SWIGLU_LIMIT)
        act = g * _sigmoid(SWIGLU_ALPHA * g) * (u + 1.0)
        o_ref[...] = jnp.dot(act.astype(BF16), wdb_ref[...],
                             preferred_element_type=F32) + bd_ref[0]

    @pl.when(i >= nb_ref[0])
    def _():
        o_ref[...] = jnp.zeros_like(o_ref)


def _expert_call(xs, blk_expert, n_used, w_gate, b_gate, w_up, b_up, w_down, b_down):
    cap, d = xs.shape
    bm = EXPERT_BLOCK
    nb = cap // bm
    wspec = lambda shape: pl.BlockSpec(shape, lambda i, be, nu: (be[i], 0, 0))
    grid_spec = pltpu.PrefetchScalarGridSpec(
        num_scalar_prefetch=2,
        grid=(nb,),
        in_specs=[pl.BlockSpec((bm, d), lambda i, be, nu: (i, 0)),
                  wspec((1, d, D_FF)), wspec((1, 1, D_FF)),
                  wspec((1, d, D_FF)), wspec((1, 1, D_FF)),
                  wspec((1, D_FF, d)), wspec((1, 1, d))],
        out_specs=pl.BlockSpec((bm, d), lambda i, be, nu: (i, 0)),
        scratch_shapes=[pltpu.VMEM((d, D_FF), BF16), pltpu.VMEM((d, D_FF), BF16),
                        pltpu.VMEM((D_FF, d), BF16)],
    )
    ne = N_EXPERTS
    return pl.pallas_call(
        _expert_kernel,
        grid_spec=grid_spec,
        out_shape=jax.ShapeDtypeStruct((cap, d), F32),
        compiler_params=_params(("arbitrary",)),
        name="moe_experts",
    )(blk_expert, n_used, xs, w_gate, b_gate.reshape(ne, 1, D_FF), w_up,
      b_up.reshape(ne, 1, D_FF), w_down, b_down.reshape(ne, 1, d))


def _combine_kernel(slot_ref, ys_hbm, wt_ref, h_ref, lg_ref, lb_ref, o_ref, buf_ref, sem):
    n = TOKEN_TILE * TOP_K

    def row_copy(a):
        return pltpu.make_async_copy(ys_hbm.at[pl.ds(slot_ref[0, 0, a], 1)],
                                     buf_ref.at[a % TOP_K, pl.ds(a // TOP_K, 1)], sem)

    def start(a, c):
        row_copy(a).start()
        return c

    def wait(a, c):
        row_copy(a).wait()
        return c

    lax.fori_loop(0, n, start, 0)
    lax.fori_loop(0, n, wait, 0)

    wt = wt_ref[...]
    ff = wt[:, 0:1] * buf_ref[0]
    for k in range(1, TOP_K):
        ff = ff + wt[:, k:k + 1] * buf_ref[k]
    o_ref[...] = _layer_norm(DN_ALPHA * h_ref[...] + ff, lg_ref[...], lb_ref[...])


def _combine_call(ys, slots, wts, h, ln_g, ln_b):
    t, d = h.shape
    nt = t // TOKEN_TILE
    n = TOKEN_TILE * TOP_K
    const = lambda i: (0, 0)
    return pl.pallas_call(
        _combine_kernel,
        grid=(nt,),
        in_specs=[pl.BlockSpec((1, 1, n), lambda i: (i, 0, 0), memory_space=pltpu.SMEM),
                  pl.BlockSpec(memory_space=pl.ANY),
                  pl.BlockSpec((TOKEN_TILE, 128), lambda i: (i, 0)),
                  pl.BlockSpec((TOKEN_TILE, d), lambda i: (i, 0)),
                  pl.BlockSpec((1, d), const),
                  pl.BlockSpec((1, d), const)],
        out_specs=pl.BlockSpec((TOKEN_TILE, d), lambda i: (i, 0)),
        out_shape=jax.ShapeDtypeStruct((t, d), F32),
        scratch_shapes=[pltpu.VMEM((TOP_K, TOKEN_TILE, d), F32), pltpu.SemaphoreType.DMA],
        compiler_params=_params(("arbitrary",)),
        name="moe_combine_ln",
    )(slots.reshape(nt, 1, n), ys, wts, h, ln_g.reshape(1, d), ln_b.reshape(1, d))


def _moe_layout(idx, rank, counts, n_blocks):
    bm = EXPERT_BLOCK
    counts = counts.reshape(N_EXPERTS).astype(jnp.int32)
    padded = (counts + bm - 1) // bm * bm
    pad_end = jnp.cumsum(padded)
    pad_start = pad_end - padded
    onehot = idx[:, :, None] == jnp.arange(N_EXPERTS, dtype=jnp.int32)[None, None, :]
    slots = rank + jnp.sum(jnp.where(onehot, pad_start[None, None, :], 0), axis=-1)
    blk_start = jnp.arange(n_blocks, dtype=jnp.int32) * bm
    blk_expert = jnp.sum((pad_end[None, :] <= blk_start[:, None]).astype(jnp.int32), axis=1)
    blk_expert = jnp.minimum(blk_expert, N_EXPERTS - 1)
    n_used = (pad_end[-1] // bm).reshape(1)
    return slots.astype(jnp.int32), blk_expert, n_used


def kernel(x, rel_bias, w_in, conv_w, conv_b, dt_bias, a_log, d_skip, ssd_norm_g, w_ssd_out, diff_lambda, diff_norm_g, w_attn_out, gate_b, w_o, ln1_g, ln1_b, w_router, b_router, w_gate, b_gate, w_up, b_up, w_down, b_down, ln2_g, ln2_b):
    batch, seq, d = x.shape
    t = batch * seq
    n_assign = t * TOP_K
    n_blocks = (n_assign + N_EXPERTS * (EXPERT_BLOCK - 1) + EXPERT_BLOCK - 1) // EXPERT_BLOCK
    cap = n_blocks * EXPERT_BLOCK
    bias_tiles = _bias_tiles(rel_bias, seq)

    h = x.reshape(t, d)
    for l in range(DEPTH):
        w_l = w_in[l]
        w_a = jnp.concatenate([w_l[:, :OFF_DT], w_l[:, OFF_G:]], axis=1).astype(BF16)
        w_b = w_l[:, OFF_Q:OFF_G].astype(BF16)
        pa = _matmul(h, w_a, F32, 1024, 1024)
        qkv = _matmul(h, w_b, BF16, 1024, 1024)
        dt, dt_t = _dt_call(h, w_l[:, OFF_DT:OFF_Q], dt_bias[l])
        y = _ssd_call(pa, conv_w[l], conv_b[l], dt, dt_t, a_log[l], d_skip[l], batch, seq)
        ao = _attn_call(qkv, bias_tiles, diff_lambda[l], diff_norm_g[l], l, batch, seq)
        h1 = _mix_call(y, pa, ao, h, ssd_norm_g[l], w_ssd_out[l], w_attn_out[l], gate_b[l],
                       w_o[l], ln1_g[l], ln1_b[l])
        idx, wts, rank, counts = _router_call(h1, w_router[l], b_router[l])
        slots, blk_expert, n_used = _moe_layout(idx[:, :TOP_K], rank[:, :TOP_K], counts, n_blocks)
        xs = _dispatch_call(h1, slots, cap)
        ys = _expert_call(xs, blk_expert, n_used, w_gate[l], b_gate[l], w_up[l], b_up[l],
                          w_down[l], b_down[l])
        h = _combine_call(ys, slots, wts, h1, ln2_g[l], ln2_b[l])
    return h.reshape(batch, seq, d)
```

```python
import functools
import math

import jax
import jax.numpy as jnp
from jax import lax
from jax.experimental import pallas as pl
from jax.experimental.pallas import tpu as pltpu

F32 = jnp.float32
BF16 = jnp.bfloat16
HIGHEST = lax.Precision.HIGHEST

D_MODEL = 1024
DEPTH = 2
CHUNK = 64

SSD_D_INNER = 2048
SSD_HEAD_DIM = 64
SSD_HEADS = 32
SSD_GROUPS = 4
SSD_HEADS_PER_GROUP = 8
SSD_STATE = 128
SSD_CONV = 4
SSD_CONV_DIM = 3072
SSD_GROUP_WIDTH = SSD_HEADS_PER_GROUP * SSD_HEAD_DIM

DIFF_HEAD_DIM = 64
DIFF_HEADS = 8
DIFF_WIDTH = 1024

REL_BUCKETS = 32
REL_MAX_DIST = 128

N_EXPERTS = 32
TOP_K = 4
D_FF = 1024
SWIGLU_ALPHA = 1.702
SWIGLU_LIMIT = 7.0

DN_ALPHA = (2 * DEPTH) ** 0.25
LN_EPS = 1e-5
RMS_EPS = 1e-5

OFF_XBC = 2048
OFF_DT = 5120
OFF_Q = 5152
OFF_V = 7200
OFF_G = 8224
IN_COLS = 10272

VMEM_LIMIT_BYTES = 56 * 1024 * 1024

ATTN_TILE = 256
SSD_TILE = 256
EXPERT_BLOCK = 256
DISPATCH_TILE = 512
COMBINE_TILE = 256
NEG_BIG = -1e30


def _params(semantics, **kwargs):
    return pltpu.CompilerParams(dimension_semantics=semantics,
                                vmem_limit_bytes=VMEM_LIMIT_BYTES, **kwargs)


def _sigmoid(x):
    return 1.0 / (1.0 + jnp.exp(-x))


def _split2(v):
    hi = v.astype(BF16)
    lo = (v - hi.astype(F32)).astype(BF16)
    return hi, lo


def _split3(v):
    hi = v.astype(BF16)
    r = v - hi.astype(F32)
    mid = r.astype(BF16)
    lo = (r - mid.astype(F32)).astype(BF16)
    return hi, mid, lo


def _mm_kernel(x_ref, w_ref, o_ref, xb_ref):
    @pl.when(pl.program_id(1) == 0)
    def _():
        xb_ref[...] = x_ref[...].astype(BF16)

    o_ref[...] = jnp.dot(xb_ref[...], w_ref[...],
                         preferred_element_type=F32).astype(o_ref.dtype)


def _matmul(x, w, out_dtype, bm, bn):
    m, k = x.shape
    n = w.shape[1]
    return pl.pallas_call(
        _mm_kernel,
        grid=(m // bm, n // bn),
        in_specs=[pl.BlockSpec((bm, k), lambda i, j: (i, 0)),
                  pl.BlockSpec((k, bn), lambda i, j: (0, j))],
        out_specs=pl.BlockSpec((bm, bn), lambda i, j: (i, j)),
        out_shape=jax.ShapeDtypeStruct((m, n), out_dtype),
        scratch_shapes=[pltpu.VMEM((bm, k), BF16)],
        compiler_params=_params(("parallel", "arbitrary")),
        name="in_proj",
    )(x, w)


def _mm_nt_kernel(x_ref, wt_ref, o_ref):
    res = lax.dot_general(wt_ref[...], x_ref[...].astype(BF16), (((1,), (1,)), ((), ())),
                          preferred_element_type=F32)
    tile = o_ref.shape[2]
    for c in range(o_ref.shape[0]):
        o_ref[c] = res[:, c * tile:(c + 1) * tile].astype(o_ref.dtype)


def _matmul_nt(x, w_t, tile, bm=1024):
    m, k = x.shape
    n = w_t.shape[0]
    return pl.pallas_call(
        _mm_nt_kernel,
        grid=(m // bm,),
        in_specs=[pl.BlockSpec((bm, k), lambda i: (i, 0)),
                  pl.BlockSpec((n, k), lambda i: (0, 0))],
        out_specs=pl.BlockSpec((bm // tile, n, tile), lambda i: (i, 0, 0)),
        out_shape=jax.ShapeDtypeStruct((m // tile, n, tile), BF16),
        compiler_params=_params(("parallel",)),
        name="v_proj_t",
    )(x, w_t)


def _softplus(x):
    return jnp.maximum(x, 0.0) + jnp.log(1.0 + jnp.exp(-jnp.abs(x)))


def _dt_kernel(x_ref, w_ref, wt_ref, b_ref, bt_ref, dt_ref, dtt_ref):
    x = x_ref[...]
    tm = x.shape[0]
    raw = jnp.dot(x, w_ref[...], precision=HIGHEST, preferred_element_type=F32)
    dt = _softplus(raw + b_ref[...])
    raw_t = lax.dot_general(wt_ref[...], x, (((1,), (1,)), ((), ())),
                            precision=HIGHEST, preferred_element_type=F32)
    dt_t = _softplus(raw_t + bt_ref[...])
    r = SSD_HEADS_PER_GROUP
    for g in range(SSD_GROUPS):
        dt_ref[g] = dt[:, g * r:(g + 1) * r]
        for j in range(tm // CHUNK):
            dtt_ref[g, j] = dt_t[g * r:(g + 1) * r, j * CHUNK:(j + 1) * CHUNK]


def _dt_call(h, w_dt, dt_bias, tm=512):
    t = h.shape[0]
    r = SSD_HEADS_PER_GROUP
    return pl.pallas_call(
        _dt_kernel,
        grid=(t // tm,),
        in_specs=[pl.BlockSpec((tm, D_MODEL), lambda i: (i, 0)),
                  pl.BlockSpec((D_MODEL, SSD_HEADS), lambda i: (0, 0)),
                  pl.BlockSpec((SSD_HEADS, D_MODEL), lambda i: (0, 0)),
                  pl.BlockSpec((1, SSD_HEADS), lambda i: (0, 0)),
                  pl.BlockSpec((SSD_HEADS, 1), lambda i: (0, 0))],
        out_specs=[pl.BlockSpec((SSD_GROUPS, tm, r), lambda i: (0, i, 0)),
                   pl.BlockSpec((SSD_GROUPS, tm // CHUNK, r, CHUNK), lambda i: (0, i, 0, 0))],
        out_shape=[jax.ShapeDtypeStruct((SSD_GROUPS, t, r), F32),
                   jax.ShapeDtypeStruct((SSD_GROUPS, t // CHUNK, r, CHUNK), F32)],
        compiler_params=_params(("parallel",)),
        name="dt_proj",
    )(h, w_dt, w_dt.T, dt_bias.reshape(1, SSD_HEADS), dt_bias.reshape(SSD_HEADS, 1))


def _ssd_kernel(x_ref, b_ref, c_ref, wx_ref, wb_ref, wc_ref, bx_ref, bb_ref, bc_ref,
                dt_ref, dtt_ref, alr_ref, alc_ref, dsk_ref, bd_ref, trit_ref, e_ref,
                y_ref,
                state_ref, ux_ref, ub_ref, uc_ref, xc_ref, bcv_ref, ccv_ref, xdt_ref,
                wst_ref, eacs_ref, acs_ref, acst_ref):
    s_idx = pl.program_id(2)
    ts = x_ref.shape[0]
    nc = ts // CHUNK
    r = SSD_HEADS_PER_GROUP

    @pl.when(s_idx == 0)
    def _():
        state_ref[...] = jnp.zeros_like(state_ref)
        ux_ref[0:8, :] = jnp.zeros((8, ux_ref.shape[1]), F32)
        ub_ref[0:8, :] = jnp.zeros((8, ub_ref.shape[1]), F32)
        uc_ref[0:8, :] = jnp.zeros((8, uc_ref.shape[1]), F32)

    def conv_silu(raw_ref, u_ref, w_ref, bias_ref):
        u_ref[8:8 + ts, :] = raw_ref[...]
        acc = bias_ref[...] + w_ref[0:1, :] * u_ref[5:5 + ts, :]
        for k in range(1, SSD_CONV):
            acc = acc + w_ref[k:k + 1, :] * u_ref[5 + k:5 + k + ts, :]
        u_ref[0:8, :] = u_ref[ts:ts + 8, :]
        return acc * _sigmoid(acc)

    xc = conv_silu(x_ref, ux_ref, wx_ref, bx_ref)
    xc_ref[...] = xc
    bcv_ref[...] = conv_silu(b_ref, ub_ref, wb_ref, bb_ref).astype(BF16)
    ccv_ref[...] = conv_silu(c_ref, uc_ref, wc_ref, bc_ref).astype(BF16)

    a_row = -jnp.exp(alr_ref[0])
    a_col = -jnp.exp(alc_ref[0])
    dt = dt_ref[0]
    a = dt * a_row
    bd = bd_ref[...]
    acs = jnp.zeros((ts, r), F32)
    for part in _split3(a):
        acs = acs + jnp.dot(bd, part, preferred_element_type=F32)
    acs_ref[...] = acs
    a_t = dtt_ref[0].reshape(nc * r, CHUNK) * jnp.concatenate([a_col] * nc, axis=0)
    acs_t = jnp.zeros((nc * r, CHUNK), F32)
    trit = trit_ref[...]
    for part in _split3(a_t):
        acs_t = acs_t + jnp.dot(part, trit, preferred_element_type=F32)
    acst_ref[...] = acs_t

    a_last = jnp.concatenate(
        [jnp.broadcast_to(acs[c * CHUNK + CHUNK - 1:c * CHUNK + CHUNK, :], (CHUNK, r))
         for c in range(nc)], axis=0)
    e_mat = e_ref[...]

    def expand(v):
        hi, lo = _split2(v)
        return (jnp.dot(hi, e_mat, preferred_element_type=F32)
                + jnp.dot(lo, e_mat, preferred_element_type=F32))

    xdt = xc * expand(dt)
    xdt_ref[...] = xdt.astype(BF16)
    wst_ref[...] = (xdt * expand(jnp.exp(a_last - acs))).astype(BF16)
    eacs_ref[...] = expand(jnp.exp(acs))

    row_i = lax.broadcasted_iota(jnp.int32, (CHUNK, CHUNK), 0)
    col_i = lax.broadcasted_iota(jnp.int32, (CHUNK, CHUNK), 1)
    tril = col_i <= row_i
    left_half = lax.broadcasted_iota(jnp.int32, (CHUNK, 128), 1) < SSD_HEAD_DIM
    dsk = dsk_ref[...]

    def chunk_body(c, carry):
        rows = pl.ds(pl.multiple_of(c * CHUNK, CHUNK), CHUNK)
        cc = ccv_ref[rows, :]
        bc = bcv_ref[rows, :]
        cb = lax.dot_general(cc, bc, (((1,), (1,)), ((), ())), preferred_element_type=F32)
        state = state_ref[...]
        eacs = eacs_ref[rows, :]
        y_off = jnp.dot(cc, state.astype(BF16), preferred_element_type=F32) * eacs
        acs_c = acs_ref[rows, :]
        acs_tc = acst_ref[pl.ds(pl.multiple_of(c * r, r), r), :]
        for p in range(r // 2):
            lanes = slice(p * 128, (p + 1) * 128)
            xp = xdt_ref[rows, lanes]
            halves = []
            for hh in (2 * p, 2 * p + 1):
                diff = acs_c[:, hh:hh + 1] - acs_tc[hh:hh + 1, :]
                decay = jnp.exp(jnp.where(tril, diff, -jnp.inf))
                halves.append(jnp.dot((cb * decay).astype(BF16), xp, preferred_element_type=F32))
            y_diag = jnp.where(left_half, halves[0], halves[1])
            y_ref[rows, lanes] = y_diag + y_off[:, lanes] + dsk[:, lanes] * xc_ref[rows, lanes]
        upd = lax.dot_general(bc, wst_ref[rows, :], (((0,), (0,)), ((), ())),
                              preferred_element_type=F32)
        state_ref[...] = state * eacs[CHUNK - 1:CHUNK, :] + upd
        return carry

    lax.fori_loop(0, nc, chunk_body, 0)


def _ssd_call(pa, conv_w, conv_b, dt, dt_t, a_log, d_skip, batch, seq):
    t = batch * seq
    ts = SSD_TILE
    ns = seq // ts
    nc = ts // CHUNK
    r = SSD_HEADS_PER_GROUP
    gw = SSD_GROUP_WIDTH
    n = SSD_STATE
    xcol0 = OFF_XBC // gw
    bcol0 = (OFF_XBC + SSD_D_INNER) // n
    ccol0 = bcol0 + SSD_GROUPS

    li = jnp.arange(ts)
    bd = ((li[None, :] <= li[:, None]) & (li[None, :] // CHUNK == li[:, None] // CHUNK)).astype(BF16)
    lc = jnp.arange(CHUNK)
    trit = (lc[:, None] <= lc[None, :]).astype(BF16)
    e_mat = (jnp.arange(gw)[None, :] // SSD_HEAD_DIM == jnp.arange(r)[:, None]).astype(BF16)
    conv_b2 = conv_b.reshape(1, SSD_CONV_DIM)
    alr = a_log.reshape(SSD_GROUPS, 1, r)
    alc = a_log.reshape(SSD_GROUPS, r, 1)
    dsk = jnp.repeat(d_skip, SSD_HEAD_DIM).reshape(1, SSD_D_INNER)

    row = lambda b, g, s: b * ns + s
    in_specs = [
        pl.BlockSpec((ts, gw), lambda b, g, s: (row(b, g, s), xcol0 + g)),
        pl.BlockSpec((ts, n), lambda b, g, s: (row(b, g, s), bcol0 + g)),
        pl.BlockSpec((ts, n), lambda b, g, s: (row(b, g, s), ccol0 + g)),
        pl.BlockSpec((SSD_CONV, gw), lambda b, g, s: (0, g)),
        pl.BlockSpec((SSD_CONV, n), lambda b, g, s: (0, SSD_D_INNER // n + g)),
        pl.BlockSpec((SSD_CONV, n), lambda b, g, s: (0, SSD_D_INNER // n + SSD_GROUPS + g)),
        pl.BlockSpec((1, gw), lambda b, g, s: (0, g)),
        pl.BlockSpec((1, n), lambda b, g, s: (0, SSD_D_INNER // n + g)),
        pl.BlockSpec((1, n), lambda b, g, s: (0, SSD_D_INNER // n + SSD_GROUPS + g)),
        pl.BlockSpec((1, ts, r), lambda b, g, s: (g, row(b, g, s), 0)),
        pl.BlockSpec((1, nc, r, CHUNK), lambda b, g, s: (g, row(b, g, s), 0, 0)),
        pl.BlockSpec((1, 1, r), lambda b, g, s: (g, 0, 0)),
        pl.BlockSpec((1, r, 1), lambda b, g, s: (g, 0, 0)),
        pl.BlockSpec((1, gw), lambda b, g, s: (0, g)),
        pl.BlockSpec((ts, ts), lambda b, g, s: (0, 0)),
        pl.BlockSpec((CHUNK, CHUNK), lambda b, g, s: (0, 0)),
        pl.BlockSpec((r, gw), lambda b, g, s: (0, 0)),
    ]
    scratch = [
        pltpu.VMEM((n, gw), F32),
        pltpu.VMEM((ts + 8, gw), F32),
        pltpu.VMEM((ts + 8, n), F32),
        pltpu.VMEM((ts + 8, n), F32),
        pltpu.VMEM((ts, gw), F32),
        pltpu.VMEM((ts, n), BF16),
        pltpu.VMEM((ts, n), BF16),
        pltpu.VMEM((ts, gw), BF16),
        pltpu.VMEM((ts, gw), BF16),
        pltpu.VMEM((ts, gw), F32),
        pltpu.VMEM((ts, r), F32),
        pltpu.VMEM((nc * r, CHUNK), F32),
    ]
    return pl.pallas_call(
        _ssd_kernel,
        grid=(batch, SSD_GROUPS, ns),
        in_specs=in_specs,
        out_specs=pl.BlockSpec((ts, gw), lambda b, g, s: (row(b, g, s), g)),
        out_shape=jax.ShapeDtypeStruct((t, SSD_D_INNER), F32),
        scratch_shapes=scratch,
        compiler_params=_params(("parallel", "parallel", "arbitrary")),
        name="ssd_scan",
    )(pa, pa, pa, conv_w, conv_w, conv_w, conv_b2, conv_b2, conv_b2,
      dt, dt_t, alr, alc, dsk, bd, trit, e_mat)


def _attn_kernel(q_ref, k_ref, vt_ref, bias_ref, lam_ref, g_ref, o_ref, sa_ref, sb_ref, *,
                 lam_init):
    i = pl.program_id(2)
    tq = q_ref.shape[0]
    dh = DIFF_HEAD_DIM
    lv = lam_ref[...]
    lam = (jnp.exp(jnp.sum(lv[0:1] * lv[1:2], axis=1, keepdims=True))
           - jnp.exp(jnp.sum(lv[2:3] * lv[3:4], axis=1, keepdims=True)) + lam_init)

    q = q_ref[...] * jnp.asarray(dh ** -0.5, BF16)
    qs = (q[:, :dh], q[:, dh:])
    nt = (((1,), (1,)), ((), ()))

    n_tiles = bias_ref.shape[1] - 1

    def tile_at(t):
        valid = t <= i
        j = jnp.where(valid, jnp.where(t == 0, i, t - 1), 0)
        d = jnp.where(valid, jnp.where(t == 0, 0, i - t + 1), n_tiles)
        return j, d

    def scores(t, s_ref):
        j, d = tile_at(t)
        kj = k_ref[pl.ds(pl.multiple_of(j * tq, tq), tq), :]
        bias = bias_ref[0, d]
        for m in range(2):
            s_ref[m] = lax.dot_general(kj[:, m * dh:(m + 1) * dh], qs[m], nt,
                                       preferred_element_type=F32) + bias

    def update(t, s_ref, carry):
        j, _ = tile_at(t)
        vtj = vt_ref[j]
        out = []
        for m in range(2):
            mx, l, acc = carry[3 * m:3 * m + 3]
            s = s_ref[m]
            mx_new = jnp.maximum(mx, jnp.max(s, axis=0, keepdims=True))
            scale = jnp.exp(mx - mx_new)
            p = jnp.exp(s - mx_new)
            out += [mx_new, scale * l + jnp.sum(p, axis=0, keepdims=True),
                    scale * acc + jnp.dot(vtj, p.astype(BF16), preferred_element_type=F32)]
        return tuple(out)

    def body(u, carry):
        scores(2 * u + 1, sb_ref)
        carry = update(2 * u, sa_ref, carry)
        scores(2 * u + 2, sa_ref)
        return update(2 * u + 1, sb_ref, carry)

    init = []
    for _ in range(2):
        init += [jnp.full((1, tq), NEG_BIG, F32), jnp.zeros((1, tq), F32),
                 jnp.zeros((2 * dh, tq), F32)]
    scores(0, sa_ref)
    _, l1, acc1, _, l2, acc2 = lax.fori_loop(0, (i + 2) // 2, body, tuple(init))
    o = acc1 / l1 - lam * (acc2 / l2)
    ms = jnp.mean(o * o, axis=0, keepdims=True)
    o = o * lax.rsqrt(ms + RMS_EPS) * g_ref[...] * (1.0 - lam_init)
    o_ref[...] = o.T.astype(o_ref.dtype)


def _rel_bucket(rel):
    half = REL_BUCKETS // 2
    max_exact = half // 2
    ret = jnp.where(rel > 0, half, 0)
    n = jnp.abs(rel)
    nf = jnp.maximum(n, 1).astype(F32)
    large = max_exact + (jnp.log(nf / max_exact) / math.log(REL_MAX_DIST / max_exact)
                         * (half - max_exact)).astype(jnp.int32)
    large = jnp.minimum(large, half - 1)
    return ret + jnp.where(n < max_exact, n, large)


def _bias_tiles(rel_bias, seq):
    tq = ATTN_TILE
    nd = seq // tq
    kk = jnp.arange(tq)[None, :, None]
    qq = jnp.arange(tq)[None, None, :]
    d = jnp.arange(nd)[:, None, None]
    rel = kk - qq - d * tq
    bucket = _rel_bucket(rel)[None]
    table = rel_bias.astype(F32)
    bias = jnp.zeros((DIFF_HEADS, nd, tq, tq), F32)
    for b in range(REL_BUCKETS):
        bias = jnp.where(bucket == b, table[b][:, None, None, None], bias)
    allowed = (d > 0) | ((kk // CHUNK) <= (qq // CHUNK))
    bias = jnp.where(allowed[None], bias, NEG_BIG)
    masked = jnp.full((DIFF_HEADS, 1, tq, tq), NEG_BIG, F32)
    return jnp.concatenate([bias, masked], axis=1)


def _attn_call(qk, vt, bias_tiles, lam_vecs, norm_g, layer_idx, batch, seq):
    t = batch * seq
    tq = ATTN_TILE
    nq = seq // tq
    w = 2 * DIFF_HEAD_DIM
    lam_init = 0.8 - 0.6 * math.exp(-0.3 * layer_idx)
    return pl.pallas_call(
        functools.partial(_attn_kernel, lam_init=lam_init),
        grid=(batch, DIFF_HEADS, nq),
        in_specs=[pl.BlockSpec((tq, w), lambda b, h, i: (b * nq + i, h)),
                  pl.BlockSpec((seq, w), lambda b, h, i: (b, DIFF_HEADS + h)),
                  pl.BlockSpec((nq, w, tq), lambda b, h, i: (b, h, 0)),
                  pl.BlockSpec((1, nq + 1, tq, tq), lambda b, h, i: (h, 0, 0, 0)),
                  pl.BlockSpec((4, DIFF_HEAD_DIM), lambda b, h, i: (0, 0)),
                  pl.BlockSpec((w, 1), lambda b, h, i: (0, 0))],
        out_specs=pl.BlockSpec((tq, w), lambda b, h, i: (b * nq + i, h)),
        out_shape=jax.ShapeDtypeStruct((t, DIFF_WIDTH), BF16),
        scratch_shapes=[pltpu.VMEM((2, tq, tq), F32), pltpu.VMEM((2, tq, tq), F32)],
        compiler_params=_params(("parallel", "parallel", "arbitrary")),
        name="diff_attn",
    )(qk, qk, vt, bias_tiles, lam_vecs, norm_g.reshape(w, 1))


def _layer_norm(x, g, b):
    mu = jnp.mean(x, axis=1, keepdims=True)
    xc = x - mu
    var = jnp.mean(xc * xc, axis=1, keepdims=True)
    return xc * lax.rsqrt(var + LN_EPS) * g + b


def _mix_kernel(y_ref, z_ref, ao_ref, g0_ref, g1_ref, h_ref, ng_ref, wso_ref, wao_ref,
                gb_ref, wo_ref, lg_ref, lb_ref, o_ref):
    z = z_ref[...]
    yg = y_ref[...] * (z * _sigmoid(z))
    ms = jnp.mean(yg * yg, axis=1, keepdims=True)
    yn = (yg * lax.rsqrt(ms + RMS_EPS) * ng_ref[...]).astype(BF16)
    y_ssd = jnp.dot(yn, wso_ref[...], preferred_element_type=F32)
    y_att = jnp.dot(ao_ref[...], wao_ref[...], preferred_element_type=F32)
    gb = gb_ref[...]
    gate0 = _sigmoid(g0_ref[...] + gb[:, :D_MODEL])
    gate1 = _sigmoid(g1_ref[...] + gb[:, D_MODEL:])
    mixed = (gate0 * y_ssd + gate1 * y_att).astype(BF16)
    mix = jnp.dot(mixed, wo_ref[...], preferred_element_type=F32)
    o_ref[...] = _layer_norm(DN_ALPHA * h_ref[...] + mix, lg_ref[...], lb_ref[...])


def _mix_call(y, pa, ao, h, norm_g, w_ssd_out, w_attn_out, gate_b, w_o, ln_g, ln_b, tm=256):
    t = h.shape[0]
    d = D_MODEL
    gcol0 = (OFF_DT) // d
    const = lambda i: (0, 0)
    return pl.pallas_call(
        _mix_kernel,
        grid=(t // tm,),
        in_specs=[pl.BlockSpec((tm, SSD_D_INNER), lambda i: (i, 0)),
                  pl.BlockSpec((tm, SSD_D_INNER), lambda i: (i, 0)),
                  pl.BlockSpec((tm, DIFF_WIDTH), lambda i: (i, 0)),
                  pl.BlockSpec((tm, d), lambda i: (i, gcol0)),
                  pl.BlockSpec((tm, d), lambda i: (i, gcol0 + 1)),
                  pl.BlockSpec((tm, d), lambda i: (i, 0)),
                  pl.BlockSpec((1, SSD_D_INNER), const),
                  pl.BlockSpec((SSD_D_INNER, d), const),
                  pl.BlockSpec((DIFF_WIDTH, d), const),
                  pl.BlockSpec((1, 2 * d), const),
                  pl.BlockSpec((d, d), const),
                  pl.BlockSpec((1, d), const),
                  pl.BlockSpec((1, d), const)],
        out_specs=pl.BlockSpec((tm, d), lambda i: (i, 0)),
        out_shape=jax.ShapeDtypeStruct((t, d), F32),
        compiler_params=_params(("parallel",)),
        name="mix_ln",
    )(y, pa, ao, pa, pa, h, norm_g.reshape(1, -1), w_ssd_out.astype(BF16),
      w_attn_out.astype(BF16), gate_b.reshape(1, -1), w_o.astype(BF16),
      ln_g.reshape(1, -1), ln_b.reshape(1, -1))


def _router_kernel(h_ref, w_ref, b_ref, tri_ref, idx_ref, wt_ref, rank_ref, cnt_ref, run_ref):
    @pl.when(pl.program_id(0) == 0)
    def _():
        run_ref[...] = jnp.zeros_like(run_ref)

    tm = h_ref.shape[0]
    ne = N_EXPERTS
    logits = jnp.dot(h_ref[...], w_ref[...], precision=HIGHEST,
                     preferred_element_type=F32) + b_ref[...]
    lane = lax.broadcasted_iota(jnp.int32, (tm, ne), 1).astype(F32)
    work = logits
    sel, vals = [], []
    for _ in range(TOP_K):
        mx = jnp.max(work, axis=1, keepdims=True)
        first = jnp.min(jnp.where(work == mx, lane, float(ne)), axis=1, keepdims=True)
        hit = lane == first
        sel.append((first, hit))
        vals.append(mx)
        work = jnp.where(hit, -jnp.inf, work)
    exps = [jnp.exp(v - vals[0]) for v in vals]
    denom = exps[0] + exps[1] + exps[2] + exps[3]

    onehot = jnp.zeros((tm, ne), F32)
    for _, hit in sel:
        onehot = onehot + hit.astype(F32)
    before = jnp.dot(tri_ref[...], onehot.astype(BF16), preferred_element_type=F32)
    before = before + run_ref[...]

    out_lane = lax.broadcasted_iota(jnp.int32, (tm, 128), 1)
    idx_out = jnp.zeros((tm, 128), jnp.int32)
    wt_out = jnp.zeros((tm, 128), F32)
    rank_out = jnp.zeros((tm, 128), jnp.int32)
    for k, (first, hit) in enumerate(sel):
        rank = jnp.sum(jnp.where(hit, before, 0.0), axis=1, keepdims=True)
        idx_out = jnp.where(out_lane == k, first.astype(jnp.int32), idx_out)
        wt_out = jnp.where(out_lane == k, exps[k] / denom, wt_out)
        rank_out = jnp.where(out_lane == k, rank.astype(jnp.int32), rank_out)
    idx_ref[...] = idx_out
    wt_ref[...] = wt_out
    rank_ref[...] = rank_out
    total = run_ref[...] + jnp.sum(onehot, axis=0, keepdims=True)
    run_ref[...] = total
    cnt_ref[...] = total


def _router_call(h, w_router, b_router, tm=512):
    t = h.shape[0]
    li = jnp.arange(tm)
    tri = (li[None, :] < li[:, None]).astype(BF16)
    const = lambda i: (0, 0)
    return pl.pallas_call(
        _router_kernel,
        grid=(t // tm,),
        in_specs=[pl.BlockSpec((tm, D_MODEL), lambda i: (i, 0)),
                  pl.BlockSpec((D_MODEL, N_EXPERTS), const),
                  pl.BlockSpec((1, N_EXPERTS), const),
                  pl.BlockSpec((tm, tm), const)],
        out_specs=[pl.BlockSpec((tm, 128), lambda i: (i, 0)),
                   pl.BlockSpec((tm, 128), lambda i: (i, 0)),
                   pl.BlockSpec((tm, 128), lambda i: (i, 0)),
                   pl.BlockSpec((1, N_EXPERTS), const)],
        out_shape=[jax.ShapeDtypeStruct((t, 128), jnp.int32),
                   jax.ShapeDtypeStruct((t, 128), F32),
                   jax.ShapeDtypeStruct((t, 128), jnp.int32),
                   jax.ShapeDtypeStruct((1, N_EXPERTS), F32)],
        scratch_shapes=[pltpu.VMEM((1, N_EXPERTS), F32)],
        compiler_params=_params(("arbitrary",)),
        name="router",
    )(h, w_router, b_router.reshape(1, N_EXPERTS), tri)


def _dispatch_kernel(slot_ref, x_ref, xs_in, xs_out, sem):
    del xs_in
    tt = x_ref.shape[0]
    n = tt * TOP_K

    def start(tok, c):
        for k in range(TOP_K):
            pltpu.make_async_copy(x_ref.at[pl.ds(tok, 1)],
                                  xs_out.at[pl.ds(slot_ref[0, 0, tok * TOP_K + k], 1)], sem).start()
        return c

    lax.fori_loop(0, tt, start, 0, unroll=2)
    pltpu.make_async_copy(xs_out.at[pl.ds(0, n)], xs_out.at[pl.ds(0, n)], sem).wait()


def _dispatch_call(x, slots, cap):
    t, d = x.shape
    tt = DISPATCH_TILE
    nt = t // tt
    n = tt * TOP_K
    xs_init = jnp.zeros((cap, d), x.dtype)
    return pl.pallas_call(
        _dispatch_kernel,
        grid=(nt,),
        in_specs=[pl.BlockSpec((1, 1, n), lambda i: (i, 0, 0), memory_space=pltpu.SMEM),
                  pl.BlockSpec((tt, d), lambda i: (i, 0)),
                  pl.BlockSpec(memory_space=pl.ANY)],
        out_specs=pl.BlockSpec(memory_space=pl.ANY),
        out_shape=jax.ShapeDtypeStruct((cap, d), x.dtype),
        scratch_shapes=[pltpu.SemaphoreType.DMA],
        input_output_aliases={2: 0},
        compiler_params=_params(("arbitrary",), disable_bounds_checks=True),
        name="moe_dispatch",
    )(slots.reshape(nt, 1, n), x, xs_init)


def _expert_kernel(be_ref, nb_ref, x_ref, wg_ref, bg_ref, wu_ref, bu_ref, wd_ref, bd_ref,
                   o_ref, wgb_ref, wub_ref, wdb_ref):
    i = pl.program_id(0)
    prev = be_ref[jnp.maximum(i - 1, 0)]
    changed = jnp.logical_or(i == 0, be_ref[i] != prev)

    @pl.when(changed)
    def _():
        wgb_ref[...] = wg_ref[0].astype(BF16)
        wub_ref[...] = wu_ref[0].astype(BF16)
        wdb_ref[...] = wd_ref[0].astype(BF16)

    @pl.when(i < nb_ref[0])
    def _():
        xb = x_ref[...].astype(BF16)
        g = jnp.dot(xb, wgb_ref[...], preferred_element_type=F32) + bg_ref[0]
        u = jnp.dot(xb, wub_ref[...], preferred_element_type=F32) + bu_ref[0]
        g = jnp.minimum(g, SWIGLU_LIMIT)
        u = jnp.clip(u, -SWIGLU_LIMIT, SWIGLU_LIMIT)
        act = g * _sigmoid(SWIGLU_ALPHA * g) * (u + 1.0)
        o_ref[...] = jnp.dot(act.astype(BF16), wdb_ref[...],
                             preferred_element_type=F32) + bd_ref[0]

    @pl.when(i >= nb_ref[0])
    def _():
        o_ref[...] = jnp.zeros_like(o_ref)


def _expert_call(xs, blk_expert, n_used, w_gate, b_gate, w_up, b_up, w_down, b_down):
    cap, d = xs.shape
    bm = EXPERT_BLOCK
    nb = cap // bm
    wspec = lambda shape: pl.BlockSpec(shape, lambda i, be, nu: (be[i], 0, 0))
    grid_spec = pltpu.PrefetchScalarGridSpec(
        num_scalar_prefetch=2,
        grid=(nb,),
        in_specs=[pl.BlockSpec((bm, d), lambda i, be, nu: (i, 0)),
                  wspec((1, d, D_FF)), wspec((1, 1, D_FF)),
                  wspec((1, d, D_FF)), wspec((1, 1, D_FF)),
                  wspec((1, D_FF, d)), wspec((1, 1, d))],
        out_specs=pl.BlockSpec((bm, d), lambda i, be, nu: (i, 0)),
        scratch_shapes=[pltpu.VMEM((d, D_FF), BF16), pltpu.VMEM((d, D_FF), BF16),
                        pltpu.VMEM((D_FF, d), BF16)],
    )
    ne = N_EXPERTS
    return pl.pallas_call(
        _expert_kernel,
        grid_spec=grid_spec,
        out_shape=jax.ShapeDtypeStruct((cap, d), F32),
        compiler_params=_params(("arbitrary",)),
        name="moe_experts",
    )(blk_expert, n_used, xs, w_gate, b_gate.reshape(ne, 1, D_FF), w_up,
      b_up.reshape(ne, 1, D_FF), w_down, b_down.reshape(ne, 1, d))


def _combine_kernel(slot_ref, next_slot_ref, ys_hbm, wt_ref, h_ref, lg_ref, lb_ref, o_ref,
                    buf_ref, sems):
    i = pl.program_id(0)
    tt = h_ref.shape[0]
    n = tt * TOP_K

    def issue(s_ref, buf):
        def start(a, c):
            pltpu.make_async_copy(ys_hbm.at[pl.ds(s_ref[0, 0, a], 1)],
                                  buf_ref.at[buf, pl.ds(a, 1)], sems.at[buf]).start()
            return c

        lax.fori_loop(0, n, start, 0, unroll=8)

    @pl.when(i == 0)
    def _():
        issue(slot_ref, 0)

    @pl.when(i + 1 < pl.num_programs(0))
    def _():
        issue(next_slot_ref, (i + 1) % 2)

    cur = i % 2
    pltpu.make_async_copy(ys_hbm.at[pl.ds(0, n)], buf_ref.at[cur], sems.at[cur]).wait()

    wt = wt_ref[...]
    ff = wt[:, 0:1] * buf_ref[cur, pl.ds(0, tt), :]
    for k in range(1, TOP_K):
        ff = ff + wt[:, k:k + 1] * buf_ref[cur, pl.ds(k * tt, tt), :]
    o_ref[...] = _layer_norm(DN_ALPHA * h_ref[...] + ff, lg_ref[...], lb_ref[...])


def _combine_call(ys, slots, wts, h, ln_g, ln_b):
    t, d = h.shape
    tt = COMBINE_TILE
    nt = t // tt
    n = tt * TOP_K
    slots_km = slots.reshape(nt, tt, TOP_K).transpose(0, 2, 1).reshape(nt, 1, n)
    const = lambda i: (0, 0)
    return pl.pallas_call(
        _combine_kernel,
        grid=(nt,),
        in_specs=[pl.BlockSpec((1, 1, n), lambda i: (i, 0, 0), memory_space=pltpu.SMEM),
                  pl.BlockSpec((1, 1, n), lambda i: (jnp.minimum(i + 1, nt - 1), 0, 0),
                               memory_space=pltpu.SMEM),
                  pl.BlockSpec(memory_space=pl.ANY),
                  pl.BlockSpec((tt, 128), lambda i: (i, 0)),
                  pl.BlockSpec((tt, d), lambda i: (i, 0)),
                  pl.BlockSpec((1, d), const),
                  pl.BlockSpec((1, d), const)],
        out_specs=pl.BlockSpec((tt, d), lambda i: (i, 0)),
        out_shape=jax.ShapeDtypeStruct((t, d), F32),
        scratch_shapes=[pltpu.VMEM((2, n, d), F32), pltpu.SemaphoreType.DMA((2,))],
        compiler_params=_params(("arbitrary",), disable_bounds_checks=True),
        name="moe_combine_ln",
    )(slots_km, slots_km, ys, wts, h, ln_g.reshape(1, d), ln_b.reshape(1, d))


def _moe_layout(idx, rank, counts, n_blocks):
    bm = EXPERT_BLOCK
    counts = counts.reshape(N_EXPERTS).astype(jnp.int32)
    padded = (counts + bm - 1) // bm * bm
    pad_end = jnp.cumsum(padded)
    pad_start = pad_end - padded
    onehot = idx[:, :, None] == jnp.arange(N_EXPERTS, dtype=jnp.int32)[None, None, :]
    slots = rank + jnp.sum(jnp.where(onehot, pad_start[None, None, :], 0), axis=-1)
    blk_start = jnp.arange(n_blocks, dtype=jnp.int32) * bm
    blk_expert = jnp.sum((pad_end[None, :] <= blk_start[:, None]).astype(jnp.int32), axis=1)
    blk_expert = jnp.minimum(blk_expert, N_EXPERTS - 1)
    n_used = (pad_end[-1] // bm).reshape(1)
    return slots.astype(jnp.int32), blk_expert, n_used


def kernel(x, rel_bias, w_in, conv_w, conv_b, dt_bias, a_log, d_skip, ssd_norm_g, w_ssd_out, diff_lambda, diff_norm_g, w_attn_out, gate_b, w_o, ln1_g, ln1_b, w_router, b_router, w_gate, b_gate, w_up, b_up, w_down, b_down, ln2_g, ln2_b):
    batch, seq, d = x.shape
    t = batch * seq
    n_assign = t * TOP_K
    n_blocks = (n_assign + N_EXPERTS * (EXPERT_BLOCK - 1) + EXPERT_BLOCK - 1) // EXPERT_BLOCK
    cap = n_blocks * EXPERT_BLOCK
    bias_tiles = _bias_tiles(rel_bias, seq)

    h = x.reshape(t, d)
    for l in range(DEPTH):
        w_l = w_in[l]
        w_a = jnp.concatenate([w_l[:, :OFF_DT], w_l[:, OFF_G:]], axis=1).astype(BF16)
        w_b = w_l[:, OFF_Q:OFF_V].astype(BF16)
        w_vt = w_l[:, OFF_V:OFF_G].T.astype(BF16)
        pa = _matmul(h, w_a, F32, 1024, 1024)
        qk = _matmul(h, w_b, BF16, 1024, 1024)
        vt = _matmul_nt(h, w_vt, ATTN_TILE)
        dt, dt_t = _dt_call(h, w_l[:, OFF_DT:OFF_Q], dt_bias[l])
        y = _ssd_call(pa, conv_w[l], conv_b[l], dt, dt_t, a_log[l], d_skip[l], batch, seq)
        ao = _attn_call(qk, vt, bias_tiles, diff_lambda[l], diff_norm_g[l], l, batch, seq)
        h1 = _mix_call(y, pa, ao, h, ssd_norm_g[l], w_ssd_out[l], w_attn_out[l], gate_b[l],
                       w_o[l], ln1_g[l], ln1_b[l])
        idx, wts, rank, counts = _router_call(h1, w_router[l], b_router[l])
        slots, blk_expert, n_used = _moe_layout(idx[:, :TOP_K], rank[:, :TOP_K], counts, n_blocks)
        xs = _dispatch_call(h1, slots, cap)
        ys = _expert_call(xs, blk_expert, n_used, w_gate[l], b_gate[l], w_up[l], b_up[l],
                          w_down[l], b_down[l])
        h = _combine_call(ys, slots, wts, h1, ln2_g[l], ln2_b[l])
    return h.reshape(batch, seq, d)
```

```python
import functools
import math

import jax
import jax.numpy as jnp
from jax import lax
from jax.experimental import pallas as pl
from jax.experimental.pallas import tpu as pltpu

F32 = jnp.float32
BF16 = jnp.bfloat16

D_MODEL = 1024
DEPTH = 2
CHUNK = 64

SSD_D_INNER = 2048
SSD_HEAD_DIM = 64
SSD_HEADS = 32
SSD_GROUPS = 4
SSD_HEADS_PER_GROUP = 8
SSD_STATE = 128
SSD_CONV = 4
SSD_CONV_DIM = 3072
SSD_GROUP_WIDTH = SSD_HEADS_PER_GROUP * SSD_HEAD_DIM

DIFF_HEAD_DIM = 64
DIFF_HEADS = 8
DIFF_WIDTH = 1024

REL_BUCKETS = 32
REL_MAX_DIST = 128

N_EXPERTS = 32
TOP_K = 4
D_FF = 1024
SWIGLU_ALPHA = 1.702
SWIGLU_LIMIT = 7.0

DN_ALPHA = (2 * DEPTH) ** 0.25
LN_EPS = 1e-5
RMS_EPS = 1e-5

OFF_XBC = 2048
OFF_DT = 5120
OFF_Q = 5152
OFF_V = 7200
OFF_G = 8224
IN_COLS = 10272

VMEM_LIMIT_BYTES = 56 * 1024 * 1024

ATTN_TILE = 256
SSD_TILE = 256
SSD_GROUPS_PER_STEP = 2
EXPERT_BLOCK = 256
DISPATCH_TILE = 512
COMBINE_TILE = 256
NEG_BIG = -1e30
LOG2_E = math.log2(math.e)


def _params(semantics, **kwargs):
    return pltpu.CompilerParams(dimension_semantics=semantics,
                                vmem_limit_bytes=VMEM_LIMIT_BYTES, **kwargs)


def _sigmoid(x):
    return 1.0 / (1.0 + jnp.exp(-x))


def _split2(v):
    hi = v.astype(BF16)
    lo = (v - hi.astype(F32)).astype(BF16)
    return hi, lo


def _dot3(a, b, dims):
    ah, al = _split2(a)
    bh, bl = _split2(b)
    dot = lambda p, q: lax.dot_general(p, q, dims, preferred_element_type=F32)
    return dot(ah, bh) + dot(ah, bl) + dot(al, bh)


def _split3(v):
    hi = v.astype(BF16)
    r = v - hi.astype(F32)
    mid = r.astype(BF16)
    lo = (r - mid.astype(F32)).astype(BF16)
    return hi, mid, lo


def _mm_kernel(x_ref, w_ref, o_ref, xb_ref):
    @pl.when(pl.program_id(1) == 0)
    def _():
        xb_ref[...] = x_ref[...].astype(BF16)

    o_ref[...] = jnp.dot(xb_ref[...], w_ref[...],
                         preferred_element_type=F32).astype(o_ref.dtype)


def _matmul(x, w, out_dtype, bm, bn):
    m, k = x.shape
    n = w.shape[1]
    return pl.pallas_call(
        _mm_kernel,
        grid=(m // bm, n // bn),
        in_specs=[pl.BlockSpec((bm, k), lambda i, j: (i, 0)),
                  pl.BlockSpec((k, bn), lambda i, j: (0, j))],
        out_specs=pl.BlockSpec((bm, bn), lambda i, j: (i, j)),
        out_shape=jax.ShapeDtypeStruct((m, n), out_dtype),
        scratch_shapes=[pltpu.VMEM((bm, k), BF16)],
        compiler_params=_params(("parallel", "arbitrary")),
        name="in_proj",
    )(x, w)


def _mm_nt_kernel(x_ref, wt_ref, o_ref):
    res = lax.dot_general(wt_ref[...], x_ref[...].astype(BF16), (((1,), (1,)), ((), ())),
                          preferred_element_type=F32)
    tile = o_ref.shape[2]
    for c in range(o_ref.shape[0]):
        o_ref[c] = res[:, c * tile:(c + 1) * tile].astype(o_ref.dtype)


def _matmul_nt(x, w_t, tile, bm=1024):
    m, k = x.shape
    n = w_t.shape[0]
    return pl.pallas_call(
        _mm_nt_kernel,
        grid=(m // bm,),
        in_specs=[pl.BlockSpec((bm, k), lambda i: (i, 0)),
                  pl.BlockSpec((n, k), lambda i: (0, 0))],
        out_specs=pl.BlockSpec((bm // tile, n, tile), lambda i: (i, 0, 0)),
        out_shape=jax.ShapeDtypeStruct((m // tile, n, tile), BF16),
        compiler_params=_params(("parallel",)),
        name="v_proj_t",
    )(x, w_t)


def _softplus(x):
    return jnp.maximum(x, 0.0) + jnp.log(1.0 + jnp.exp(-jnp.abs(x)))


def _dt_kernel(x_ref, w_ref, wt_ref, b_ref, bt_ref, dt_ref, dtt_ref):
    x = x_ref[...]
    tm = x.shape[0]
    raw = _dot3(x, w_ref[...], (((1,), (0,)), ((), ())))
    dt = _softplus(raw + b_ref[...])
    raw_t = _dot3(wt_ref[...], x, (((1,), (1,)), ((), ())))
    dt_t = _softplus(raw_t + bt_ref[...])
    r = SSD_HEADS_PER_GROUP
    for g in range(SSD_GROUPS):
        dt_ref[g] = dt[:, g * r:(g + 1) * r]
        for j in range(tm // CHUNK):
            dtt_ref[g, j] = dt_t[g * r:(g + 1) * r, j * CHUNK:(j + 1) * CHUNK]


def _dt_call(h, w_dt, dt_bias, tm=512):
    t = h.shape[0]
    r = SSD_HEADS_PER_GROUP
    return pl.pallas_call(
        _dt_kernel,
        grid=(t // tm,),
        in_specs=[pl.BlockSpec((tm, D_MODEL), lambda i: (i, 0)),
                  pl.BlockSpec((D_MODEL, SSD_HEADS), lambda i: (0, 0)),
                  pl.BlockSpec((SSD_HEADS, D_MODEL), lambda i: (0, 0)),
                  pl.BlockSpec((1, SSD_HEADS), lambda i: (0, 0)),
                  pl.BlockSpec((SSD_HEADS, 1), lambda i: (0, 0))],
        out_specs=[pl.BlockSpec((SSD_GROUPS, tm, r), lambda i: (0, i, 0)),
                   pl.BlockSpec((SSD_GROUPS, tm // CHUNK, r, CHUNK), lambda i: (0, i, 0, 0))],
        out_shape=[jax.ShapeDtypeStruct((SSD_GROUPS, t, r), F32),
                   jax.ShapeDtypeStruct((SSD_GROUPS, t // CHUNK, r, CHUNK), F32)],
        compiler_params=_params(("parallel",)),
        name="dt_proj",
    )(h, w_dt, w_dt.T, dt_bias.reshape(1, SSD_HEADS), dt_bias.reshape(SSD_HEADS, 1))


def _ssd_kernel(x_ref, b_ref, c_ref, wx_ref, wb_ref, wc_ref, bx_ref, bb_ref, bc_ref,
                dt_ref, dtt_ref, alr_ref, alc_ref, dsk_ref, bd_ref, trit_ref, e_ref,
                y_ref,
                state_ref, ux_ref, ub_ref, uc_ref, xc_ref, bcv_ref, ccv_ref, xdt_ref,
                wst_ref, eacs_ref, acs_ref, acst_ref):
    s_idx = pl.program_id(2)
    ts = x_ref.shape[0]
    nc = ts // CHUNK
    r = SSD_HEADS_PER_GROUP
    gw = SSD_GROUP_WIDTH
    n = SSD_STATE
    ng = dt_ref.shape[0]

    @pl.when(s_idx == 0)
    def _():
        state_ref[...] = jnp.zeros_like(state_ref)
        ux_ref[0:8, :] = jnp.zeros((8, ux_ref.shape[1]), F32)
        ub_ref[0:8, :] = jnp.zeros((8, ub_ref.shape[1]), F32)
        uc_ref[0:8, :] = jnp.zeros((8, uc_ref.shape[1]), F32)

    def conv_silu(raw_ref, u_ref, w_ref, bias_ref):
        u_ref[8:8 + ts, :] = raw_ref[...]
        acc = bias_ref[...] + w_ref[0:1, :] * u_ref[5:5 + ts, :]
        for k in range(1, SSD_CONV):
            acc = acc + w_ref[k:k + 1, :] * u_ref[5 + k:5 + k + ts, :]
        u_ref[0:8, :] = u_ref[ts:ts + 8, :]
        return acc * _sigmoid(acc)

    xc_ref[...] = conv_silu(x_ref, ux_ref, wx_ref, bx_ref)
    bcv_ref[...] = conv_silu(b_ref, ub_ref, wb_ref, bb_ref).astype(BF16)
    ccv_ref[...] = conv_silu(c_ref, uc_ref, wc_ref, bc_ref).astype(BF16)

    bd = bd_ref[...]
    trit = trit_ref[...]
    e_mat = e_ref[...]

    def expand(v):
        hi, lo = _split2(v)
        return (jnp.dot(hi, e_mat, preferred_element_type=F32)
                + jnp.dot(lo, e_mat, preferred_element_type=F32))

    for g in range(ng):
        gl = slice(g * gw, (g + 1) * gw)
        a_row = -jnp.exp(alr_ref[g])
        a_col = -jnp.exp(alc_ref[g])
        dt = dt_ref[g]
        a = dt * a_row
        acs = jnp.zeros((ts, r), F32)
        for part in _split3(a):
            acs = acs + jnp.dot(bd, part, preferred_element_type=F32)
        acs_ref[g] = acs
        a_t = dtt_ref[g].reshape(nc * r, CHUNK) * jnp.concatenate([a_col] * nc, axis=0)
        acs_t = jnp.zeros((nc * r, CHUNK), F32)
        for part in _split3(a_t):
            acs_t = acs_t + jnp.dot(part, trit, preferred_element_type=F32)
        acst_ref[g] = acs_t
        a_last = jnp.concatenate(
            [jnp.broadcast_to(acs[c * CHUNK + CHUNK - 1:c * CHUNK + CHUNK, :], (CHUNK, r))
             for c in range(nc)], axis=0)
        xdt = xc_ref[:, gl] * expand(dt)
        xdt_ref[:, gl] = xdt.astype(BF16)
        wst_ref[:, gl] = (xdt * expand(jnp.exp(a_last - acs))).astype(BF16)
        eacs_ref[:, gl] = expand(jnp.exp(acs))

    row_i = lax.broadcasted_iota(jnp.int32, (CHUNK, CHUNK), 0)
    col_i = lax.broadcasted_iota(jnp.int32, (CHUNK, CHUNK), 1)
    tril = col_i <= row_i
    left_half = lax.broadcasted_iota(jnp.int32, (CHUNK, 128), 1) < SSD_HEAD_DIM

    def chunk_body(c, carry):
        rows = pl.ds(pl.multiple_of(c * CHUNK, CHUNK), CHUNK)
        for g in range(ng):
            cc = ccv_ref[rows, g * n:(g + 1) * n]
            bc = bcv_ref[rows, g * n:(g + 1) * n]
            cb = lax.dot_general(cc, bc, (((1,), (1,)), ((), ())), preferred_element_type=F32)
            state = state_ref[g]
            eacs = eacs_ref[rows, g * gw:(g + 1) * gw]
            y_off = jnp.dot(cc, state.astype(BF16), preferred_element_type=F32) * eacs
            acs_c = acs_ref[g, rows, :]
            acs_tc = acst_ref[g, pl.ds(pl.multiple_of(c * r, r), r), :]
            for p in range(r // 2):
                lo = g * gw + p * 128
                xp = xdt_ref[rows, lo:lo + 128]
                halves = []
                for hh in (2 * p, 2 * p + 1):
                    diff = acs_c[:, hh:hh + 1] - acs_tc[hh:hh + 1, :]
                    decay = jnp.exp(jnp.where(tril, diff, -jnp.inf))
                    halves.append(jnp.dot((cb * decay).astype(BF16), xp,
                                          preferred_element_type=F32))
                y_diag = jnp.where(left_half, halves[0], halves[1])
                y_ref[rows, lo:lo + 128] = (y_diag + y_off[:, p * 128:(p + 1) * 128]
                                            + dsk_ref[:, lo:lo + 128] * xc_ref[rows, lo:lo + 128])
            upd = lax.dot_general(bc, wst_ref[rows, g * gw:(g + 1) * gw], (((0,), (0,)), ((), ())),
                                  preferred_element_type=F32)
            state_ref[g] = state * eacs[CHUNK - 1:CHUNK, :] + upd
        return carry

    lax.fori_loop(0, nc, chunk_body, 0)


def _ssd_call(pa, conv_w, conv_b, dt, dt_t, a_log, d_skip, batch, seq):
    t = batch * seq
    ts = SSD_TILE
    ns = seq // ts
    nc = ts // CHUNK
    r = SSD_HEADS_PER_GROUP
    ng = SSD_GROUPS_PER_STEP
    gw = ng * SSD_GROUP_WIDTH
    n = ng * SSD_STATE
    xcol0 = OFF_XBC // gw
    bcol0 = (OFF_XBC + SSD_D_INNER) // n
    ccol0 = bcol0 + SSD_GROUPS // ng
    wb0 = SSD_D_INNER // n
    wc0 = wb0 + SSD_GROUPS // ng

    li = jnp.arange(ts)
    bd = ((li[None, :] <= li[:, None]) & (li[None, :] // CHUNK == li[:, None] // CHUNK)).astype(BF16)
    lc = jnp.arange(CHUNK)
    trit = (lc[:, None] <= lc[None, :]).astype(BF16)
    e_mat = (jnp.arange(SSD_GROUP_WIDTH)[None, :] // SSD_HEAD_DIM
             == jnp.arange(r)[:, None]).astype(BF16)
    conv_b2 = conv_b.reshape(1, SSD_CONV_DIM)
    alr = a_log.reshape(SSD_GROUPS, 1, r)
    alc = a_log.reshape(SSD_GROUPS, r, 1)
    dsk = jnp.repeat(d_skip, SSD_HEAD_DIM).reshape(1, SSD_D_INNER)

    row = lambda b, g, s: b * ns + s
    in_specs = [
        pl.BlockSpec((ts, gw), lambda b, g, s: (row(b, g, s), xcol0 + g)),
        pl.BlockSpec((ts, n), lambda b, g, s: (row(b, g, s), bcol0 + g)),
        pl.BlockSpec((ts, n), lambda b, g, s: (row(b, g, s), ccol0 + g)),
        pl.BlockSpec((SSD_CONV, gw), lambda b, g, s: (0, g)),
        pl.BlockSpec((SSD_CONV, n), lambda b, g, s: (0, wb0 + g)),
        pl.BlockSpec((SSD_CONV, n), lambda b, g, s: (0, wc0 + g)),
        pl.BlockSpec((1, gw), lambda b, g, s: (0, g)),
        pl.BlockSpec((1, n), lambda b, g, s: (0, wb0 + g)),
        pl.BlockSpec((1, n), lambda b, g, s: (0, wc0 + g)),
        pl.BlockSpec((ng, ts, r), lambda b, g, s: (g, row(b, g, s), 0)),
        pl.BlockSpec((ng, nc, r, CHUNK), lambda b, g, s: (g, row(b, g, s), 0, 0)),
        pl.BlockSpec((ng, 1, r), lambda b, g, s: (g, 0, 0)),
        pl.BlockSpec((ng, r, 1), lambda b, g, s: (g, 0, 0)),
        pl.BlockSpec((1, gw), lambda b, g, s: (0, g)),
        pl.BlockSpec((ts, ts), lambda b, g, s: (0, 0)),
        pl.BlockSpec((CHUNK, CHUNK), lambda b, g, s: (0, 0)),
        pl.BlockSpec((r, SSD_GROUP_WIDTH), lambda b, g, s: (0, 0)),
    ]
    scratch = [
        pltpu.VMEM((ng, SSD_STATE, SSD_GROUP_WIDTH), F32),
        pltpu.VMEM((ts + 8, gw), F32),
        pltpu.VMEM((ts + 8, n), F32),
        pltpu.VMEM((ts + 8, n), F32),
        pltpu.VMEM((ts, gw), F32),
        pltpu.VMEM((ts, n), BF16),
        pltpu.VMEM((ts, n), BF16),
        pltpu.VMEM((ts, gw), BF16),
        pltpu.VMEM((ts, gw), BF16),
        pltpu.VMEM((ts, gw), F32),
        pltpu.VMEM((ng, ts, r), F32),
        pltpu.VMEM((ng, nc * r, CHUNK), F32),
    ]
    return pl.pallas_call(
        _ssd_kernel,
        grid=(batch, SSD_GROUPS // ng, ns),
        in_specs=in_specs,
        out_specs=pl.BlockSpec((ts, gw), lambda b, g, s: (row(b, g, s), g)),
        out_shape=jax.ShapeDtypeStruct((t, SSD_D_INNER), F32),
        scratch_shapes=scratch,
        compiler_params=_params(("parallel", "parallel", "arbitrary")),
        name="ssd_scan",
    )(pa, pa, pa, conv_w, conv_w, conv_w, conv_b2, conv_b2, conv_b2,
      dt, dt_t, alr, alc, dsk, bd, trit, e_mat)


def _attn_kernel(q_ref, k_ref, vt_ref, bias_ref, lam_ref, g_ref, o_ref, sa_ref, sb_ref, *,
                 lam_init):
    i = pl.program_id(2)
    tq = q_ref.shape[0]
    dh = DIFF_HEAD_DIM
    lv = lam_ref[...]
    lam = (jnp.exp(jnp.sum(lv[0:1] * lv[1:2], axis=1, keepdims=True))
           - jnp.exp(jnp.sum(lv[2:3] * lv[3:4], axis=1, keepdims=True)) + lam_init)

    q = q_ref[...]
    qs = (q[:, :dh], q[:, dh:])
    nt = (((1,), (1,)), ((), ()))

    n_tiles = bias_ref.shape[1] - 1

    def tile_at(t):
        valid = t <= i
        j = jnp.where(valid, jnp.where(t == 0, i, t - 1), 0)
        d = jnp.where(valid, jnp.where(t == 0, 0, i - t + 1), n_tiles)
        return j, d

    def scores(t, s_ref):
        j, d = tile_at(t)
        kj = k_ref[pl.ds(pl.multiple_of(j * tq, tq), tq), :]
        bias = bias_ref[0, d]
        for m in range(2):
            s_ref[m] = lax.dot_general(kj[:, m * dh:(m + 1) * dh], qs[m], nt,
                                       preferred_element_type=F32) + bias

    def update(t, s_ref, carry):
        j, _ = tile_at(t)
        vtj = vt_ref[j]
        out = []
        for m in range(2):
            mx, l, acc = carry[3 * m:3 * m + 3]
            s = s_ref[m]
            mx_new = jnp.maximum(mx, jnp.max(s, axis=0, keepdims=True))
            scale = jnp.exp2(mx - mx_new)
            p = jnp.exp2(s - mx_new)
            out += [mx_new, scale * l + jnp.sum(p, axis=0, keepdims=True),
                    scale * acc + jnp.dot(vtj, p.astype(BF16), preferred_element_type=F32)]
        return tuple(out)

    def body(u, carry):
        scores(2 * u + 1, sb_ref)
        carry = update(2 * u, sa_ref, carry)
        scores(2 * u + 2, sa_ref)
        return update(2 * u + 1, sb_ref, carry)

    init = []
    for _ in range(2):
        init += [jnp.full((1, tq), NEG_BIG, F32), jnp.zeros((1, tq), F32),
                 jnp.zeros((2 * dh, tq), F32)]
    scores(0, sa_ref)
    _, l1, acc1, _, l2, acc2 = lax.fori_loop(0, (i + 2) // 2, body, tuple(init))
    o = acc1 / l1 - lam * (acc2 / l2)
    ms = jnp.mean(o * o, axis=0, keepdims=True)
    o = o * lax.rsqrt(ms + RMS_EPS) * g_ref[...] * (1.0 - lam_init)
    o_ref[...] = o.T.astype(o_ref.dtype)


def _rel_bucket(rel):
    half = REL_BUCKETS // 2
    max_exact = half // 2
    ret = jnp.where(rel > 0, half, 0)
    n = jnp.abs(rel)
    nf = jnp.maximum(n, 1).astype(F32)
    large = max_exact + (jnp.log(nf / max_exact) / math.log(REL_MAX_DIST / max_exact)
                         * (half - max_exact)).astype(jnp.int32)
    large = jnp.minimum(large, half - 1)
    return ret + jnp.where(n < max_exact, n, large)


def _bias_tiles(rel_bias, seq):
    tq = ATTN_TILE
    nd = seq // tq
    kk = jnp.arange(tq)[None, :, None]
    qq = jnp.arange(tq)[None, None, :]
    d = jnp.arange(nd)[:, None, None]
    rel = kk - qq - d * tq
    bucket = _rel_bucket(rel)[None]
    table = rel_bias.astype(F32)
    bias = jnp.zeros((DIFF_HEADS, nd, tq, tq), F32)
    for b in range(REL_BUCKETS):
        bias = jnp.where(bucket == b, table[b][:, None, None, None], bias)
    allowed = (d > 0) | ((kk // CHUNK) <= (qq // CHUNK))
    bias = jnp.where(allowed[None], bias * LOG2_E, NEG_BIG)
    masked = jnp.full((DIFF_HEADS, 1, tq, tq), NEG_BIG, F32)
    return jnp.concatenate([bias, masked], axis=1)


def _attn_call(qk, vt, bias_tiles, lam_vecs, norm_g, layer_idx, batch, seq):
    t = batch * seq
    tq = ATTN_TILE
    nq = seq // tq
    w = 2 * DIFF_HEAD_DIM
    lam_init = 0.8 - 0.6 * math.exp(-0.3 * layer_idx)
    return pl.pallas_call(
        functools.partial(_attn_kernel, lam_init=lam_init),
        grid=(batch, DIFF_HEADS, nq),
        in_specs=[pl.BlockSpec((tq, w), lambda b, h, i: (b * nq + i, h)),
                  pl.BlockSpec((seq, w), lambda b, h, i: (b, DIFF_HEADS + h)),
                  pl.BlockSpec((nq, w, tq), lambda b, h, i: (b, h, 0)),
                  pl.BlockSpec((1, nq + 1, tq, tq), lambda b, h, i: (h, 0, 0, 0)),
                  pl.BlockSpec((4, DIFF_HEAD_DIM), lambda b, h, i: (0, 0)),
                  pl.BlockSpec((w, 1), lambda b, h, i: (0, 0))],
        out_specs=pl.BlockSpec((tq, w), lambda b, h, i: (b * nq + i, h)),
        out_shape=jax.ShapeDtypeStruct((t, DIFF_WIDTH), BF16),
        scratch_shapes=[pltpu.VMEM((2, tq, tq), F32), pltpu.VMEM((2, tq, tq), F32)],
        compiler_params=_params(("parallel", "parallel", "arbitrary")),
        name="diff_attn",
    )(qk, qk, vt, bias_tiles, lam_vecs, norm_g.reshape(w, 1))


def _layer_norm(x, g, b):
    mu = jnp.mean(x, axis=1, keepdims=True)
    xc = x - mu
    var = jnp.mean(xc * xc, axis=1, keepdims=True)
    return xc * lax.rsqrt(var + LN_EPS) * g + b


def _mix_kernel(y_ref, z_ref, ao_ref, g0_ref, g1_ref, h_ref, ng_ref, wso_ref, wao_ref,
                gb_ref, wo_ref, lg_ref, lb_ref, o_ref):
    z = z_ref[...]
    yg = y_ref[...] * (z * _sigmoid(z))
    ms = jnp.mean(yg * yg, axis=1, keepdims=True)
    yn = (yg * lax.rsqrt(ms + RMS_EPS) * ng_ref[...]).astype(BF16)
    y_ssd = jnp.dot(yn, wso_ref[...], preferred_element_type=F32)
    y_att = jnp.dot(ao_ref[...], wao_ref[...], preferred_element_type=F32)
    gb = gb_ref[...]
    gate0 = _sigmoid(g0_ref[...] + gb[:, :D_MODEL])
    gate1 = _sigmoid(g1_ref[...] + gb[:, D_MODEL:])
    mixed = (gate0 * y_ssd + gate1 * y_att).astype(BF16)
    mix = jnp.dot(mixed, wo_ref[...], preferred_element_type=F32)
    o_ref[...] = _layer_norm(DN_ALPHA * h_ref[...] + mix, lg_ref[...], lb_ref[...])


def _mix_call(y, pa, ao, h, norm_g, w_ssd_out, w_attn_out, gate_b, w_o, ln_g, ln_b, tm=256):
    t = h.shape[0]
    d = D_MODEL
    gcol0 = (OFF_DT) // d
    const = lambda i: (0, 0)
    return pl.pallas_call(
        _mix_kernel,
        grid=(t // tm,),
        in_specs=[pl.BlockSpec((tm, SSD_D_INNER), lambda i: (i, 0)),
                  pl.BlockSpec((tm, SSD_D_INNER), lambda i: (i, 0)),
                  pl.BlockSpec((tm, DIFF_WIDTH), lambda i: (i, 0)),
                  pl.BlockSpec((tm, d), lambda i: (i, gcol0)),
                  pl.BlockSpec((tm, d), lambda i: (i, gcol0 + 1)),
                  pl.BlockSpec((tm, d), lambda i: (i, 0)),
                  pl.BlockSpec((1, SSD_D_INNER), const),
                  pl.BlockSpec((SSD_D_INNER, d), const),
                  pl.BlockSpec((DIFF_WIDTH, d), const),
                  pl.BlockSpec((1, 2 * d), const),
                  pl.BlockSpec((d, d), const),
                  pl.BlockSpec((1, d), const),
                  pl.BlockSpec((1, d), const)],
        out_specs=pl.BlockSpec((tm, d), lambda i: (i, 0)),
        out_shape=jax.ShapeDtypeStruct((t, d), F32),
        compiler_params=_params(("parallel",)),
        name="mix_ln",
    )(y, pa, ao, pa, pa, h, norm_g.reshape(1, -1), w_ssd_out.astype(BF16),
      w_attn_out.astype(BF16), gate_b.reshape(1, -1), w_o.astype(BF16),
      ln_g.reshape(1, -1), ln_b.reshape(1, -1))


def _router_kernel(h_ref, w_ref, b_ref, tri_ref, idx_ref, wt_ref, rank_ref, cnt_ref, run_ref):
    @pl.when(pl.program_id(0) == 0)
    def _():
        run_ref[...] = jnp.zeros_like(run_ref)

    tm = h_ref.shape[0]
    ne = N_EXPERTS
    logits = _dot3(h_ref[...], w_ref[...], (((1,), (0,)), ((), ()))) + b_ref[...]
    lane = lax.broadcasted_iota(jnp.int32, (tm, ne), 1).astype(F32)
    work = logits
    sel, vals = [], []
    for _ in range(TOP_K):
        mx = jnp.max(work, axis=1, keepdims=True)
        first = jnp.min(jnp.where(work == mx, lane, float(ne)), axis=1, keepdims=True)
        hit = lane == first
        sel.append((first, hit))
        vals.append(mx)
        work = jnp.where(hit, -jnp.inf, work)
    exps = [jnp.exp(v - vals[0]) for v in vals]
    denom = exps[0] + exps[1] + exps[2] + exps[3]

    onehot = jnp.zeros((tm, ne), F32)
    for _, hit in sel:
        onehot = onehot + hit.astype(F32)
    before = jnp.dot(tri_ref[...], onehot.astype(BF16), preferred_element_type=F32)
    before = before + run_ref[...]

    out_lane = lax.broadcasted_iota(jnp.int32, (tm, 128), 1)
    idx_out = jnp.zeros((tm, 128), jnp.int32)
    wt_out = jnp.zeros((tm, 128), F32)
    rank_out = jnp.zeros((tm, 128), jnp.int32)
    for k, (first, hit) in enumerate(sel):
        rank = jnp.sum(jnp.where(hit, before, 0.0), axis=1, keepdims=True)
        idx_out = jnp.where(out_lane == k, first.astype(jnp.int32), idx_out)
        wt_out = jnp.where(out_lane == k, exps[k] / denom, wt_out)
        rank_out = jnp.where(out_lane == k, rank.astype(jnp.int32), rank_out)
    idx_ref[...] = idx_out
    wt_ref[...] = wt_out
    rank_ref[...] = rank_out
    total = run_ref[...] + jnp.sum(onehot, axis=0, keepdims=True)
    run_ref[...] = total
    cnt_ref[...] = total


def _router_call(h, w_router, b_router, tm=512):
    t = h.shape[0]
    li = jnp.arange(tm)
    tri = (li[None, :] < li[:, None]).astype(BF16)
    const = lambda i: (0, 0)
    return pl.pallas_call(
        _router_kernel,
        grid=(t // tm,),
        in_specs=[pl.BlockSpec((tm, D_MODEL), lambda i: (i, 0)),
                  pl.BlockSpec((D_MODEL, N_EXPERTS), const),
                  pl.BlockSpec((1, N_EXPERTS), const),
                  pl.BlockSpec((tm, tm), const)],
        out_specs=[pl.BlockSpec((tm, 128), lambda i: (i, 0)),
                   pl.BlockSpec((tm, 128), lambda i: (i, 0)),
                   pl.BlockSpec((tm, 128), lambda i: (i, 0)),
                   pl.BlockSpec((1, N_EXPERTS), const)],
        out_shape=[jax.ShapeDtypeStruct((t, 128), jnp.int32),
                   jax.ShapeDtypeStruct((t, 128), F32),
                   jax.ShapeDtypeStruct((t, 128), jnp.int32),
                   jax.ShapeDtypeStruct((1, N_EXPERTS), F32)],
        scratch_shapes=[pltpu.VMEM((1, N_EXPERTS), F32)],
        compiler_params=_params(("arbitrary",)),
        name="router",
    )(h, w_router, b_router.reshape(1, N_EXPERTS), tri)


def _dispatch_kernel(slot_ref, x_ref, xs_in, xs_out, sem):
    del xs_in
    tt = x_ref.shape[0]
    n = tt * TOP_K

    def start(tok, c):
        for k in range(TOP_K):
            pltpu.make_async_copy(x_ref.at[pl.ds(tok, 1)],
                                  xs_out.at[pl.ds(slot_ref[0, 0, tok * TOP_K + k], 1)], sem).start()
        return c

    lax.fori_loop(0, tt, start, 0, unroll=2)
    pltpu.make_async_copy(xs_out.at[pl.ds(0, n)], xs_out.at[pl.ds(0, n)], sem).wait()


def _dispatch_call(x, slots, cap):
    t, d = x.shape
    tt = DISPATCH_TILE
    nt = t // tt
    n = tt * TOP_K
    xs_init = jnp.zeros((cap, d), x.dtype)
    return pl.pallas_call(
        _dispatch_kernel,
        grid=(nt,),
        in_specs=[pl.BlockSpec((1, 1, n), lambda i: (i, 0, 0), memory_space=pltpu.SMEM),
                  pl.BlockSpec((tt, d), lambda i: (i, 0)),
                  pl.BlockSpec(memory_space=pl.ANY)],
        out_specs=pl.BlockSpec(memory_space=pl.ANY),
        out_shape=jax.ShapeDtypeStruct((cap, d), x.dtype),
        scratch_shapes=[pltpu.SemaphoreType.DMA],
        input_output_aliases={2: 0},
        compiler_params=_params(("arbitrary",), disable_bounds_checks=True),
        name="moe_dispatch",
    )(slots.reshape(nt, 1, n), x, xs_init)


def _expert_kernel(be_ref, nb_ref, x_ref, wg_ref, bg_ref, wu_ref, bu_ref, wd_ref, bd_ref,
                   o_ref, wgb_ref, wub_ref, wdb_ref):
    i = pl.program_id(0)
    prev = be_ref[jnp.maximum(i - 1, 0)]
    changed = jnp.logical_or(i == 0, be_ref[i] != prev)

    @pl.when(changed)
    def _():
        wgb_ref[...] = wg_ref[0].astype(BF16)
        wub_ref[...] = wu_ref[0].astype(BF16)
        wdb_ref[...] = wd_ref[0].astype(BF16)

    @pl.when(i < nb_ref[0])
    def _():
        xb = x_ref[...].astype(BF16)
        g = jnp.dot(xb, wgb_ref[...], preferred_element_type=F32) + bg_ref[0]
        u = jnp.dot(xb, wub_ref[...], preferred_element_type=F32) + bu_ref[0]
        g = jnp.minimum(g, SWIGLU_LIMIT)
        u = jnp.clip(u, -SWIGLU_LIMIT, SWIGLU_LIMIT)
        act = g * _sigmoid(SWIGLU_ALPHA * g) * (u + 1.0)
        o_ref[...] = jnp.dot(act.astype(BF16), wdb_ref[...],
                             preferred_element_type=F32) + bd_ref[0]

    @pl.when(i >= nb_ref[0])
    def _():
        o_ref[...] = jnp.zeros_like(o_ref)


def _expert_call(xs, blk_expert, n_used, w_gate, b_gate, w_up, b_up, w_down, b_down):
    cap, d = xs.shape
    bm = EXPERT_BLOCK
    nb = cap // bm
    wspec = lambda shape: pl.BlockSpec(shape, lambda i, be, nu: (be[i], 0, 0))
    grid_spec = pltpu.PrefetchScalarGridSpec(
        num_scalar_prefetch=2,
        grid=(nb,),
        in_specs=[pl.BlockSpec((bm, d), lambda i, be, nu: (i, 0)),
                  wspec((1, d, D_FF)), wspec((1, 1, D_FF)),
                  wspec((1, d, D_FF)), wspec((1, 1, D_FF)),
                  wspec((1, D_FF, d)), wspec((1, 1, d))],
        out_specs=pl.BlockSpec((bm, d), lambda i, be, nu: (i, 0)),
        scratch_shapes=[pltpu.VMEM((d, D_FF), BF16), pltpu.VMEM((d, D_FF), BF16),
                        pltpu.VMEM((D_FF, d), BF16)],
    )
    ne = w_gate.shape[0] * w_gate.shape[1]
    return pl.pallas_call(
        _expert_kernel,
        grid_spec=grid_spec,
        out_shape=jax.ShapeDtypeStruct((cap, d), F32),
        compiler_params=_params(("arbitrary",)),
        name="moe_experts",
    )(blk_expert, n_used, xs, w_gate.reshape(ne, d, D_FF), b_gate.reshape(ne, 1, D_FF),
      w_up.reshape(ne, d, D_FF), b_up.reshape(ne, 1, D_FF), w_down.reshape(ne, D_FF, d),
      b_down.reshape(ne, 1, d))


def _combine_kernel(slot_ref, next_slot_ref, ys_hbm, wt_ref, h_ref, lg_ref, lb_ref, o_ref,
                    buf_ref, sems):
    i = pl.program_id(0)
    tt = h_ref.shape[0]
    n = tt * TOP_K

    def issue(s_ref, buf):
        def start(a, c):
            pltpu.make_async_copy(ys_hbm.at[pl.ds(s_ref[0, 0, a], 1)],
                                  buf_ref.at[buf, pl.ds(a, 1)], sems.at[buf]).start()
            return c

        lax.fori_loop(0, n, start, 0, unroll=8)

    @pl.when(i == 0)
    def _():
        issue(slot_ref, 0)

    @pl.when(i + 1 < pl.num_programs(0))
    def _():
        issue(next_slot_ref, (i + 1) % 2)

    cur = i % 2
    pltpu.make_async_copy(ys_hbm.at[pl.ds(0, n)], buf_ref.at[cur], sems.at[cur]).wait()

    wt = wt_ref[...]
    ff = wt[:, 0:1] * buf_ref[cur, pl.ds(0, tt), :]
    for k in range(1, TOP_K):
        ff = ff + wt[:, k:k + 1] * buf_ref[cur, pl.ds(k * tt, tt), :]
    o_ref[...] = _layer_norm(DN_ALPHA * h_ref[...] + ff, lg_ref[...], lb_ref[...])


def _combine_call(ys, slots, wts, h, ln_g, ln_b):
    t, d = h.shape
    tt = COMBINE_TILE
    nt = t // tt
    n = tt * TOP_K
    slots_km = slots.reshape(nt, tt, TOP_K).transpose(0, 2, 1).reshape(nt, 1, n)
    const = lambda i: (0, 0)
    return pl.pallas_call(
        _combine_kernel,
        grid=(nt,),
        in_specs=[pl.BlockSpec((1, 1, n), lambda i: (i, 0, 0), memory_space=pltpu.SMEM),
                  pl.BlockSpec((1, 1, n), lambda i: (jnp.minimum(i + 1, nt - 1), 0, 0),
                               memory_space=pltpu.SMEM),
                  pl.BlockSpec(memory_space=pl.ANY),
                  pl.BlockSpec((tt, 128), lambda i: (i, 0)),
                  pl.BlockSpec((tt, d), lambda i: (i, 0)),
                  pl.BlockSpec((1, d), const),
                  pl.BlockSpec((1, d), const)],
        out_specs=pl.BlockSpec((tt, d), lambda i: (i, 0)),
        out_shape=jax.ShapeDtypeStruct((t, d), F32),
        scratch_shapes=[pltpu.VMEM((2, n, d), F32), pltpu.SemaphoreType.DMA((2,))],
        compiler_params=_params(("arbitrary",), disable_bounds_checks=True),
        name="moe_combine_ln",
    )(slots_km, slots_km, ys, wts, h, ln_g.reshape(1, d), ln_b.reshape(1, d))


def _moe_layout(idx, rank, counts, n_blocks):
    bm = EXPERT_BLOCK
    counts = counts.reshape(N_EXPERTS).astype(jnp.int32)
    padded = (counts + bm - 1) // bm * bm
    pad_end = jnp.cumsum(padded)
    pad_start = pad_end - padded
    onehot = idx[:, :, None] == jnp.arange(N_EXPERTS, dtype=jnp.int32)[None, None, :]
    slots = rank + jnp.sum(jnp.where(onehot, pad_start[None, None, :], 0), axis=-1)
    blk_start = jnp.arange(n_blocks, dtype=jnp.int32) * bm
    blk_expert = jnp.sum((pad_end[None, :] <= blk_start[:, None]).astype(jnp.int32), axis=1)
    blk_expert = jnp.minimum(blk_expert, N_EXPERTS - 1)
    n_used = (pad_end[-1] // bm).reshape(1)
    return slots.astype(jnp.int32), blk_expert, n_used


def kernel(x, rel_bias, w_in, conv_w, conv_b, dt_bias, a_log, d_skip, ssd_norm_g, w_ssd_out, diff_lambda, diff_norm_g, w_attn_out, gate_b, w_o, ln1_g, ln1_b, w_router, b_router, w_gate, b_gate, w_up, b_up, w_down, b_down, ln2_g, ln2_b):
    batch, seq, d = x.shape
    t = batch * seq
    n_assign = t * TOP_K
    n_blocks = (n_assign + N_EXPERTS * (EXPERT_BLOCK - 1) + EXPERT_BLOCK - 1) // EXPERT_BLOCK
    cap = n_blocks * EXPERT_BLOCK
    bias_tiles = _bias_tiles(rel_bias, seq)
    qk_scale = jnp.concatenate([jnp.full((DIFF_WIDTH,), DIFF_HEAD_DIM ** -0.5 * LOG2_E, F32),
                                jnp.ones((DIFF_WIDTH,), F32)])[None, :]

    h = x.reshape(t, d)
    for l in range(DEPTH):
        w_l = w_in[l]
        w_a = jnp.concatenate([w_l[:, :OFF_DT], w_l[:, OFF_G:]], axis=1).astype(BF16)
        w_b = (w_l[:, OFF_Q:OFF_V] * qk_scale).astype(BF16)
        w_vt = w_l[:, OFF_V:OFF_G].T.astype(BF16)
        pa = _matmul(h, w_a, F32, 1024, 1024)
        qk = _matmul(h, w_b, BF16, 1024, 1024)
        vt = _matmul_nt(h, w_vt, ATTN_TILE)
        dt, dt_t = _dt_call(h, w_l[:, OFF_DT:OFF_Q], dt_bias[l])
        y = _ssd_call(pa, conv_w[l], conv_b[l], dt, dt_t, a_log[l], d_skip[l], batch, seq)
        ao = _attn_call(qk, vt, bias_tiles, diff_lambda[l], diff_norm_g[l], l, batch, seq)
        h1 = _mix_call(y, pa, ao, h, ssd_norm_g[l], w_ssd_out[l], w_attn_out[l], gate_b[l],
                       w_o[l], ln1_g[l], ln1_b[l])
        idx, wts, rank, counts = _router_call(h1, w_router[l], b_router[l])
        slots, blk_expert, n_used = _moe_layout(idx[:, :TOP_K], rank[:, :TOP_K], counts, n_blocks)
        xs = _dispatch_call(h1, slots, cap)
        ys = _expert_call(xs, blk_expert + l * N_EXPERTS, n_used, w_gate, b_gate, w_up, b_up,
                          w_down, b_down)
        h = _combine_call(ys, slots, wts, h1, ln2_g[l], ln2_b[l])
    return h.reshape(batch, seq, d)
```

```python
import functools
import math

import jax
import jax.numpy as jnp
from jax import lax
from jax.experimental import pallas as pl
from jax.experimental.pallas import tpu as pltpu

F32 = jnp.float32
BF16 = jnp.bfloat16

D_MODEL = 1024
DEPTH = 2
CHUNK = 64

SSD_D_INNER = 2048
SSD_HEAD_DIM = 64
SSD_HEADS = 32
SSD_GROUPS = 4
SSD_HEADS_PER_GROUP = 8
SSD_STATE = 128
SSD_CONV = 4
SSD_CONV_DIM = 3072
SSD_GROUP_WIDTH = SSD_HEADS_PER_GROUP * SSD_HEAD_DIM

DIFF_HEAD_DIM = 64
DIFF_HEADS = 8
DIFF_WIDTH = 1024

REL_BUCKETS = 32
REL_MAX_DIST = 128

N_EXPERTS = 32
TOP_K = 4
D_FF = 1024
SWIGLU_ALPHA = 1.702
SWIGLU_LIMIT = 7.0

DN_ALPHA = (2 * DEPTH) ** 0.25
LN_EPS = 1e-5
RMS_EPS = 1e-5

OFF_XBC = 2048
OFF_DT = 5120
OFF_Q = 5152
OFF_V = 7200
OFF_G = 8224
IN_COLS = 10272
PB_GATES = 2048
PB_Q = 4096
PB_K = 5120

VMEM_LIMIT_BYTES = 56 * 1024 * 1024

ATTN_TILE = 256
SSD_TILE = 256
SSD_GROUPS_PER_STEP = 2
EXPERT_BLOCK = 256
DISPATCH_TILE = 512
COMBINE_TILE = 256
NEG_BIG = -1e30
LOG2_E = math.log2(math.e)


def _params(semantics, **kwargs):
    return pltpu.CompilerParams(dimension_semantics=semantics,
                                vmem_limit_bytes=VMEM_LIMIT_BYTES, **kwargs)


def _sigmoid(x):
    return 1.0 / (1.0 + jnp.exp(-x))


def _split2(v):
    hi = v.astype(BF16)
    lo = (v - hi.astype(F32)).astype(BF16)
    return hi, lo


def _dot3(a, b, dims):
    ah, al = _split2(a)
    bh, bl = _split2(b)
    dot = lambda p, q: lax.dot_general(p, q, dims, preferred_element_type=F32)
    return dot(ah, bh) + dot(ah, bl) + dot(al, bh)


def _split3(v):
    hi = v.astype(BF16)
    r = v - hi.astype(F32)
    mid = r.astype(BF16)
    lo = (r - mid.astype(F32)).astype(BF16)
    return hi, mid, lo


def _mm_kernel(x_ref, w_ref, o_ref, xb_ref):
    @pl.when(pl.program_id(1) == 0)
    def _():
        xb_ref[...] = x_ref[...].astype(BF16)

    o_ref[...] = jnp.dot(xb_ref[...], w_ref[...],
                         preferred_element_type=F32).astype(o_ref.dtype)


def _matmul(x, w, out_dtype, bm, bn):
    m, k = x.shape
    n = w.shape[1]
    return pl.pallas_call(
        _mm_kernel,
        grid=(m // bm, n // bn),
        in_specs=[pl.BlockSpec((bm, k), lambda i, j: (i, 0)),
                  pl.BlockSpec((k, bn), lambda i, j: (0, j))],
        out_specs=pl.BlockSpec((bm, bn), lambda i, j: (i, j)),
        out_shape=jax.ShapeDtypeStruct((m, n), out_dtype),
        scratch_shapes=[pltpu.VMEM((bm, k), BF16)],
        compiler_params=_params(("parallel", "arbitrary")),
        name="in_proj",
    )(x, w)


def _mm_nt_kernel(x_ref, wt_ref, o_ref):
    res = lax.dot_general(wt_ref[...], x_ref[...].astype(BF16), (((1,), (1,)), ((), ())),
                          preferred_element_type=F32)
    tile = o_ref.shape[2]
    for c in range(o_ref.shape[0]):
        o_ref[c] = res[:, c * tile:(c + 1) * tile].astype(o_ref.dtype)


def _matmul_nt(x, w_t, tile, bm=1024):
    m, k = x.shape
    n = w_t.shape[0]
    return pl.pallas_call(
        _mm_nt_kernel,
        grid=(m // bm,),
        in_specs=[pl.BlockSpec((bm, k), lambda i: (i, 0)),
                  pl.BlockSpec((n, k), lambda i: (0, 0))],
        out_specs=pl.BlockSpec((bm // tile, n, tile), lambda i: (i, 0, 0)),
        out_shape=jax.ShapeDtypeStruct((m // tile, n, tile), BF16),
        compiler_params=_params(("parallel",)),
        name="v_proj_t",
    )(x, w_t)


def _softplus(x):
    return jnp.maximum(x, 0.0) + jnp.log(1.0 + jnp.exp(-jnp.abs(x)))


def _dt_kernel(x_ref, w_ref, wt_ref, b_ref, bt_ref, dt_ref, dtt_ref):
    x = x_ref[...]
    tm = x.shape[0]
    raw = _dot3(x, w_ref[...], (((1,), (0,)), ((), ())))
    dt = _softplus(raw + b_ref[...])
    raw_t = _dot3(wt_ref[...], x, (((1,), (1,)), ((), ())))
    dt_t = _softplus(raw_t + bt_ref[...])
    r = SSD_HEADS_PER_GROUP
    for g in range(SSD_GROUPS):
        dt_ref[g] = dt[:, g * r:(g + 1) * r]
        for j in range(tm // CHUNK):
            dtt_ref[g, j] = dt_t[g * r:(g + 1) * r, j * CHUNK:(j + 1) * CHUNK]


def _dt_call(h, w_dt, dt_bias, tm=512):
    t = h.shape[0]
    r = SSD_HEADS_PER_GROUP
    return pl.pallas_call(
        _dt_kernel,
        grid=(t // tm,),
        in_specs=[pl.BlockSpec((tm, D_MODEL), lambda i: (i, 0)),
                  pl.BlockSpec((D_MODEL, SSD_HEADS), lambda i: (0, 0)),
                  pl.BlockSpec((SSD_HEADS, D_MODEL), lambda i: (0, 0)),
                  pl.BlockSpec((1, SSD_HEADS), lambda i: (0, 0)),
                  pl.BlockSpec((SSD_HEADS, 1), lambda i: (0, 0))],
        out_specs=[pl.BlockSpec((SSD_GROUPS, tm, r), lambda i: (0, i, 0)),
                   pl.BlockSpec((SSD_GROUPS, tm // CHUNK, r, CHUNK), lambda i: (0, i, 0, 0))],
        out_shape=[jax.ShapeDtypeStruct((SSD_GROUPS, t, r), F32),
                   jax.ShapeDtypeStruct((SSD_GROUPS, t // CHUNK, r, CHUNK), F32)],
        compiler_params=_params(("parallel",)),
        name="dt_proj",
    )(h, w_dt, w_dt.T, dt_bias.reshape(1, SSD_HEADS), dt_bias.reshape(SSD_HEADS, 1))


def _ssd_kernel(x_ref, b_ref, c_ref, wx_ref, wb_ref, wc_ref, bx_ref, bb_ref, bc_ref,
                dt_ref, dtt_ref, alr_ref, alc_ref, dsk_ref, bd_ref, trit_ref, e_ref,
                y_ref,
                state_ref, ux_ref, ub_ref, uc_ref, xc_ref, bcv_ref, ccv_ref, xdt_ref,
                wst_ref, eacs_ref, acs_ref, acst_ref):
    s_idx = pl.program_id(2)
    ts = x_ref.shape[0]
    nc = ts // CHUNK
    r = SSD_HEADS_PER_GROUP
    gw = SSD_GROUP_WIDTH
    n = SSD_STATE
    ng = dt_ref.shape[0]

    @pl.when(s_idx == 0)
    def _():
        state_ref[...] = jnp.zeros_like(state_ref)
        ux_ref[0:8, :] = jnp.zeros((8, ux_ref.shape[1]), F32)
        ub_ref[0:8, :] = jnp.zeros((8, ub_ref.shape[1]), F32)
        uc_ref[0:8, :] = jnp.zeros((8, uc_ref.shape[1]), F32)

    def conv_silu(raw_ref, u_ref, w_ref, bias_ref):
        u_ref[8:8 + ts, :] = raw_ref[...]
        acc = bias_ref[...] + w_ref[0:1, :] * u_ref[5:5 + ts, :]
        for k in range(1, SSD_CONV):
            acc = acc + w_ref[k:k + 1, :] * u_ref[5 + k:5 + k + ts, :]
        u_ref[0:8, :] = u_ref[ts:ts + 8, :]
        return acc * _sigmoid(acc)

    xc_ref[...] = conv_silu(x_ref, ux_ref, wx_ref, bx_ref)
    bcv_ref[...] = conv_silu(b_ref, ub_ref, wb_ref, bb_ref).astype(BF16)
    ccv_ref[...] = conv_silu(c_ref, uc_ref, wc_ref, bc_ref).astype(BF16)

    bd = bd_ref[...]
    trit = trit_ref[...]
    e_mat = e_ref[...]

    def expand(v):
        hi, lo = _split2(v)
        return (jnp.dot(hi, e_mat, preferred_element_type=F32)
                + jnp.dot(lo, e_mat, preferred_element_type=F32))

    for g in range(ng):
        gl = slice(g * gw, (g + 1) * gw)
        a_row = -jnp.exp(alr_ref[g])
        a_col = -jnp.exp(alc_ref[g])
        dt = dt_ref[g]
        a = dt * a_row
        acs = jnp.zeros((ts, r), F32)
        for part in _split3(a):
            acs = acs + jnp.dot(bd, part, preferred_element_type=F32)
        acs_ref[g] = acs
        a_t = dtt_ref[g].reshape(nc * r, CHUNK) * jnp.concatenate([a_col] * nc, axis=0)
        acs_t = jnp.zeros((nc * r, CHUNK), F32)
        for part in _split3(a_t):
            acs_t = acs_t + jnp.dot(part, trit, preferred_element_type=F32)
        acst_ref[g] = acs_t
        a_last = jnp.concatenate(
            [jnp.broadcast_to(acs[c * CHUNK + CHUNK - 1:c * CHUNK + CHUNK, :], (CHUNK, r))
             for c in range(nc)], axis=0)
        xdt = xc_ref[:, gl] * expand(dt)
        xdt_ref[:, gl] = xdt.astype(BF16)
        wst_ref[:, gl] = (xdt * expand(jnp.exp(a_last - acs))).astype(BF16)
        eacs_ref[:, gl] = expand(jnp.exp(acs))

    row_i = lax.broadcasted_iota(jnp.int32, (CHUNK, CHUNK), 0)
    col_i = lax.broadcasted_iota(jnp.int32, (CHUNK, CHUNK), 1)
    tril = col_i <= row_i
    left_half = lax.broadcasted_iota(jnp.int32, (CHUNK, 128), 1) < SSD_HEAD_DIM

    def chunk_body(c, carry):
        rows = pl.ds(pl.multiple_of(c * CHUNK, CHUNK), CHUNK)
        for g in range(ng):
            cc = ccv_ref[rows, g * n:(g + 1) * n]
            bc = bcv_ref[rows, g * n:(g + 1) * n]
            cb = lax.dot_general(cc, bc, (((1,), (1,)), ((), ())), preferred_element_type=F32)
            state = state_ref[g]
            eacs = eacs_ref[rows, g * gw:(g + 1) * gw]
            y_off = jnp.dot(cc, state.astype(BF16), preferred_element_type=F32) * eacs
            acs_c = acs_ref[g, rows, :]
            acs_tc = acst_ref[g, pl.ds(pl.multiple_of(c * r, r), r), :]
            for p in range(r // 2):
                lo = g * gw + p * 128
                xp = xdt_ref[rows, lo:lo + 128]
                halves = []
                for hh in (2 * p, 2 * p + 1):
                    diff = acs_c[:, hh:hh + 1] - acs_tc[hh:hh + 1, :]
                    decay = jnp.exp(jnp.where(tril, diff, -jnp.inf))
                    halves.append(jnp.dot((cb * decay).astype(BF16), xp,
                                          preferred_element_type=F32))
                y_diag = jnp.where(left_half, halves[0], halves[1])
                y_ref[rows, lo:lo + 128] = (y_diag + y_off[:, p * 128:(p + 1) * 128]
                                            + dsk_ref[:, lo:lo + 128] * xc_ref[rows, lo:lo + 128])
            upd = lax.dot_general(bc, wst_ref[rows, g * gw:(g + 1) * gw], (((0,), (0,)), ((), ())),
                                  preferred_element_type=F32)
            state_ref[g] = state * eacs[CHUNK - 1:CHUNK, :] + upd
        return carry

    lax.fori_loop(0, nc, chunk_body, 0)


def _ssd_call(px, conv_w, conv_b, dt, dt_t, a_log, d_skip, batch, seq):
    t = batch * seq
    ts = SSD_TILE
    ns = seq // ts
    nc = ts // CHUNK
    r = SSD_HEADS_PER_GROUP
    ng = SSD_GROUPS_PER_STEP
    gw = ng * SSD_GROUP_WIDTH
    n = ng * SSD_STATE
    wb0 = SSD_D_INNER // n
    wc0 = wb0 + SSD_GROUPS // ng

    li = jnp.arange(ts)
    bd = ((li[None, :] <= li[:, None]) & (li[None, :] // CHUNK == li[:, None] // CHUNK)).astype(BF16)
    lc = jnp.arange(CHUNK)
    trit = (lc[:, None] <= lc[None, :]).astype(BF16)
    e_mat = (jnp.arange(SSD_GROUP_WIDTH)[None, :] // SSD_HEAD_DIM
             == jnp.arange(r)[:, None]).astype(BF16)
    conv_b2 = conv_b.reshape(1, SSD_CONV_DIM)
    alr = a_log.reshape(SSD_GROUPS, 1, r)
    alc = a_log.reshape(SSD_GROUPS, r, 1)
    dsk = jnp.repeat(d_skip, SSD_HEAD_DIM).reshape(1, SSD_D_INNER)

    row = lambda b, g, s: b * ns + s
    in_specs = [
        pl.BlockSpec((ts, gw), lambda b, g, s: (row(b, g, s), g)),
        pl.BlockSpec((ts, n), lambda b, g, s: (row(b, g, s), wb0 + g)),
        pl.BlockSpec((ts, n), lambda b, g, s: (row(b, g, s), wc0 + g)),
        pl.BlockSpec((SSD_CONV, gw), lambda b, g, s: (0, g)),
        pl.BlockSpec((SSD_CONV, n), lambda b, g, s: (0, wb0 + g)),
        pl.BlockSpec((SSD_CONV, n), lambda b, g, s: (0, wc0 + g)),
        pl.BlockSpec((1, gw), lambda b, g, s: (0, g)),
        pl.BlockSpec((1, n), lambda b, g, s: (0, wb0 + g)),
        pl.BlockSpec((1, n), lambda b, g, s: (0, wc0 + g)),
        pl.BlockSpec((ng, ts, r), lambda b, g, s: (g, row(b, g, s), 0)),
        pl.BlockSpec((ng, nc, r, CHUNK), lambda b, g, s: (g, row(b, g, s), 0, 0)),
        pl.BlockSpec((ng, 1, r), lambda b, g, s: (g, 0, 0)),
        pl.BlockSpec((ng, r, 1), lambda b, g, s: (g, 0, 0)),
        pl.BlockSpec((1, gw), lambda b, g, s: (0, g)),
        pl.BlockSpec((ts, ts), lambda b, g, s: (0, 0)),
        pl.BlockSpec((CHUNK, CHUNK), lambda b, g, s: (0, 0)),
        pl.BlockSpec((r, SSD_GROUP_WIDTH), lambda b, g, s: (0, 0)),
    ]
    scratch = [
        pltpu.VMEM((ng, SSD_STATE, SSD_GROUP_WIDTH), F32),
        pltpu.VMEM((ts + 8, gw), F32),
        pltpu.VMEM((ts + 8, n), F32),
        pltpu.VMEM((ts + 8, n), F32),
        pltpu.VMEM((ts, gw), F32),
        pltpu.VMEM((ts, n), BF16),
        pltpu.VMEM((ts, n), BF16),
        pltpu.VMEM((ts, gw), BF16),
        pltpu.VMEM((ts, gw), BF16),
        pltpu.VMEM((ts, gw), F32),
        pltpu.VMEM((ng, ts, r), F32),
        pltpu.VMEM((ng, nc * r, CHUNK), F32),
    ]
    return pl.pallas_call(
        _ssd_kernel,
        grid=(batch, SSD_GROUPS // ng, ns),
        in_specs=in_specs,
        out_specs=pl.BlockSpec((ts, gw), lambda b, g, s: (row(b, g, s), g)),
        out_shape=jax.ShapeDtypeStruct((t, SSD_D_INNER), F32),
        scratch_shapes=scratch,
        compiler_params=_params(("parallel", "parallel", "arbitrary")),
        name="ssd_scan",
    )(px, px, px, conv_w, conv_w, conv_w, conv_b2, conv_b2, conv_b2,
      dt, dt_t, alr, alc, dsk, bd, trit, e_mat)


def _attn_kernel(q_ref, k_ref, vt_ref, bias_ref, far_ref, lam_ref, g_ref, o_ref, sa_ref, sb_ref,
                 *, lam_init):
    i = pl.program_id(2)
    tq = q_ref.shape[0]
    dh = DIFF_HEAD_DIM
    lv = lam_ref[...]
    lam = (jnp.exp(jnp.sum(lv[0:1] * lv[1:2], axis=1, keepdims=True))
           - jnp.exp(jnp.sum(lv[2:3] * lv[3:4], axis=1, keepdims=True)) + lam_init)

    q = q_ref[...]
    qs = (q[:, :dh], q[:, dh:])
    nt = (((1,), (1,)), ((), ()))

    n_tiles = bias_ref.shape[1] - 1
    h = pl.program_id(1)
    ones = jnp.ones((16, tq), BF16)
    dv = 2 * dh

    def near_tile(t):
        valid = t <= i
        return jnp.where(valid, i - t, 0), jnp.where(valid, t, n_tiles)

    def far_tile(t):
        valid = t <= i
        j = jnp.where(valid, t - 2, 0)
        shift = jnp.where(valid, far_ref[h, jnp.where(valid, i - j, 0)], NEG_BIG)
        return j, shift

    def scores(j, s_ref, bias):
        kj = k_ref[pl.ds(pl.multiple_of(j * tq, tq), tq), :]
        for m in range(2):
            s = lax.dot_general(kj[:, m * dh:(m + 1) * dh], qs[m], nt,
                                preferred_element_type=F32)
            s_ref[m] = s if bias is None else s + bias

    def update(j, s_ref, shift, carry):
        vtj = jnp.concatenate([vt_ref[j], ones], axis=0)
        out = []
        for m in range(2):
            mx, acc = carry[2 * m:2 * m + 2]
            s = s_ref[m]
            mx_new = jnp.maximum(mx, jnp.max(s, axis=0, keepdims=True) + shift)
            p = jnp.exp2(s - (mx_new - shift))
            out += [mx_new, jnp.exp2(mx - mx_new) * acc
                    + jnp.dot(vtj, p.astype(BF16), preferred_element_type=F32)]
        return tuple(out)

    def scores_near(t, s_ref):
        j, d = near_tile(t)
        scores(j, s_ref, bias_ref[0, d])

    def scores_far(t, s_ref):
        scores(far_tile(t)[0], s_ref, None)

    def update_near(t, s_ref, carry):
        return update(near_tile(t)[0], s_ref, 0.0, carry)

    def update_far(t, s_ref, carry):
        j, shift = far_tile(t)
        return update(j, s_ref, shift, carry)

    def body(u, carry):
        scores_far(2 * u + 1, sb_ref)
        carry = update_far(2 * u, sa_ref, carry)
        scores_far(2 * u + 2, sa_ref)
        return update_far(2 * u + 1, sb_ref, carry)

    carry = []
    for _ in range(2):
        carry += [jnp.full((1, tq), NEG_BIG, F32), jnp.zeros((dv + 16, tq), F32)]
    scores_near(0, sa_ref)
    scores_near(1, sb_ref)
    carry = update_near(0, sa_ref, tuple(carry))
    scores_far(2, sa_ref)
    carry = update_near(1, sb_ref, carry)
    _, acc1, _, acc2 = lax.fori_loop(1, (i + 2) // 2, body, carry)
    o = acc1[:dv] / acc1[dv:dv + 1] - lam * (acc2[:dv] / acc2[dv:dv + 1])
    ms = jnp.mean(o * o, axis=0, keepdims=True)
    o = o * lax.rsqrt(ms + RMS_EPS) * g_ref[...] * (1.0 - lam_init)
    o_ref[...] = o.T.astype(o_ref.dtype)


def _rel_bucket(rel):
    half = REL_BUCKETS // 2
    max_exact = half // 2
    ret = jnp.where(rel > 0, half, 0)
    n = jnp.abs(rel)
    nf = jnp.maximum(n, 1).astype(F32)
    large = max_exact + (jnp.log(nf / max_exact) / math.log(REL_MAX_DIST / max_exact)
                         * (half - max_exact)).astype(jnp.int32)
    large = jnp.minimum(large, half - 1)
    return ret + jnp.where(n < max_exact, n, large)


def _bias_tiles(rel_bias, seq):
    tq = ATTN_TILE
    nd = seq // tq
    kk = jnp.arange(tq)[None, :, None]
    qq = jnp.arange(tq)[None, None, :]
    d = jnp.arange(nd)[:, None, None]
    rel = kk - qq - d * tq
    bucket = _rel_bucket(rel)[None]
    table = rel_bias.astype(F32)
    bias = jnp.zeros((DIFF_HEADS, nd, tq, tq), F32)
    for b in range(REL_BUCKETS):
        bias = jnp.where(bucket == b, table[b][:, None, None, None], bias)
    allowed = (d > 0) | ((kk // CHUNK) <= (qq // CHUNK))
    bias = jnp.where(allowed[None], bias * LOG2_E, NEG_BIG)
    masked = jnp.full((DIFF_HEADS, 1, tq, tq), NEG_BIG, F32)
    return jnp.concatenate([bias, masked], axis=1)


def _attn_call(pb, vt, bias_tiles, lam_vecs, norm_g, layer_idx, batch, seq):
    t = batch * seq
    tq = ATTN_TILE
    nq = seq // tq
    w = 2 * DIFF_HEAD_DIM
    qcol0 = PB_Q // w
    kcol0 = PB_K // w
    lam_init = 0.8 - 0.6 * math.exp(-0.3 * layer_idx)
    return pl.pallas_call(
        functools.partial(_attn_kernel, lam_init=lam_init),
        grid=(batch, DIFF_HEADS, nq),
        in_specs=[pl.BlockSpec((tq, w), lambda b, h, i: (b * nq + i, qcol0 + h)),
                  pl.BlockSpec((seq, w), lambda b, h, i: (b, kcol0 + h)),
                  pl.BlockSpec((nq, w, tq), lambda b, h, i: (b, h, 0)),
                  pl.BlockSpec((1, nq + 1, tq, tq), lambda b, h, i: (h, 0, 0, 0)),
                  pl.BlockSpec(memory_space=pltpu.SMEM),
                  pl.BlockSpec((4, DIFF_HEAD_DIM), lambda b, h, i: (0, 0)),
                  pl.BlockSpec((w, 1), lambda b, h, i: (0, 0))],
        out_specs=pl.BlockSpec((tq, w), lambda b, h, i: (b * nq + i, h)),
        out_shape=jax.ShapeDtypeStruct((t, DIFF_WIDTH), BF16),
        scratch_shapes=[pltpu.VMEM((2, tq, tq), F32), pltpu.VMEM((2, tq, tq), F32)],
        compiler_params=_params(("parallel", "parallel", "arbitrary")),
        name="diff_attn",
    )(pb, pb, vt, bias_tiles, bias_tiles[:, :, 0, 0], lam_vecs, norm_g.reshape(w, 1))


def _layer_norm(x, g, b):
    mu = jnp.mean(x, axis=1, keepdims=True)
    xc = x - mu
    var = jnp.mean(xc * xc, axis=1, keepdims=True)
    return xc * lax.rsqrt(var + LN_EPS) * g + b


def _mix_kernel(y_ref, z_ref, ao_ref, g0_ref, g1_ref, h_ref, ng_ref, wso_ref, wao_ref,
                gb_ref, wo_ref, lg_ref, lb_ref, o_ref):
    z = z_ref[...].astype(F32)
    yg = y_ref[...] * (z * _sigmoid(z))
    ms = jnp.mean(yg * yg, axis=1, keepdims=True)
    yn = (yg * lax.rsqrt(ms + RMS_EPS) * ng_ref[...]).astype(BF16)
    y_ssd = jnp.dot(yn, wso_ref[...], preferred_element_type=F32)
    y_att = jnp.dot(ao_ref[...], wao_ref[...], preferred_element_type=F32)
    gb = gb_ref[...]
    gate0 = _sigmoid(g0_ref[...].astype(F32) + gb[:, :D_MODEL])
    gate1 = _sigmoid(g1_ref[...].astype(F32) + gb[:, D_MODEL:])
    mixed = (gate0 * y_ssd + gate1 * y_att).astype(BF16)
    mix = jnp.dot(mixed, wo_ref[...], preferred_element_type=F32)
    o_ref[...] = _layer_norm(DN_ALPHA * h_ref[...] + mix, lg_ref[...], lb_ref[...])


def _mix_call(y, pb, ao, h, norm_g, w_ssd_out, w_attn_out, gate_b, w_o, ln_g, ln_b, tm=256):
    t = h.shape[0]
    d = D_MODEL
    gcol0 = PB_GATES // d
    const = lambda i: (0, 0)
    return pl.pallas_call(
        _mix_kernel,
        grid=(t // tm,),
        in_specs=[pl.BlockSpec((tm, SSD_D_INNER), lambda i: (i, 0)),
                  pl.BlockSpec((tm, SSD_D_INNER), lambda i: (i, 0)),
                  pl.BlockSpec((tm, DIFF_WIDTH), lambda i: (i, 0)),
                  pl.BlockSpec((tm, d), lambda i: (i, gcol0)),
                  pl.BlockSpec((tm, d), lambda i: (i, gcol0 + 1)),
                  pl.BlockSpec((tm, d), lambda i: (i, 0)),
                  pl.BlockSpec((1, SSD_D_INNER), const),
                  pl.BlockSpec((SSD_D_INNER, d), const),
                  pl.BlockSpec((DIFF_WIDTH, d), const),
                  pl.BlockSpec((1, 2 * d), const),
                  pl.BlockSpec((d, d), const),
                  pl.BlockSpec((1, d), const),
                  pl.BlockSpec((1, d), const)],
        out_specs=pl.BlockSpec((tm, d), lambda i: (i, 0)),
        out_shape=jax.ShapeDtypeStruct((t, d), F32),
        compiler_params=_params(("parallel",)),
        name="mix_ln",
    )(y, pb, ao, pb, pb, h, norm_g.reshape(1, -1), w_ssd_out.astype(BF16),
      w_attn_out.astype(BF16), gate_b.reshape(1, -1), w_o.astype(BF16),
      ln_g.reshape(1, -1), ln_b.reshape(1, -1))


def _router_kernel(h_ref, w_ref, b_ref, tri_ref, idx_ref, wt_ref, rank_ref, cnt_ref, run_ref):
    @pl.when(pl.program_id(0) == 0)
    def _():
        run_ref[...] = jnp.zeros_like(run_ref)

    tm = h_ref.shape[0]
    ne = N_EXPERTS
    logits = _dot3(h_ref[...], w_ref[...], (((1,), (0,)), ((), ()))) + b_ref[...]
    lane = lax.broadcasted_iota(jnp.int32, (tm, ne), 1).astype(F32)
    work = logits
    sel, vals = [], []
    for _ in range(TOP_K):
        mx = jnp.max(work, axis=1, keepdims=True)
        first = jnp.min(jnp.where(work == mx, lane, float(ne)), axis=1, keepdims=True)
        hit = lane == first
        sel.append((first, hit))
        vals.append(mx)
        work = jnp.where(hit, -jnp.inf, work)
    exps = [jnp.exp(v - vals[0]) for v in vals]
    denom = exps[0] + exps[1] + exps[2] + exps[3]

    onehot = jnp.zeros((tm, ne), F32)
    for _, hit in sel:
        onehot = onehot + hit.astype(F32)
    before = jnp.dot(tri_ref[...], onehot.astype(BF16), preferred_element_type=F32)
    before = before + run_ref[...]

    out_lane = lax.broadcasted_iota(jnp.int32, (tm, 128), 1)
    idx_out = jnp.zeros((tm, 128), jnp.int32)
    wt_out = jnp.zeros((tm, 128), F32)
    rank_out = jnp.zeros((tm, 128), jnp.int32)
    for k, (first, hit) in enumerate(sel):
        rank = jnp.sum(jnp.where(hit, before, 0.0), axis=1, keepdims=True)
        idx_out = jnp.where(out_lane == k, first.astype(jnp.int32), idx_out)
        wt_out = jnp.where(out_lane == k, exps[k] / denom, wt_out)
        rank_out = jnp.where(out_lane == k, rank.astype(jnp.int32), rank_out)
    idx_ref[...] = idx_out
    wt_ref[...] = wt_out
    rank_ref[...] = rank_out
    total = run_ref[...] + jnp.sum(onehot, axis=0, keepdims=True)
    run_ref[...] = total
    cnt_ref[...] = total


def _router_call(h, w_router, b_router, tm=512):
    t = h.shape[0]
    li = jnp.arange(tm)
    tri = (li[None, :] < li[:, None]).astype(BF16)
    const = lambda i: (0, 0)
    return pl.pallas_call(
        _router_kernel,
        grid=(t // tm,),
        in_specs=[pl.BlockSpec((tm, D_MODEL), lambda i: (i, 0)),
                  pl.BlockSpec((D_MODEL, N_EXPERTS), const),
                  pl.BlockSpec((1, N_EXPERTS), const),
                  pl.BlockSpec((tm, tm), const)],
        out_specs=[pl.BlockSpec((tm, 128), lambda i: (i, 0)),
                   pl.BlockSpec((tm, 128), lambda i: (i, 0)),
                   pl.BlockSpec((tm, 128), lambda i: (i, 0)),
                   pl.BlockSpec((1, N_EXPERTS), const)],
        out_shape=[jax.ShapeDtypeStruct((t, 128), jnp.int32),
                   jax.ShapeDtypeStruct((t, 128), F32),
                   jax.ShapeDtypeStruct((t, 128), jnp.int32),
                   jax.ShapeDtypeStruct((1, N_EXPERTS), F32)],
        scratch_shapes=[pltpu.VMEM((1, N_EXPERTS), F32)],
        compiler_params=_params(("arbitrary",)),
        name="router",
    )(h, w_router, b_router.reshape(1, N_EXPERTS), tri)


def _dispatch_kernel(slot_ref, x_ref, xs_in, xs_out, sem):
    del xs_in
    groups = x_ref.shape[0]
    n = groups * 8 * TOP_K

    def start(r, c):
        for sub in range(8):
            for k in range(TOP_K):
                slot = slot_ref[0, 0, (r * 8 + sub) * TOP_K + k]
                pltpu.make_async_copy(x_ref.at[r, pl.ds(sub, 1)],
                                      xs_out.at[pl.ds(slot, 1)], sem).start()
        return c

    lax.fori_loop(0, groups, start, 0)
    pltpu.make_async_copy(xs_out.at[pl.ds(0, n)], xs_out.at[pl.ds(0, n)], sem).wait()


def _dispatch_call(x, slots, xs_init):
    t, d = x.shape
    cap = xs_init.shape[0]
    tt = DISPATCH_TILE
    nt = t // tt
    n = tt * TOP_K
    return pl.pallas_call(
        _dispatch_kernel,
        grid=(nt,),
        in_specs=[pl.BlockSpec((1, 1, n), lambda i: (i, 0, 0), memory_space=pltpu.SMEM),
                  pl.BlockSpec((tt // 8, 8, d), lambda i: (i, 0, 0)),
                  pl.BlockSpec(memory_space=pl.ANY)],
        out_specs=pl.BlockSpec(memory_space=pl.ANY),
        out_shape=jax.ShapeDtypeStruct((cap, d), x.dtype),
        scratch_shapes=[pltpu.SemaphoreType.DMA],
        input_output_aliases={2: 0},
        compiler_params=_params(("arbitrary",), disable_bounds_checks=True),
        name="moe_dispatch",
    )(slots.reshape(nt, 1, n), x.reshape(t // 8, 8, d), xs_init)


def _expert_kernel(be_ref, nb_ref, x_ref, wg_ref, bg_ref, wu_ref, bu_ref, wd_ref, bd_ref,
                   o_ref, wgb_ref, wub_ref, wdb_ref):
    i = pl.program_id(0)
    prev = be_ref[jnp.maximum(i - 1, 0)]
    changed = jnp.logical_or(i == 0, be_ref[i] != prev)

    @pl.when(changed)
    def _():
        wgb_ref[...] = wg_ref[0].astype(BF16)
        wub_ref[...] = wu_ref[0].astype(BF16)
        wdb_ref[...] = wd_ref[0].astype(BF16)

    @pl.when(i < nb_ref[0])
    def _():
        xb = x_ref[...].astype(BF16)
        g = jnp.dot(xb, wgb_ref[...], preferred_element_type=F32) + bg_ref[0]
        u = jnp.dot(xb, wub_ref[...], preferred_element_type=F32) + bu_ref[0]
        g = jnp.minimum(g, SWIGLU_LIMIT)
        u = jnp.clip(u, -SWIGLU_LIMIT, SWIGLU_LIMIT)
        act = g * _sigmoid(SWIGLU_ALPHA * g) * (u + 1.0)
        o_ref[...] = jnp.dot(act.astype(BF16), wdb_ref[...],
                             preferred_element_type=F32) + bd_ref[0]

    @pl.when(i >= nb_ref[0])
    def _():
        o_ref[...] = jnp.zeros_like(o_ref)


def _expert_call(xs, blk_expert, n_used, w_gate, b_gate, w_up, b_up, w_down, b_down):
    cap, d = xs.shape
    bm = EXPERT_BLOCK
    nb = cap // bm
    wspec = lambda shape: pl.BlockSpec(shape, lambda i, be, nu: (be[i], 0, 0))
    grid_spec = pltpu.PrefetchScalarGridSpec(
        num_scalar_prefetch=2,
        grid=(nb,),
        in_specs=[pl.BlockSpec((bm, d), lambda i, be, nu: (i, 0)),
                  wspec((1, d, D_FF)), wspec((1, 1, D_FF)),
                  wspec((1, d, D_FF)), wspec((1, 1, D_FF)),
                  wspec((1, D_FF, d)), wspec((1, 1, d))],
        out_specs=pl.BlockSpec((bm, d), lambda i, be, nu: (i, 0)),
        scratch_shapes=[pltpu.VMEM((d, D_FF), BF16), pltpu.VMEM((d, D_FF), BF16),
                        pltpu.VMEM((D_FF, d), BF16)],
    )
    ne = w_gate.shape[0] * w_gate.shape[1]
    return pl.pallas_call(
        _expert_kernel,
        grid_spec=grid_spec,
        out_shape=jax.ShapeDtypeStruct((cap, d), F32),
        compiler_params=_params(("arbitrary",)),
        name="moe_experts",
    )(blk_expert, n_used, xs, w_gate.reshape(ne, d, D_FF), b_gate.reshape(ne, 1, D_FF),
      w_up.reshape(ne, d, D_FF), b_up.reshape(ne, 1, D_FF), w_down.reshape(ne, D_FF, d),
      b_down.reshape(ne, 1, d))


def _combine_kernel(slot_ref, next_slot_ref, ys_hbm, wt_ref, h_ref, lg_ref, lb_ref, o_ref,
                    buf_ref, sems):
    i = pl.program_id(0)
    tt = h_ref.shape[0]
    n = tt * TOP_K

    def issue(s_ref, buf):
        def start(r, c):
            for sub in range(8):
                pltpu.make_async_copy(ys_hbm.at[pl.ds(s_ref[0, 0, r * 8 + sub], 1)],
                                      buf_ref.at[buf, r, pl.ds(sub, 1)], sems.at[buf]).start()
            return c

        lax.fori_loop(0, n // 8, start, 0)

    @pl.when(i == 0)
    def _():
        issue(slot_ref, 0)

    @pl.when(i + 1 < pl.num_programs(0))
    def _():
        issue(next_slot_ref, (i + 1) % 2)

    cur = i % 2
    pltpu.make_async_copy(buf_ref.at[cur], buf_ref.at[cur], sems.at[cur]).wait()

    wt = wt_ref[...]
    d = h_ref.shape[1]
    rows = lambda k: buf_ref[cur, pl.ds(k * (tt // 8), tt // 8)].reshape(tt, d)
    ff = wt[:, 0:1] * rows(0)
    for k in range(1, TOP_K):
        ff = ff + wt[:, k:k + 1] * rows(k)
    o_ref[...] = _layer_norm(DN_ALPHA * h_ref[...] + ff, lg_ref[...], lb_ref[...])


def _combine_call(ys, slots, wts, h, ln_g, ln_b):
    t, d = h.shape
    tt = COMBINE_TILE
    nt = t // tt
    n = tt * TOP_K
    slots_km = slots.reshape(nt, tt, TOP_K).transpose(0, 2, 1).reshape(nt, 1, n)
    const = lambda i: (0, 0)
    return pl.pallas_call(
        _combine_kernel,
        grid=(nt,),
        in_specs=[pl.BlockSpec((1, 1, n), lambda i: (i, 0, 0), memory_space=pltpu.SMEM),
                  pl.BlockSpec((1, 1, n), lambda i: (jnp.minimum(i + 1, nt - 1), 0, 0),
                               memory_space=pltpu.SMEM),
                  pl.BlockSpec(memory_space=pl.ANY),
                  pl.BlockSpec((tt, 128), lambda i: (i, 0)),
                  pl.BlockSpec((tt, d), lambda i: (i, 0)),
                  pl.BlockSpec((1, d), const),
                  pl.BlockSpec((1, d), const)],
        out_specs=pl.BlockSpec((tt, d), lambda i: (i, 0)),
        out_shape=jax.ShapeDtypeStruct((t, d), F32),
        scratch_shapes=[pltpu.VMEM((2, n // 8, 8, d), F32), pltpu.SemaphoreType.DMA((2,))],
        compiler_params=_params(("arbitrary",), disable_bounds_checks=True),
        name="moe_combine_ln",
    )(slots_km, slots_km, ys, wts, h, ln_g.reshape(1, d), ln_b.reshape(1, d))


def _moe_layout(idx, rank, counts, n_blocks):
    bm = EXPERT_BLOCK
    counts = counts.reshape(N_EXPERTS).astype(jnp.int32)
    padded = (counts + bm - 1) // bm * bm
    pad_end = jnp.cumsum(padded)
    pad_start = pad_end - padded
    onehot = idx[:, :, None] == jnp.arange(N_EXPERTS, dtype=jnp.int32)[None, None, :]
    slots = rank + jnp.sum(jnp.where(onehot, pad_start[None, None, :], 0), axis=-1)
    blk_start = jnp.arange(n_blocks, dtype=jnp.int32) * bm
    blk_expert = jnp.sum((pad_end[None, :] <= blk_start[:, None]).astype(jnp.int32), axis=1)
    blk_expert = jnp.minimum(blk_expert, N_EXPERTS - 1)
    n_used = (pad_end[-1] // bm).reshape(1)
    return slots.astype(jnp.int32), blk_expert, n_used


def kernel(x, rel_bias, w_in, conv_w, conv_b, dt_bias, a_log, d_skip, ssd_norm_g, w_ssd_out, diff_lambda, diff_norm_g, w_attn_out, gate_b, w_o, ln1_g, ln1_b, w_router, b_router, w_gate, b_gate, w_up, b_up, w_down, b_down, ln2_g, ln2_b):
    batch, seq, d = x.shape
    t = batch * seq
    n_assign = t * TOP_K
    n_blocks = (n_assign + N_EXPERTS * (EXPERT_BLOCK - 1) + EXPERT_BLOCK - 1) // EXPERT_BLOCK
    cap = n_blocks * EXPERT_BLOCK
    bias_tiles = _bias_tiles(rel_bias, seq)
    qk_scale = jnp.concatenate([jnp.full((DIFF_WIDTH,), DIFF_HEAD_DIM ** -0.5 * LOG2_E, F32),
                                jnp.ones((DIFF_WIDTH,), F32)])[None, :]

    h = x.reshape(t, d)
    xs = None
    for l in range(DEPTH):
        w_l = w_in[l]
        w_x = w_l[:, OFF_XBC:OFF_DT].astype(BF16)
        w_b = jnp.concatenate([w_l[:, :OFF_XBC], w_l[:, OFF_G:],
                               w_l[:, OFF_Q:OFF_V] * qk_scale], axis=1).astype(BF16)
        w_vt = w_l[:, OFF_V:OFF_G].T.astype(BF16)
        px = _matmul(h, w_x, F32, 1024, 1024)
        pb = _matmul(h, w_b, BF16, 1024, 1024)
        vt = _matmul_nt(h, w_vt, ATTN_TILE)
        dt, dt_t = _dt_call(h, w_l[:, OFF_DT:OFF_Q], dt_bias[l])
        y = _ssd_call(px, conv_w[l], conv_b[l], dt, dt_t, a_log[l], d_skip[l], batch, seq)
        ao = _attn_call(pb, vt, bias_tiles, diff_lambda[l], diff_norm_g[l], l, batch, seq)
        h1 = _mix_call(y, pb, ao, h, ssd_norm_g[l], w_ssd_out[l], w_attn_out[l], gate_b[l],
                       w_o[l], ln1_g[l], ln1_b[l])
        idx, wts, rank, counts = _router_call(h1, w_router[l], b_router[l])
        slots, blk_expert, n_used = _moe_layout(idx[:, :TOP_K], rank[:, :TOP_K], counts, n_blocks)
        xs = _dispatch_call(h1, slots, jnp.zeros((cap, d), F32) if xs is None else xs)
        ys = _expert_call(xs, blk_expert + l * N_EXPERTS, n_used, w_gate, b_gate, w_up, b_up,
                          w_down, b_down)
        h = _combine_call(ys, slots, wts, h1, ln2_g[l], ln2_b[l])
    return h.reshape(batch, seq, d)
```

```python
import functools
import math

import jax
import jax.numpy as jnp
from jax import lax
from jax.experimental import pallas as pl
from jax.experimental.pallas import tpu as pltpu

F32 = jnp.float32
BF16 = jnp.bfloat16

D_MODEL = 1024
DEPTH = 2
CHUNK = 64

SSD_D_INNER = 2048
SSD_HEAD_DIM = 64
SSD_HEADS = 32
SSD_GROUPS = 4
SSD_HEADS_PER_GROUP = 8
SSD_STATE = 128
SSD_CONV = 4
SSD_CONV_DIM = 3072
SSD_GROUP_WIDTH = SSD_HEADS_PER_GROUP * SSD_HEAD_DIM

DIFF_HEAD_DIM = 64
DIFF_HEADS = 8
DIFF_WIDTH = 1024

REL_BUCKETS = 32
REL_MAX_DIST = 128

N_EXPERTS = 32
TOP_K = 4
D_FF = 1024
SWIGLU_ALPHA = 1.702
SWIGLU_LIMIT = 7.0

DN_ALPHA = (2 * DEPTH) ** 0.25
LN_EPS = 1e-5
RMS_EPS = 1e-5

OFF_XBC = 2048
OFF_DT = 5120
OFF_Q = 5152
OFF_V = 7200
OFF_G = 8224
IN_COLS = 10272
PB_GATES = 2048
PB_Q = 4096
PB_K = 5120

VMEM_LIMIT_BYTES = 56 * 1024 * 1024

ATTN_TILE = 256
ATTN_HEADS_PER_STEP = 2
SSD_TILE = 256
SSD_GROUPS_PER_STEP = 4
EXPERT_BLOCK = 256
DISPATCH_TILE = 512
COMBINE_TILE = 256
NEG_BIG = -1e30
LOG2_E = math.log2(math.e)


def _params(semantics, **kwargs):
    return pltpu.CompilerParams(dimension_semantics=semantics,
                                vmem_limit_bytes=VMEM_LIMIT_BYTES, **kwargs)


def _sigmoid(x):
    return 1.0 / (1.0 + jnp.exp(-x))


def _split2(v):
    hi = v.astype(BF16)
    lo = (v - hi.astype(F32)).astype(BF16)
    return hi, lo


def _dot3(a, b, dims):
    ah, al = _split2(a)
    bh, bl = _split2(b)
    dot = lambda p, q: lax.dot_general(p, q, dims, preferred_element_type=F32)
    return dot(ah, bh) + dot(ah, bl) + dot(al, bh)


def _split3(v):
    hi = v.astype(BF16)
    r = v - hi.astype(F32)
    mid = r.astype(BF16)
    lo = (r - mid.astype(F32)).astype(BF16)
    return hi, mid, lo


def _mm_kernel(x_ref, w_ref, o_ref, xb_ref):
    @pl.when(pl.program_id(1) == 0)
    def _():
        xb_ref[...] = x_ref[...].astype(BF16)

    o_ref[...] = jnp.dot(xb_ref[...], w_ref[...],
                         preferred_element_type=F32).astype(o_ref.dtype)


def _matmul(x, w, out_dtype, bm, bn):
    m, k = x.shape
    n = w.shape[1]
    return pl.pallas_call(
        _mm_kernel,
        grid=(m // bm, n // bn),
        in_specs=[pl.BlockSpec((bm, k), lambda i, j: (i, 0)),
                  pl.BlockSpec((k, bn), lambda i, j: (0, j))],
        out_specs=pl.BlockSpec((bm, bn), lambda i, j: (i, j)),
        out_shape=jax.ShapeDtypeStruct((m, n), out_dtype),
        scratch_shapes=[pltpu.VMEM((bm, k), BF16)],
        compiler_params=_params(("parallel", "arbitrary")),
        name="in_proj",
    )(x, w)


def _mm_nt_kernel(x_ref, wt_ref, o_ref):
    res = lax.dot_general(wt_ref[...], x_ref[...].astype(BF16), (((1,), (1,)), ((), ())),
                          preferred_element_type=F32)
    tile = o_ref.shape[2]
    for c in range(o_ref.shape[0]):
        o_ref[c] = res[:, c * tile:(c + 1) * tile].astype(o_ref.dtype)


def _matmul_nt(x, w_t, tile, bm=1024):
    m, k = x.shape
    n = w_t.shape[0]
    return pl.pallas_call(
        _mm_nt_kernel,
        grid=(m // bm,),
        in_specs=[pl.BlockSpec((bm, k), lambda i: (i, 0)),
                  pl.BlockSpec((n, k), lambda i: (0, 0))],
        out_specs=pl.BlockSpec((bm // tile, n, tile), lambda i: (i, 0, 0)),
        out_shape=jax.ShapeDtypeStruct((m // tile, n, tile), BF16),
        compiler_params=_params(("parallel",)),
        name="v_proj_t",
    )(x, w_t)


def _softplus(x):
    return jnp.maximum(x, 0.0) + jnp.log(1.0 + jnp.exp(-jnp.abs(x)))


def _dt_kernel(x_ref, w_ref, wt_ref, b_ref, bt_ref, dt_ref, dtt_ref):
    x = x_ref[...]
    tm = x.shape[0]
    raw = _dot3(x, w_ref[...], (((1,), (0,)), ((), ())))
    dt = _softplus(raw + b_ref[...])
    raw_t = _dot3(wt_ref[...], x, (((1,), (1,)), ((), ())))
    dt_t = _softplus(raw_t + bt_ref[...])
    r = SSD_HEADS_PER_GROUP
    for g in range(SSD_GROUPS):
        dt_ref[g] = dt[:, g * r:(g + 1) * r]
        for j in range(tm // CHUNK):
            dtt_ref[g, j] = dt_t[g * r:(g + 1) * r, j * CHUNK:(j + 1) * CHUNK]


def _dt_call(h, w_dt, dt_bias, tm=512):
    t = h.shape[0]
    r = SSD_HEADS_PER_GROUP
    return pl.pallas_call(
        _dt_kernel,
        grid=(t // tm,),
        in_specs=[pl.BlockSpec((tm, D_MODEL), lambda i: (i, 0)),
                  pl.BlockSpec((D_MODEL, SSD_HEADS), lambda i: (0, 0)),
                  pl.BlockSpec((SSD_HEADS, D_MODEL), lambda i: (0, 0)),
                  pl.BlockSpec((1, SSD_HEADS), lambda i: (0, 0)),
                  pl.BlockSpec((SSD_HEADS, 1), lambda i: (0, 0))],
        out_specs=[pl.BlockSpec((SSD_GROUPS, tm, r), lambda i: (0, i, 0)),
                   pl.BlockSpec((SSD_GROUPS, tm // CHUNK, r, CHUNK), lambda i: (0, i, 0, 0))],
        out_shape=[jax.ShapeDtypeStruct((SSD_GROUPS, t, r), F32),
                   jax.ShapeDtypeStruct((SSD_GROUPS, t // CHUNK, r, CHUNK), F32)],
        compiler_params=_params(("parallel",)),
        name="dt_proj",
    )(h, w_dt, w_dt.T, dt_bias.reshape(1, SSD_HEADS), dt_bias.reshape(SSD_HEADS, 1))


def _ssd_kernel(x_ref, b_ref, c_ref, wx_ref, wb_ref, wc_ref, bx_ref, bb_ref, bc_ref,
                dt_ref, dtt_ref, alr_ref, alc_ref, dsk_ref, bd_ref, trit_ref, e_ref,
                y_ref,
                state_ref, ux_ref, ub_ref, uc_ref, xc_ref, bcv_ref, ccv_ref, xdt_ref,
                wst_ref, eacs_ref, acs_ref, acst_ref):
    s_idx = pl.program_id(2)
    ts = x_ref.shape[0]
    nc = ts // CHUNK
    r = SSD_HEADS_PER_GROUP
    gw = SSD_GROUP_WIDTH
    n = SSD_STATE
    ng = dt_ref.shape[0]

    @pl.when(s_idx == 0)
    def _():
        state_ref[...] = jnp.zeros_like(state_ref)
        ux_ref[0:8, :] = jnp.zeros((8, ux_ref.shape[1]), F32)
        ub_ref[0:8, :] = jnp.zeros((8, ub_ref.shape[1]), F32)
        uc_ref[0:8, :] = jnp.zeros((8, uc_ref.shape[1]), F32)

    def conv_silu(raw_ref, u_ref, w_ref, bias_ref):
        u_ref[8:8 + ts, :] = raw_ref[...]
        acc = bias_ref[...] + w_ref[0:1, :] * u_ref[5:5 + ts, :]
        for k in range(1, SSD_CONV):
            acc = acc + w_ref[k:k + 1, :] * u_ref[5 + k:5 + k + ts, :]
        u_ref[0:8, :] = u_ref[ts:ts + 8, :]
        return acc * _sigmoid(acc)

    xc_ref[...] = conv_silu(x_ref, ux_ref, wx_ref, bx_ref)
    bcv_ref[...] = conv_silu(b_ref, ub_ref, wb_ref, bb_ref).astype(BF16)
    ccv_ref[...] = conv_silu(c_ref, uc_ref, wc_ref, bc_ref).astype(BF16)

    bd = bd_ref[...]
    trit = trit_ref[...]
    e_mat = e_ref[...]

    def expand(v):
        hi, lo = _split2(v)
        return (jnp.dot(hi, e_mat, preferred_element_type=F32)
                + jnp.dot(lo, e_mat, preferred_element_type=F32))

    for g in range(ng):
        gl = slice(g * gw, (g + 1) * gw)
        a_row = -jnp.exp(alr_ref[g])
        a_col = -jnp.exp(alc_ref[g])
        dt = dt_ref[g]
        a = dt * a_row
        acs = jnp.zeros((ts, r), F32)
        for part in _split3(a):
            acs = acs + jnp.dot(bd, part, preferred_element_type=F32)
        acs_ref[g] = acs
        a_t = dtt_ref[g].reshape(nc * r, CHUNK) * jnp.concatenate([a_col] * nc, axis=0)
        acs_t = jnp.zeros((nc * r, CHUNK), F32)
        for part in _split3(a_t):
            acs_t = acs_t + jnp.dot(part, trit, preferred_element_type=F32)
        acst_ref[g] = acs_t
        a_last = jnp.concatenate(
            [jnp.broadcast_to(acs[c * CHUNK + CHUNK - 1:c * CHUNK + CHUNK, :], (CHUNK, r))
             for c in range(nc)], axis=0)
        xdt = xc_ref[:, gl] * expand(dt)
        xdt_ref[:, gl] = xdt.astype(BF16)
        wst_ref[:, gl] = (xdt * expand(jnp.exp(a_last - acs))).astype(BF16)
        eacs_ref[:, gl] = expand(jnp.exp(acs))

    row_i = lax.broadcasted_iota(jnp.int32, (CHUNK, CHUNK), 0)
    col_i = lax.broadcasted_iota(jnp.int32, (CHUNK, CHUNK), 1)
    tril = col_i <= row_i
    left_half = lax.broadcasted_iota(jnp.int32, (CHUNK, 128), 1) < SSD_HEAD_DIM

    def chunk_body(c, carry):
        rows = pl.ds(pl.multiple_of(c * CHUNK, CHUNK), CHUNK)
        for g in range(ng):
            cc = ccv_ref[rows, g * n:(g + 1) * n]
            bc = bcv_ref[rows, g * n:(g + 1) * n]
            cb = lax.dot_general(cc, bc, (((1,), (1,)), ((), ())), preferred_element_type=F32)
            state = state_ref[g]
            eacs = eacs_ref[rows, g * gw:(g + 1) * gw]
            y_off = jnp.dot(cc, state.astype(BF16), preferred_element_type=F32) * eacs
            acs_c = acs_ref[g, rows, :]
            acs_tc = acst_ref[g, pl.ds(pl.multiple_of(c * r, r), r), :]
            for p in range(r // 2):
                lo = g * gw + p * 128
                xp = xdt_ref[rows, lo:lo + 128]
                halves = []
                for hh in (2 * p, 2 * p + 1):
                    diff = acs_c[:, hh:hh + 1] - acs_tc[hh:hh + 1, :]
                    decay = jnp.exp(jnp.where(tril, diff, -jnp.inf))
                    halves.append(jnp.dot((cb * decay).astype(BF16), xp,
                                          preferred_element_type=F32))
                y_diag = jnp.where(left_half, halves[0], halves[1])
                y_ref[rows, lo:lo + 128] = (y_diag + y_off[:, p * 128:(p + 1) * 128]
                                            + dsk_ref[:, lo:lo + 128] * xc_ref[rows, lo:lo + 128])
            upd = lax.dot_general(bc, wst_ref[rows, g * gw:(g + 1) * gw], (((0,), (0,)), ((), ())),
                                  preferred_element_type=F32)
            state_ref[g] = state * eacs[CHUNK - 1:CHUNK, :] + upd
        return carry

    lax.fori_loop(0, nc, chunk_body, 0)


def _ssd_call(px, conv_w, conv_b, dt, dt_t, a_log, d_skip, batch, seq):
    t = batch * seq
    ts = SSD_TILE
    ns = seq // ts
    nc = ts // CHUNK
    r = SSD_HEADS_PER_GROUP
    ng = SSD_GROUPS_PER_STEP
    gw = ng * SSD_GROUP_WIDTH
    n = ng * SSD_STATE
    wb0 = SSD_D_INNER // n
    wc0 = wb0 + SSD_GROUPS // ng

    li = jnp.arange(ts)
    bd = ((li[None, :] <= li[:, None]) & (li[None, :] // CHUNK == li[:, None] // CHUNK)).astype(BF16)
    lc = jnp.arange(CHUNK)
    trit = (lc[:, None] <= lc[None, :]).astype(BF16)
    e_mat = (jnp.arange(SSD_GROUP_WIDTH)[None, :] // SSD_HEAD_DIM
             == jnp.arange(r)[:, None]).astype(BF16)
    conv_b2 = conv_b.reshape(1, SSD_CONV_DIM)
    alr = a_log.reshape(SSD_GROUPS, 1, r)
    alc = a_log.reshape(SSD_GROUPS, r, 1)
    dsk = jnp.repeat(d_skip, SSD_HEAD_DIM).reshape(1, SSD_D_INNER)

    row = lambda b, g, s: b * ns + s
    in_specs = [
        pl.BlockSpec((ts, gw), lambda b, g, s: (row(b, g, s), g)),
        pl.BlockSpec((ts, n), lambda b, g, s: (row(b, g, s), wb0 + g)),
        pl.BlockSpec((ts, n), lambda b, g, s: (row(b, g, s), wc0 + g)),
        pl.BlockSpec((SSD_CONV, gw), lambda b, g, s: (0, g)),
        pl.BlockSpec((SSD_CONV, n), lambda b, g, s: (0, wb0 + g)),
        pl.BlockSpec((SSD_CONV, n), lambda b, g, s: (0, wc0 + g)),
        pl.BlockSpec((1, gw), lambda b, g, s: (0, g)),
        pl.BlockSpec((1, n), lambda b, g, s: (0, wb0 + g)),
        pl.BlockSpec((1, n), lambda b, g, s: (0, wc0 + g)),
        pl.BlockSpec((ng, ts, r), lambda b, g, s: (g, row(b, g, s), 0)),
        pl.BlockSpec((ng, nc, r, CHUNK), lambda b, g, s: (g, row(b, g, s), 0, 0)),
        pl.BlockSpec((ng, 1, r), lambda b, g, s: (g, 0, 0)),
        pl.BlockSpec((ng, r, 1), lambda b, g, s: (g, 0, 0)),
        pl.BlockSpec((1, gw), lambda b, g, s: (0, g)),
        pl.BlockSpec((ts, ts), lambda b, g, s: (0, 0)),
        pl.BlockSpec((CHUNK, CHUNK), lambda b, g, s: (0, 0)),
        pl.BlockSpec((r, SSD_GROUP_WIDTH), lambda b, g, s: (0, 0)),
    ]
    scratch = [
        pltpu.VMEM((ng, SSD_STATE, SSD_GROUP_WIDTH), F32),
        pltpu.VMEM((ts + 8, gw), F32),
        pltpu.VMEM((ts + 8, n), F32),
        pltpu.VMEM((ts + 8, n), F32),
        pltpu.VMEM((ts, gw), F32),
        pltpu.VMEM((ts, n), BF16),
        pltpu.VMEM((ts, n), BF16),
        pltpu.VMEM((ts, gw), BF16),
        pltpu.VMEM((ts, gw), BF16),
        pltpu.VMEM((ts, gw), F32),
        pltpu.VMEM((ng, ts, r), F32),
        pltpu.VMEM((ng, nc * r, CHUNK), F32),
    ]
    return pl.pallas_call(
        _ssd_kernel,
        grid=(batch, SSD_GROUPS // ng, ns),
        in_specs=in_specs,
        out_specs=pl.BlockSpec((ts, gw), lambda b, g, s: (row(b, g, s), g)),
        out_shape=jax.ShapeDtypeStruct((t, SSD_D_INNER), F32),
        scratch_shapes=scratch,
        compiler_params=_params(("parallel", "parallel", "arbitrary")),
        name="ssd_scan",
    )(px, px, px, conv_w, conv_w, conv_w, conv_b2, conv_b2, conv_b2,
      dt, dt_t, alr, alc, dsk, bd, trit, e_mat)


def _attn_kernel(q_ref, k_ref, vt_ref, bias_ref, far_ref, lam_ref, g_ref, o_ref, sa_ref, sb_ref,
                 *, lam_init):
    i = pl.program_id(2)
    tq = q_ref.shape[0]
    dh = DIFF_HEAD_DIM
    lv = lam_ref[...]
    lam = (jnp.exp(jnp.sum(lv[0:1] * lv[1:2], axis=1, keepdims=True))
           - jnp.exp(jnp.sum(lv[2:3] * lv[3:4], axis=1, keepdims=True)) + lam_init)

    q = q_ref[...]
    hp = q.shape[1] // (2 * dh)
    ns = 2 * hp
    qs = [q[:, s * dh:(s + 1) * dh] for s in range(ns)]
    nt = (((1,), (1,)), ((), ()))

    n_tiles = bias_ref.shape[1] - 1
    h0 = pl.program_id(1) * hp
    ones = jnp.ones((16, tq), BF16)
    dv = 2 * dh

    def near_tile(t):
        valid = t <= i
        return jnp.where(valid, i - t, 0), jnp.where(valid, t, n_tiles)

    def far_tile(t):
        valid = t <= i
        j = jnp.where(valid, t - 2, 0)
        d = jnp.where(valid, i - j, 0)
        return j, [jnp.where(valid, far_ref[h0 + hh, d], NEG_BIG) for hh in range(hp)]

    def scores(j, s_ref, biases):
        kj = k_ref[pl.ds(pl.multiple_of(j * tq, tq), tq), :]
        for s in range(ns):
            sc = lax.dot_general(kj[:, s * dh:(s + 1) * dh], qs[s], nt,
                                 preferred_element_type=F32)
            s_ref[s] = sc if biases is None else sc + biases[s // 2]

    def update(j, s_ref, shifts, carry):
        vtj = vt_ref[j]
        out = []
        for s in range(ns):
            hh = s // 2
            vth = jnp.concatenate([vtj[hh * dv:(hh + 1) * dv], ones], axis=0)
            mx, acc = carry[2 * s:2 * s + 2]
            sc = s_ref[s]
            mx_new = jnp.maximum(mx, jnp.max(sc, axis=0, keepdims=True) + shifts[hh])
            p = jnp.exp2(sc - (mx_new - shifts[hh]))
            out += [mx_new, jnp.exp2(mx - mx_new) * acc
                    + jnp.dot(vth, p.astype(BF16), preferred_element_type=F32)]
        return tuple(out)

    def scores_near(t, s_ref):
        j, d = near_tile(t)
        scores(j, s_ref, [bias_ref[hh, d] for hh in range(hp)])

    def scores_far(t, s_ref):
        scores(far_tile(t)[0], s_ref, None)

    def update_near(t, s_ref, carry):
        return update(near_tile(t)[0], s_ref, [0.0] * hp, carry)

    def update_far(t, s_ref, carry):
        j, shifts = far_tile(t)
        return update(j, s_ref, shifts, carry)

    def body(u, carry):
        scores_far(2 * u + 1, sb_ref)
        carry = update_far(2 * u, sa_ref, carry)
        scores_far(2 * u + 2, sa_ref)
        return update_far(2 * u + 1, sb_ref, carry)

    carry = []
    for _ in range(ns):
        carry += [jnp.full((1, tq), NEG_BIG, F32), jnp.zeros((dv + 16, tq), F32)]
    scores_near(0, sa_ref)
    scores_near(1, sb_ref)
    carry = update_near(0, sa_ref, tuple(carry))
    scores_far(2, sa_ref)
    carry = update_near(1, sb_ref, carry)
    carry = lax.fori_loop(1, (i + 2) // 2, body, carry)
    for hh in range(hp):
        acc1, acc2 = carry[4 * hh + 1], carry[4 * hh + 3]
        o = acc1[:dv] / acc1[dv:dv + 1] - lam * (acc2[:dv] / acc2[dv:dv + 1])
        ms = jnp.mean(o * o, axis=0, keepdims=True)
        o = o * lax.rsqrt(ms + RMS_EPS) * g_ref[...] * (1.0 - lam_init)
        o_ref[:, hh * dv:(hh + 1) * dv] = o.T.astype(o_ref.dtype)


def _rel_bucket(rel):
    half = REL_BUCKETS // 2
    max_exact = half // 2
    ret = jnp.where(rel > 0, half, 0)
    n = jnp.abs(rel)
    nf = jnp.maximum(n, 1).astype(F32)
    large = max_exact + (jnp.log(nf / max_exact) / math.log(REL_MAX_DIST / max_exact)
                         * (half - max_exact)).astype(jnp.int32)
    large = jnp.minimum(large, half - 1)
    return ret + jnp.where(n < max_exact, n, large)


def _bias_lookup(rel_bias, rel):
    bucket = _rel_bucket(rel)[None]
    table = rel_bias.astype(F32)
    lead = (slice(None),) + (None,) * rel.ndim
    bias = jnp.zeros((DIFF_HEADS,) + rel.shape, F32)
    for b in range(REL_BUCKETS):
        bias = jnp.where(bucket == b, table[b][lead], bias)
    return bias * LOG2_E


def _bias_tiles(rel_bias, seq):
    tq = ATTN_TILE
    kk = jnp.arange(tq)[None, :, None]
    qq = jnp.arange(tq)[None, None, :]
    d = jnp.arange(2)[:, None, None]
    near = _bias_lookup(rel_bias, kk - qq - d * tq)
    allowed = (d > 0) | ((kk // CHUNK) <= (qq // CHUNK))
    near = jnp.where(allowed[None], near, NEG_BIG)
    masked = jnp.full((DIFF_HEADS, 1, tq, tq), NEG_BIG, F32)
    far = _bias_lookup(rel_bias, -tq * jnp.arange(seq // tq + 1))
    return jnp.concatenate([near, masked], axis=1), far


def _attn_call(pb, vt, bias_near, bias_far, lam_vecs, norm_g, layer_idx, batch, seq):
    t = batch * seq
    tq = ATTN_TILE
    nq = seq // tq
    hp = ATTN_HEADS_PER_STEP
    w = 2 * DIFF_HEAD_DIM
    wb = hp * w
    qcol0 = PB_Q // wb
    kcol0 = PB_K // wb
    lam_init = 0.8 - 0.6 * math.exp(-0.3 * layer_idx)
    return pl.pallas_call(
        functools.partial(_attn_kernel, lam_init=lam_init),
        grid=(batch, DIFF_HEADS // hp, nq),
        in_specs=[pl.BlockSpec((tq, wb), lambda b, h, i: (b * nq + i, qcol0 + h)),
                  pl.BlockSpec((seq, wb), lambda b, h, i: (b, kcol0 + h)),
                  pl.BlockSpec((nq, wb, tq), lambda b, h, i: (b, h, 0)),
                  pl.BlockSpec((hp, 3, tq, tq), lambda b, h, i: (h, 0, 0, 0)),
                  pl.BlockSpec(memory_space=pltpu.SMEM),
                  pl.BlockSpec((4, DIFF_HEAD_DIM), lambda b, h, i: (0, 0)),
                  pl.BlockSpec((w, 1), lambda b, h, i: (0, 0))],
        out_specs=pl.BlockSpec((tq, wb), lambda b, h, i: (b * nq + i, h)),
        out_shape=jax.ShapeDtypeStruct((t, DIFF_WIDTH), BF16),
        scratch_shapes=[pltpu.VMEM((2 * hp, tq, tq), F32), pltpu.VMEM((2 * hp, tq, tq), F32)],
        compiler_params=_params(("parallel", "parallel", "arbitrary")),
        name="diff_attn",
    )(pb, pb, vt, bias_near, bias_far, lam_vecs, norm_g.reshape(w, 1))


def _layer_norm(x, g, b):
    mu = jnp.mean(x, axis=1, keepdims=True)
    xc = x - mu
    var = jnp.mean(xc * xc, axis=1, keepdims=True)
    return xc * lax.rsqrt(var + LN_EPS) * g + b


def _mix_kernel(y_ref, z_ref, ao_ref, g0_ref, g1_ref, h_ref, ng_ref, wso_ref, wao_ref,
                gb_ref, wo_ref, lg_ref, lb_ref, o_ref):
    z = z_ref[...].astype(F32)
    yg = y_ref[...] * (z * _sigmoid(z))
    ms = jnp.mean(yg * yg, axis=1, keepdims=True)
    yn = (yg * lax.rsqrt(ms + RMS_EPS) * ng_ref[...]).astype(BF16)
    y_ssd = jnp.dot(yn, wso_ref[...], preferred_element_type=F32)
    y_att = jnp.dot(ao_ref[...], wao_ref[...], preferred_element_type=F32)
    gb = gb_ref[...]
    gate0 = _sigmoid(g0_ref[...].astype(F32) + gb[:, :D_MODEL])
    gate1 = _sigmoid(g1_ref[...].astype(F32) + gb[:, D_MODEL:])
    mixed = (gate0 * y_ssd + gate1 * y_att).astype(BF16)
    mix = jnp.dot(mixed, wo_ref[...], preferred_element_type=F32)
    o_ref[...] = _layer_norm(DN_ALPHA * h_ref[...] + mix, lg_ref[...], lb_ref[...])


def _mix_call(y, pb, ao, h, norm_g, w_ssd_out, w_attn_out, gate_b, w_o, ln_g, ln_b, tm=256):
    t = h.shape[0]
    d = D_MODEL
    gcol0 = PB_GATES // d
    const = lambda i: (0, 0)
    return pl.pallas_call(
        _mix_kernel,
        grid=(t // tm,),
        in_specs=[pl.BlockSpec((tm, SSD_D_INNER), lambda i: (i, 0)),
                  pl.BlockSpec((tm, SSD_D_INNER), lambda i: (i, 0)),
                  pl.BlockSpec((tm, DIFF_WIDTH), lambda i: (i, 0)),
                  pl.BlockSpec((tm, d), lambda i: (i, gcol0)),
                  pl.BlockSpec((tm, d), lambda i: (i, gcol0 + 1)),
                  pl.BlockSpec((tm, d), lambda i: (i, 0)),
                  pl.BlockSpec((1, SSD_D_INNER), const),
                  pl.BlockSpec((SSD_D_INNER, d), const),
                  pl.BlockSpec((DIFF_WIDTH, d), const),
                  pl.BlockSpec((1, 2 * d), const),
                  pl.BlockSpec((d, d), const),
                  pl.BlockSpec((1, d), const),
                  pl.BlockSpec((1, d), const)],
        out_specs=pl.BlockSpec((tm, d), lambda i: (i, 0)),
        out_shape=jax.ShapeDtypeStruct((t, d), F32),
        compiler_params=_params(("parallel",)),
        name="mix_ln",
    )(y, pb, ao, pb, pb, h, norm_g.reshape(1, -1), w_ssd_out.astype(BF16),
      w_attn_out.astype(BF16), gate_b.reshape(1, -1), w_o.astype(BF16),
      ln_g.reshape(1, -1), ln_b.reshape(1, -1))


def _router_kernel(h_ref, w_ref, b_ref, tri_ref, idx_ref, wt_ref, rank_ref, cnt_ref, run_ref):
    @pl.when(pl.program_id(0) == 0)
    def _():
        run_ref[...] = jnp.zeros_like(run_ref)

    tm = h_ref.shape[0]
    ne = N_EXPERTS
    logits = _dot3(h_ref[...], w_ref[...], (((1,), (0,)), ((), ()))) + b_ref[...]
    lane = lax.broadcasted_iota(jnp.int32, (tm, ne), 1).astype(F32)
    work = logits
    sel, vals = [], []
    for _ in range(TOP_K):
        mx = jnp.max(work, axis=1, keepdims=True)
        first = jnp.min(jnp.where(work == mx, lane, float(ne)), axis=1, keepdims=True)
        hit = lane == first
        sel.append((first, hit))
        vals.append(mx)
        work = jnp.where(hit, -jnp.inf, work)
    exps = [jnp.exp(v - vals[0]) for v in vals]
    denom = exps[0] + exps[1] + exps[2] + exps[3]

    onehot = jnp.zeros((tm, ne), F32)
    for _, hit in sel:
        onehot = onehot + hit.astype(F32)
    before = jnp.dot(tri_ref[...], onehot.astype(BF16), preferred_element_type=F32)
    before = before + run_ref[...]

    out_lane = lax.broadcasted_iota(jnp.int32, (tm, 128), 1)
    idx_out = jnp.zeros((tm, 128), jnp.int32)
    wt_out = jnp.zeros((tm, 128), F32)
    rank_out = jnp.zeros((tm, 128), jnp.int32)
    for k, (first, hit) in enumerate(sel):
        rank = jnp.sum(jnp.where(hit, before, 0.0), axis=1, keepdims=True)
        idx_out = jnp.where(out_lane == k, first.astype(jnp.int32), idx_out)
        wt_out = jnp.where(out_lane == k, exps[k] / denom, wt_out)
        rank_out = jnp.where(out_lane == k, rank.astype(jnp.int32), rank_out)
    idx_ref[...] = idx_out
    wt_ref[...] = wt_out
    rank_ref[...] = rank_out
    total = run_ref[...] + jnp.sum(onehot, axis=0, keepdims=True)
    run_ref[...] = total
    cnt_ref[...] = total


def _router_call(h, w_router, b_router, tm=512):
    t = h.shape[0]
    li = jnp.arange(tm)
    tri = (li[None, :] < li[:, None]).astype(BF16)
    const = lambda i: (0, 0)
    return pl.pallas_call(
        _router_kernel,
        grid=(t // tm,),
        in_specs=[pl.BlockSpec((tm, D_MODEL), lambda i: (i, 0)),
                  pl.BlockSpec((D_MODEL, N_EXPERTS), const),
                  pl.BlockSpec((1, N_EXPERTS), const),
                  pl.BlockSpec((tm, tm), const)],
        out_specs=[pl.BlockSpec((tm, 128), lambda i: (i, 0)),
                   pl.BlockSpec((tm, 128), lambda i: (i, 0)),
                   pl.BlockSpec((tm, 128), lambda i: (i, 0)),
                   pl.BlockSpec((1, N_EXPERTS), const)],
        out_shape=[jax.ShapeDtypeStruct((t, 128), jnp.int32),
                   jax.ShapeDtypeStruct((t, 128), F32),
                   jax.ShapeDtypeStruct((t, 128), jnp.int32),
                   jax.ShapeDtypeStruct((1, N_EXPERTS), F32)],
        scratch_shapes=[pltpu.VMEM((1, N_EXPERTS), F32)],
        compiler_params=_params(("arbitrary",)),
        name="router",
    )(h, w_router, b_router.reshape(1, N_EXPERTS), tri)


def _dispatch_kernel(pad_end_ref, slot_ref, x_ref, xs_out, zero_ref, sem, zero_sem):
    groups = x_ref.shape[0]
    n = groups * 8 * TOP_K
    zb = zero_ref.shape[0]

    @pl.when(pl.program_id(0) == 0)
    def _():
        zero_ref[...] = jnp.zeros_like(zero_ref)
        for e in range(N_EXPERTS):
            first = pl.multiple_of(jnp.maximum(pad_end_ref[e] - zb, 0), zb)
            pltpu.make_async_copy(zero_ref, xs_out.at[pl.ds(first, zb)], zero_sem).start()
        pltpu.make_async_copy(xs_out.at[pl.ds(0, N_EXPERTS * zb)],
                              xs_out.at[pl.ds(0, N_EXPERTS * zb)], zero_sem).wait()

    def start(r, c):
        for sub in range(8):
            for k in range(TOP_K):
                slot = slot_ref[0, 0, (r * 8 + sub) * TOP_K + k]
                pltpu.make_async_copy(x_ref.at[r, pl.ds(sub, 1)],
                                      xs_out.at[pl.ds(slot, 1)], sem).start()
        return c

    lax.fori_loop(0, groups, start, 0)
    pltpu.make_async_copy(xs_out.at[pl.ds(0, n)], xs_out.at[pl.ds(0, n)], sem).wait()


def _dispatch_call(x, slots, pad_end, cap):
    t, d = x.shape
    tt = DISPATCH_TILE
    nt = t // tt
    n = tt * TOP_K
    assert cap >= N_EXPERTS * EXPERT_BLOCK
    grid_spec = pltpu.PrefetchScalarGridSpec(
        num_scalar_prefetch=1,
        grid=(nt,),
        in_specs=[pl.BlockSpec((1, 1, n), lambda i, pe: (i, 0, 0), memory_space=pltpu.SMEM),
                  pl.BlockSpec((tt // 8, 8, d), lambda i, pe: (i, 0, 0))],
        out_specs=pl.BlockSpec(memory_space=pl.ANY),
        scratch_shapes=[pltpu.VMEM((EXPERT_BLOCK, d), x.dtype), pltpu.SemaphoreType.DMA,
                        pltpu.SemaphoreType.DMA],
    )
    return pl.pallas_call(
        _dispatch_kernel,
        grid_spec=grid_spec,
        out_shape=jax.ShapeDtypeStruct((cap, d), x.dtype),
        compiler_params=_params(("arbitrary",), disable_bounds_checks=True),
        name="moe_dispatch",
    )(pad_end, slots.reshape(nt, 1, n), x.reshape(t // 8, 8, d))


def _expert_kernel(be_ref, nb_ref, x_ref, wg_ref, bg_ref, wu_ref, bu_ref, wd_ref, bd_ref,
                   o_ref, wgb_ref, wub_ref, wdb_ref):
    i = pl.program_id(0)
    prev = be_ref[jnp.maximum(i - 1, 0)]
    changed = jnp.logical_or(i == 0, be_ref[i] != prev)

    @pl.when(changed)
    def _():
        wgb_ref[...] = wg_ref[0].astype(BF16)
        wub_ref[...] = wu_ref[0].astype(BF16)
        wdb_ref[...] = wd_ref[0].astype(BF16)

    @pl.when(i < nb_ref[0])
    def _():
        xb = x_ref[...].astype(BF16)
        g = jnp.dot(xb, wgb_ref[...], preferred_element_type=F32) + bg_ref[0]
        u = jnp.dot(xb, wub_ref[...], preferred_element_type=F32) + bu_ref[0]
        g = jnp.minimum(g, SWIGLU_LIMIT)
        u = jnp.clip(u, -SWIGLU_LIMIT, SWIGLU_LIMIT)
        act = g * _sigmoid(SWIGLU_ALPHA * g) * (u + 1.0)
        o_ref[...] = jnp.dot(act.astype(BF16), wdb_ref[...],
                             preferred_element_type=F32) + bd_ref[0]

    @pl.when(i >= nb_ref[0])
    def _():
        o_ref[...] = jnp.zeros_like(o_ref)


def _expert_call(xs, blk_expert, n_used, w_gate, b_gate, w_up, b_up, w_down, b_down):
    cap, d = xs.shape
    bm = EXPERT_BLOCK
    nb = cap // bm
    wspec = lambda shape: pl.BlockSpec(shape, lambda i, be, nu: (be[i], 0, 0))
    grid_spec = pltpu.PrefetchScalarGridSpec(
        num_scalar_prefetch=2,
        grid=(nb,),
        in_specs=[pl.BlockSpec((bm, d), lambda i, be, nu: (jnp.minimum(i, nu[0] - 1), 0)),
                  wspec((1, d, D_FF)), wspec((1, 1, D_FF)),
                  wspec((1, d, D_FF)), wspec((1, 1, D_FF)),
                  wspec((1, D_FF, d)), wspec((1, 1, d))],
        out_specs=pl.BlockSpec((bm, d), lambda i, be, nu: (i, 0)),
        scratch_shapes=[pltpu.VMEM((d, D_FF), BF16), pltpu.VMEM((d, D_FF), BF16),
                        pltpu.VMEM((D_FF, d), BF16)],
    )
    ne = w_gate.shape[0] * w_gate.shape[1]
    return pl.pallas_call(
        _expert_kernel,
        grid_spec=grid_spec,
        out_shape=jax.ShapeDtypeStruct((cap, d), F32),
        compiler_params=_params(("arbitrary",)),
        name="moe_experts",
    )(blk_expert, n_used, xs, w_gate.reshape(ne, d, D_FF), b_gate.reshape(ne, 1, D_FF),
      w_up.reshape(ne, d, D_FF), b_up.reshape(ne, 1, D_FF), w_down.reshape(ne, D_FF, d),
      b_down.reshape(ne, 1, d))


def _combine_kernel(slot_ref, next_slot_ref, ys_hbm, wt_ref, h_ref, lg_ref, lb_ref, o_ref,
                    buf_ref, sems):
    i = pl.program_id(0)
    tt = h_ref.shape[0]
    n = tt * TOP_K

    def issue(s_ref, buf):
        def start(r, c):
            for sub in range(8):
                pltpu.make_async_copy(ys_hbm.at[pl.ds(s_ref[0, 0, r * 8 + sub], 1)],
                                      buf_ref.at[buf, r, pl.ds(sub, 1)], sems.at[buf]).start()
            return c

        lax.fori_loop(0, n // 8, start, 0)

    @pl.when(i == 0)
    def _():
        issue(slot_ref, 0)

    @pl.when(i + 1 < pl.num_programs(0))
    def _():
        issue(next_slot_ref, (i + 1) % 2)

    cur = i % 2
    pltpu.make_async_copy(buf_ref.at[cur], buf_ref.at[cur], sems.at[cur]).wait()

    wt = wt_ref[...]
    d = h_ref.shape[1]
    rows = lambda k: buf_ref[cur, pl.ds(k * (tt // 8), tt // 8)].reshape(tt, d)
    ff = wt[:, 0:1] * rows(0)
    for k in range(1, TOP_K):
        ff = ff + wt[:, k:k + 1] * rows(k)
    o_ref[...] = _layer_norm(DN_ALPHA * h_ref[...] + ff, lg_ref[...], lb_ref[...])


def _combine_call(ys, slots, wts, h, ln_g, ln_b):
    t, d = h.shape
    tt = COMBINE_TILE
    nt = t // tt
    n = tt * TOP_K
    slots_km = slots.reshape(nt, tt, TOP_K).transpose(0, 2, 1).reshape(nt, 1, n)
    const = lambda i: (0, 0)
    return pl.pallas_call(
        _combine_kernel,
        grid=(nt,),
        in_specs=[pl.BlockSpec((1, 1, n), lambda i: (i, 0, 0), memory_space=pltpu.SMEM),
                  pl.BlockSpec((1, 1, n), lambda i: (jnp.minimum(i + 1, nt - 1), 0, 0),
                               memory_space=pltpu.SMEM),
                  pl.BlockSpec(memory_space=pl.ANY),
                  pl.BlockSpec((tt, 128), lambda i: (i, 0)),
                  pl.BlockSpec((tt, d), lambda i: (i, 0)),
                  pl.BlockSpec((1, d), const),
                  pl.BlockSpec((1, d), const)],
        out_specs=pl.BlockSpec((tt, d), lambda i: (i, 0)),
        out_shape=jax.ShapeDtypeStruct((t, d), F32),
        scratch_shapes=[pltpu.VMEM((2, n // 8, 8, d), F32), pltpu.SemaphoreType.DMA((2,))],
        compiler_params=_params(("arbitrary",), disable_bounds_checks=True),
        name="moe_combine_ln",
    )(slots_km, slots_km, ys, wts, h, ln_g.reshape(1, d), ln_b.reshape(1, d))


def _moe_layout(idx, rank, counts, n_blocks):
    bm = EXPERT_BLOCK
    counts = counts.reshape(N_EXPERTS).astype(jnp.int32)
    padded = (counts + bm - 1) // bm * bm
    pad_end = jnp.cumsum(padded)
    pad_start = pad_end - padded
    onehot = idx[:, :, None] == jnp.arange(N_EXPERTS, dtype=jnp.int32)[None, None, :]
    slots = rank + jnp.sum(jnp.where(onehot, pad_start[None, None, :], 0), axis=-1)
    blk_start = jnp.arange(n_blocks, dtype=jnp.int32) * bm
    blk_expert = jnp.sum((pad_end[None, :] <= blk_start[:, None]).astype(jnp.int32), axis=1)
    blk_expert = jnp.minimum(blk_expert, N_EXPERTS - 1)
    n_used = (pad_end[-1] // bm).reshape(1)
    return slots.astype(jnp.int32), blk_expert, n_used, pad_end.astype(jnp.int32)


def kernel(x, rel_bias, w_in, conv_w, conv_b, dt_bias, a_log, d_skip, ssd_norm_g, w_ssd_out, diff_lambda, diff_norm_g, w_attn_out, gate_b, w_o, ln1_g, ln1_b, w_router, b_router, w_gate, b_gate, w_up, b_up, w_down, b_down, ln2_g, ln2_b):
    batch, seq, d = x.shape
    t = batch * seq
    n_assign = t * TOP_K
    n_blocks = (n_assign + N_EXPERTS * (EXPERT_BLOCK - 1) + EXPERT_BLOCK - 1) // EXPERT_BLOCK
    cap = n_blocks * EXPERT_BLOCK
    bias_near, bias_far = _bias_tiles(rel_bias, seq)
    qk_scale = jnp.concatenate([jnp.full((DIFF_WIDTH,), DIFF_HEAD_DIM ** -0.5 * LOG2_E, F32),
                                jnp.ones((DIFF_WIDTH,), F32)])[None, :]

    h = x.reshape(t, d)
    for l in range(DEPTH):
        w_l = w_in[l]
        w_x = w_l[:, OFF_XBC:OFF_DT].astype(BF16)
        w_b = jnp.concatenate([w_l[:, :OFF_XBC], w_l[:, OFF_G:],
                               w_l[:, OFF_Q:OFF_V] * qk_scale], axis=1).astype(BF16)
        w_vt = w_l[:, OFF_V:OFF_G].T.astype(BF16)
        px = _matmul(h, w_x, F32, 1024, 1024)
        pb = _matmul(h, w_b, BF16, 1024, 1024)
        vt = _matmul_nt(h, w_vt, ATTN_TILE)
        dt, dt_t = _dt_call(h, w_l[:, OFF_DT:OFF_Q], dt_bias[l])
        y = _ssd_call(px, conv_w[l], conv_b[l], dt, dt_t, a_log[l], d_skip[l], batch, seq)
        ao = _attn_call(pb, vt, bias_near, bias_far, diff_lambda[l], diff_norm_g[l], l, batch,
                        seq)
        h1 = _mix_call(y, pb, ao, h, ssd_norm_g[l], w_ssd_out[l], w_attn_out[l], gate_b[l],
                       w_o[l], ln1_g[l], ln1_b[l])
        idx, wts, rank, counts = _router_call(h1, w_router[l], b_router[l])
        slots, blk_expert, n_used, pad_end = _moe_layout(idx[:, :TOP_K], rank[:, :TOP_K], counts,
                                                         n_blocks)
        xs = _dispatch_call(h1, slots, pad_end, cap)
        ys = _expert_call(xs, blk_expert + l * N_EXPERTS, n_used, w_gate, b_gate, w_up, b_up,
                          w_down, b_down)
        h = _combine_call(ys, slots, wts, h1, ln2_g[l], ln2_b[l])
    return h.reshape(batch, seq, d)
```

```python
import functools
import math

import jax
import jax.numpy as jnp
from jax import lax
from jax.experimental import pallas as pl
from jax.experimental.pallas import tpu as pltpu

F32 = jnp.float32
BF16 = jnp.bfloat16

D_MODEL = 1024
DEPTH = 2
CHUNK = 64

SSD_D_INNER = 2048
SSD_HEAD_DIM = 64
SSD_HEADS = 32
SSD_GROUPS = 4
SSD_HEADS_PER_GROUP = 8
SSD_STATE = 128
SSD_CONV = 4
SSD_CONV_DIM = 3072
SSD_GROUP_WIDTH = SSD_HEADS_PER_GROUP * SSD_HEAD_DIM

DIFF_HEAD_DIM = 64
DIFF_HEADS = 8
DIFF_WIDTH = 1024

REL_BUCKETS = 32
REL_MAX_DIST = 128

N_EXPERTS = 32
TOP_K = 4
D_FF = 1024
SWIGLU_ALPHA = 1.702
SWIGLU_LIMIT = 7.0

DN_ALPHA = (2 * DEPTH) ** 0.25
LN_EPS = 1e-5
RMS_EPS = 1e-5

OFF_XBC = 2048
OFF_DT = 5120
OFF_Q = 5152
OFF_V = 7200
OFF_G = 8224
IN_COLS = 10272
PB_GATES = 2048
PB_Q = 4096
PB_K = 5120

VMEM_LIMIT_BYTES = 56 * 1024 * 1024

ATTN_TILE = 256
ATTN_HEADS_PER_STEP = 2
SSD_TILE = 256
SSD_GROUPS_PER_STEP = 4
EXPERT_BLOCK = 256
DISPATCH_TILE = 512
COMBINE_TILE = 256
NEG_BIG = -1e30
LOG2_E = math.log2(math.e)


def _params(semantics, **kwargs):
    return pltpu.CompilerParams(dimension_semantics=semantics,
                                vmem_limit_bytes=VMEM_LIMIT_BYTES, **kwargs)


def _sigmoid(x):
    return 1.0 / (1.0 + jnp.exp(-x))


def _split2(v):
    hi = v.astype(BF16)
    lo = (v - hi.astype(F32)).astype(BF16)
    return hi, lo


def _dot3(a, b, dims):
    ah, al = _split2(a)
    bh, bl = _split2(b)
    dot = lambda p, q: lax.dot_general(p, q, dims, preferred_element_type=F32)
    return dot(ah, bh) + dot(ah, bl) + dot(al, bh)


def _split3(v):
    hi = v.astype(BF16)
    r = v - hi.astype(F32)
    mid = r.astype(BF16)
    lo = (r - mid.astype(F32)).astype(BF16)
    return hi, mid, lo


def _mm_kernel(x_ref, w_ref, o_ref, xb_ref):
    @pl.when(pl.program_id(1) == 0)
    def _():
        xb_ref[...] = x_ref[...].astype(BF16)

    o_ref[...] = jnp.dot(xb_ref[...], w_ref[...],
                         preferred_element_type=F32).astype(o_ref.dtype)


def _matmul(x, w, out_dtype, bm, bn):
    m, k = x.shape
    n = w.shape[1]
    return pl.pallas_call(
        _mm_kernel,
        grid=(m // bm, n // bn),
        in_specs=[pl.BlockSpec((bm, k), lambda i, j: (i, 0)),
                  pl.BlockSpec((k, bn), lambda i, j: (0, j))],
        out_specs=pl.BlockSpec((bm, bn), lambda i, j: (i, j)),
        out_shape=jax.ShapeDtypeStruct((m, n), out_dtype),
        scratch_shapes=[pltpu.VMEM((bm, k), BF16)],
        compiler_params=_params(("parallel", "arbitrary")),
        name="in_proj",
    )(x, w)


def _mm_nt_kernel(x_ref, wt_ref, o_ref):
    res = lax.dot_general(wt_ref[...], x_ref[...].astype(BF16), (((1,), (1,)), ((), ())),
                          preferred_element_type=F32)
    tile = o_ref.shape[2]
    for c in range(o_ref.shape[0]):
        o_ref[c] = res[:, c * tile:(c + 1) * tile].astype(o_ref.dtype)


def _matmul_nt(x, w_t, tile, bm=1024):
    m, k = x.shape
    n = w_t.shape[0]
    return pl.pallas_call(
        _mm_nt_kernel,
        grid=(m // bm,),
        in_specs=[pl.BlockSpec((bm, k), lambda i: (i, 0)),
                  pl.BlockSpec((n, k), lambda i: (0, 0))],
        out_specs=pl.BlockSpec((bm // tile, n, tile), lambda i: (i, 0, 0)),
        out_shape=jax.ShapeDtypeStruct((m // tile, n, tile), BF16),
        compiler_params=_params(("parallel",)),
        name="v_proj_t",
    )(x, w_t)


def _softplus(x):
    return jnp.maximum(x, 0.0) + jnp.log(1.0 + jnp.exp(-jnp.abs(x)))


def _dt_kernel(x_ref, w_ref, wt_ref, b_ref, bt_ref, dt_ref, dtt_ref):
    x = x_ref[...]
    tm = x.shape[0]
    raw = _dot3(x, w_ref[...], (((1,), (0,)), ((), ())))
    dt = _softplus(raw + b_ref[...])
    raw_t = _dot3(wt_ref[...], x, (((1,), (1,)), ((), ())))
    dt_t = _softplus(raw_t + bt_ref[...])
    r = SSD_HEADS_PER_GROUP
    for g in range(SSD_GROUPS):
        dt_ref[g] = dt[:, g * r:(g + 1) * r]
        for j in range(tm // CHUNK):
            dtt_ref[g, j] = dt_t[g * r:(g + 1) * r, j * CHUNK:(j + 1) * CHUNK]


def _dt_call(h, w_dt, dt_bias, tm=512):
    t = h.shape[0]
    r = SSD_HEADS_PER_GROUP
    return pl.pallas_call(
        _dt_kernel,
        grid=(t // tm,),
        in_specs=[pl.BlockSpec((tm, D_MODEL), lambda i: (i, 0)),
                  pl.BlockSpec((D_MODEL, SSD_HEADS), lambda i: (0, 0)),
                  pl.BlockSpec((SSD_HEADS, D_MODEL), lambda i: (0, 0)),
                  pl.BlockSpec((1, SSD_HEADS), lambda i: (0, 0)),
                  pl.BlockSpec((SSD_HEADS, 1), lambda i: (0, 0))],
        out_specs=[pl.BlockSpec((SSD_GROUPS, tm, r), lambda i: (0, i, 0)),
                   pl.BlockSpec((SSD_GROUPS, tm // CHUNK, r, CHUNK), lambda i: (0, i, 0, 0))],
        out_shape=[jax.ShapeDtypeStruct((SSD_GROUPS, t, r), F32),
                   jax.ShapeDtypeStruct((SSD_GROUPS, t // CHUNK, r, CHUNK), F32)],
        compiler_params=_params(("parallel",)),
        name="dt_proj",
    )(h, w_dt, w_dt.T, dt_bias.reshape(1, SSD_HEADS), dt_bias.reshape(SSD_HEADS, 1))


def _ssd_kernel(x_ref, b_ref, c_ref, wx_ref, wb_ref, wc_ref, bx_ref, bb_ref, bc_ref,
                dt_ref, dtt_ref, alr_ref, alc_ref, dsk_ref, bd_ref, trit_ref, e_ref,
                y_ref,
                state_ref, ux_ref, ub_ref, uc_ref, xc_ref, bcv_ref, ccv_ref, xdt_ref,
                wst_ref, eacs_ref, acs_ref, acst_ref):
    s_idx = pl.program_id(2)
    ts = x_ref.shape[0]
    nc = ts // CHUNK
    r = SSD_HEADS_PER_GROUP
    gw = SSD_GROUP_WIDTH
    n = SSD_STATE
    ng = dt_ref.shape[0]

    @pl.when(s_idx == 0)
    def _():
        state_ref[...] = jnp.zeros_like(state_ref)
        ux_ref[0:8, :] = jnp.zeros((8, ux_ref.shape[1]), F32)
        ub_ref[0:8, :] = jnp.zeros((8, ub_ref.shape[1]), F32)
        uc_ref[0:8, :] = jnp.zeros((8, uc_ref.shape[1]), F32)

    def conv_silu(raw_ref, u_ref, w_ref, bias_ref):
        u_ref[8:8 + ts, :] = raw_ref[...]
        acc = bias_ref[...] + w_ref[0:1, :] * u_ref[5:5 + ts, :]
        for k in range(1, SSD_CONV):
            acc = acc + w_ref[k:k + 1, :] * u_ref[5 + k:5 + k + ts, :]
        u_ref[0:8, :] = u_ref[ts:ts + 8, :]
        return acc * _sigmoid(acc)

    xc_ref[...] = conv_silu(x_ref, ux_ref, wx_ref, bx_ref)
    bcv_ref[...] = conv_silu(b_ref, ub_ref, wb_ref, bb_ref).astype(BF16)
    ccv_ref[...] = conv_silu(c_ref, uc_ref, wc_ref, bc_ref).astype(BF16)

    bd = bd_ref[...]
    trit = trit_ref[...]
    e_mat = e_ref[...]

    def expand(v):
        hi, lo = _split2(v)
        return (jnp.dot(hi, e_mat, preferred_element_type=F32)
                + jnp.dot(lo, e_mat, preferred_element_type=F32))

    for g in range(ng):
        gl = slice(g * gw, (g + 1) * gw)
        a_row = -jnp.exp(alr_ref[g])
        a_col = -jnp.exp(alc_ref[g])
        dt = dt_ref[g]
        a = dt * a_row
        acs = jnp.zeros((ts, r), F32)
        for part in _split3(a):
            acs = acs + jnp.dot(bd, part, preferred_element_type=F32)
        acs_ref[g] = acs
        a_t = dtt_ref[g].reshape(nc * r, CHUNK) * jnp.concatenate([a_col] * nc, axis=0)
        acs_t = jnp.zeros((nc * r, CHUNK), F32)
        for part in _split3(a_t):
            acs_t = acs_t + jnp.dot(part, trit, preferred_element_type=F32)
        acst_ref[g] = acs_t
        a_last = jnp.concatenate(
            [jnp.broadcast_to(acs[c * CHUNK + CHUNK - 1:c * CHUNK + CHUNK, :], (CHUNK, r))
             for c in range(nc)], axis=0)
        xdt = xc_ref[:, gl] * expand(dt)
        xdt_ref[:, gl] = xdt.astype(BF16)
        wst_ref[:, gl] = (xdt * expand(jnp.exp(a_last - acs))).astype(BF16)
        eacs_ref[:, gl] = expand(jnp.exp(acs))

    row_i = lax.broadcasted_iota(jnp.int32, (CHUNK, CHUNK), 0)
    col_i = lax.broadcasted_iota(jnp.int32, (CHUNK, CHUNK), 1)
    tril = col_i <= row_i
    left_half = lax.broadcasted_iota(jnp.int32, (CHUNK, 128), 1) < SSD_HEAD_DIM

    def chunk_body(c, carry):
        rows = pl.ds(pl.multiple_of(c * CHUNK, CHUNK), CHUNK)
        for g in range(ng):
            cc = ccv_ref[rows, g * n:(g + 1) * n]
            bc = bcv_ref[rows, g * n:(g + 1) * n]
            cb = lax.dot_general(cc, bc, (((1,), (1,)), ((), ())), preferred_element_type=F32)
            state = state_ref[g]
            eacs = eacs_ref[rows, g * gw:(g + 1) * gw]
            y_off = jnp.dot(cc, state.astype(BF16), preferred_element_type=F32) * eacs
            acs_c = acs_ref[g, rows, :]
            acs_tc = acst_ref[g, pl.ds(pl.multiple_of(c * r, r), r), :]
            for p in range(r // 2):
                lo = g * gw + p * 128
                xp = xdt_ref[rows, lo:lo + 128]
                halves = []
                for hh in (2 * p, 2 * p + 1):
                    diff = acs_c[:, hh:hh + 1] - acs_tc[hh:hh + 1, :]
                    decay = jnp.exp(jnp.where(tril, diff, -jnp.inf))
                    halves.append(jnp.dot((cb * decay).astype(BF16), xp,
                                          preferred_element_type=F32))
                y_diag = jnp.where(left_half, halves[0], halves[1])
                y_ref[rows, lo:lo + 128] = (y_diag + y_off[:, p * 128:(p + 1) * 128]
                                            + dsk_ref[:, lo:lo + 128] * xc_ref[rows, lo:lo + 128])
            upd = lax.dot_general(bc, wst_ref[rows, g * gw:(g + 1) * gw], (((0,), (0,)), ((), ())),
                                  preferred_element_type=F32)
            state_ref[g] = state * eacs[CHUNK - 1:CHUNK, :] + upd
        return carry

    lax.fori_loop(0, nc, chunk_body, 0)


def _ssd_call(px, conv_w, conv_b, dt, dt_t, a_log, d_skip, batch, seq):
    t = batch * seq
    ts = SSD_TILE
    ns = seq // ts
    nc = ts // CHUNK
    r = SSD_HEADS_PER_GROUP
    ng = SSD_GROUPS_PER_STEP
    gw = ng * SSD_GROUP_WIDTH
    n = ng * SSD_STATE
    wb0 = SSD_D_INNER // n
    wc0 = wb0 + SSD_GROUPS // ng

    li = jnp.arange(ts)
    bd = ((li[None, :] <= li[:, None]) & (li[None, :] // CHUNK == li[:, None] // CHUNK)).astype(BF16)
    lc = jnp.arange(CHUNK)
    trit = (lc[:, None] <= lc[None, :]).astype(BF16)
    e_mat = (jnp.arange(SSD_GROUP_WIDTH)[None, :] // SSD_HEAD_DIM
             == jnp.arange(r)[:, None]).astype(BF16)
    conv_b2 = conv_b.reshape(1, SSD_CONV_DIM)
    alr = a_log.reshape(SSD_GROUPS, 1, r)
    alc = a_log.reshape(SSD_GROUPS, r, 1)
    dsk = jnp.repeat(d_skip, SSD_HEAD_DIM).reshape(1, SSD_D_INNER)

    row = lambda b, g, s: b * ns + s
    in_specs = [
        pl.BlockSpec((ts, gw), lambda b, g, s: (row(b, g, s), g)),
        pl.BlockSpec((ts, n), lambda b, g, s: (row(b, g, s), wb0 + g)),
        pl.BlockSpec((ts, n), lambda b, g, s: (row(b, g, s), wc0 + g)),
        pl.BlockSpec((SSD_CONV, gw), lambda b, g, s: (0, g)),
        pl.BlockSpec((SSD_CONV, n), lambda b, g, s: (0, wb0 + g)),
        pl.BlockSpec((SSD_CONV, n), lambda b, g, s: (0, wc0 + g)),
        pl.BlockSpec((1, gw), lambda b, g, s: (0, g)),
        pl.BlockSpec((1, n), lambda b, g, s: (0, wb0 + g)),
        pl.BlockSpec((1, n), lambda b, g, s: (0, wc0 + g)),
        pl.BlockSpec((ng, ts, r), lambda b, g, s: (g, row(b, g, s), 0)),
        pl.BlockSpec((ng, nc, r, CHUNK), lambda b, g, s: (g, row(b, g, s), 0, 0)),
        pl.BlockSpec((ng, 1, r), lambda b, g, s: (g, 0, 0)),
        pl.BlockSpec((ng, r, 1), lambda b, g, s: (g, 0, 0)),
        pl.BlockSpec((1, gw), lambda b, g, s: (0, g)),
        pl.BlockSpec((ts, ts), lambda b, g, s: (0, 0)),
        pl.BlockSpec((CHUNK, CHUNK), lambda b, g, s: (0, 0)),
        pl.BlockSpec((r, SSD_GROUP_WIDTH), lambda b, g, s: (0, 0)),
    ]
    scratch = [
        pltpu.VMEM((ng, SSD_STATE, SSD_GROUP_WIDTH), F32),
        pltpu.VMEM((ts + 8, gw), F32),
        pltpu.VMEM((ts + 8, n), F32),
        pltpu.VMEM((ts + 8, n), F32),
        pltpu.VMEM((ts, gw), F32),
        pltpu.VMEM((ts, n), BF16),
        pltpu.VMEM((ts, n), BF16),
        pltpu.VMEM((ts, gw), BF16),
        pltpu.VMEM((ts, gw), BF16),
        pltpu.VMEM((ts, gw), F32),
        pltpu.VMEM((ng, ts, r), F32),
        pltpu.VMEM((ng, nc * r, CHUNK), F32),
    ]
    return pl.pallas_call(
        _ssd_kernel,
        grid=(batch, SSD_GROUPS // ng, ns),
        in_specs=in_specs,
        out_specs=pl.BlockSpec((ts, gw), lambda b, g, s: (row(b, g, s), g)),
        out_shape=jax.ShapeDtypeStruct((t, SSD_D_INNER), F32),
        scratch_shapes=scratch,
        compiler_params=_params(("parallel", "parallel", "arbitrary")),
        name="ssd_scan",
    )(px, px, px, conv_w, conv_w, conv_w, conv_b2, conv_b2, conv_b2,
      dt, dt_t, alr, alc, dsk, bd, trit, e_mat)


def _attn_kernel(q_ref, k_ref, vt_ref, bias_ref, far_ref, lam_ref, g_ref, o_ref, sa_ref, sb_ref,
                 *, lam_init):
    i = pl.program_id(2)
    tq = q_ref.shape[0]
    dh = DIFF_HEAD_DIM
    lv = lam_ref[...]
    lam = (jnp.exp(jnp.sum(lv[0:1] * lv[1:2], axis=1, keepdims=True))
           - jnp.exp(jnp.sum(lv[2:3] * lv[3:4], axis=1, keepdims=True)) + lam_init)

    q = q_ref[...]
    hp = q.shape[1] // (2 * dh)
    ns = 2 * hp
    qs = [q[:, s * dh:(s + 1) * dh] for s in range(ns)]
    nt = (((1,), (1,)), ((), ()))

    n_tiles = bias_ref.shape[1] - 1
    h0 = pl.program_id(1) * hp
    ones = jnp.ones((16, tq), BF16)
    dv = 2 * dh

    def near_tile(t):
        valid = t <= i
        return jnp.where(valid, i - t, 0), jnp.where(valid, t, n_tiles)

    def far_tile(t):
        valid = t <= i
        j = jnp.where(valid, t - 2, 0)
        d = jnp.where(valid, i - j, 0)
        return j, [jnp.where(valid, far_ref[h0 + hh, d], NEG_BIG) for hh in range(hp)]

    def scores(j, s_ref, biases):
        kj = k_ref[pl.ds(pl.multiple_of(j * tq, tq), tq), :]
        for s in range(ns):
            sc = lax.dot_general(kj[:, s * dh:(s + 1) * dh], qs[s], nt,
                                 preferred_element_type=F32)
            s_ref[s] = sc if biases is None else sc + biases[s // 2]

    def update(j, s_ref, shifts, carry):
        vtj = vt_ref[j]
        out = []
        for s in range(ns):
            hh = s // 2
            vth = jnp.concatenate([vtj[hh * dv:(hh + 1) * dv], ones], axis=0)
            mx, acc = carry[2 * s:2 * s + 2]
            sc = s_ref[s]
            mx_new = jnp.maximum(mx, jnp.max(sc, axis=0, keepdims=True) + shifts[hh])
            p = jnp.exp2(sc - (mx_new - shifts[hh]))
            out += [mx_new, jnp.exp2(mx - mx_new) * acc
                    + jnp.dot(vth, p.astype(BF16), preferred_element_type=F32)]
        return tuple(out)

    def scores_near(t, s_ref):
        j, d = near_tile(t)
        scores(j, s_ref, [bias_ref[hh, d] for hh in range(hp)])

    def scores_far(t, s_ref):
        scores(far_tile(t)[0], s_ref, None)

    def update_near(t, s_ref, carry):
        return update(near_tile(t)[0], s_ref, [0.0] * hp, carry)

    def update_far(t, s_ref, carry):
        j, shifts = far_tile(t)
        return update(j, s_ref, shifts, carry)

    def body(u, carry):
        scores_far(2 * u + 1, sb_ref)
        carry = update_far(2 * u, sa_ref, carry)
        scores_far(2 * u + 2, sa_ref)
        return update_far(2 * u + 1, sb_ref, carry)

    carry = []
    for _ in range(ns):
        carry += [jnp.full((1, tq), NEG_BIG, F32), jnp.zeros((dv + 16, tq), F32)]
    scores_near(0, sa_ref)
    scores_near(1, sb_ref)
    carry = update_near(0, sa_ref, tuple(carry))
    scores_far(2, sa_ref)
    carry = update_near(1, sb_ref, carry)
    carry = lax.fori_loop(1, (i + 2) // 2, body, carry)
    for hh in range(hp):
        acc1, acc2 = carry[4 * hh + 1], carry[4 * hh + 3]
        o = acc1[:dv] / acc1[dv:dv + 1] - lam * (acc2[:dv] / acc2[dv:dv + 1])
        ms = jnp.mean(o * o, axis=0, keepdims=True)
        o = o * lax.rsqrt(ms + RMS_EPS) * g_ref[...] * (1.0 - lam_init)
        o_ref[:, hh * dv:(hh + 1) * dv] = o.T.astype(o_ref.dtype)


def _rel_bucket(rel):
    half = REL_BUCKETS // 2
    max_exact = half // 2
    ret = jnp.where(rel > 0, half, 0)
    n = jnp.abs(rel)
    nf = jnp.maximum(n, 1).astype(F32)
    large = max_exact + (jnp.log(nf / max_exact) / math.log(REL_MAX_DIST / max_exact)
                         * (half - max_exact)).astype(jnp.int32)
    large = jnp.minimum(large, half - 1)
    return ret + jnp.where(n < max_exact, n, large)


def _bias_lookup(rel_bias, rel):
    bucket = _rel_bucket(rel)[None]
    table = rel_bias.astype(F32)
    lead = (slice(None),) + (None,) * rel.ndim
    bias = jnp.zeros((DIFF_HEADS,) + rel.shape, F32)
    for b in range(REL_BUCKETS):
        bias = jnp.where(bucket == b, table[b][lead], bias)
    return bias * LOG2_E


def _bias_tiles(rel_bias, seq):
    tq = ATTN_TILE
    kk = jnp.arange(tq)[None, :, None]
    qq = jnp.arange(tq)[None, None, :]
    d = jnp.arange(2)[:, None, None]
    near = _bias_lookup(rel_bias, kk - qq - d * tq)
    allowed = (d > 0) | ((kk // CHUNK) <= (qq // CHUNK))
    near = jnp.where(allowed[None], near, NEG_BIG)
    masked = jnp.full((DIFF_HEADS, 1, tq, tq), NEG_BIG, F32)
    far = _bias_lookup(rel_bias, -tq * jnp.arange(seq // tq + 1))
    return jnp.concatenate([near, masked], axis=1), far


def _attn_call(pb, vt, bias_near, bias_far, lam_vecs, norm_g, layer_idx, batch, seq):
    t = batch * seq
    tq = ATTN_TILE
    nq = seq // tq
    hp = ATTN_HEADS_PER_STEP
    w = 2 * DIFF_HEAD_DIM
    wb = hp * w
    qcol0 = PB_Q // wb
    kcol0 = PB_K // wb
    lam_init = 0.8 - 0.6 * math.exp(-0.3 * layer_idx)
    return pl.pallas_call(
        functools.partial(_attn_kernel, lam_init=lam_init),
        grid=(batch, DIFF_HEADS // hp, nq),
        in_specs=[pl.BlockSpec((tq, wb), lambda b, h, i: (b * nq + i, qcol0 + h)),
                  pl.BlockSpec((seq, wb), lambda b, h, i: (b, kcol0 + h)),
                  pl.BlockSpec((nq, wb, tq), lambda b, h, i: (b, h, 0)),
                  pl.BlockSpec((hp, 3, tq, tq), lambda b, h, i: (h, 0, 0, 0)),
                  pl.BlockSpec(memory_space=pltpu.SMEM),
                  pl.BlockSpec((4, DIFF_HEAD_DIM), lambda b, h, i: (0, 0)),
                  pl.BlockSpec((w, 1), lambda b, h, i: (0, 0))],
        out_specs=pl.BlockSpec((tq, wb), lambda b, h, i: (b * nq + i, h)),
        out_shape=jax.ShapeDtypeStruct((t, DIFF_WIDTH), BF16),
        scratch_shapes=[pltpu.VMEM((2 * hp, tq, tq), F32), pltpu.VMEM((2 * hp, tq, tq), F32)],
        compiler_params=_params(("parallel", "parallel", "arbitrary")),
        name="diff_attn",
    )(pb, pb, vt, bias_near, bias_far, lam_vecs, norm_g.reshape(w, 1))


def _layer_norm(x, g, b):
    mu = jnp.mean(x, axis=1, keepdims=True)
    xc = x - mu
    var = jnp.mean(xc * xc, axis=1, keepdims=True)
    return xc * lax.rsqrt(var + LN_EPS) * g + b


def _mix_kernel(y_ref, z_ref, ao_ref, g0_ref, g1_ref, h_ref, ng_ref, wso_ref, wao_ref,
                gb_ref, wo_ref, lg_ref, lb_ref, o_ref):
    z = z_ref[...].astype(F32)
    yg = y_ref[...] * (z * _sigmoid(z))
    ms = jnp.mean(yg * yg, axis=1, keepdims=True)
    yn = (yg * lax.rsqrt(ms + RMS_EPS) * ng_ref[...]).astype(BF16)
    y_ssd = jnp.dot(yn, wso_ref[...], preferred_element_type=F32)
    y_att = jnp.dot(ao_ref[...], wao_ref[...], preferred_element_type=F32)
    gb = gb_ref[...]
    gate0 = _sigmoid(g0_ref[...].astype(F32) + gb[:, :D_MODEL])
    gate1 = _sigmoid(g1_ref[...].astype(F32) + gb[:, D_MODEL:])
    mixed = (gate0 * y_ssd + gate1 * y_att).astype(BF16)
    mix = jnp.dot(mixed, wo_ref[...], preferred_element_type=F32)
    o_ref[...] = _layer_norm(DN_ALPHA * h_ref[...] + mix, lg_ref[...], lb_ref[...])


def _mix_call(y, pb, ao, h, norm_g, w_ssd_out, w_attn_out, gate_b, w_o, ln_g, ln_b, tm=256):
    t = h.shape[0]
    d = D_MODEL
    gcol0 = PB_GATES // d
    const = lambda i: (0, 0)
    return pl.pallas_call(
        _mix_kernel,
        grid=(t // tm,),
        in_specs=[pl.BlockSpec((tm, SSD_D_INNER), lambda i: (i, 0)),
                  pl.BlockSpec((tm, SSD_D_INNER), lambda i: (i, 0)),
                  pl.BlockSpec((tm, DIFF_WIDTH), lambda i: (i, 0)),
                  pl.BlockSpec((tm, d), lambda i: (i, gcol0)),
                  pl.BlockSpec((tm, d), lambda i: (i, gcol0 + 1)),
                  pl.BlockSpec((tm, d), lambda i: (i, 0)),
                  pl.BlockSpec((1, SSD_D_INNER), const),
                  pl.BlockSpec((SSD_D_INNER, d), const),
                  pl.BlockSpec((DIFF_WIDTH, d), const),
                  pl.BlockSpec((1, 2 * d), const),
                  pl.BlockSpec((d, d), const),
                  pl.BlockSpec((1, d), const),
                  pl.BlockSpec((1, d), const)],
        out_specs=pl.BlockSpec((tm, d), lambda i: (i, 0)),
        out_shape=jax.ShapeDtypeStruct((t, d), F32),
        compiler_params=_params(("parallel",)),
        name="mix_ln",
    )(y, pb, ao, pb, pb, h, norm_g.reshape(1, -1), w_ssd_out.astype(BF16),
      w_attn_out.astype(BF16), gate_b.reshape(1, -1), w_o.astype(BF16),
      ln_g.reshape(1, -1), ln_b.reshape(1, -1))


def _router_kernel(h_ref, w_ref, b_ref, tri_ref, idx_ref, wt_ref, rank_ref, cnt_ref, run_ref):
    @pl.when(pl.program_id(0) == 0)
    def _():
        run_ref[...] = jnp.zeros_like(run_ref)

    tm = h_ref.shape[0]
    ne = N_EXPERTS
    logits = _dot3(h_ref[...], w_ref[...], (((1,), (0,)), ((), ()))) + b_ref[...]
    lane = lax.broadcasted_iota(jnp.int32, (tm, ne), 1).astype(F32)
    work = logits
    sel, vals = [], []
    for _ in range(TOP_K):
        mx = jnp.max(work, axis=1, keepdims=True)
        first = jnp.min(jnp.where(work == mx, lane, float(ne)), axis=1, keepdims=True)
        hit = lane == first
        sel.append((first, hit))
        vals.append(mx)
        work = jnp.where(hit, -jnp.inf, work)
    exps = [jnp.exp(v - vals[0]) for v in vals]
    denom = exps[0] + exps[1] + exps[2] + exps[3]

    onehot = jnp.zeros((tm, ne), F32)
    for _, hit in sel:
        onehot = onehot + hit.astype(F32)
    before = jnp.dot(tri_ref[...], onehot.astype(BF16), preferred_element_type=F32)
    before = before + run_ref[...]

    out_lane = lax.broadcasted_iota(jnp.int32, (tm, 128), 1)
    idx_out = jnp.zeros((tm, 128), jnp.int32)
    wt_out = jnp.zeros((tm, 128), F32)
    rank_out = jnp.zeros((tm, 128), jnp.int32)
    for k, (first, hit) in enumerate(sel):
        rank = jnp.sum(jnp.where(hit, before, 0.0), axis=1, keepdims=True)
        idx_out = jnp.where(out_lane == k, first.astype(jnp.int32), idx_out)
        wt_out = jnp.where(out_lane == k, exps[k] / denom, wt_out)
        rank_out = jnp.where(out_lane == k, rank.astype(jnp.int32), rank_out)
    idx_ref[...] = idx_out
    wt_ref[...] = wt_out
    rank_ref[...] = rank_out
    total = run_ref[...] + jnp.sum(onehot, axis=0, keepdims=True)
    run_ref[...] = total
    cnt_ref[...] = total


def _router_call(h, w_router, b_router, tm=512):
    t = h.shape[0]
    li = jnp.arange(tm)
    tri = (li[None, :] < li[:, None]).astype(BF16)
    const = lambda i: (0, 0)
    return pl.pallas_call(
        _router_kernel,
        grid=(t // tm,),
        in_specs=[pl.BlockSpec((tm, D_MODEL), lambda i: (i, 0)),
                  pl.BlockSpec((D_MODEL, N_EXPERTS), const),
                  pl.BlockSpec((1, N_EXPERTS), const),
                  pl.BlockSpec((tm, tm), const)],
        out_specs=[pl.BlockSpec((tm, 128), lambda i: (i, 0)),
                   pl.BlockSpec((tm, 128), lambda i: (i, 0)),
                   pl.BlockSpec((tm, 128), lambda i: (i, 0)),
                   pl.BlockSpec((1, N_EXPERTS), const)],
        out_shape=[jax.ShapeDtypeStruct((t, 128), jnp.int32),
                   jax.ShapeDtypeStruct((t, 128), F32),
                   jax.ShapeDtypeStruct((t, 128), jnp.int32),
                   jax.ShapeDtypeStruct((1, N_EXPERTS), F32)],
        scratch_shapes=[pltpu.VMEM((1, N_EXPERTS), F32)],
        compiler_params=_params(("arbitrary",)),
        name="router",
    )(h, w_router, b_router.reshape(1, N_EXPERTS), tri)


def _dispatch_kernel(pad_end_ref, slot_ref, x_ref, xs_out, zero_ref, sem, zero_sem):
    groups = x_ref.shape[0]
    n = groups * 8 * TOP_K
    zb = zero_ref.shape[0]

    @pl.when(pl.program_id(0) == 0)
    def _():
        zero_ref[...] = jnp.zeros_like(zero_ref)

        def clear(first):
            return pltpu.make_async_copy(
                zero_ref, xs_out.at[pl.ds(pl.multiple_of(first, zb), zb)], zero_sem)

        for e in range(N_EXPERTS):
            clear(jnp.maximum(pad_end_ref[e] - zb, 0)).start()
        used = pad_end_ref[N_EXPERTS - 1] // zb
        total = xs_out.shape[0] // zb

        def clear_tail(b, c):
            clear(b * zb).start()
            return c

        def wait_one(b, c):
            clear(0).wait()
            return c

        lax.fori_loop(used, total, clear_tail, 0)
        lax.fori_loop(used - N_EXPERTS, total, wait_one, 0)

    def start(r, c):
        for sub in range(8):
            for k in range(TOP_K):
                slot = slot_ref[0, 0, (r * 8 + sub) * TOP_K + k]
                pltpu.make_async_copy(x_ref.at[r, pl.ds(sub, 1)],
                                      xs_out.at[pl.ds(slot, 1)], sem).start()
        return c

    lax.fori_loop(0, groups, start, 0)
    pltpu.make_async_copy(xs_out.at[pl.ds(0, n)], xs_out.at[pl.ds(0, n)], sem).wait()


def _dispatch_call(x, slots, pad_end, cap):
    t, d = x.shape
    tt = DISPATCH_TILE
    nt = t // tt
    n = tt * TOP_K
    assert cap >= N_EXPERTS * EXPERT_BLOCK
    grid_spec = pltpu.PrefetchScalarGridSpec(
        num_scalar_prefetch=1,
        grid=(nt,),
        in_specs=[pl.BlockSpec((1, 1, n), lambda i, pe: (i, 0, 0), memory_space=pltpu.SMEM),
                  pl.BlockSpec((tt // 8, 8, d), lambda i, pe: (i, 0, 0))],
        out_specs=pl.BlockSpec(memory_space=pl.ANY),
        scratch_shapes=[pltpu.VMEM((EXPERT_BLOCK, d), x.dtype), pltpu.SemaphoreType.DMA,
                        pltpu.SemaphoreType.DMA],
    )
    return pl.pallas_call(
        _dispatch_kernel,
        grid_spec=grid_spec,
        out_shape=jax.ShapeDtypeStruct((cap, d), x.dtype),
        compiler_params=_params(("arbitrary",), disable_bounds_checks=True),
        name="moe_dispatch",
    )(pad_end, slots.reshape(nt, 1, n), x.reshape(t // 8, 8, d))


def _expert_kernel(be_ref, nb_ref, x_ref, wg_ref, bg_ref, wu_ref, bu_ref, wd_ref, bd_ref,
                   o_ref, wgb_ref, wub_ref, wdb_ref):
    i = pl.program_id(0)
    prev = be_ref[jnp.maximum(i - 1, 0)]
    changed = jnp.logical_or(i == 0, be_ref[i] != prev)

    @pl.when(changed)
    def _():
        wgb_ref[...] = wg_ref[0].astype(BF16)
        wub_ref[...] = wu_ref[0].astype(BF16)
        wdb_ref[...] = wd_ref[0].astype(BF16)

    @pl.when(i < nb_ref[0])
    def _():
        xb = x_ref[...].astype(BF16)
        g = jnp.dot(xb, wgb_ref[...], preferred_element_type=F32) + bg_ref[0]
        u = jnp.dot(xb, wub_ref[...], preferred_element_type=F32) + bu_ref[0]
        g = jnp.minimum(g, SWIGLU_LIMIT)
        u = jnp.clip(u, -SWIGLU_LIMIT, SWIGLU_LIMIT)
        act = g * _sigmoid(SWIGLU_ALPHA * g) * (u + 1.0)
        o_ref[...] = jnp.dot(act.astype(BF16), wdb_ref[...],
                             preferred_element_type=F32) + bd_ref[0]

    @pl.when(i >= nb_ref[0])
    def _():
        o_ref[...] = jnp.zeros_like(o_ref)


def _expert_call(xs, blk_expert, n_used, w_gate, b_gate, w_up, b_up, w_down, b_down):
    cap, d = xs.shape
    bm = EXPERT_BLOCK
    nb = cap // bm
    wspec = lambda shape: pl.BlockSpec(shape, lambda i, be, nu: (be[i], 0, 0))
    grid_spec = pltpu.PrefetchScalarGridSpec(
        num_scalar_prefetch=2,
        grid=(nb,),
        in_specs=[pl.BlockSpec((bm, d), lambda i, be, nu: (jnp.minimum(i, nu[0] - 1), 0)),
                  wspec((1, d, D_FF)), wspec((1, 1, D_FF)),
                  wspec((1, d, D_FF)), wspec((1, 1, D_FF)),
                  wspec((1, D_FF, d)), wspec((1, 1, d))],
        out_specs=pl.BlockSpec((bm, d), lambda i, be, nu: (i, 0)),
        scratch_shapes=[pltpu.VMEM((d, D_FF), BF16), pltpu.VMEM((d, D_FF), BF16),
                        pltpu.VMEM((D_FF, d), BF16)],
    )
    ne = w_gate.shape[0] * w_gate.shape[1]
    return pl.pallas_call(
        _expert_kernel,
        grid_spec=grid_spec,
        out_shape=jax.ShapeDtypeStruct((cap, d), F32),
        compiler_params=_params(("arbitrary",)),
        name="moe_experts",
    )(blk_expert, n_used, xs, w_gate.reshape(ne, d, D_FF), b_gate.reshape(ne, 1, D_FF),
      w_up.reshape(ne, d, D_FF), b_up.reshape(ne, 1, D_FF), w_down.reshape(ne, D_FF, d),
      b_down.reshape(ne, 1, d))


def _combine_kernel(slot_ref, next_slot_ref, ys_hbm, wt_ref, h_ref, lg_ref, lb_ref, o_ref,
                    buf_ref, sems):
    i = pl.program_id(0)
    tt = h_ref.shape[0]
    n = tt * TOP_K

    def issue(s_ref, buf):
        def start(r, c):
            for sub in range(8):
                pltpu.make_async_copy(ys_hbm.at[pl.ds(s_ref[0, 0, r * 8 + sub], 1)],
                                      buf_ref.at[buf, r, pl.ds(sub, 1)], sems.at[buf]).start()
            return c

        lax.fori_loop(0, n // 8, start, 0)

    @pl.when(i == 0)
    def _():
        issue(slot_ref, 0)

    @pl.when(i + 1 < pl.num_programs(0))
    def _():
        issue(next_slot_ref, (i + 1) % 2)

    cur = i % 2
    pltpu.make_async_copy(buf_ref.at[cur], buf_ref.at[cur], sems.at[cur]).wait()

    wt = wt_ref[...]
    d = h_ref.shape[1]
    rows = lambda k: buf_ref[cur, pl.ds(k * (tt // 8), tt // 8)].reshape(tt, d)
    ff = wt[:, 0:1] * rows(0)
    for k in range(1, TOP_K):
        ff = ff + wt[:, k:k + 1] * rows(k)
    o_ref[...] = _layer_norm(DN_ALPHA * h_ref[...] + ff, lg_ref[...], lb_ref[...])


def _combine_call(ys, slots, wts, h, ln_g, ln_b):
    t, d = h.shape
    tt = COMBINE_TILE
    nt = t // tt
    n = tt * TOP_K
    slots_km = slots.reshape(nt, tt, TOP_K).transpose(0, 2, 1).reshape(nt, 1, n)
    const = lambda i: (0, 0)
    return pl.pallas_call(
        _combine_kernel,
        grid=(nt,),
        in_specs=[pl.BlockSpec((1, 1, n), lambda i: (i, 0, 0), memory_space=pltpu.SMEM),
                  pl.BlockSpec((1, 1, n), lambda i: (jnp.minimum(i + 1, nt - 1), 0, 0),
                               memory_space=pltpu.SMEM),
                  pl.BlockSpec(memory_space=pl.ANY),
                  pl.BlockSpec((tt, 128), lambda i: (i, 0)),
                  pl.BlockSpec((tt, d), lambda i: (i, 0)),
                  pl.BlockSpec((1, d), const),
                  pl.BlockSpec((1, d), const)],
        out_specs=pl.BlockSpec((tt, d), lambda i: (i, 0)),
        out_shape=jax.ShapeDtypeStruct((t, d), F32),
        scratch_shapes=[pltpu.VMEM((2, n // 8, 8, d), F32), pltpu.SemaphoreType.DMA((2,))],
        compiler_params=_params(("arbitrary",), disable_bounds_checks=True),
        name="moe_combine_ln",
    )(slots_km, slots_km, ys, wts, h, ln_g.reshape(1, d), ln_b.reshape(1, d))


def _moe_layout(idx, rank, counts, n_blocks):
    bm = EXPERT_BLOCK
    counts = counts.reshape(N_EXPERTS).astype(jnp.int32)
    padded = (counts + bm - 1) // bm * bm
    pad_end = jnp.cumsum(padded)
    pad_start = pad_end - padded
    onehot = idx[:, :, None] == jnp.arange(N_EXPERTS, dtype=jnp.int32)[None, None, :]
    slots = rank + jnp.sum(jnp.where(onehot, pad_start[None, None, :], 0), axis=-1)
    blk_start = jnp.arange(n_blocks, dtype=jnp.int32) * bm
    blk_expert = jnp.sum((pad_end[None, :] <= blk_start[:, None]).astype(jnp.int32), axis=1)
    blk_expert = jnp.minimum(blk_expert, N_EXPERTS - 1)
    n_used = (pad_end[-1] // bm).reshape(1)
    return slots.astype(jnp.int32), blk_expert, n_used, pad_end.astype(jnp.int32)


def kernel(x, rel_bias, w_in, conv_w, conv_b, dt_bias, a_log, d_skip, ssd_norm_g, w_ssd_out, diff_lambda, diff_norm_g, w_attn_out, gate_b, w_o, ln1_g, ln1_b, w_router, b_router, w_gate, b_gate, w_up, b_up, w_down, b_down, ln2_g, ln2_b):
    batch, seq, d = x.shape
    t = batch * seq
    n_assign = t * TOP_K
    n_blocks = (n_assign + N_EXPERTS * (EXPERT_BLOCK - 1) + EXPERT_BLOCK - 1) // EXPERT_BLOCK
    cap = n_blocks * EXPERT_BLOCK
    bias_near, bias_far = _bias_tiles(rel_bias, seq)
    qk_scale = jnp.concatenate([jnp.full((DIFF_WIDTH,), DIFF_HEAD_DIM ** -0.5 * LOG2_E, F32),
                                jnp.ones((DIFF_WIDTH,), F32)])[None, :]

    h = x.reshape(t, d)
    for l in range(DEPTH):
        w_l = w_in[l]
        w_x = w_l[:, OFF_XBC:OFF_DT].astype(BF16)
        w_b = jnp.concatenate([w_l[:, :OFF_XBC], w_l[:, OFF_G:],
                               w_l[:, OFF_Q:OFF_V] * qk_scale], axis=1).astype(BF16)
        w_vt = w_l[:, OFF_V:OFF_G].T.astype(BF16)
        px = _matmul(h, w_x, F32, 1024, 1024)
        pb = _matmul(h, w_b, BF16, 1024, 1024)
        vt = _matmul_nt(h, w_vt, ATTN_TILE)
        dt, dt_t = _dt_call(h, w_l[:, OFF_DT:OFF_Q], dt_bias[l])
        y = _ssd_call(px, conv_w[l], conv_b[l], dt, dt_t, a_log[l], d_skip[l], batch, seq)
        ao = _attn_call(pb, vt, bias_near, bias_far, diff_lambda[l], diff_norm_g[l], l, batch,
                        seq)
        h1 = _mix_call(y, pb, ao, h, ssd_norm_g[l], w_ssd_out[l], w_attn_out[l], gate_b[l],
                       w_o[l], ln1_g[l], ln1_b[l])
        idx, wts, rank, counts = _router_call(h1, w_router[l], b_router[l])
        slots, blk_expert, n_used, pad_end = _moe_layout(idx[:, :TOP_K], rank[:, :TOP_K], counts,
                                                         n_blocks)
        xs = _dispatch_call(h1, slots, pad_end, cap)
        ys = _expert_call(xs, blk_expert + l * N_EXPERTS, n_used, w_gate, b_gate, w_up, b_up,
                          w_down, b_down)
        h = _combine_call(ys, slots, wts, h1, ln2_g[l], ln2_b[l])
    return h.reshape(batch, seq, d)
```

```python
import functools
import math

import jax
import jax.numpy as jnp
from jax import lax
from jax.experimental import pallas as pl
from jax.experimental.pallas import tpu as pltpu

F32 = jnp.float32
BF16 = jnp.bfloat16

D_MODEL = 1024
DEPTH = 2
CHUNK = 64

SSD_D_INNER = 2048
SSD_HEAD_DIM = 64
SSD_HEADS = 32
SSD_GROUPS = 4
SSD_HEADS_PER_GROUP = 8
SSD_STATE = 128
SSD_CONV = 4
SSD_CONV_DIM = 3072
SSD_GROUP_WIDTH = SSD_HEADS_PER_GROUP * SSD_HEAD_DIM

DIFF_HEAD_DIM = 64
DIFF_HEADS = 8
DIFF_WIDTH = 1024

REL_BUCKETS = 32
REL_MAX_DIST = 128

N_EXPERTS = 32
TOP_K = 4
D_FF = 1024
SWIGLU_ALPHA = 1.702
SWIGLU_LIMIT = 7.0

DN_ALPHA = (2 * DEPTH) ** 0.25
LN_EPS = 1e-5
RMS_EPS = 1e-5

OFF_XBC = 2048
OFF_DT = 5120
OFF_Q = 5152
OFF_V = 7200
OFF_G = 8224
IN_COLS = 10272
PB_GATES = 2048
PB_Q = 4096
PB_K = 5120

VMEM_LIMIT_BYTES = 56 * 1024 * 1024

ATTN_TILE = 256
ATTN_HEADS_PER_STEP = 2
SSD_TILE = 256
SSD_GROUPS_PER_STEP = 4
EXPERT_BLOCK = 256
DISPATCH_TILE = 512
COMBINE_TILE = 256
NEG_BIG = -1e30
LOG2_E = math.log2(math.e)


def _params(semantics, **kwargs):
    return pltpu.CompilerParams(dimension_semantics=semantics,
                                vmem_limit_bytes=VMEM_LIMIT_BYTES, **kwargs)


def _sigmoid(x):
    return 1.0 / (1.0 + jnp.exp(-x))


def _split2(v):
    hi = v.astype(BF16)
    lo = (v - hi.astype(F32)).astype(BF16)
    return hi, lo


def _dot3(a, b, dims):
    ah, al = _split2(a)
    bh, bl = _split2(b)
    dot = lambda p, q: lax.dot_general(p, q, dims, preferred_element_type=F32)
    return dot(ah, bh) + dot(ah, bl) + dot(al, bh)


def _split3(v):
    hi = v.astype(BF16)
    r = v - hi.astype(F32)
    mid = r.astype(BF16)
    lo = (r - mid.astype(F32)).astype(BF16)
    return hi, mid, lo


def _mm_kernel(x_ref, w_ref, o_ref, xb_ref):
    @pl.when(pl.program_id(1) == 0)
    def _():
        xb_ref[...] = x_ref[...].astype(BF16)

    o_ref[...] = jnp.dot(xb_ref[...], w_ref[...],
                         preferred_element_type=F32).astype(o_ref.dtype)


def _matmul(x, w, out_dtype, bm, bn):
    m, k = x.shape
    n = w.shape[1]
    return pl.pallas_call(
        _mm_kernel,
        grid=(m // bm, n // bn),
        in_specs=[pl.BlockSpec((bm, k), lambda i, j: (i, 0)),
                  pl.BlockSpec((k, bn), lambda i, j: (0, j))],
        out_specs=pl.BlockSpec((bm, bn), lambda i, j: (i, j)),
        out_shape=jax.ShapeDtypeStruct((m, n), out_dtype),
        scratch_shapes=[pltpu.VMEM((bm, k), BF16)],
        compiler_params=_params(("parallel", "arbitrary")),
        name="in_proj",
    )(x, w)


def _mm_nt_kernel(x_ref, wt_ref, o_ref):
    res = lax.dot_general(wt_ref[...], x_ref[...].astype(BF16), (((1,), (1,)), ((), ())),
                          preferred_element_type=F32)
    tile = o_ref.shape[2]
    for c in range(o_ref.shape[0]):
        o_ref[c] = res[:, c * tile:(c + 1) * tile].astype(o_ref.dtype)


def _matmul_nt(x, w_t, tile, bm=1024):
    m, k = x.shape
    n = w_t.shape[0]
    return pl.pallas_call(
        _mm_nt_kernel,
        grid=(m // bm,),
        in_specs=[pl.BlockSpec((bm, k), lambda i: (i, 0)),
                  pl.BlockSpec((n, k), lambda i: (0, 0))],
        out_specs=pl.BlockSpec((bm // tile, n, tile), lambda i: (i, 0, 0)),
        out_shape=jax.ShapeDtypeStruct((m // tile, n, tile), BF16),
        compiler_params=_params(("parallel",)),
        name="v_proj_t",
    )(x, w_t)


def _softplus(x):
    return jnp.maximum(x, 0.0) + jnp.log(1.0 + jnp.exp(-jnp.abs(x)))


def _dt_kernel(x_ref, w_ref, b_ref, dt_ref, dtt_ref):
    x = x_ref[...]
    tm = x.shape[0]
    raw = _dot3(x, w_ref[...], (((1,), (0,)), ((), ())))
    dt = _softplus(raw + b_ref[...])
    dt_t = dt.T
    r = SSD_HEADS_PER_GROUP
    for g in range(SSD_GROUPS):
        dt_ref[g] = dt[:, g * r:(g + 1) * r]
        for j in range(tm // CHUNK):
            dtt_ref[g, j] = dt_t[g * r:(g + 1) * r, j * CHUNK:(j + 1) * CHUNK]


def _dt_call(h, w_dt, dt_bias, tm=512):
    t = h.shape[0]
    r = SSD_HEADS_PER_GROUP
    lanes = 128
    w_pad = jnp.pad(w_dt, ((0, 0), (0, lanes - SSD_HEADS)))
    b_pad = jnp.pad(dt_bias, (0, lanes - SSD_HEADS)).reshape(1, lanes)
    return pl.pallas_call(
        _dt_kernel,
        grid=(t // tm,),
        in_specs=[pl.BlockSpec((tm, D_MODEL), lambda i: (i, 0)),
                  pl.BlockSpec((D_MODEL, lanes), lambda i: (0, 0)),
                  pl.BlockSpec((1, lanes), lambda i: (0, 0))],
        out_specs=[pl.BlockSpec((SSD_GROUPS, tm, r), lambda i: (0, i, 0)),
                   pl.BlockSpec((SSD_GROUPS, tm // CHUNK, r, CHUNK), lambda i: (0, i, 0, 0))],
        out_shape=[jax.ShapeDtypeStruct((SSD_GROUPS, t, r), F32),
                   jax.ShapeDtypeStruct((SSD_GROUPS, t // CHUNK, r, CHUNK), F32)],
        compiler_params=_params(("parallel",)),
        name="dt_proj",
    )(h, w_pad, b_pad)


def _ssd_kernel(x_ref, b_ref, c_ref, wx_ref, wb_ref, wc_ref, bx_ref, bb_ref, bc_ref,
                dt_ref, dtt_ref, alr_ref, alc_ref, dsk_ref, bd_ref, trit_ref, e_ref,
                y_ref,
                state_ref, ux_ref, ub_ref, uc_ref, xc_ref, bcv_ref, ccv_ref, xdt_ref,
                wst_ref, eacs_ref, acs_ref, acst_ref):
    s_idx = pl.program_id(2)
    ts = x_ref.shape[0]
    nc = ts // CHUNK
    r = SSD_HEADS_PER_GROUP
    gw = SSD_GROUP_WIDTH
    n = SSD_STATE
    ng = dt_ref.shape[0]

    @pl.when(s_idx == 0)
    def _():
        state_ref[...] = jnp.zeros_like(state_ref)
        ux_ref[0:8, :] = jnp.zeros((8, ux_ref.shape[1]), F32)
        ub_ref[0:8, :] = jnp.zeros((8, ub_ref.shape[1]), F32)
        uc_ref[0:8, :] = jnp.zeros((8, uc_ref.shape[1]), F32)

    def conv_silu(raw_ref, u_ref, w_ref, bias_ref):
        u_ref[8:8 + ts, :] = raw_ref[...]
        u = u_ref[...]
        u2 = pltpu.roll(u, 2, axis=0)
        even = w_ref[3:4, :] * u + w_ref[1:2, :] * u2
        odd = w_ref[2:3, :] * u + w_ref[0:1, :] * u2
        acc = (even + pltpu.roll(odd, 1, axis=0))[8:8 + ts] + bias_ref[...]
        u_ref[0:8, :] = u_ref[ts:ts + 8, :]
        return acc * _sigmoid(acc)

    xc_ref[...] = conv_silu(x_ref, ux_ref, wx_ref, bx_ref)
    bcv_ref[...] = conv_silu(b_ref, ub_ref, wb_ref, bb_ref).astype(BF16)
    ccv_ref[...] = conv_silu(c_ref, uc_ref, wc_ref, bc_ref).astype(BF16)

    bd = bd_ref[...]
    trit = trit_ref[...]
    e_mat = e_ref[...]

    def expand(v):
        return jnp.dot(jnp.concatenate(_split2(v), axis=1), e_mat, preferred_element_type=F32)

    for g in range(ng):
        gl = slice(g * gw, (g + 1) * gw)
        a_row = -jnp.exp(alr_ref[g])
        a_col = -jnp.exp(alc_ref[g])
        dt = dt_ref[g]
        a = dt * a_row
        acs = jnp.zeros((ts, r), F32)
        for part in _split3(a):
            acs = acs + jnp.dot(bd, part, preferred_element_type=F32)
        acs_ref[g] = acs
        a_t = dtt_ref[g].reshape(nc * r, CHUNK) * jnp.concatenate([a_col] * nc, axis=0)
        acs_t = jnp.zeros((nc * r, CHUNK), F32)
        for part in _split3(a_t):
            acs_t = acs_t + jnp.dot(part, trit, preferred_element_type=F32)
        acst_ref[g] = acs_t
        a_last = jnp.concatenate(
            [jnp.broadcast_to(acs[c * CHUNK + CHUNK - 1:c * CHUNK + CHUNK, :], (CHUNK, r))
             for c in range(nc)], axis=0)
        xdt = xc_ref[:, gl] * expand(dt)
        xdt_ref[:, gl] = xdt.astype(BF16)
        wst_ref[:, gl] = (xdt * expand(jnp.exp(a_last - acs))).astype(BF16)
        eacs_ref[:, gl] = expand(jnp.exp(acs))

    row_i = lax.broadcasted_iota(jnp.int32, (CHUNK, CHUNK), 0)
    col_i = lax.broadcasted_iota(jnp.int32, (CHUNK, CHUNK), 1)
    tril = col_i <= row_i
    left_half = lax.broadcasted_iota(jnp.int32, (CHUNK, 128), 1) < SSD_HEAD_DIM

    def chunk_body(c, carry):
        rows = pl.ds(pl.multiple_of(c * CHUNK, CHUNK), CHUNK)
        for g in range(ng):
            cc = ccv_ref[rows, g * n:(g + 1) * n]
            bc = bcv_ref[rows, g * n:(g + 1) * n]
            cb = lax.dot_general(cc, bc, (((1,), (1,)), ((), ())), preferred_element_type=F32)
            state = state_ref[g]
            eacs = eacs_ref[rows, g * gw:(g + 1) * gw]
            y_off = jnp.dot(cc, state.astype(BF16), preferred_element_type=F32) * eacs
            acs_c = acs_ref[g, rows, :]
            acs_tc = acst_ref[g, pl.ds(pl.multiple_of(c * r, r), r), :]
            for p in range(r // 2):
                lo = g * gw + p * 128
                xp = xdt_ref[rows, lo:lo + 128]
                halves = []
                for hh in (2 * p, 2 * p + 1):
                    diff = acs_c[:, hh:hh + 1] - acs_tc[hh:hh + 1, :]
                    decay = jnp.exp(jnp.where(tril, diff, -jnp.inf))
                    halves.append(jnp.dot((cb * decay).astype(BF16), xp,
                                          preferred_element_type=F32))
                y_diag = jnp.where(left_half, halves[0], halves[1])
                y_ref[rows, lo:lo + 128] = (y_diag + y_off[:, p * 128:(p + 1) * 128]
                                            + dsk_ref[:, lo:lo + 128] * xc_ref[rows, lo:lo + 128])
            upd = lax.dot_general(bc, wst_ref[rows, g * gw:(g + 1) * gw], (((0,), (0,)), ((), ())),
                                  preferred_element_type=F32)
            state_ref[g] = state * eacs[CHUNK - 1:CHUNK, :] + upd
        return carry

    lax.fori_loop(0, nc, chunk_body, 0)


def _ssd_call(px, conv_w, conv_b, dt, dt_t, a_log, d_skip, batch, seq):
    t = batch * seq
    ts = SSD_TILE
    ns = seq // ts
    nc = ts // CHUNK
    r = SSD_HEADS_PER_GROUP
    ng = SSD_GROUPS_PER_STEP
    gw = ng * SSD_GROUP_WIDTH
    n = ng * SSD_STATE
    wb0 = SSD_D_INNER // n
    wc0 = wb0 + SSD_GROUPS // ng

    li = jnp.arange(ts)
    bd = ((li[None, :] <= li[:, None]) & (li[None, :] // CHUNK == li[:, None] // CHUNK)).astype(BF16)
    lc = jnp.arange(CHUNK)
    trit = (lc[:, None] <= lc[None, :]).astype(BF16)
    e_mat = (jnp.arange(SSD_GROUP_WIDTH)[None, :] // SSD_HEAD_DIM
             == jnp.arange(2 * r)[:, None] % r).astype(BF16)
    conv_b2 = conv_b.reshape(1, SSD_CONV_DIM)
    alr = a_log.reshape(SSD_GROUPS, 1, r)
    alc = a_log.reshape(SSD_GROUPS, r, 1)
    dsk = jnp.repeat(d_skip, SSD_HEAD_DIM).reshape(1, SSD_D_INNER)

    row = lambda b, g, s: b * ns + s
    in_specs = [
        pl.BlockSpec((ts, gw), lambda b, g, s: (row(b, g, s), g)),
        pl.BlockSpec((ts, n), lambda b, g, s: (row(b, g, s), wb0 + g)),
        pl.BlockSpec((ts, n), lambda b, g, s: (row(b, g, s), wc0 + g)),
        pl.BlockSpec((SSD_CONV, gw), lambda b, g, s: (0, g)),
        pl.BlockSpec((SSD_CONV, n), lambda b, g, s: (0, wb0 + g)),
        pl.BlockSpec((SSD_CONV, n), lambda b, g, s: (0, wc0 + g)),
        pl.BlockSpec((1, gw), lambda b, g, s: (0, g)),
        pl.BlockSpec((1, n), lambda b, g, s: (0, wb0 + g)),
        pl.BlockSpec((1, n), lambda b, g, s: (0, wc0 + g)),
        pl.BlockSpec((ng, ts, r), lambda b, g, s: (g, row(b, g, s), 0)),
        pl.BlockSpec((ng, nc, r, CHUNK), lambda b, g, s: (g, row(b, g, s), 0, 0)),
        pl.BlockSpec((ng, 1, r), lambda b, g, s: (g, 0, 0)),
        pl.BlockSpec((ng, r, 1), lambda b, g, s: (g, 0, 0)),
        pl.BlockSpec((1, gw), lambda b, g, s: (0, g)),
        pl.BlockSpec((ts, ts), lambda b, g, s: (0, 0)),
        pl.BlockSpec((CHUNK, CHUNK), lambda b, g, s: (0, 0)),
        pl.BlockSpec((2 * r, SSD_GROUP_WIDTH), lambda b, g, s: (0, 0)),
    ]
    scratch = [
        pltpu.VMEM((ng, SSD_STATE, SSD_GROUP_WIDTH), F32),
        pltpu.VMEM((ts + 8, gw), F32),
        pltpu.VMEM((ts + 8, n), F32),
        pltpu.VMEM((ts + 8, n), F32),
        pltpu.VMEM((ts, gw), F32),
        pltpu.VMEM((ts, n), BF16),
        pltpu.VMEM((ts, n), BF16),
        pltpu.VMEM((ts, gw), BF16),
        pltpu.VMEM((ts, gw), BF16),
        pltpu.VMEM((ts, gw), F32),
        pltpu.VMEM((ng, ts, r), F32),
        pltpu.VMEM((ng, nc * r, CHUNK), F32),
    ]
    return pl.pallas_call(
        _ssd_kernel,
        grid=(batch, SSD_GROUPS // ng, ns),
        in_specs=in_specs,
        out_specs=pl.BlockSpec((ts, gw), lambda b, g, s: (row(b, g, s), g)),
        out_shape=jax.ShapeDtypeStruct((t, SSD_D_INNER), F32),
        scratch_shapes=scratch,
        compiler_params=_params(("parallel", "parallel", "arbitrary")),
        name="ssd_scan",
    )(px, px, px, conv_w, conv_w, conv_w, conv_b2, conv_b2, conv_b2,
      dt, dt_t, alr, alc, dsk, bd, trit, e_mat)


def _attn_kernel(q_ref, k_ref, vt_ref, bias_ref, far_ref, lam_ref, g_ref, o_ref, sa_ref, sb_ref,
                 *, lam_init):
    i = pl.program_id(2)
    tq = q_ref.shape[0]
    dh = DIFF_HEAD_DIM
    lv = lam_ref[...]
    lam = (jnp.exp(jnp.sum(lv[0:1] * lv[1:2], axis=1, keepdims=True))
           - jnp.exp(jnp.sum(lv[2:3] * lv[3:4], axis=1, keepdims=True)) + lam_init)

    q = q_ref[...]
    hp = q.shape[1] // (2 * dh)
    ns = 2 * hp
    qs = [q[:, s * dh:(s + 1) * dh] for s in range(ns)]
    nt = (((1,), (1,)), ((), ()))

    n_tiles = bias_ref.shape[1] - 1
    h0 = pl.program_id(1) * hp
    ones = jnp.ones((16, tq), BF16)
    dv = 2 * dh

    def near_tile(t):
        valid = t <= i
        return jnp.where(valid, i - t, 0), jnp.where(valid, t, n_tiles)

    def far_tile(t):
        valid = t <= i
        j = jnp.where(valid, t - 2, 0)
        d = jnp.where(valid, i - j, 0)
        return j, [jnp.where(valid, far_ref[h0 + hh, d], NEG_BIG) for hh in range(hp)]

    def scores(j, s_ref, biases):
        kj = k_ref[pl.ds(pl.multiple_of(j * tq, tq), tq), :]
        for s in range(ns):
            sc = lax.dot_general(kj[:, s * dh:(s + 1) * dh], qs[s], nt,
                                 preferred_element_type=F32)
            s_ref[s] = sc if biases is None else sc + biases[s // 2]

    def update(j, s_ref, shifts, carry):
        vtj = vt_ref[j]
        out = []
        for s in range(ns):
            hh = s // 2
            vth = jnp.concatenate([vtj[hh * dv:(hh + 1) * dv], ones], axis=0)
            mx, acc = carry[2 * s:2 * s + 2]
            sc = s_ref[s]
            mx_new = jnp.maximum(mx, jnp.max(sc, axis=0, keepdims=True) + shifts[hh])
            p = jnp.exp2(sc - (mx_new - shifts[hh]))
            out += [mx_new, jnp.exp2(mx - mx_new) * acc
                    + jnp.dot(vth, p.astype(BF16), preferred_element_type=F32)]
        return tuple(out)

    def scores_near(t, s_ref):
        j, d = near_tile(t)
        scores(j, s_ref, [bias_ref[hh, d] for hh in range(hp)])

    def scores_far(t, s_ref):
        scores(far_tile(t)[0], s_ref, None)

    def update_near(t, s_ref, carry):
        return update(near_tile(t)[0], s_ref, [0.0] * hp, carry)

    def update_far(t, s_ref, carry):
        j, shifts = far_tile(t)
        return update(j, s_ref, shifts, carry)

    def body(u, carry):
        scores_far(2 * u + 1, sb_ref)
        carry = update_far(2 * u, sa_ref, carry)
        scores_far(2 * u + 2, sa_ref)
        return update_far(2 * u + 1, sb_ref, carry)

    carry = []
    for _ in range(ns):
        carry += [jnp.full((1, tq), NEG_BIG, F32), jnp.zeros((dv + 16, tq), F32)]
    scores_near(0, sa_ref)
    scores_near(1, sb_ref)
    carry = update_near(0, sa_ref, tuple(carry))
    scores_far(2, sa_ref)
    carry = update_near(1, sb_ref, carry)
    carry = lax.fori_loop(1, (i + 2) // 2, body, carry)
    for hh in range(hp):
        acc1, acc2 = carry[4 * hh + 1], carry[4 * hh + 3]
        o = acc1[:dv] / acc1[dv:dv + 1] - lam * (acc2[:dv] / acc2[dv:dv + 1])
        ms = jnp.mean(o * o, axis=0, keepdims=True)
        o = o * lax.rsqrt(ms + RMS_EPS) * g_ref[...] * (1.0 - lam_init)
        o_ref[:, hh * dv:(hh + 1) * dv] = o.T.astype(o_ref.dtype)


def _rel_bucket(rel):
    half = REL_BUCKETS // 2
    max_exact = half // 2
    ret = jnp.where(rel > 0, half, 0)
    n = jnp.abs(rel)
    nf = jnp.maximum(n, 1).astype(F32)
    large = max_exact + (jnp.log(nf / max_exact) / math.log(REL_MAX_DIST / max_exact)
                         * (half - max_exact)).astype(jnp.int32)
    large = jnp.minimum(large, half - 1)
    return ret + jnp.where(n < max_exact, n, large)


def _bias_lookup(rel_bias, rel):
    bucket = _rel_bucket(rel)[None]
    table = rel_bias.astype(F32)
    lead = (slice(None),) + (None,) * rel.ndim
    bias = jnp.zeros((DIFF_HEADS,) + rel.shape, F32)
    for b in range(REL_BUCKETS):
        bias = jnp.where(bucket == b, table[b][lead], bias)
    return bias * LOG2_E


def _bias_tiles(rel_bias, seq):
    tq = ATTN_TILE
    kk = jnp.arange(tq)[None, :, None]
    qq = jnp.arange(tq)[None, None, :]
    d = jnp.arange(2)[:, None, None]
    near = _bias_lookup(rel_bias, kk - qq - d * tq)
    allowed = (d > 0) | ((kk // CHUNK) <= (qq // CHUNK))
    near = jnp.where(allowed[None], near, NEG_BIG)
    masked = jnp.full((DIFF_HEADS, 1, tq, tq), NEG_BIG, F32)
    far = _bias_lookup(rel_bias, -tq * jnp.arange(seq // tq + 1))
    return jnp.concatenate([near, masked], axis=1), far


def _attn_call(pb, vt, bias_near, bias_far, lam_vecs, norm_g, layer_idx, batch, seq):
    t = batch * seq
    tq = ATTN_TILE
    nq = seq // tq
    hp = ATTN_HEADS_PER_STEP
    w = 2 * DIFF_HEAD_DIM
    wb = hp * w
    qcol0 = PB_Q // wb
    kcol0 = PB_K // wb
    lam_init = 0.8 - 0.6 * math.exp(-0.3 * layer_idx)
    return pl.pallas_call(
        functools.partial(_attn_kernel, lam_init=lam_init),
        grid=(batch, DIFF_HEADS // hp, nq),
        in_specs=[pl.BlockSpec((tq, wb), lambda b, h, i: (b * nq + i, qcol0 + h)),
                  pl.BlockSpec((seq, wb), lambda b, h, i: (b, kcol0 + h)),
                  pl.BlockSpec((nq, wb, tq), lambda b, h, i: (b, h, 0)),
                  pl.BlockSpec((hp, 3, tq, tq), lambda b, h, i: (h, 0, 0, 0)),
                  pl.BlockSpec(memory_space=pltpu.SMEM),
                  pl.BlockSpec((4, DIFF_HEAD_DIM), lambda b, h, i: (0, 0)),
                  pl.BlockSpec((w, 1), lambda b, h, i: (0, 0))],
        out_specs=pl.BlockSpec((tq, wb), lambda b, h, i: (b * nq + i, h)),
        out_shape=jax.ShapeDtypeStruct((t, DIFF_WIDTH), BF16),
        scratch_shapes=[pltpu.VMEM((2 * hp, tq, tq), F32), pltpu.VMEM((2 * hp, tq, tq), F32)],
        compiler_params=_params(("parallel", "parallel", "arbitrary")),
        name="diff_attn",
    )(pb, pb, vt, bias_near, bias_far, lam_vecs, norm_g.reshape(w, 1))


def _layer_norm(x, g, b):
    mu = jnp.mean(x, axis=1, keepdims=True)
    xc = x - mu
    var = jnp.mean(xc * xc, axis=1, keepdims=True)
    return xc * lax.rsqrt(var + LN_EPS) * g + b


def _mix_kernel(y_ref, z_ref, ao_ref, g0_ref, g1_ref, h_ref, ng_ref, wso_ref, wao_ref,
                gb_ref, wo_ref, lg_ref, lb_ref, o_ref):
    z = z_ref[...].astype(F32)
    yg = y_ref[...] * (z * _sigmoid(z))
    ms = jnp.mean(yg * yg, axis=1, keepdims=True)
    yn = (yg * lax.rsqrt(ms + RMS_EPS) * ng_ref[...]).astype(BF16)
    y_ssd = jnp.dot(yn, wso_ref[...], preferred_element_type=F32)
    y_att = jnp.dot(ao_ref[...], wao_ref[...], preferred_element_type=F32)
    gb = gb_ref[...]
    gate0 = _sigmoid(g0_ref[...].astype(F32) + gb[:, :D_MODEL])
    gate1 = _sigmoid(g1_ref[...].astype(F32) + gb[:, D_MODEL:])
    mixed = (gate0 * y_ssd + gate1 * y_att).astype(BF16)
    mix = jnp.dot(mixed, wo_ref[...], preferred_element_type=F32)
    o_ref[...] = _layer_norm(DN_ALPHA * h_ref[...] + mix, lg_ref[...], lb_ref[...])


def _mix_call(y, pb, ao, h, norm_g, w_ssd_out, w_attn_out, gate_b, w_o, ln_g, ln_b, tm=256):
    t = h.shape[0]
    d = D_MODEL
    gcol0 = PB_GATES // d
    const = lambda i: (0, 0)
    return pl.pallas_call(
        _mix_kernel,
        grid=(t // tm,),
        in_specs=[pl.BlockSpec((tm, SSD_D_INNER), lambda i: (i, 0)),
                  pl.BlockSpec((tm, SSD_D_INNER), lambda i: (i, 0)),
                  pl.BlockSpec((tm, DIFF_WIDTH), lambda i: (i, 0)),
                  pl.BlockSpec((tm, d), lambda i: (i, gcol0)),
                  pl.BlockSpec((tm, d), lambda i: (i, gcol0 + 1)),
                  pl.BlockSpec((tm, d), lambda i: (i, 0)),
                  pl.BlockSpec((1, SSD_D_INNER), const),
                  pl.BlockSpec((SSD_D_INNER, d), const),
                  pl.BlockSpec((DIFF_WIDTH, d), const),
                  pl.BlockSpec((1, 2 * d), const),
                  pl.BlockSpec((d, d), const),
                  pl.BlockSpec((1, d), const),
                  pl.BlockSpec((1, d), const)],
        out_specs=pl.BlockSpec((tm, d), lambda i: (i, 0)),
        out_shape=jax.ShapeDtypeStruct((t, d), F32),
        compiler_params=_params(("parallel",)),
        name="mix_ln",
    )(y, pb, ao, pb, pb, h, norm_g.reshape(1, -1), w_ssd_out.astype(BF16),
      w_attn_out.astype(BF16), gate_b.reshape(1, -1), w_o.astype(BF16),
      ln_g.reshape(1, -1), ln_b.reshape(1, -1))


def _router_kernel(h_ref, w_ref, b_ref, tri_ref, idx_ref, wt_ref, rank_ref, cnt_ref, run_ref):
    @pl.when(pl.program_id(0) == 0)
    def _():
        run_ref[...] = jnp.zeros_like(run_ref)

    tm = h_ref.shape[0]
    ne = N_EXPERTS
    logits = _dot3(h_ref[...], w_ref[...], (((1,), (0,)), ((), ()))) + b_ref[...]
    lane = lax.broadcasted_iota(jnp.int32, (tm, ne), 1).astype(F32)
    work = logits
    sel, vals = [], []
    for _ in range(TOP_K):
        mx = jnp.max(work, axis=1, keepdims=True)
        first = jnp.min(jnp.where(work == mx, lane, float(ne)), axis=1, keepdims=True)
        hit = lane == first
        sel.append((first, hit))
        vals.append(mx)
        work = jnp.where(hit, -jnp.inf, work)
    exps = [jnp.exp(v - vals[0]) for v in vals]
    denom = exps[0] + exps[1] + exps[2] + exps[3]

    onehot = jnp.zeros((tm, ne), F32)
    for _, hit in sel:
        onehot = onehot + hit.astype(F32)
    before = jnp.dot(tri_ref[...], onehot.astype(BF16), preferred_element_type=F32)
    before = before + run_ref[...]

    out_lane = lax.broadcasted_iota(jnp.int32, (tm, 128), 1)
    idx_out = jnp.zeros((tm, 128), jnp.int32)
    wt_out = jnp.zeros((tm, 128), F32)
    rank_out = jnp.zeros((tm, 128), jnp.int32)
    for k, (first, hit) in enumerate(sel):
        rank = jnp.sum(jnp.where(hit, before, 0.0), axis=1, keepdims=True)
        idx_out = jnp.where(out_lane == k, first.astype(jnp.int32), idx_out)
        wt_out = jnp.where(out_lane == k, exps[k] / denom, wt_out)
        rank_out = jnp.where(out_lane == k, rank.astype(jnp.int32), rank_out)
    idx_ref[...] = idx_out
    wt_ref[...] = wt_out
    rank_ref[...] = rank_out
    total = run_ref[...] + jnp.sum(onehot, axis=0, keepdims=True)
    run_ref[...] = total
    cnt_ref[...] = total


def _router_call(h, w_router, b_router, tm=512):
    t = h.shape[0]
    li = jnp.arange(tm)
    tri = (li[None, :] < li[:, None]).astype(BF16)
    const = lambda i: (0, 0)
    return pl.pallas_call(
        _router_kernel,
        grid=(t // tm,),
        in_specs=[pl.BlockSpec((tm, D_MODEL), lambda i: (i, 0)),
                  pl.BlockSpec((D_MODEL, N_EXPERTS), const),
                  pl.BlockSpec((1, N_EXPERTS), const),
                  pl.BlockSpec((tm, tm), const)],
        out_specs=[pl.BlockSpec((tm, 128), lambda i: (i, 0)),
                   pl.BlockSpec((tm, 128), lambda i: (i, 0)),
                   pl.BlockSpec((tm, 128), lambda i: (i, 0)),
                   pl.BlockSpec((1, N_EXPERTS), const)],
        out_shape=[jax.ShapeDtypeStruct((t, 128), jnp.int32),
                   jax.ShapeDtypeStruct((t, 128), F32),
                   jax.ShapeDtypeStruct((t, 128), jnp.int32),
                   jax.ShapeDtypeStruct((1, N_EXPERTS), F32)],
        scratch_shapes=[pltpu.VMEM((1, N_EXPERTS), F32)],
        compiler_params=_params(("arbitrary",)),
        name="router",
    )(h, w_router, b_router.reshape(1, N_EXPERTS), tri)


def _dispatch_kernel(pad_end_ref, slot_ref, x_ref, xs_out, zero_ref, sem, zero_sem):
    groups = x_ref.shape[0]
    n = groups * 8 * TOP_K
    zb = zero_ref.shape[0]

    @pl.when(pl.program_id(0) == 0)
    def _():
        zero_ref[...] = jnp.zeros_like(zero_ref)

        def clear(first):
            return pltpu.make_async_copy(
                zero_ref, xs_out.at[pl.ds(pl.multiple_of(first, zb), zb)], zero_sem)

        for e in range(N_EXPERTS):
            clear(jnp.maximum(pad_end_ref[e] - zb, 0)).start()
        used = pad_end_ref[N_EXPERTS - 1] // zb
        total = xs_out.shape[0] // zb

        def clear_tail(b, c):
            clear(b * zb).start()
            return c

        def wait_one(b, c):
            clear(0).wait()
            return c

        lax.fori_loop(used, total, clear_tail, 0)
        lax.fori_loop(used - N_EXPERTS, total, wait_one, 0)

    def start(r, c):
        for sub in range(8):
            for k in range(TOP_K):
                slot = slot_ref[0, 0, (r * 8 + sub) * TOP_K + k]
                pltpu.make_async_copy(x_ref.at[r, pl.ds(sub, 1)],
                                      xs_out.at[pl.ds(slot, 1)], sem).start()
        return c

    lax.fori_loop(0, groups, start, 0)
    pltpu.make_async_copy(xs_out.at[pl.ds(0, n)], xs_out.at[pl.ds(0, n)], sem).wait()


def _dispatch_call(x, slots, pad_end, cap):
    t, d = x.shape
    tt = DISPATCH_TILE
    nt = t // tt
    n = tt * TOP_K
    assert cap >= N_EXPERTS * EXPERT_BLOCK
    grid_spec = pltpu.PrefetchScalarGridSpec(
        num_scalar_prefetch=1,
        grid=(nt,),
        in_specs=[pl.BlockSpec((1, 1, n), lambda i, pe: (i, 0, 0), memory_space=pltpu.SMEM),
                  pl.BlockSpec((tt // 8, 8, d), lambda i, pe: (i, 0, 0))],
        out_specs=pl.BlockSpec(memory_space=pl.ANY),
        scratch_shapes=[pltpu.VMEM((EXPERT_BLOCK, d), x.dtype), pltpu.SemaphoreType.DMA,
                        pltpu.SemaphoreType.DMA],
    )
    return pl.pallas_call(
        _dispatch_kernel,
        grid_spec=grid_spec,
        out_shape=jax.ShapeDtypeStruct((cap, d), x.dtype),
        compiler_params=_params(("arbitrary",), disable_bounds_checks=True),
        name="moe_dispatch",
    )(pad_end, slots.reshape(nt, 1, n), x.reshape(t // 8, 8, d))


def _expert_kernel(be_ref, nb_ref, x_ref, wg_ref, bg_ref, wu_ref, bu_ref, wd_ref, bd_ref,
                   o_ref, wgb_ref, wub_ref, wdb_ref):
    i = pl.program_id(0)
    prev = be_ref[jnp.maximum(i - 1, 0)]
    changed = jnp.logical_or(i == 0, be_ref[i] != prev)

    @pl.when(changed)
    def _():
        wgb_ref[...] = wg_ref[0].astype(BF16)
        wub_ref[...] = wu_ref[0].astype(BF16)
        wdb_ref[...] = wd_ref[0].astype(BF16)

    @pl.when(i < nb_ref[0])
    def _():
        xb = x_ref[...].astype(BF16)
        g = jnp.dot(xb, wgb_ref[...], preferred_element_type=F32) + bg_ref[0]
        u = jnp.dot(xb, wub_ref[...], preferred_element_type=F32) + bu_ref[0]
        g = jnp.minimum(g, SWIGLU_LIMIT)
        u = jnp.clip(u, -SWIGLU_LIMIT, SWIGLU_LIMIT)
        act = g * _sigmoid(SWIGLU_ALPHA * g) * (u + 1.0)
        o_ref[...] = jnp.dot(act.astype(BF16), wdb_ref[...],
                             preferred_element_type=F32) + bd_ref[0]

    @pl.when(i >= nb_ref[0])
    def _():
        o_ref[...] = jnp.zeros_like(o_ref)


def _expert_call(xs, blk_expert, n_used, w_gate, b_gate, w_up, b_up, w_down, b_down):
    cap, d = xs.shape
    bm = EXPERT_BLOCK
    nb = cap // bm
    wspec = lambda shape: pl.BlockSpec(shape, lambda i, be, nu: (be[i], 0, 0))
    grid_spec = pltpu.PrefetchScalarGridSpec(
        num_scalar_prefetch=2,
        grid=(nb,),
        in_specs=[pl.BlockSpec((bm, d), lambda i, be, nu: (jnp.minimum(i, nu[0] - 1), 0)),
                  wspec((1, d, D_FF)), wspec((1, 1, D_FF)),
                  wspec((1, d, D_FF)), wspec((1, 1, D_FF)),
                  wspec((1, D_FF, d)), wspec((1, 1, d))],
        out_specs=pl.BlockSpec((bm, d), lambda i, be, nu: (i, 0)),
        scratch_shapes=[pltpu.VMEM((d, D_FF), BF16), pltpu.VMEM((d, D_FF), BF16),
                        pltpu.VMEM((D_FF, d), BF16)],
    )
    ne = w_gate.shape[0] * w_gate.shape[1]
    return pl.pallas_call(
        _expert_kernel,
        grid_spec=grid_spec,
        out_shape=jax.ShapeDtypeStruct((cap, d), F32),
        compiler_params=_params(("arbitrary",)),
        name="moe_experts",
    )(blk_expert, n_used, xs, w_gate.reshape(ne, d, D_FF), b_gate.reshape(ne, 1, D_FF),
      w_up.reshape(ne, d, D_FF), b_up.reshape(ne, 1, D_FF), w_down.reshape(ne, D_FF, d),
      b_down.reshape(ne, 1, d))


def _combine_kernel(slot_ref, next_slot_ref, ys_hbm, wt_ref, h_ref, lg_ref, lb_ref, o_ref,
                    buf_ref, sems):
    i = pl.program_id(0)
    tt = h_ref.shape[0]
    n = tt * TOP_K

    def issue(s_ref, buf):
        def start(r, c):
            for sub in range(8):
                pltpu.make_async_copy(ys_hbm.at[pl.ds(s_ref[0, 0, r * 8 + sub], 1)],
                                      buf_ref.at[buf, r, pl.ds(sub, 1)], sems.at[buf]).start()
            return c

        lax.fori_loop(0, n // 8, start, 0)

    @pl.when(i == 0)
    def _():
        issue(slot_ref, 0)

    @pl.when(i + 1 < pl.num_programs(0))
    def _():
        issue(next_slot_ref, (i + 1) % 2)

    cur = i % 2
    pltpu.make_async_copy(buf_ref.at[cur], buf_ref.at[cur], sems.at[cur]).wait()

    wt = wt_ref[...]
    d = h_ref.shape[1]
    rows = lambda k: buf_ref[cur, pl.ds(k * (tt // 8), tt // 8)].reshape(tt, d)
    ff = wt[:, 0:1] * rows(0)
    for k in range(1, TOP_K):
        ff = ff + wt[:, k:k + 1] * rows(k)
    o_ref[...] = _layer_norm(DN_ALPHA * h_ref[...] + ff, lg_ref[...], lb_ref[...])


def _combine_call(ys, slots, wts, h, ln_g, ln_b):
    t, d = h.shape
    tt = COMBINE_TILE
    nt = t // tt
    n = tt * TOP_K
    slots_km = slots.reshape(nt, tt, TOP_K).transpose(0, 2, 1).reshape(nt, 1, n)
    const = lambda i: (0, 0)
    return pl.pallas_call(
        _combine_kernel,
        grid=(nt,),
        in_specs=[pl.BlockSpec((1, 1, n), lambda i: (i, 0, 0), memory_space=pltpu.SMEM),
                  pl.BlockSpec((1, 1, n), lambda i: (jnp.minimum(i + 1, nt - 1), 0, 0),
                               memory_space=pltpu.SMEM),
                  pl.BlockSpec(memory_space=pl.ANY),
                  pl.BlockSpec((tt, 128), lambda i: (i, 0)),
                  pl.BlockSpec((tt, d), lambda i: (i, 0)),
                  pl.BlockSpec((1, d), const),
                  pl.BlockSpec((1, d), const)],
        out_specs=pl.BlockSpec((tt, d), lambda i: (i, 0)),
        out_shape=jax.ShapeDtypeStruct((t, d), F32),
        scratch_shapes=[pltpu.VMEM((2, n // 8, 8, d), F32), pltpu.SemaphoreType.DMA((2,))],
        compiler_params=_params(("arbitrary",), disable_bounds_checks=True),
        name="moe_combine_ln",
    )(slots_km, slots_km, ys, wts, h, ln_g.reshape(1, d), ln_b.reshape(1, d))


def _moe_layout(idx, rank, counts, n_blocks):
    bm = EXPERT_BLOCK
    counts = counts.reshape(N_EXPERTS).astype(jnp.int32)
    padded = (counts + bm - 1) // bm * bm
    pad_end = jnp.cumsum(padded)
    pad_start = pad_end - padded
    onehot = idx[:, :, None] == jnp.arange(N_EXPERTS, dtype=jnp.int32)[None, None, :]
    slots = rank + jnp.sum(jnp.where(onehot, pad_start[None, None, :], 0), axis=-1)
    blk_start = jnp.arange(n_blocks, dtype=jnp.int32) * bm
    blk_expert = jnp.sum((pad_end[None, :] <= blk_start[:, None]).astype(jnp.int32), axis=1)
    blk_expert = jnp.minimum(blk_expert, N_EXPERTS - 1)
    n_used = (pad_end[-1] // bm).reshape(1)
    return slots.astype(jnp.int32), blk_expert, n_used, pad_end.astype(jnp.int32)


def kernel(x, rel_bias, w_in, conv_w, conv_b, dt_bias, a_log, d_skip, ssd_norm_g, w_ssd_out, diff_lambda, diff_norm_g, w_attn_out, gate_b, w_o, ln1_g, ln1_b, w_router, b_router, w_gate, b_gate, w_up, b_up, w_down, b_down, ln2_g, ln2_b):
    batch, seq, d = x.shape
    t = batch * seq
    n_assign = t * TOP_K
    n_blocks = (n_assign + N_EXPERTS * (EXPERT_BLOCK - 1) + EXPERT_BLOCK - 1) // EXPERT_BLOCK
    cap = n_blocks * EXPERT_BLOCK
    bias_near, bias_far = _bias_tiles(rel_bias, seq)
    qk_scale = jnp.concatenate([jnp.full((DIFF_WIDTH,), DIFF_HEAD_DIM ** -0.5 * LOG2_E, F32),
                                jnp.ones((DIFF_WIDTH,), F32)])[None, :]

    h = x.reshape(t, d)
    for l in range(DEPTH):
        w_l = w_in[l]
        w_x = w_l[:, OFF_XBC:OFF_DT].astype(BF16)
        w_b = jnp.concatenate([w_l[:, :OFF_XBC], w_l[:, OFF_G:],
                               w_l[:, OFF_Q:OFF_V] * qk_scale], axis=1).astype(BF16)
        w_vt = w_l[:, OFF_V:OFF_G].T.astype(BF16)
        px = _matmul(h, w_x, F32, 1024, 1024)
        pb = _matmul(h, w_b, BF16, 1024, 1024)
        vt = _matmul_nt(h, w_vt, ATTN_TILE)
        dt, dt_t = _dt_call(h, w_l[:, OFF_DT:OFF_Q], dt_bias[l])
        y = _ssd_call(px, conv_w[l], conv_b[l], dt, dt_t, a_log[l], d_skip[l], batch, seq)
        ao = _attn_call(pb, vt, bias_near, bias_far, diff_lambda[l], diff_norm_g[l], l, batch,
                        seq)
        h1 = _mix_call(y, pb, ao, h, ssd_norm_g[l], w_ssd_out[l], w_attn_out[l], gate_b[l],
                       w_o[l], ln1_g[l], ln1_b[l])
        idx, wts, rank, counts = _router_call(h1, w_router[l], b_router[l])
        slots, blk_expert, n_used, pad_end = _moe_layout(idx[:, :TOP_K], rank[:, :TOP_K], counts,
                                                         n_blocks)
        xs = _dispatch_call(h1, slots, pad_end, cap)
        ys = _expert_call(xs, blk_expert + l * N_EXPERTS, n_used, w_gate, b_gate, w_up, b_up,
                          w_down, b_down)
        h = _combine_call(ys, slots, wts, h1, ln2_g[l], ln2_b[l])
    return h.reshape(batch, seq, d)
```

```python
import functools
import math

import jax
import jax.numpy as jnp
from jax import lax
from jax.experimental import pallas as pl
from jax.experimental.pallas import tpu as pltpu

F32 = jnp.float32
BF16 = jnp.bfloat16

D_MODEL = 1024
DEPTH = 2
CHUNK = 64

SSD_D_INNER = 2048
SSD_HEAD_DIM = 64
SSD_HEADS = 32
SSD_GROUPS = 4
SSD_HEADS_PER_GROUP = 8
SSD_STATE = 128
SSD_CONV = 4
SSD_CONV_DIM = 3072
SSD_GROUP_WIDTH = SSD_HEADS_PER_GROUP * SSD_HEAD_DIM

DIFF_HEAD_DIM = 64
DIFF_HEADS = 8
DIFF_WIDTH = 1024

REL_BUCKETS = 32
REL_MAX_DIST = 128

N_EXPERTS = 32
TOP_K = 4
D_FF = 1024
SWIGLU_ALPHA = 1.702
SWIGLU_LIMIT = 7.0

DN_ALPHA = (2 * DEPTH) ** 0.25
LN_EPS = 1e-5
RMS_EPS = 1e-5

OFF_XBC = 2048
OFF_DT = 5120
OFF_Q = 5152
OFF_V = 7200
OFF_G = 8224
IN_COLS = 10272
PB_GATES = 2048
PB_Q = 4096
PB_K = 5120

VMEM_LIMIT_BYTES = 56 * 1024 * 1024

ATTN_TILE = 256
ATTN_HEADS_PER_STEP = 2
SSD_TILE = 256
SSD_GROUPS_PER_STEP = 4
EXPERT_BLOCK = 256
DISPATCH_TILE = 1024
COMBINE_TILE = 256
NEG_BIG = -1e30
LOG2_E = math.log2(math.e)


def _params(semantics, **kwargs):
    return pltpu.CompilerParams(dimension_semantics=semantics,
                                vmem_limit_bytes=VMEM_LIMIT_BYTES, **kwargs)


def _sigmoid(x):
    return 1.0 / (1.0 + jnp.exp(-x))


def _split2(v):
    hi = v.astype(BF16)
    lo = (v - hi.astype(F32)).astype(BF16)
    return hi, lo


def _dot3(a, b, dims):
    ah, al = _split2(a)
    bh, bl = _split2(b)
    dot = lambda p, q: lax.dot_general(p, q, dims, preferred_element_type=F32)
    return dot(ah, bh) + dot(ah, bl) + dot(al, bh)


def _split3(v):
    hi = v.astype(BF16)
    r = v - hi.astype(F32)
    mid = r.astype(BF16)
    lo = (r - mid.astype(F32)).astype(BF16)
    return hi, mid, lo


def _mm_kernel(x_ref, w_ref, o_ref, xb_ref):
    @pl.when(pl.program_id(1) == 0)
    def _():
        xb_ref[...] = x_ref[...].astype(BF16)

    o_ref[...] = jnp.dot(xb_ref[...], w_ref[...],
                         preferred_element_type=F32).astype(o_ref.dtype)


def _matmul(x, w, out_dtype, bm, bn):
    m, k = x.shape
    n = w.shape[1]
    return pl.pallas_call(
        _mm_kernel,
        grid=(m // bm, n // bn),
        in_specs=[pl.BlockSpec((bm, k), lambda i, j: (i, 0)),
                  pl.BlockSpec((k, bn), lambda i, j: (0, j))],
        out_specs=pl.BlockSpec((bm, bn), lambda i, j: (i, j)),
        out_shape=jax.ShapeDtypeStruct((m, n), out_dtype),
        scratch_shapes=[pltpu.VMEM((bm, k), BF16)],
        compiler_params=_params(("parallel", "arbitrary")),
        name="in_proj",
    )(x, w)


def _mm_nt_kernel(x_ref, w_ref, o_ref, wt_ref):
    @pl.when(pl.program_id(0) == 0)
    def _():
        wt_ref[...] = w_ref[...].T.astype(BF16)

    res = lax.dot_general(wt_ref[...], x_ref[...].astype(BF16), (((1,), (1,)), ((), ())),
                          preferred_element_type=F32)
    tile = o_ref.shape[2]
    for c in range(o_ref.shape[0]):
        o_ref[c] = res[:, c * tile:(c + 1) * tile].astype(o_ref.dtype)


def _matmul_nt(x, w, tile, bm=1024):
    m, k = x.shape
    n = w.shape[1]
    return pl.pallas_call(
        _mm_nt_kernel,
        grid=(m // bm,),
        in_specs=[pl.BlockSpec((bm, k), lambda i: (i, 0)),
                  pl.BlockSpec((k, n), lambda i: (0, 0))],
        out_specs=pl.BlockSpec((bm // tile, n, tile), lambda i: (i, 0, 0)),
        out_shape=jax.ShapeDtypeStruct((m // tile, n, tile), BF16),
        scratch_shapes=[pltpu.VMEM((n, k), BF16)],
        compiler_params=_params(("arbitrary",)),
        name="v_proj_t",
    )(x, w)


def _softplus(x):
    return jnp.maximum(x, 0.0) + jnp.log(1.0 + jnp.exp(-jnp.abs(x)))


def _dt_kernel(x_ref, w_ref, b_ref, dt_ref, dtt_ref):
    x = x_ref[...]
    tm = x.shape[0]
    raw = _dot3(x, w_ref[...], (((1,), (0,)), ((), ())))
    dt = _softplus(raw + b_ref[...])
    dt_t = dt.T
    r = SSD_HEADS_PER_GROUP
    for g in range(SSD_GROUPS):
        dt_ref[g] = dt[:, g * r:(g + 1) * r]
        for j in range(tm // CHUNK):
            dtt_ref[g, j] = dt_t[g * r:(g + 1) * r, j * CHUNK:(j + 1) * CHUNK]


def _dt_call(h, w_dt, dt_bias, tm=512):
    t = h.shape[0]
    r = SSD_HEADS_PER_GROUP
    lanes = 128
    w_pad = jnp.pad(w_dt, ((0, 0), (0, lanes - SSD_HEADS)))
    b_pad = jnp.pad(dt_bias, (0, lanes - SSD_HEADS)).reshape(1, lanes)
    return pl.pallas_call(
        _dt_kernel,
        grid=(t // tm,),
        in_specs=[pl.BlockSpec((tm, D_MODEL), lambda i: (i, 0)),
                  pl.BlockSpec((D_MODEL, lanes), lambda i: (0, 0)),
                  pl.BlockSpec((1, lanes), lambda i: (0, 0))],
        out_specs=[pl.BlockSpec((SSD_GROUPS, tm, r), lambda i: (0, i, 0)),
                   pl.BlockSpec((SSD_GROUPS, tm // CHUNK, r, CHUNK), lambda i: (0, i, 0, 0))],
        out_shape=[jax.ShapeDtypeStruct((SSD_GROUPS, t, r), F32),
                   jax.ShapeDtypeStruct((SSD_GROUPS, t // CHUNK, r, CHUNK), F32)],
        compiler_params=_params(("parallel",)),
        name="dt_proj",
    )(h, w_pad, b_pad)


def _ssd_kernel(x_ref, b_ref, c_ref, wx_ref, wb_ref, wc_ref, bx_ref, bb_ref, bc_ref,
                dt_ref, dtt_ref, alr_ref, alc_ref, dsk_ref, bd_ref, trit_ref, e_ref,
                y_ref,
                state_ref, ux_ref, ub_ref, uc_ref, xc_ref, bcv_ref, ccv_ref, xdt_ref,
                wst_ref, eacs_ref, acs_ref, acst_ref):
    s_idx = pl.program_id(2)
    ts = x_ref.shape[0]
    nc = ts // CHUNK
    r = SSD_HEADS_PER_GROUP
    gw = SSD_GROUP_WIDTH
    n = SSD_STATE
    ng = dt_ref.shape[0]

    @pl.when(s_idx == 0)
    def _():
        state_ref[...] = jnp.zeros_like(state_ref)
        ux_ref[0:8, :] = jnp.zeros((8, ux_ref.shape[1]), F32)
        ub_ref[0:8, :] = jnp.zeros((8, ub_ref.shape[1]), F32)
        uc_ref[0:8, :] = jnp.zeros((8, uc_ref.shape[1]), F32)

    def conv_silu(raw_ref, u_ref, w_ref, bias_ref):
        u_ref[8:8 + ts, :] = raw_ref[...]
        u = u_ref[...]
        u2 = pltpu.roll(u, 2, axis=0)
        even = w_ref[3:4, :] * u + w_ref[1:2, :] * u2
        odd = w_ref[2:3, :] * u + w_ref[0:1, :] * u2
        acc = (even + pltpu.roll(odd, 1, axis=0))[8:8 + ts] + bias_ref[...]
        u_ref[0:8, :] = u_ref[ts:ts + 8, :]
        return acc * _sigmoid(acc)

    xc_ref[...] = conv_silu(x_ref, ux_ref, wx_ref, bx_ref)
    bcv_ref[...] = conv_silu(b_ref, ub_ref, wb_ref, bb_ref).astype(BF16)
    ccv_ref[...] = conv_silu(c_ref, uc_ref, wc_ref, bc_ref).astype(BF16)

    bd = bd_ref[...]
    trit = trit_ref[...]
    e_mat = e_ref[...]

    def expand(v):
        return jnp.dot(jnp.concatenate(_split2(v), axis=1), e_mat, preferred_element_type=F32)

    for g in range(ng):
        gl = slice(g * gw, (g + 1) * gw)
        a_row = -jnp.exp(alr_ref[g])
        a_col = -jnp.exp(alc_ref[g])
        dt = dt_ref[g]
        a = dt * a_row
        acs = jnp.zeros((ts, r), F32)
        for part in _split3(a):
            acs = acs + jnp.dot(bd, part, preferred_element_type=F32)
        acs_ref[g] = acs
        a_t = dtt_ref[g].reshape(nc * r, CHUNK) * jnp.concatenate([a_col] * nc, axis=0)
        acs_t = jnp.zeros((nc * r, CHUNK), F32)
        for part in _split3(a_t):
            acs_t = acs_t + jnp.dot(part, trit, preferred_element_type=F32)
        acst_ref[g] = acs_t
        a_last = jnp.concatenate(
            [jnp.broadcast_to(acs[c * CHUNK + CHUNK - 1:c * CHUNK + CHUNK, :], (CHUNK, r))
             for c in range(nc)], axis=0)
        xdt = xc_ref[:, gl] * expand(dt)
        xdt_ref[:, gl] = xdt.astype(BF16)
        wst_ref[:, gl] = (xdt * expand(jnp.exp(a_last - acs))).astype(BF16)
        eacs_ref[:, gl] = expand(jnp.exp(acs))

    row_i = lax.broadcasted_iota(jnp.int32, (CHUNK, CHUNK), 0)
    col_i = lax.broadcasted_iota(jnp.int32, (CHUNK, CHUNK), 1)
    tril = col_i <= row_i
    left_half = lax.broadcasted_iota(jnp.int32, (CHUNK, 128), 1) < SSD_HEAD_DIM

    def chunk_body(c, carry):
        rows = pl.ds(pl.multiple_of(c * CHUNK, CHUNK), CHUNK)
        for g in range(ng):
            cc = ccv_ref[rows, g * n:(g + 1) * n]
            bc = bcv_ref[rows, g * n:(g + 1) * n]
            cb = lax.dot_general(cc, bc, (((1,), (1,)), ((), ())), preferred_element_type=F32)
            state = state_ref[g]
            eacs = eacs_ref[rows, g * gw:(g + 1) * gw]
            y_off = jnp.dot(cc, state.astype(BF16), preferred_element_type=F32) * eacs
            acs_c = acs_ref[g, rows, :]
            acs_tc = acst_ref[g, pl.ds(pl.multiple_of(c * r, r), r), :]
            for p in range(r // 2):
                lo = g * gw + p * 128
                xp = xdt_ref[rows, lo:lo + 128]
                halves = []
                for hh in (2 * p, 2 * p + 1):
                    diff = acs_c[:, hh:hh + 1] - acs_tc[hh:hh + 1, :]
                    decay = jnp.exp(jnp.where(tril, diff, -jnp.inf))
                    halves.append(jnp.dot((cb * decay).astype(BF16), xp,
                                          preferred_element_type=F32))
                y_diag = jnp.where(left_half, halves[0], halves[1])
                y_ref[rows, lo:lo + 128] = (y_diag + y_off[:, p * 128:(p + 1) * 128]
                                            + dsk_ref[:, lo:lo + 128] * xc_ref[rows, lo:lo + 128])
            upd = lax.dot_general(bc, wst_ref[rows, g * gw:(g + 1) * gw], (((0,), (0,)), ((), ())),
                                  preferred_element_type=F32)
            state_ref[g] = state * eacs[CHUNK - 1:CHUNK, :] + upd
        return carry

    lax.fori_loop(0, nc, chunk_body, 0)


def _ssd_call(px, conv_w, conv_b, dt, dt_t, a_log, d_skip, batch, seq):
    t = batch * seq
    ts = SSD_TILE
    ns = seq // ts
    nc = ts // CHUNK
    r = SSD_HEADS_PER_GROUP
    ng = SSD_GROUPS_PER_STEP
    gw = ng * SSD_GROUP_WIDTH
    n = ng * SSD_STATE
    wb0 = SSD_D_INNER // n
    wc0 = wb0 + SSD_GROUPS // ng

    li = jnp.arange(ts)
    bd = ((li[None, :] <= li[:, None]) & (li[None, :] // CHUNK == li[:, None] // CHUNK)).astype(BF16)
    lc = jnp.arange(CHUNK)
    trit = (lc[:, None] <= lc[None, :]).astype(BF16)
    e_mat = (jnp.arange(SSD_GROUP_WIDTH)[None, :] // SSD_HEAD_DIM
             == jnp.arange(2 * r)[:, None] % r).astype(BF16)
    conv_b2 = conv_b.reshape(1, SSD_CONV_DIM)
    alr = a_log.reshape(SSD_GROUPS, 1, r)
    alc = a_log.reshape(SSD_GROUPS, r, 1)
    dsk = jnp.repeat(d_skip, SSD_HEAD_DIM).reshape(1, SSD_D_INNER)

    row = lambda b, g, s: b * ns + s
    in_specs = [
        pl.BlockSpec((ts, gw), lambda b, g, s: (row(b, g, s), g)),
        pl.BlockSpec((ts, n), lambda b, g, s: (row(b, g, s), wb0 + g)),
        pl.BlockSpec((ts, n), lambda b, g, s: (row(b, g, s), wc0 + g)),
        pl.BlockSpec((SSD_CONV, gw), lambda b, g, s: (0, g)),
        pl.BlockSpec((SSD_CONV, n), lambda b, g, s: (0, wb0 + g)),
        pl.BlockSpec((SSD_CONV, n), lambda b, g, s: (0, wc0 + g)),
        pl.BlockSpec((1, gw), lambda b, g, s: (0, g)),
        pl.BlockSpec((1, n), lambda b, g, s: (0, wb0 + g)),
        pl.BlockSpec((1, n), lambda b, g, s: (0, wc0 + g)),
        pl.BlockSpec((ng, ts, r), lambda b, g, s: (g, row(b, g, s), 0)),
        pl.BlockSpec((ng, nc, r, CHUNK), lambda b, g, s: (g, row(b, g, s), 0, 0)),
        pl.BlockSpec((ng, 1, r), lambda b, g, s: (g, 0, 0)),
        pl.BlockSpec((ng, r, 1), lambda b, g, s: (g, 0, 0)),
        pl.BlockSpec((1, gw), lambda b, g, s: (0, g)),
        pl.BlockSpec((ts, ts), lambda b, g, s: (0, 0)),
        pl.BlockSpec((CHUNK, CHUNK), lambda b, g, s: (0, 0)),
        pl.BlockSpec((2 * r, SSD_GROUP_WIDTH), lambda b, g, s: (0, 0)),
    ]
    scratch = [
        pltpu.VMEM((ng, SSD_STATE, SSD_GROUP_WIDTH), F32),
        pltpu.VMEM((ts + 8, gw), F32),
        pltpu.VMEM((ts + 8, n), F32),
        pltpu.VMEM((ts + 8, n), F32),
        pltpu.VMEM((ts, gw), F32),
        pltpu.VMEM((ts, n), BF16),
        pltpu.VMEM((ts, n), BF16),
        pltpu.VMEM((ts, gw), BF16),
        pltpu.VMEM((ts, gw), BF16),
        pltpu.VMEM((ts, gw), F32),
        pltpu.VMEM((ng, ts, r), F32),
        pltpu.VMEM((ng, nc * r, CHUNK), F32),
    ]
    return pl.pallas_call(
        _ssd_kernel,
        grid=(batch, SSD_GROUPS // ng, ns),
        in_specs=in_specs,
        out_specs=pl.BlockSpec((ts, gw), lambda b, g, s: (row(b, g, s), g)),
        out_shape=jax.ShapeDtypeStruct((t, SSD_D_INNER), F32),
        scratch_shapes=scratch,
        compiler_params=_params(("parallel", "parallel", "arbitrary")),
        name="ssd_scan",
    )(px, px, px, conv_w, conv_w, conv_w, conv_b2, conv_b2, conv_b2,
      dt, dt_t, alr, alc, dsk, bd, trit, e_mat)


def _attn_kernel(q_ref, k_ref, vt_ref, bias_ref, far_ref, lam_ref, g_ref, o_ref, sa_ref, sb_ref,
                 *, lam_init):
    i = pl.program_id(2)
    tq = q_ref.shape[0]
    dh = DIFF_HEAD_DIM
    lv = lam_ref[...]
    lam = (jnp.exp(jnp.sum(lv[0:1] * lv[1:2], axis=1, keepdims=True))
           - jnp.exp(jnp.sum(lv[2:3] * lv[3:4], axis=1, keepdims=True)) + lam_init)

    q = q_ref[...]
    hp = q.shape[1] // (2 * dh)
    ns = 2 * hp
    qs = [q[:, s * dh:(s + 1) * dh] for s in range(ns)]
    nt = (((1,), (1,)), ((), ()))

    n_tiles = bias_ref.shape[1] - 1
    h0 = pl.program_id(1) * hp
    ones = jnp.ones((16, tq), BF16)
    dv = 2 * dh

    def near_tile(t):
        valid = t <= i
        return jnp.where(valid, i - t, 0), jnp.where(valid, t, n_tiles)

    def far_tile(t):
        valid = t <= i
        j = jnp.where(valid, t - 2, 0)
        d = jnp.where(valid, i - j, 0)
        return j, [jnp.where(valid, far_ref[h0 + hh, d], NEG_BIG) for hh in range(hp)]

    def scores(j, s_ref, biases):
        kj = k_ref[pl.ds(pl.multiple_of(j * tq, tq), tq), :]
        for s in range(ns):
            sc = lax.dot_general(kj[:, s * dh:(s + 1) * dh], qs[s], nt,
                                 preferred_element_type=F32)
            s_ref[s] = sc if biases is None else sc + biases[s // 2]

    def update(j, s_ref, shifts, carry):
        vtj = vt_ref[j]
        out = []
        for s in range(ns):
            hh = s // 2
            vth = jnp.concatenate([vtj[hh * dv:(hh + 1) * dv], ones], axis=0)
            mx, acc = carry[2 * s:2 * s + 2]
            sc = s_ref[s]
            mx_new = jnp.maximum(mx, jnp.max(sc, axis=0, keepdims=True) + shifts[hh])
            p = jnp.exp2(sc - (mx_new - shifts[hh]))
            out += [mx_new, jnp.exp2(mx - mx_new) * acc
                    + jnp.dot(vth, p.astype(BF16), preferred_element_type=F32)]
        return tuple(out)

    def scores_near(t, s_ref):
        j, d = near_tile(t)
        scores(j, s_ref, [bias_ref[hh, d] for hh in range(hp)])

    def scores_far(t, s_ref):
        scores(far_tile(t)[0], s_ref, None)

    def update_near(t, s_ref, carry):
        return update(near_tile(t)[0], s_ref, [0.0] * hp, carry)

    def update_far(t, s_ref, carry):
        j, shifts = far_tile(t)
        return update(j, s_ref, shifts, carry)

    def body(u, carry):
        scores_far(2 * u + 1, sb_ref)
        carry = update_far(2 * u, sa_ref, carry)
        scores_far(2 * u + 2, sa_ref)
        return update_far(2 * u + 1, sb_ref, carry)

    carry = []
    for _ in range(ns):
        carry += [jnp.full((1, tq), NEG_BIG, F32), jnp.zeros((dv + 16, tq), F32)]
    scores_near(0, sa_ref)
    scores_near(1, sb_ref)
    carry = update_near(0, sa_ref, tuple(carry))
    scores_far(2, sa_ref)
    carry = update_near(1, sb_ref, carry)
    carry = lax.fori_loop(1, (i + 2) // 2, body, carry)
    for hh in range(hp):
        acc1, acc2 = carry[4 * hh + 1], carry[4 * hh + 3]
        o = acc1[:dv] / acc1[dv:dv + 1] - lam * (acc2[:dv] / acc2[dv:dv + 1])
        ms = jnp.mean(o * o, axis=0, keepdims=True)
        o = o * lax.rsqrt(ms + RMS_EPS) * g_ref[...] * (1.0 - lam_init)
        o_ref[:, hh * dv:(hh + 1) * dv] = o.T.astype(o_ref.dtype)


def _rel_bucket(rel):
    half = REL_BUCKETS // 2
    max_exact = half // 2
    ret = jnp.where(rel > 0, half, 0)
    n = jnp.abs(rel)
    nf = jnp.maximum(n, 1).astype(F32)
    large = max_exact + (jnp.log(nf / max_exact) / math.log(REL_MAX_DIST / max_exact)
                         * (half - max_exact)).astype(jnp.int32)
    large = jnp.minimum(large, half - 1)
    return ret + jnp.where(n < max_exact, n, large)


def _bias_lookup(rel_bias, rel):
    bucket = _rel_bucket(rel)[None]
    table = rel_bias.astype(F32)
    lead = (slice(None),) + (None,) * rel.ndim
    bias = jnp.zeros((DIFF_HEADS,) + rel.shape, F32)
    for b in range(REL_BUCKETS):
        bias = jnp.where(bucket == b, table[b][lead], bias)
    return bias * LOG2_E


def _bias_tiles(rel_bias, seq):
    tq = ATTN_TILE
    kk = jnp.arange(tq)[None, :, None]
    qq = jnp.arange(tq)[None, None, :]
    d = jnp.arange(2)[:, None, None]
    near = _bias_lookup(rel_bias, kk - qq - d * tq)
    allowed = (d > 0) | ((kk // CHUNK) <= (qq // CHUNK))
    near = jnp.where(allowed[None], near, NEG_BIG)
    masked = jnp.full((DIFF_HEADS, 1, tq, tq), NEG_BIG, F32)
    far = _bias_lookup(rel_bias, -tq * jnp.arange(seq // tq + 1))
    return jnp.concatenate([near, masked], axis=1), far


def _attn_call(pb, vt, bias_near, bias_far, lam_vecs, norm_g, layer_idx, batch, seq):
    t = batch * seq
    tq = ATTN_TILE
    nq = seq // tq
    hp = ATTN_HEADS_PER_STEP
    w = 2 * DIFF_HEAD_DIM
    wb = hp * w
    qcol0 = PB_Q // wb
    kcol0 = PB_K // wb
    lam_init = 0.8 - 0.6 * math.exp(-0.3 * layer_idx)
    return pl.pallas_call(
        functools.partial(_attn_kernel, lam_init=lam_init),
        grid=(batch, DIFF_HEADS // hp, nq),
        in_specs=[pl.BlockSpec((tq, wb), lambda b, h, i: (b * nq + i, qcol0 + h)),
                  pl.BlockSpec((seq, wb), lambda b, h, i: (b, kcol0 + h)),
                  pl.BlockSpec((nq, wb, tq), lambda b, h, i: (b, h, 0)),
                  pl.BlockSpec((hp, 3, tq, tq), lambda b, h, i: (h, 0, 0, 0)),
                  pl.BlockSpec(memory_space=pltpu.SMEM),
                  pl.BlockSpec((4, DIFF_HEAD_DIM), lambda b, h, i: (0, 0)),
                  pl.BlockSpec((w, 1), lambda b, h, i: (0, 0))],
        out_specs=pl.BlockSpec((tq, wb), lambda b, h, i: (b * nq + i, h)),
        out_shape=jax.ShapeDtypeStruct((t, DIFF_WIDTH), BF16),
        scratch_shapes=[pltpu.VMEM((2 * hp, tq, tq), F32), pltpu.VMEM((2 * hp, tq, tq), F32)],
        compiler_params=_params(("parallel", "parallel", "arbitrary")),
        name="diff_attn",
    )(pb, pb, vt, bias_near, bias_far, lam_vecs, norm_g.reshape(w, 1))


def _layer_norm(x, g, b):
    mu = jnp.mean(x, axis=1, keepdims=True)
    xc = x - mu
    var = jnp.mean(xc * xc, axis=1, keepdims=True)
    return xc * lax.rsqrt(var + LN_EPS) * g + b


def _mix_kernel(y_ref, z_ref, ao_ref, g0_ref, g1_ref, h_ref, ng_ref, wso_ref, wao_ref,
                gb_ref, wo_ref, lg_ref, lb_ref, o_ref):
    z = z_ref[...].astype(F32)
    yg = y_ref[...] * (z * _sigmoid(z))
    ms = jnp.mean(yg * yg, axis=1, keepdims=True)
    yn = (yg * lax.rsqrt(ms + RMS_EPS) * ng_ref[...]).astype(BF16)
    y_ssd = jnp.dot(yn, wso_ref[...], preferred_element_type=F32)
    y_att = jnp.dot(ao_ref[...], wao_ref[...], preferred_element_type=F32)
    gb = gb_ref[...]
    gate0 = _sigmoid(g0_ref[...].astype(F32) + gb[:, :D_MODEL])
    gate1 = _sigmoid(g1_ref[...].astype(F32) + gb[:, D_MODEL:])
    mixed = (gate0 * y_ssd + gate1 * y_att).astype(BF16)
    mix = jnp.dot(mixed, wo_ref[...], preferred_element_type=F32)
    o_ref[...] = _layer_norm(DN_ALPHA * h_ref[...] + mix, lg_ref[...], lb_ref[...])


def _mix_call(y, pb, ao, h, norm_g, w_ssd_out, w_attn_out, gate_b, w_o, ln_g, ln_b, tm=256):
    t = h.shape[0]
    d = D_MODEL
    gcol0 = PB_GATES // d
    const = lambda i: (0, 0)
    return pl.pallas_call(
        _mix_kernel,
        grid=(t // tm,),
        in_specs=[pl.BlockSpec((tm, SSD_D_INNER), lambda i: (i, 0)),
                  pl.BlockSpec((tm, SSD_D_INNER), lambda i: (i, 0)),
                  pl.BlockSpec((tm, DIFF_WIDTH), lambda i: (i, 0)),
                  pl.BlockSpec((tm, d), lambda i: (i, gcol0)),
                  pl.BlockSpec((tm, d), lambda i: (i, gcol0 + 1)),
                  pl.BlockSpec((tm, d), lambda i: (i, 0)),
                  pl.BlockSpec((1, SSD_D_INNER), const),
                  pl.BlockSpec((SSD_D_INNER, d), const),
                  pl.BlockSpec((DIFF_WIDTH, d), const),
                  pl.BlockSpec((1, 2 * d), const),
                  pl.BlockSpec((d, d), const),
                  pl.BlockSpec((1, d), const),
                  pl.BlockSpec((1, d), const)],
        out_specs=pl.BlockSpec((tm, d), lambda i: (i, 0)),
        out_shape=jax.ShapeDtypeStruct((t, d), F32),
        compiler_params=_params(("parallel",)),
        name="mix_ln",
    )(y, pb, ao, pb, pb, h, norm_g.reshape(1, -1), w_ssd_out.astype(BF16),
      w_attn_out.astype(BF16), gate_b.reshape(1, -1), w_o.astype(BF16),
      ln_g.reshape(1, -1), ln_b.reshape(1, -1))


def _router_kernel(h_ref, w_ref, b_ref, tri_ref, idx_ref, wt_ref, rank_ref, cnt_ref, run_ref):
    @pl.when(pl.program_id(0) == 0)
    def _():
        run_ref[...] = jnp.zeros_like(run_ref)

    tm = h_ref.shape[0]
    ne = N_EXPERTS
    logits = _dot3(h_ref[...], w_ref[...], (((1,), (0,)), ((), ()))) + b_ref[...]
    lane = lax.broadcasted_iota(jnp.int32, (tm, ne), 1).astype(F32)
    work = logits
    sel, vals = [], []
    for _ in range(TOP_K):
        mx = jnp.max(work, axis=1, keepdims=True)
        first = jnp.min(jnp.where(work == mx, lane, float(ne)), axis=1, keepdims=True)
        hit = lane == first
        sel.append((first, hit))
        vals.append(mx)
        work = jnp.where(hit, -jnp.inf, work)
    exps = [jnp.exp(v - vals[0]) for v in vals]
    denom = exps[0] + exps[1] + exps[2] + exps[3]

    onehot = jnp.zeros((tm, ne), F32)
    for _, hit in sel:
        onehot = onehot + hit.astype(F32)
    before = jnp.dot(tri_ref[...], onehot.astype(BF16), preferred_element_type=F32)
    before = before + run_ref[...]

    out_lane = lax.broadcasted_iota(jnp.int32, (tm, 128), 1)
    idx_out = jnp.zeros((tm, 128), F32)
    wt_out = jnp.zeros((tm, 128), F32)
    rank_out = jnp.zeros((tm, 128), F32)
    for k, (first, hit) in enumerate(sel):
        rank = jnp.sum(jnp.where(hit, before, 0.0), axis=1, keepdims=True)
        idx_out = jnp.where(out_lane == k, first, idx_out)
        wt_out = jnp.where(out_lane == k, exps[k] / denom, wt_out)
        rank_out = jnp.where(out_lane == k, rank, rank_out)
    idx_ref[...] = idx_out.T[0:8].astype(jnp.int32)
    wt_ref[...] = wt_out
    rank_ref[...] = rank_out.T[0:8].astype(jnp.int32)
    total = run_ref[...] + jnp.sum(onehot, axis=0, keepdims=True)
    run_ref[...] = total
    cnt_ref[...] = total


def _router_call(h, w_router, b_router, tm=512):
    t = h.shape[0]
    li = jnp.arange(tm)
    tri = (li[None, :] < li[:, None]).astype(BF16)
    const = lambda i: (0, 0)
    return pl.pallas_call(
        _router_kernel,
        grid=(t // tm,),
        in_specs=[pl.BlockSpec((tm, D_MODEL), lambda i: (i, 0)),
                  pl.BlockSpec((D_MODEL, N_EXPERTS), const),
                  pl.BlockSpec((1, N_EXPERTS), const),
                  pl.BlockSpec((tm, tm), const)],
        out_specs=[pl.BlockSpec((8, tm), lambda i: (0, i)),
                   pl.BlockSpec((tm, 128), lambda i: (i, 0)),
                   pl.BlockSpec((8, tm), lambda i: (0, i)),
                   pl.BlockSpec((1, N_EXPERTS), const)],
        out_shape=[jax.ShapeDtypeStruct((8, t), jnp.int32),
                   jax.ShapeDtypeStruct((t, 128), F32),
                   jax.ShapeDtypeStruct((8, t), jnp.int32),
                   jax.ShapeDtypeStruct((1, N_EXPERTS), F32)],
        scratch_shapes=[pltpu.VMEM((1, N_EXPERTS), F32)],
        compiler_params=_params(("arbitrary",)),
        name="router",
    )(h, w_router, b_router.reshape(1, N_EXPERTS), tri)


def _dispatch_kernel(pad_end_ref, slot_ref, x_ref, xs_out, zero_ref, sem, zero_sem):
    groups = x_ref.shape[0]
    n = groups * 8 * TOP_K
    zb = zero_ref.shape[0]

    @pl.when(pl.program_id(0) == 0)
    def _():
        zero_ref[...] = jnp.zeros_like(zero_ref)

        def clear(first):
            return pltpu.make_async_copy(
                zero_ref, xs_out.at[pl.ds(pl.multiple_of(first, zb), zb)], zero_sem)

        for e in range(N_EXPERTS):
            clear(jnp.maximum(pad_end_ref[e] - zb, 0)).start()
        used = pad_end_ref[N_EXPERTS - 1] // zb
        total = xs_out.shape[0] // zb

        def clear_tail(b, c):
            clear(b * zb).start()
            return c

        def wait_one(b, c):
            clear(0).wait()
            return c

        lax.fori_loop(used, total, clear_tail, 0)
        lax.fori_loop(used - N_EXPERTS, total, wait_one, 0)

    def start(r, c):
        for sub in range(8):
            for k in range(TOP_K):
                slot = slot_ref[0, 0, k * (groups * 8) + r * 8 + sub]
                pltpu.make_async_copy(x_ref.at[r, pl.ds(sub, 1)],
                                      xs_out.at[pl.ds(slot, 1)], sem).start()
        return c

    lax.fori_loop(0, groups, start, 0)
    pltpu.make_async_copy(xs_out.at[pl.ds(0, n)], xs_out.at[pl.ds(0, n)], sem).wait()


def _dispatch_call(x, slots, pad_end, cap):
    t, d = x.shape
    tt = DISPATCH_TILE
    nt = t // tt
    n = tt * TOP_K
    assert cap >= N_EXPERTS * EXPERT_BLOCK
    grid_spec = pltpu.PrefetchScalarGridSpec(
        num_scalar_prefetch=1,
        grid=(nt,),
        in_specs=[pl.BlockSpec((1, 1, n), lambda i, pe: (i, 0, 0), memory_space=pltpu.SMEM),
                  pl.BlockSpec((tt // 8, 8, d), lambda i, pe: (i, 0, 0))],
        out_specs=pl.BlockSpec(memory_space=pl.ANY),
        scratch_shapes=[pltpu.VMEM((EXPERT_BLOCK, d), x.dtype), pltpu.SemaphoreType.DMA,
                        pltpu.SemaphoreType.DMA],
    )
    return pl.pallas_call(
        _dispatch_kernel,
        grid_spec=grid_spec,
        out_shape=jax.ShapeDtypeStruct((cap, d), x.dtype),
        compiler_params=_params(("arbitrary",), disable_bounds_checks=True),
        name="moe_dispatch",
    )(pad_end, _tile_slots(slots, tt), x.reshape(t // 8, 8, d))


def _expert_kernel(be_ref, nb_ref, x_ref, wg_ref, bg_ref, wu_ref, bu_ref, wd_ref, bd_ref,
                   o_ref, wgb_ref, wub_ref, wdb_ref):
    i = pl.program_id(0)
    prev = be_ref[jnp.maximum(i - 1, 0)]
    changed = jnp.logical_or(i == 0, be_ref[i] != prev)

    @pl.when(changed)
    def _():
        wgb_ref[...] = wg_ref[0].astype(BF16)
        wub_ref[...] = wu_ref[0].astype(BF16)
        wdb_ref[...] = wd_ref[0].astype(BF16)

    @pl.when(i < nb_ref[0])
    def _():
        xb = x_ref[...].astype(BF16)
        g = jnp.dot(xb, wgb_ref[...], preferred_element_type=F32) + bg_ref[0]
        u = jnp.dot(xb, wub_ref[...], preferred_element_type=F32) + bu_ref[0]
        g = jnp.minimum(g, SWIGLU_LIMIT)
        u = jnp.clip(u, -SWIGLU_LIMIT, SWIGLU_LIMIT)
        act = g * _sigmoid(SWIGLU_ALPHA * g) * (u + 1.0)
        o_ref[...] = jnp.dot(act.astype(BF16), wdb_ref[...],
                             preferred_element_type=F32) + bd_ref[0]

    @pl.when(i >= nb_ref[0])
    def _():
        o_ref[...] = jnp.zeros_like(o_ref)


def _expert_call(xs, blk_expert, n_used, w_gate, b_gate, w_up, b_up, w_down, b_down):
    cap, d = xs.shape
    bm = EXPERT_BLOCK
    nb = cap // bm
    wspec = lambda shape: pl.BlockSpec(shape, lambda i, be, nu: (be[i], 0, 0))
    grid_spec = pltpu.PrefetchScalarGridSpec(
        num_scalar_prefetch=2,
        grid=(nb,),
        in_specs=[pl.BlockSpec((bm, d), lambda i, be, nu: (jnp.minimum(i, nu[0] - 1), 0)),
                  wspec((1, d, D_FF)), wspec((1, 1, D_FF)),
                  wspec((1, d, D_FF)), wspec((1, 1, D_FF)),
                  wspec((1, D_FF, d)), wspec((1, 1, d))],
        out_specs=pl.BlockSpec((bm, d), lambda i, be, nu: (i, 0)),
        scratch_shapes=[pltpu.VMEM((d, D_FF), BF16), pltpu.VMEM((d, D_FF), BF16),
                        pltpu.VMEM((D_FF, d), BF16)],
    )
    ne = w_gate.shape[0] * w_gate.shape[1]
    return pl.pallas_call(
        _expert_kernel,
        grid_spec=grid_spec,
        out_shape=jax.ShapeDtypeStruct((cap, d), F32),
        compiler_params=_params(("arbitrary",)),
        name="moe_experts",
    )(blk_expert, n_used, xs, w_gate.reshape(ne, d, D_FF), b_gate.reshape(ne, 1, D_FF),
      w_up.reshape(ne, d, D_FF), b_up.reshape(ne, 1, D_FF), w_down.reshape(ne, D_FF, d),
      b_down.reshape(ne, 1, d))


def _combine_kernel(slot_ref, next_slot_ref, ys_hbm, wt_ref, h_ref, lg_ref, lb_ref, o_ref,
                    buf_ref, sems):
    i = pl.program_id(0)
    tt = h_ref.shape[0]
    n = tt * TOP_K

    def issue(s_ref, buf):
        def start(r, c):
            for sub in range(8):
                pltpu.make_async_copy(ys_hbm.at[pl.ds(s_ref[0, 0, r * 8 + sub], 1)],
                                      buf_ref.at[buf, r, pl.ds(sub, 1)], sems.at[buf]).start()
            return c

        lax.fori_loop(0, n // 8, start, 0)

    @pl.when(i == 0)
    def _():
        issue(slot_ref, 0)

    @pl.when(i + 1 < pl.num_programs(0))
    def _():
        issue(next_slot_ref, (i + 1) % 2)

    cur = i % 2
    pltpu.make_async_copy(buf_ref.at[cur], buf_ref.at[cur], sems.at[cur]).wait()

    wt = wt_ref[...]
    d = h_ref.shape[1]
    rows = lambda k: buf_ref[cur, pl.ds(k * (tt // 8), tt // 8)].reshape(tt, d)
    ff = wt[:, 0:1] * rows(0)
    for k in range(1, TOP_K):
        ff = ff + wt[:, k:k + 1] * rows(k)
    o_ref[...] = _layer_norm(DN_ALPHA * h_ref[...] + ff, lg_ref[...], lb_ref[...])


def _combine_call(ys, slots, wts, h, ln_g, ln_b):
    t, d = h.shape
    tt = COMBINE_TILE
    nt = t // tt
    n = tt * TOP_K
    slots_km = _tile_slots(slots, tt)
    const = lambda i: (0, 0)
    return pl.pallas_call(
        _combine_kernel,
        grid=(nt,),
        in_specs=[pl.BlockSpec((1, 1, n), lambda i: (i, 0, 0), memory_space=pltpu.SMEM),
                  pl.BlockSpec((1, 1, n), lambda i: (jnp.minimum(i + 1, nt - 1), 0, 0),
                               memory_space=pltpu.SMEM),
                  pl.BlockSpec(memory_space=pl.ANY),
                  pl.BlockSpec((tt, 128), lambda i: (i, 0)),
                  pl.BlockSpec((tt, d), lambda i: (i, 0)),
                  pl.BlockSpec((1, d), const),
                  pl.BlockSpec((1, d), const)],
        out_specs=pl.BlockSpec((tt, d), lambda i: (i, 0)),
        out_shape=jax.ShapeDtypeStruct((t, d), F32),
        scratch_shapes=[pltpu.VMEM((2, n // 8, 8, d), F32), pltpu.SemaphoreType.DMA((2,))],
        compiler_params=_params(("arbitrary",), disable_bounds_checks=True),
        name="moe_combine_ln",
    )(slots_km, slots_km, ys, wts, h, ln_g.reshape(1, d), ln_b.reshape(1, d))


def _tile_slots(slots, tt):
    t = slots.shape[1]
    return slots.reshape(TOP_K, t // tt, tt).transpose(1, 0, 2).reshape(t // tt, 1, TOP_K * tt)


def _moe_layout(idx, rank, counts, n_blocks):
    bm = EXPERT_BLOCK
    counts = counts.reshape(N_EXPERTS).astype(jnp.int32)
    padded = (counts + bm - 1) // bm * bm
    pad_end = jnp.cumsum(padded)
    pad_start = pad_end - padded
    slots = rank
    for e in range(N_EXPERTS):
        slots = slots + jnp.where(idx == e, pad_start[e], 0)
    blk_start = jnp.arange(n_blocks, dtype=jnp.int32) * bm
    blk_expert = jnp.sum((pad_end[None, :] <= blk_start[:, None]).astype(jnp.int32), axis=1)
    blk_expert = jnp.minimum(blk_expert, N_EXPERTS - 1)
    n_used = (pad_end[-1] // bm).reshape(1)
    return slots.astype(jnp.int32), blk_expert, n_used, pad_end.astype(jnp.int32)


def kernel(x, rel_bias, w_in, conv_w, conv_b, dt_bias, a_log, d_skip, ssd_norm_g, w_ssd_out, diff_lambda, diff_norm_g, w_attn_out, gate_b, w_o, ln1_g, ln1_b, w_router, b_router, w_gate, b_gate, w_up, b_up, w_down, b_down, ln2_g, ln2_b):
    batch, seq, d = x.shape
    t = batch * seq
    n_assign = t * TOP_K
    n_blocks = (n_assign + N_EXPERTS * (EXPERT_BLOCK - 1) + EXPERT_BLOCK - 1) // EXPERT_BLOCK
    cap = n_blocks * EXPERT_BLOCK
    bias_near, bias_far = _bias_tiles(rel_bias, seq)
    qk_scale = jnp.concatenate([jnp.full((DIFF_WIDTH,), DIFF_HEAD_DIM ** -0.5 * LOG2_E, F32),
                                jnp.ones((DIFF_WIDTH,), F32)])[None, :]

    h = x.reshape(t, d)
    for l in range(DEPTH):
        w_l = w_in[l]
        w_x = w_l[:, OFF_XBC:OFF_DT].astype(BF16)
        w_b = jnp.concatenate([w_l[:, :OFF_XBC], w_l[:, OFF_G:],
                               w_l[:, OFF_Q:OFF_V] * qk_scale], axis=1).astype(BF16)
        w_v = w_l[:, OFF_V:OFF_G]
        px = _matmul(h, w_x, F32, 1024, 1024)
        pb = _matmul(h, w_b, BF16, 1024, 1024)
        vt = _matmul_nt(h, w_v, ATTN_TILE)
        dt, dt_t = _dt_call(h, w_l[:, OFF_DT:OFF_Q], dt_bias[l])
        y = _ssd_call(px, conv_w[l], conv_b[l], dt, dt_t, a_log[l], d_skip[l], batch, seq)
        ao = _attn_call(pb, vt, bias_near, bias_far, diff_lambda[l], diff_norm_g[l], l, batch,
                        seq)
        h1 = _mix_call(y, pb, ao, h, ssd_norm_g[l], w_ssd_out[l], w_attn_out[l], gate_b[l],
                       w_o[l], ln1_g[l], ln1_b[l])
        idx, wts, rank, counts = _router_call(h1, w_router[l], b_router[l])
        slots, blk_expert, n_used, pad_end = _moe_layout(idx[:TOP_K], rank[:TOP_K], counts,
                                                         n_blocks)
        xs = _dispatch_call(h1, slots, pad_end, cap)
        ys = _expert_call(xs, blk_expert + l * N_EXPERTS, n_used, w_gate, b_gate, w_up, b_up,
                          w_down, b_down)
        h = _combine_call(ys, slots, wts, h1, ln2_g[l], ln2_b[l])
    return h.reshape(batch, seq, d)
```

```python
import functools
import math

import jax
import jax.numpy as jnp
from jax import lax
from jax.experimental import pallas as pl
from jax.experimental.pallas import tpu as pltpu

F32 = jnp.float32
BF16 = jnp.bfloat16

D_MODEL = 1024
DEPTH = 2
CHUNK = 64

SSD_D_INNER = 2048
SSD_HEAD_DIM = 64
SSD_HEADS = 32
SSD_GROUPS = 4
SSD_HEADS_PER_GROUP = 8
SSD_STATE = 128
SSD_CONV = 4
SSD_CONV_DIM = 3072
SSD_GROUP_WIDTH = SSD_HEADS_PER_GROUP * SSD_HEAD_DIM

DIFF_HEAD_DIM = 64
DIFF_HEADS = 8
DIFF_WIDTH = 1024

REL_BUCKETS = 32
REL_MAX_DIST = 128

N_EXPERTS = 32
TOP_K = 4
D_FF = 1024
SWIGLU_ALPHA = 1.702
SWIGLU_LIMIT = 7.0

DN_ALPHA = (2 * DEPTH) ** 0.25
LN_EPS = 1e-5
RMS_EPS = 1e-5

OFF_XBC = 2048
OFF_DT = 5120
OFF_Q = 5152
OFF_V = 7200
OFF_G = 8224
IN_COLS = 10272
PB_GATES = 2048
PB_Q = 4096
PB_K = 5120

VMEM_LIMIT_BYTES = 56 * 1024 * 1024

ATTN_TILE = 256
ATTN_HEADS_PER_STEP = 2
SSD_TILE = 256
SSD_GROUPS_PER_STEP = 4
EXPERT_BLOCK = 256
DISPATCH_TILE = 1024
COMBINE_TILE = 256
NEG_BIG = -1e30
LOG2_E = math.log2(math.e)


def _params(semantics, **kwargs):
    return pltpu.CompilerParams(dimension_semantics=semantics,
                                vmem_limit_bytes=VMEM_LIMIT_BYTES, **kwargs)


def _sigmoid(x):
    return 1.0 / (1.0 + jnp.exp(-x))


def _split2(v):
    hi = v.astype(BF16)
    lo = (v - hi.astype(F32)).astype(BF16)
    return hi, lo


def _dot3(a, b, dims):
    ah, al = _split2(a)
    bh, bl = _split2(b)
    dot = lambda p, q: lax.dot_general(p, q, dims, preferred_element_type=F32)
    return dot(ah, bh) + dot(ah, bl) + dot(al, bh)


def _split3(v):
    hi = v.astype(BF16)
    r = v - hi.astype(F32)
    mid = r.astype(BF16)
    lo = (r - mid.astype(F32)).astype(BF16)
    return hi, mid, lo


def _mm_kernel(x_ref, w_ref, o_ref, xb_ref):
    @pl.when(pl.program_id(1) == 0)
    def _():
        xb_ref[...] = x_ref[...].astype(BF16)

    o_ref[...] = jnp.dot(xb_ref[...], w_ref[...],
                         preferred_element_type=F32).astype(o_ref.dtype)


def _matmul(x, w, out_dtype, bm, bn):
    m, k = x.shape
    n = w.shape[1]
    return pl.pallas_call(
        _mm_kernel,
        grid=(m // bm, n // bn),
        in_specs=[pl.BlockSpec((bm, k), lambda i, j: (i, 0)),
                  pl.BlockSpec((k, bn), lambda i, j: (0, j))],
        out_specs=pl.BlockSpec((bm, bn), lambda i, j: (i, j)),
        out_shape=jax.ShapeDtypeStruct((m, n), out_dtype),
        scratch_shapes=[pltpu.VMEM((bm, k), BF16)],
        compiler_params=_params(("parallel", "arbitrary")),
        name="in_proj",
    )(x, w)


def _mm_nt_kernel(x_ref, w_ref, o_ref, wt_ref):
    @pl.when(pl.program_id(0) == 0)
    def _():
        wt_ref[...] = w_ref[...].T.astype(BF16)

    res = lax.dot_general(wt_ref[...], x_ref[...].astype(BF16), (((1,), (1,)), ((), ())),
                          preferred_element_type=F32)
    tile = o_ref.shape[2]
    for c in range(o_ref.shape[0]):
        o_ref[c] = res[:, c * tile:(c + 1) * tile].astype(o_ref.dtype)


def _matmul_nt(x, w, tile, bm=1024):
    m, k = x.shape
    n = w.shape[1]
    return pl.pallas_call(
        _mm_nt_kernel,
        grid=(m // bm,),
        in_specs=[pl.BlockSpec((bm, k), lambda i: (i, 0)),
                  pl.BlockSpec((k, n), lambda i: (0, 0))],
        out_specs=pl.BlockSpec((bm // tile, n, tile), lambda i: (i, 0, 0)),
        out_shape=jax.ShapeDtypeStruct((m // tile, n, tile), BF16),
        scratch_shapes=[pltpu.VMEM((n, k), BF16)],
        compiler_params=_params(("arbitrary",)),
        name="v_proj_t",
    )(x, w)


def _softplus(x):
    return jnp.maximum(x, 0.0) + jnp.log(1.0 + jnp.exp(-jnp.abs(x)))


def _dt_kernel(x_ref, w_ref, b_ref, dt_ref, dtt_ref):
    x = x_ref[...]
    tm = x.shape[0]
    raw = _dot3(x, w_ref[...], (((1,), (0,)), ((), ())))
    dt = _softplus(raw + b_ref[...])
    dt_t = dt.T
    r = SSD_HEADS_PER_GROUP
    for g in range(SSD_GROUPS):
        dt_ref[g] = dt[:, g * r:(g + 1) * r]
        for j in range(tm // CHUNK):
            dtt_ref[g, j] = dt_t[g * r:(g + 1) * r, j * CHUNK:(j + 1) * CHUNK]


def _dt_call(h, w_dt, dt_bias, tm=512):
    t = h.shape[0]
    r = SSD_HEADS_PER_GROUP
    lanes = 128
    w_pad = jnp.pad(w_dt, ((0, 0), (0, lanes - SSD_HEADS)))
    b_pad = jnp.pad(dt_bias, (0, lanes - SSD_HEADS)).reshape(1, lanes)
    return pl.pallas_call(
        _dt_kernel,
        grid=(t // tm,),
        in_specs=[pl.BlockSpec((tm, D_MODEL), lambda i: (i, 0)),
                  pl.BlockSpec((D_MODEL, lanes), lambda i: (0, 0)),
                  pl.BlockSpec((1, lanes), lambda i: (0, 0))],
        out_specs=[pl.BlockSpec((SSD_GROUPS, tm, r), lambda i: (0, i, 0)),
                   pl.BlockSpec((SSD_GROUPS, tm // CHUNK, r, CHUNK), lambda i: (0, i, 0, 0))],
        out_shape=[jax.ShapeDtypeStruct((SSD_GROUPS, t, r), F32),
                   jax.ShapeDtypeStruct((SSD_GROUPS, t // CHUNK, r, CHUNK), F32)],
        compiler_params=_params(("parallel",)),
        name="dt_proj",
    )(h, w_pad, b_pad)


def _ssd_kernel(x_ref, b_ref, c_ref, wx_ref, wb_ref, wc_ref, bx_ref, bb_ref, bc_ref,
                dt_ref, dtt_ref, alr_ref, alc_ref, dsk_ref, bd_ref, trit_ref, e_ref,
                y_ref,
                state_ref, ux_ref, ub_ref, uc_ref, xc_ref, bcv_ref, ccv_ref, xdt_ref,
                wst_ref, eacs_ref, acs_ref, acst_ref):
    s_idx = pl.program_id(2)
    ts = x_ref.shape[0]
    nc = ts // CHUNK
    r = SSD_HEADS_PER_GROUP
    gw = SSD_GROUP_WIDTH
    n = SSD_STATE
    ng = dt_ref.shape[0]

    @pl.when(s_idx == 0)
    def _():
        state_ref[...] = jnp.zeros_like(state_ref)
        ux_ref[0:8, :] = jnp.zeros((8, ux_ref.shape[1]), F32)
        ub_ref[0:8, :] = jnp.zeros((8, ub_ref.shape[1]), F32)
        uc_ref[0:8, :] = jnp.zeros((8, uc_ref.shape[1]), F32)

    def conv_silu(raw_ref, u_ref, w_ref, bias_ref):
        u_ref[8:8 + ts, :] = raw_ref[...]
        u = u_ref[...]
        u2 = pltpu.roll(u, 2, axis=0)
        even = w_ref[3:4, :] * u + w_ref[1:2, :] * u2
        odd = w_ref[2:3, :] * u + w_ref[0:1, :] * u2
        acc = (even + pltpu.roll(odd, 1, axis=0))[8:8 + ts] + bias_ref[...]
        u_ref[0:8, :] = u_ref[ts:ts + 8, :]
        return acc * _sigmoid(acc)

    xc_ref[...] = conv_silu(x_ref, ux_ref, wx_ref, bx_ref)
    bcv_ref[...] = conv_silu(b_ref, ub_ref, wb_ref, bb_ref).astype(BF16)
    ccv_ref[...] = conv_silu(c_ref, uc_ref, wc_ref, bc_ref).astype(BF16)

    bd = bd_ref[...]
    trit = trit_ref[...]
    e_mat = e_ref[...]

    def expand(v):
        return jnp.dot(jnp.concatenate(_split2(v), axis=1), e_mat, preferred_element_type=F32)

    for g in range(ng):
        gl = slice(g * gw, (g + 1) * gw)
        a_row = -jnp.exp(alr_ref[g])
        a_col = -jnp.exp(alc_ref[g])
        dt = dt_ref[g]
        a = dt * a_row
        acs = jnp.zeros((ts, r), F32)
        for part in _split3(a):
            acs = acs + jnp.dot(bd, part, preferred_element_type=F32)
        acs_ref[g] = acs
        a_t = dtt_ref[g].reshape(nc * r, CHUNK) * jnp.concatenate([a_col] * nc, axis=0)
        acs_t = jnp.zeros((nc * r, CHUNK), F32)
        for part in _split3(a_t):
            acs_t = acs_t + jnp.dot(part, trit, preferred_element_type=F32)
        acst_ref[g] = acs_t
        a_last = jnp.concatenate(
            [jnp.broadcast_to(acs[c * CHUNK + CHUNK - 1:c * CHUNK + CHUNK, :], (CHUNK, r))
             for c in range(nc)], axis=0)
        xdt = xc_ref[:, gl] * expand(dt)
        xdt_ref[:, gl] = xdt.astype(BF16)
        wst_ref[:, gl] = (xdt * expand(jnp.exp(a_last - acs))).astype(BF16)
        eacs_ref[:, gl] = expand(jnp.exp(acs))

    row_i = lax.broadcasted_iota(jnp.int32, (CHUNK, CHUNK), 0)
    col_i = lax.broadcasted_iota(jnp.int32, (CHUNK, CHUNK), 1)
    tril = col_i <= row_i
    left_half = lax.broadcasted_iota(jnp.int32, (CHUNK, 128), 1) < SSD_HEAD_DIM

    for c in range(nc):
        rows = slice(c * CHUNK, (c + 1) * CHUNK)
        for g in range(ng):
            cc = ccv_ref[rows, g * n:(g + 1) * n]
            bc = bcv_ref[rows, g * n:(g + 1) * n]
            cb = lax.dot_general(cc, bc, (((1,), (1,)), ((), ())), preferred_element_type=F32)
            acs_c = acs_ref[g, rows, :]
            acs_tc = acst_ref[g, c * r:(c + 1) * r, :]
            for p in range(r // 2):
                lo = g * gw + p * 128
                xp = xdt_ref[rows, lo:lo + 128]
                halves = []
                for hh in (2 * p, 2 * p + 1):
                    diff = acs_c[:, hh:hh + 1] - acs_tc[hh:hh + 1, :]
                    decay = jnp.exp(jnp.where(tril, diff, -jnp.inf))
                    halves.append(jnp.dot((cb * decay).astype(BF16), xp,
                                          preferred_element_type=F32))
                y_diag = jnp.where(left_half, halves[0], halves[1])
                y_ref[rows, lo:lo + 128] = (y_diag
                                            + dsk_ref[:, lo:lo + 128] * xc_ref[rows, lo:lo + 128])

    def chunk_body(c, carry):
        rows = pl.ds(pl.multiple_of(c * CHUNK, CHUNK), CHUNK)
        for g in range(ng):
            gl = slice(g * gw, (g + 1) * gw)
            cc = ccv_ref[rows, g * n:(g + 1) * n]
            bc = bcv_ref[rows, g * n:(g + 1) * n]
            state = state_ref[g]
            eacs = eacs_ref[rows, gl]
            y_off = jnp.dot(cc, state.astype(BF16), preferred_element_type=F32) * eacs
            y_ref[rows, gl] = y_ref[rows, gl] + y_off
            upd = lax.dot_general(bc, wst_ref[rows, gl], (((0,), (0,)), ((), ())),
                                  preferred_element_type=F32)
            state_ref[g] = state * eacs[CHUNK - 1:CHUNK, :] + upd
        return carry

    lax.fori_loop(0, nc, chunk_body, 0)


def _ssd_call(px, conv_w, conv_b, dt, dt_t, a_log, d_skip, batch, seq):
    t = batch * seq
    ts = SSD_TILE
    ns = seq // ts
    nc = ts // CHUNK
    r = SSD_HEADS_PER_GROUP
    ng = SSD_GROUPS_PER_STEP
    gw = ng * SSD_GROUP_WIDTH
    n = ng * SSD_STATE
    wb0 = SSD_D_INNER // n
    wc0 = wb0 + SSD_GROUPS // ng

    li = jnp.arange(ts)
    bd = ((li[None, :] <= li[:, None]) & (li[None, :] // CHUNK == li[:, None] // CHUNK)).astype(BF16)
    lc = jnp.arange(CHUNK)
    trit = (lc[:, None] <= lc[None, :]).astype(BF16)
    e_mat = (jnp.arange(SSD_GROUP_WIDTH)[None, :] // SSD_HEAD_DIM
             == jnp.arange(2 * r)[:, None] % r).astype(BF16)
    conv_b2 = conv_b.reshape(1, SSD_CONV_DIM)
    alr = a_log.reshape(SSD_GROUPS, 1, r)
    alc = a_log.reshape(SSD_GROUPS, r, 1)
    dsk = jnp.repeat(d_skip, SSD_HEAD_DIM).reshape(1, SSD_D_INNER)

    row = lambda b, g, s: b * ns + s
    in_specs = [
        pl.BlockSpec((ts, gw), lambda b, g, s: (row(b, g, s), g)),
        pl.BlockSpec((ts, n), lambda b, g, s: (row(b, g, s), wb0 + g)),
        pl.BlockSpec((ts, n), lambda b, g, s: (row(b, g, s), wc0 + g)),
        pl.BlockSpec((SSD_CONV, gw), lambda b, g, s: (0, g)),
        pl.BlockSpec((SSD_CONV, n), lambda b, g, s: (0, wb0 + g)),
        pl.BlockSpec((SSD_CONV, n), lambda b, g, s: (0, wc0 + g)),
        pl.BlockSpec((1, gw), lambda b, g, s: (0, g)),
        pl.BlockSpec((1, n), lambda b, g, s: (0, wb0 + g)),
        pl.BlockSpec((1, n), lambda b, g, s: (0, wc0 + g)),
        pl.BlockSpec((ng, ts, r), lambda b, g, s: (g, row(b, g, s), 0)),
        pl.BlockSpec((ng, nc, r, CHUNK), lambda b, g, s: (g, row(b, g, s), 0, 0)),
        pl.BlockSpec((ng, 1, r), lambda b, g, s: (g, 0, 0)),
        pl.BlockSpec((ng, r, 1), lambda b, g, s: (g, 0, 0)),
        pl.BlockSpec((1, gw), lambda b, g, s: (0, g)),
        pl.BlockSpec((ts, ts), lambda b, g, s: (0, 0)),
        pl.BlockSpec((CHUNK, CHUNK), lambda b, g, s: (0, 0)),
        pl.BlockSpec((2 * r, SSD_GROUP_WIDTH), lambda b, g, s: (0, 0)),
    ]
    scratch = [
        pltpu.VMEM((ng, SSD_STATE, SSD_GROUP_WIDTH), F32),
        pltpu.VMEM((ts + 8, gw), F32),
        pltpu.VMEM((ts + 8, n), F32),
        pltpu.VMEM((ts + 8, n), F32),
        pltpu.VMEM((ts, gw), F32),
        pltpu.VMEM((ts, n), BF16),
        pltpu.VMEM((ts, n), BF16),
        pltpu.VMEM((ts, gw), BF16),
        pltpu.VMEM((ts, gw), BF16),
        pltpu.VMEM((ts, gw), F32),
        pltpu.VMEM((ng, ts, r), F32),
        pltpu.VMEM((ng, nc * r, CHUNK), F32),
    ]
    return pl.pallas_call(
        _ssd_kernel,
        grid=(batch, SSD_GROUPS // ng, ns),
        in_specs=in_specs,
        out_specs=pl.BlockSpec((ts, gw), lambda b, g, s: (row(b, g, s), g)),
        out_shape=jax.ShapeDtypeStruct((t, SSD_D_INNER), F32),
        scratch_shapes=scratch,
        compiler_params=_params(("parallel", "parallel", "arbitrary")),
        name="ssd_scan",
    )(px, px, px, conv_w, conv_w, conv_w, conv_b2, conv_b2, conv_b2,
      dt, dt_t, alr, alc, dsk, bd, trit, e_mat)


def _attn_kernel(q_ref, k_ref, vt_ref, bias_ref, far_ref, lam_ref, g_ref, o_ref, sa_ref, sb_ref,
                 *, lam_init):
    i = pl.program_id(2)
    tq = q_ref.shape[0]
    dh = DIFF_HEAD_DIM
    lv = lam_ref[...]
    lam = (jnp.exp(jnp.sum(lv[0:1] * lv[1:2], axis=1, keepdims=True))
           - jnp.exp(jnp.sum(lv[2:3] * lv[3:4], axis=1, keepdims=True)) + lam_init)

    q = q_ref[...]
    hp = q.shape[1] // (2 * dh)
    ns = 2 * hp
    qs = [q[:, s * dh:(s + 1) * dh] for s in range(ns)]
    nt = (((1,), (1,)), ((), ()))

    n_tiles = bias_ref.shape[1] - 1
    h0 = pl.program_id(1) * hp
    ones = jnp.ones((16, tq), BF16)
    dv = 2 * dh

    def near_tile(t):
        valid = t <= i
        return jnp.where(valid, i - t, 0), jnp.where(valid, t, n_tiles)

    def far_tile(t):
        valid = t <= i
        j = jnp.where(valid, t - 2, 0)
        d = jnp.where(valid, i - j, 0)
        return j, [jnp.where(valid, far_ref[h0 + hh, d], NEG_BIG) for hh in range(hp)]

    def scores(j, s_ref, biases):
        kj = k_ref[pl.ds(pl.multiple_of(j * tq, tq), tq), :]
        for s in range(ns):
            sc = lax.dot_general(kj[:, s * dh:(s + 1) * dh], qs[s], nt,
                                 preferred_element_type=F32)
            s_ref[s] = sc if biases is None else sc + biases[s // 2]

    def update(j, s_ref, shifts, carry):
        vtj = vt_ref[j]
        out = []
        for s in range(ns):
            hh = s // 2
            vth = jnp.concatenate([vtj[hh * dv:(hh + 1) * dv], ones], axis=0)
            mx, acc = carry[2 * s:2 * s + 2]
            sc = s_ref[s]
            mx_new = jnp.maximum(mx, jnp.max(sc, axis=0, keepdims=True) + shifts[hh])
            p = jnp.exp2(sc - (mx_new - shifts[hh]))
            out += [mx_new, jnp.exp2(mx - mx_new) * acc
                    + jnp.dot(vth, p.astype(BF16), preferred_element_type=F32)]
        return tuple(out)

    def scores_near(t, s_ref):
        j, d = near_tile(t)
        scores(j, s_ref, [bias_ref[hh, d] for hh in range(hp)])

    def scores_far(t, s_ref):
        scores(far_tile(t)[0], s_ref, None)

    def update_near(t, s_ref, carry):
        return update(near_tile(t)[0], s_ref, [0.0] * hp, carry)

    def update_far(t, s_ref, carry):
        j, shifts = far_tile(t)
        return update(j, s_ref, shifts, carry)

    def body(u, carry):
        scores_far(2 * u + 1, sb_ref)
        carry = update_far(2 * u, sa_ref, carry)
        scores_far(2 * u + 2, sa_ref)
        return update_far(2 * u + 1, sb_ref, carry)

    carry = []
    for _ in range(ns):
        carry += [jnp.full((1, tq), NEG_BIG, F32), jnp.zeros((dv + 16, tq), F32)]
    scores_near(0, sa_ref)
    scores_near(1, sb_ref)
    carry = update_near(0, sa_ref, tuple(carry))
    scores_far(2, sa_ref)
    carry = update_near(1, sb_ref, carry)
    carry = lax.fori_loop(1, (i + 2) // 2, body, carry)
    for hh in range(hp):
        acc1, acc2 = carry[4 * hh + 1], carry[4 * hh + 3]
        o = acc1[:dv] / acc1[dv:dv + 1] - lam * (acc2[:dv] / acc2[dv:dv + 1])
        ms = jnp.mean(o * o, axis=0, keepdims=True)
        o = o * lax.rsqrt(ms + RMS_EPS) * g_ref[...] * (1.0 - lam_init)
        o_ref[:, hh * dv:(hh + 1) * dv] = o.T.astype(o_ref.dtype)


def _rel_bucket(rel):
    half = REL_BUCKETS // 2
    max_exact = half // 2
    ret = jnp.where(rel > 0, half, 0)
    n = jnp.abs(rel)
    nf = jnp.maximum(n, 1).astype(F32)
    large = max_exact + (jnp.log(nf / max_exact) / math.log(REL_MAX_DIST / max_exact)
                         * (half - max_exact)).astype(jnp.int32)
    large = jnp.minimum(large, half - 1)
    return ret + jnp.where(n < max_exact, n, large)


def _bias_lookup(rel_bias, rel):
    bucket = _rel_bucket(rel)[None]
    table = rel_bias.astype(F32)
    lead = (slice(None),) + (None,) * rel.ndim
    bias = jnp.zeros((DIFF_HEADS,) + rel.shape, F32)
    for b in range(REL_BUCKETS):
        bias = jnp.where(bucket == b, table[b][lead], bias)
    return bias * LOG2_E


def _bias_tiles(rel_bias, seq):
    tq = ATTN_TILE
    kk = jnp.arange(tq)[None, :, None]
    qq = jnp.arange(tq)[None, None, :]
    d = jnp.arange(2)[:, None, None]
    near = _bias_lookup(rel_bias, kk - qq - d * tq)
    allowed = (d > 0) | ((kk // CHUNK) <= (qq // CHUNK))
    near = jnp.where(allowed[None], near, NEG_BIG)
    masked = jnp.full((DIFF_HEADS, 1, tq, tq), NEG_BIG, F32)
    far = _bias_lookup(rel_bias, -tq * jnp.arange(seq // tq + 1))
    return jnp.concatenate([near, masked], axis=1), far


def _attn_call(pb, vt, bias_near, bias_far, lam_vecs, norm_g, layer_idx, batch, seq):
    t = batch * seq
    tq = ATTN_TILE
    nq = seq // tq
    hp = ATTN_HEADS_PER_STEP
    w = 2 * DIFF_HEAD_DIM
    wb = hp * w
    qcol0 = PB_Q // wb
    kcol0 = PB_K // wb
    lam_init = 0.8 - 0.6 * math.exp(-0.3 * layer_idx)
    return pl.pallas_call(
        functools.partial(_attn_kernel, lam_init=lam_init),
        grid=(batch, DIFF_HEADS // hp, nq),
        in_specs=[pl.BlockSpec((tq, wb), lambda b, h, i: (b * nq + i, qcol0 + h)),
                  pl.BlockSpec((seq, wb), lambda b, h, i: (b, kcol0 + h)),
                  pl.BlockSpec((nq, wb, tq), lambda b, h, i: (b, h, 0)),
                  pl.BlockSpec((hp, 3, tq, tq), lambda b, h, i: (h, 0, 0, 0)),
                  pl.BlockSpec(memory_space=pltpu.SMEM),
                  pl.BlockSpec((4, DIFF_HEAD_DIM), lambda b, h, i: (0, 0)),
                  pl.BlockSpec((w, 1), lambda b, h, i: (0, 0))],
        out_specs=pl.BlockSpec((tq, wb), lambda b, h, i: (b * nq + i, h)),
        out_shape=jax.ShapeDtypeStruct((t, DIFF_WIDTH), BF16),
        scratch_shapes=[pltpu.VMEM((2 * hp, tq, tq), F32), pltpu.VMEM((2 * hp, tq, tq), F32)],
        compiler_params=_params(("parallel", "parallel", "arbitrary")),
        name="diff_attn",
    )(pb, pb, vt, bias_near, bias_far, lam_vecs, norm_g.reshape(w, 1))


def _layer_norm(x, g, b):
    mu = jnp.mean(x, axis=1, keepdims=True)
    xc = x - mu
    var = jnp.mean(xc * xc, axis=1, keepdims=True)
    return xc * lax.rsqrt(var + LN_EPS) * g + b


def _mix_kernel(y_ref, z_ref, ao_ref, g0_ref, g1_ref, h_ref, ng_ref, wso_ref, wao_ref,
                gb_ref, wo_ref, lg_ref, lb_ref, o_ref, ot_ref):
    z = z_ref[...].astype(F32)
    yg = y_ref[...] * (z * _sigmoid(z))
    ms = jnp.mean(yg * yg, axis=1, keepdims=True)
    yn = (yg * lax.rsqrt(ms + RMS_EPS) * ng_ref[...]).astype(BF16)
    y_ssd = jnp.dot(yn, wso_ref[...], preferred_element_type=F32)
    y_att = jnp.dot(ao_ref[...], wao_ref[...], preferred_element_type=F32)
    gb = gb_ref[...]
    gate0 = _sigmoid(g0_ref[...].astype(F32) + gb[:, :D_MODEL])
    gate1 = _sigmoid(g1_ref[...].astype(F32) + gb[:, D_MODEL:])
    mixed = (gate0 * y_ssd + gate1 * y_att).astype(BF16)
    mix = jnp.dot(mixed, wo_ref[...], preferred_element_type=F32)
    out = _layer_norm(DN_ALPHA * h_ref[...] + mix, lg_ref[...], lb_ref[...])
    o_ref[...] = out
    _store_row_tiles(ot_ref, out)


ROW_TILE = (8, 128)


def _store_row_tiles(ref, rows):
    sub, lanes = ROW_TILE
    m = rows.shape[0]
    for s in range(sub):
        ref[pl.ds(s, m, stride=sub), :] = rows[:, s * lanes:(s + 1) * lanes]


def _load_row_tiles(ref):
    sub, lanes = ROW_TILE
    m = ref.shape[0] // sub
    return jnp.concatenate([ref[pl.ds(s, m, stride=sub), :] for s in range(sub)], axis=1)


def _mix_call(y, pb, ao, h, norm_g, w_ssd_out, w_attn_out, gate_b, w_o, ln_g, ln_b, tm=256):
    t = h.shape[0]
    d = D_MODEL
    gcol0 = PB_GATES // d
    const = lambda i: (0, 0)
    return pl.pallas_call(
        _mix_kernel,
        grid=(t // tm,),
        in_specs=[pl.BlockSpec((tm, SSD_D_INNER), lambda i: (i, 0)),
                  pl.BlockSpec((tm, SSD_D_INNER), lambda i: (i, 0)),
                  pl.BlockSpec((tm, DIFF_WIDTH), lambda i: (i, 0)),
                  pl.BlockSpec((tm, d), lambda i: (i, gcol0)),
                  pl.BlockSpec((tm, d), lambda i: (i, gcol0 + 1)),
                  pl.BlockSpec((tm, d), lambda i: (i, 0)),
                  pl.BlockSpec((1, SSD_D_INNER), const),
                  pl.BlockSpec((SSD_D_INNER, d), const),
                  pl.BlockSpec((DIFF_WIDTH, d), const),
                  pl.BlockSpec((1, 2 * d), const),
                  pl.BlockSpec((d, d), const),
                  pl.BlockSpec((1, d), const),
                  pl.BlockSpec((1, d), const)],
        out_specs=[pl.BlockSpec((tm, d), lambda i: (i, 0)),
                   pl.BlockSpec((tm * ROW_TILE[0], ROW_TILE[1]), lambda i: (i, 0))],
        out_shape=[jax.ShapeDtypeStruct((t, d), F32),
                   jax.ShapeDtypeStruct((t * ROW_TILE[0], ROW_TILE[1]), F32)],
        compiler_params=_params(("parallel",)),
        name="mix_ln",
    )(y, pb, ao, pb, pb, h, norm_g.reshape(1, -1), w_ssd_out.astype(BF16),
      w_attn_out.astype(BF16), gate_b.reshape(1, -1), w_o.astype(BF16),
      ln_g.reshape(1, -1), ln_b.reshape(1, -1))


def _router_kernel(h_ref, w_ref, b_ref, tri_ref, idx_ref, wt_ref, rank_ref, cnt_ref, run_ref):
    @pl.when(pl.program_id(0) == 0)
    def _():
        run_ref[...] = jnp.zeros_like(run_ref)

    tm = h_ref.shape[0]
    ne = N_EXPERTS
    logits = _dot3(h_ref[...], w_ref[...], (((1,), (0,)), ((), ()))) + b_ref[...]
    lane = lax.broadcasted_iota(jnp.int32, (tm, ne), 1).astype(F32)
    work = logits
    sel, vals = [], []
    for _ in range(TOP_K):
        mx = jnp.max(work, axis=1, keepdims=True)
        first = jnp.min(jnp.where(work == mx, lane, float(ne)), axis=1, keepdims=True)
        hit = lane == first
        sel.append((first, hit))
        vals.append(mx)
        work = jnp.where(hit, -jnp.inf, work)
    exps = [jnp.exp(v - vals[0]) for v in vals]
    denom = exps[0] + exps[1] + exps[2] + exps[3]

    onehot = jnp.zeros((tm, ne), F32)
    for _, hit in sel:
        onehot = onehot + hit.astype(F32)
    before = jnp.dot(tri_ref[...], onehot.astype(BF16), preferred_element_type=F32)
    before = before + run_ref[...]

    out_lane = lax.broadcasted_iota(jnp.int32, (tm, 128), 1)
    idx_out = jnp.zeros((tm, 128), F32)
    wt_out = jnp.zeros((tm, 128), F32)
    rank_out = jnp.zeros((tm, 128), F32)
    for k, (first, hit) in enumerate(sel):
        rank = jnp.sum(jnp.where(hit, before, 0.0), axis=1, keepdims=True)
        idx_out = jnp.where(out_lane == k, first, idx_out)
        wt_out = jnp.where(out_lane == k, exps[k] / denom, wt_out)
        rank_out = jnp.where(out_lane == k, rank, rank_out)
    idx_ref[...] = idx_out.T[0:8].astype(jnp.int32)
    wt_ref[...] = wt_out
    rank_ref[...] = rank_out.T[0:8].astype(jnp.int32)
    total = run_ref[...] + jnp.sum(onehot, axis=0, keepdims=True)
    run_ref[...] = total
    cnt_ref[...] = total


def _router_call(h, w_router, b_router, tm=512):
    t = h.shape[0]
    li = jnp.arange(tm)
    tri = (li[None, :] < li[:, None]).astype(BF16)
    const = lambda i: (0, 0)
    return pl.pallas_call(
        _router_kernel,
        grid=(t // tm,),
        in_specs=[pl.BlockSpec((tm, D_MODEL), lambda i: (i, 0)),
                  pl.BlockSpec((D_MODEL, N_EXPERTS), const),
                  pl.BlockSpec((1, N_EXPERTS), const),
                  pl.BlockSpec((tm, tm), const)],
        out_specs=[pl.BlockSpec((8, tm), lambda i: (0, i)),
                   pl.BlockSpec((tm, 128), lambda i: (i, 0)),
                   pl.BlockSpec((8, tm), lambda i: (0, i)),
                   pl.BlockSpec((1, N_EXPERTS), const)],
        out_shape=[jax.ShapeDtypeStruct((8, t), jnp.int32),
                   jax.ShapeDtypeStruct((t, 128), F32),
                   jax.ShapeDtypeStruct((8, t), jnp.int32),
                   jax.ShapeDtypeStruct((1, N_EXPERTS), F32)],
        scratch_shapes=[pltpu.VMEM((1, N_EXPERTS), F32)],
        compiler_params=_params(("arbitrary",)),
        name="router",
    )(h, w_router, b_router.reshape(1, N_EXPERTS), tri)


def _dispatch_kernel(pad_end_ref, slot_ref, x_ref, xs_out, zero_ref, sem, zero_sem):
    sub = ROW_TILE[0]
    tt = x_ref.shape[0] // sub
    n = tt * TOP_K
    zb = zero_ref.shape[0] // sub

    def row(ref, r, count=1):
        return ref.at[pl.ds(pl.multiple_of(r * sub, sub), count * sub)]

    @pl.when(pl.program_id(0) == 0)
    def _():
        zero_ref[...] = jnp.zeros_like(zero_ref)

        def clear(first):
            return pltpu.make_async_copy(zero_ref, row(xs_out, first, zb), zero_sem)

        for e in range(N_EXPERTS):
            clear(jnp.maximum(pad_end_ref[e] - zb, 0)).start()
        used = pad_end_ref[N_EXPERTS - 1] // zb
        total = xs_out.shape[0] // (zb * sub)

        def clear_tail(b, c):
            clear(b * zb).start()
            return c

        def wait_one(b, c):
            clear(0).wait()
            return c

        lax.fori_loop(used, total, clear_tail, 0)
        lax.fori_loop(used - N_EXPERTS, total, wait_one, 0)

    def start(tok, c):
        for k in range(TOP_K):
            slot = slot_ref[0, 0, k * tt + tok]
            pltpu.make_async_copy(row(x_ref, tok), row(xs_out, slot), sem).start()
        return c

    lax.fori_loop(0, tt, start, 0, unroll=8)
    pltpu.make_async_copy(row(xs_out, 0, n), row(xs_out, 0, n), sem).wait()


def _dispatch_call(xt, slots, pad_end, cap):
    sub, lanes = ROW_TILE
    t = xt.shape[0] // sub
    tt = DISPATCH_TILE
    nt = t // tt
    n = tt * TOP_K
    assert cap >= N_EXPERTS * EXPERT_BLOCK
    grid_spec = pltpu.PrefetchScalarGridSpec(
        num_scalar_prefetch=1,
        grid=(nt,),
        in_specs=[pl.BlockSpec((1, 1, n), lambda i, pe: (i, 0, 0), memory_space=pltpu.SMEM),
                  pl.BlockSpec((tt * sub, lanes), lambda i, pe: (i, 0))],
        out_specs=pl.BlockSpec(memory_space=pl.ANY),
        scratch_shapes=[pltpu.VMEM((EXPERT_BLOCK * sub, lanes), xt.dtype),
                        pltpu.SemaphoreType.DMA, pltpu.SemaphoreType.DMA],
    )
    return pl.pallas_call(
        _dispatch_kernel,
        grid_spec=grid_spec,
        out_shape=jax.ShapeDtypeStruct((cap * sub, lanes), xt.dtype),
        compiler_params=_params(("arbitrary",), disable_bounds_checks=True),
        name="moe_dispatch",
    )(pad_end, _tile_slots(slots, tt), xt)


def _expert_kernel(be_ref, nb_ref, x_ref, wg_ref, bg_ref, wu_ref, bu_ref, wd_ref, bd_ref,
                   o_ref, wgb_ref, wub_ref, wdb_ref):
    i = pl.program_id(0)
    prev = be_ref[jnp.maximum(i - 1, 0)]
    changed = jnp.logical_or(i == 0, be_ref[i] != prev)

    @pl.when(changed)
    def _():
        wgb_ref[...] = wg_ref[0].astype(BF16)
        wub_ref[...] = wu_ref[0].astype(BF16)
        wdb_ref[...] = wd_ref[0].astype(BF16)

    @pl.when(i < nb_ref[0])
    def _():
        xb = _load_row_tiles(x_ref).astype(BF16)
        g = jnp.dot(xb, wgb_ref[...], preferred_element_type=F32) + bg_ref[0]
        u = jnp.dot(xb, wub_ref[...], preferred_element_type=F32) + bu_ref[0]
        g = jnp.minimum(g, SWIGLU_LIMIT)
        u = jnp.clip(u, -SWIGLU_LIMIT, SWIGLU_LIMIT)
        act = g * _sigmoid(SWIGLU_ALPHA * g) * (u + 1.0)
        _store_row_tiles(o_ref, jnp.dot(act.astype(BF16), wdb_ref[...],
                                        preferred_element_type=F32) + bd_ref[0])

    @pl.when(i >= nb_ref[0])
    def _():
        o_ref[...] = jnp.zeros_like(o_ref)


def _expert_call(xs, blk_expert, n_used, w_gate, b_gate, w_up, b_up, w_down, b_down):
    sub, lanes = ROW_TILE
    cap = xs.shape[0] // sub
    d = sub * lanes
    bm = EXPERT_BLOCK
    nb = cap // bm
    wspec = lambda shape: pl.BlockSpec(shape, lambda i, be, nu: (be[i], 0, 0))
    grid_spec = pltpu.PrefetchScalarGridSpec(
        num_scalar_prefetch=2,
        grid=(nb,),
        in_specs=[pl.BlockSpec((bm * sub, lanes),
                               lambda i, be, nu: (jnp.minimum(i, nu[0] - 1), 0)),
                  wspec((1, d, D_FF)), wspec((1, 1, D_FF)),
                  wspec((1, d, D_FF)), wspec((1, 1, D_FF)),
                  wspec((1, D_FF, d)), wspec((1, 1, d))],
        out_specs=pl.BlockSpec((bm * sub, lanes), lambda i, be, nu: (i, 0)),
        scratch_shapes=[pltpu.VMEM((d, D_FF), BF16), pltpu.VMEM((d, D_FF), BF16),
                        pltpu.VMEM((D_FF, d), BF16)],
    )
    ne = w_gate.shape[0] * w_gate.shape[1]
    return pl.pallas_call(
        _expert_kernel,
        grid_spec=grid_spec,
        out_shape=jax.ShapeDtypeStruct((cap * sub, lanes), F32),
        compiler_params=_params(("arbitrary",)),
        name="moe_experts",
    )(blk_expert, n_used, xs, w_gate.reshape(ne, d, D_FF), b_gate.reshape(ne, 1, D_FF),
      w_up.reshape(ne, d, D_FF), b_up.reshape(ne, 1, D_FF), w_down.reshape(ne, D_FF, d),
      b_down.reshape(ne, 1, d))


def _combine_kernel(slot_ref, next_slot_ref, ys_hbm, wt_ref, h_ref, lg_ref, lb_ref, o_ref,
                    buf_ref, sems):
    i = pl.program_id(0)
    tt = h_ref.shape[0]
    n = tt * TOP_K
    sub = ROW_TILE[0]

    def issue(s_ref, buf):
        def start(a, c):
            src = pl.ds(pl.multiple_of(s_ref[0, 0, a] * sub, sub), sub)
            dst = pl.ds(pl.multiple_of(a * sub, sub), sub)
            pltpu.make_async_copy(ys_hbm.at[src], buf_ref.at[buf, dst], sems.at[buf]).start()
            return c

        lax.fori_loop(0, n, start, 0, unroll=8)

    @pl.when(i == 0)
    def _():
        issue(slot_ref, 0)

    @pl.when(i + 1 < pl.num_programs(0))
    def _():
        issue(next_slot_ref, (i + 1) % 2)

    cur = i % 2
    pltpu.make_async_copy(buf_ref.at[cur], buf_ref.at[cur], sems.at[cur]).wait()

    wt = wt_ref[...]
    rows = lambda k: _load_row_tiles(buf_ref.at[cur, pl.ds(k * tt * sub, tt * sub)])
    ff = wt[:, 0:1] * rows(0)
    for k in range(1, TOP_K):
        ff = ff + wt[:, k:k + 1] * rows(k)
    o_ref[...] = _layer_norm(DN_ALPHA * h_ref[...] + ff, lg_ref[...], lb_ref[...])


def _combine_call(ys, slots, wts, h, ln_g, ln_b):
    t, d = h.shape
    tt = COMBINE_TILE
    nt = t // tt
    n = tt * TOP_K
    slots_km = _tile_slots(slots, tt)
    const = lambda i: (0, 0)
    return pl.pallas_call(
        _combine_kernel,
        grid=(nt,),
        in_specs=[pl.BlockSpec((1, 1, n), lambda i: (i, 0, 0), memory_space=pltpu.SMEM),
                  pl.BlockSpec((1, 1, n), lambda i: (jnp.minimum(i + 1, nt - 1), 0, 0),
                               memory_space=pltpu.SMEM),
                  pl.BlockSpec(memory_space=pl.ANY),
                  pl.BlockSpec((tt, 128), lambda i: (i, 0)),
                  pl.BlockSpec((tt, d), lambda i: (i, 0)),
                  pl.BlockSpec((1, d), const),
                  pl.BlockSpec((1, d), const)],
        out_specs=pl.BlockSpec((tt, d), lambda i: (i, 0)),
        out_shape=jax.ShapeDtypeStruct((t, d), F32),
        scratch_shapes=[pltpu.VMEM((2, n * ROW_TILE[0], ROW_TILE[1]), F32),
                        pltpu.SemaphoreType.DMA((2,))],
        compiler_params=_params(("arbitrary",), disable_bounds_checks=True),
        name="moe_combine_ln",
    )(slots_km, slots_km, ys, wts, h, ln_g.reshape(1, d), ln_b.reshape(1, d))


def _tile_slots(slots, tt):
    t = slots.shape[1]
    return slots.reshape(TOP_K, t // tt, tt).transpose(1, 0, 2).reshape(t // tt, 1, TOP_K * tt)


def _moe_layout(idx, rank, counts, n_blocks):
    bm = EXPERT_BLOCK
    counts = counts.reshape(N_EXPERTS).astype(jnp.int32)
    padded = (counts + bm - 1) // bm * bm
    pad_end = jnp.cumsum(padded)
    pad_start = pad_end - padded
    slots = rank
    for e in range(N_EXPERTS):
        slots = slots + jnp.where(idx == e, pad_start[e], 0)
    blk_start = jnp.arange(n_blocks, dtype=jnp.int32) * bm
    blk_expert = jnp.sum((pad_end[None, :] <= blk_start[:, None]).astype(jnp.int32), axis=1)
    blk_expert = jnp.minimum(blk_expert, N_EXPERTS - 1)
    n_used = (pad_end[-1] // bm).reshape(1)
    return slots.astype(jnp.int32), blk_expert, n_used, pad_end.astype(jnp.int32)


def kernel(x, rel_bias, w_in, conv_w, conv_b, dt_bias, a_log, d_skip, ssd_norm_g, w_ssd_out, diff_lambda, diff_norm_g, w_attn_out, gate_b, w_o, ln1_g, ln1_b, w_router, b_router, w_gate, b_gate, w_up, b_up, w_down, b_down, ln2_g, ln2_b):
    batch, seq, d = x.shape
    t = batch * seq
    n_assign = t * TOP_K
    n_blocks = (n_assign + N_EXPERTS * (EXPERT_BLOCK - 1) + EXPERT_BLOCK - 1) // EXPERT_BLOCK
    cap = n_blocks * EXPERT_BLOCK
    bias_near, bias_far = _bias_tiles(rel_bias, seq)
    qk_scale = jnp.concatenate([jnp.full((DIFF_WIDTH,), DIFF_HEAD_DIM ** -0.5 * LOG2_E, F32),
                                jnp.ones((DIFF_WIDTH,), F32)])[None, :]

    h = x.reshape(t, d)
    for l in range(DEPTH):
        w_l = w_in[l]
        w_x = w_l[:, OFF_XBC:OFF_DT].astype(BF16)
        w_b = jnp.concatenate([w_l[:, :OFF_XBC], w_l[:, OFF_G:],
                               w_l[:, OFF_Q:OFF_V] * qk_scale], axis=1).astype(BF16)
        w_v = w_l[:, OFF_V:OFF_G]
        px = _matmul(h, w_x, F32, 1024, 1024)
        pb = _matmul(h, w_b, BF16, 1024, 1024)
        vt = _matmul_nt(h, w_v, ATTN_TILE)
        dt, dt_t = _dt_call(h, w_l[:, OFF_DT:OFF_Q], dt_bias[l])
        y = _ssd_call(px, conv_w[l], conv_b[l], dt, dt_t, a_log[l], d_skip[l], batch, seq)
        ao = _attn_call(pb, vt, bias_near, bias_far, diff_lambda[l], diff_norm_g[l], l, batch,
                        seq)
        h1, h1_tiles = _mix_call(y, pb, ao, h, ssd_norm_g[l], w_ssd_out[l], w_attn_out[l], gate_b[l],
                       w_o[l], ln1_g[l], ln1_b[l])
        idx, wts, rank, counts = _router_call(h1, w_router[l], b_router[l])
        slots, blk_expert, n_used, pad_end = _moe_layout(idx[:TOP_K], rank[:TOP_K], counts,
                                                         n_blocks)
        xs = _dispatch_call(h1_tiles, slots, pad_end, cap)
        ys = _expert_call(xs, blk_expert + l * N_EXPERTS, n_used, w_gate, b_gate, w_up, b_up,
                          w_down, b_down)
        h = _combine_call(ys, slots, wts, h1, ln2_g[l], ln2_b[l])
    return h.reshape(batch, seq, d)
```

```python
import functools
import math

import jax
import jax.numpy as jnp
from jax import lax
from jax.experimental import pallas as pl
from jax.experimental.pallas import tpu as pltpu

F32 = jnp.float32
BF16 = jnp.bfloat16

D_MODEL = 1024
DEPTH = 2
CHUNK = 64

SSD_D_INNER = 2048
SSD_HEAD_DIM = 64
SSD_HEADS = 32
SSD_GROUPS = 4
SSD_HEADS_PER_GROUP = 8
SSD_STATE = 128
SSD_CONV = 4
SSD_CONV_DIM = 3072
SSD_GROUP_WIDTH = SSD_HEADS_PER_GROUP * SSD_HEAD_DIM

DIFF_HEAD_DIM = 64
DIFF_HEADS = 8
DIFF_WIDTH = 1024

REL_BUCKETS = 32
REL_MAX_DIST = 128

N_EXPERTS = 32
TOP_K = 4
D_FF = 1024
SWIGLU_ALPHA = 1.702
SWIGLU_LIMIT = 7.0

DN_ALPHA = (2 * DEPTH) ** 0.25
LN_EPS = 1e-5
RMS_EPS = 1e-5

OFF_XBC = 2048
OFF_DT = 5120
OFF_Q = 5152
OFF_V = 7200
OFF_G = 8224
IN_COLS = 10272
PB_GATES = 2048
PB_Q = 4096
PB_K = 5120

VMEM_LIMIT_BYTES = 56 * 1024 * 1024

ATTN_TILE = 256
ATTN_HEADS_PER_STEP = 2
SSD_TILE = 256
SSD_GROUPS_PER_STEP = 4
EXPERT_BLOCK = 256
DISPATCH_TILE = 1024
COMBINE_TILE = 256
NEG_BIG = -1e30
LOG2_E = math.log2(math.e)


def _params(semantics, **kwargs):
    return pltpu.CompilerParams(dimension_semantics=semantics,
                                vmem_limit_bytes=VMEM_LIMIT_BYTES, **kwargs)


def _sigmoid(x):
    return 1.0 / (1.0 + jnp.exp(-x))


def _split2(v):
    hi = v.astype(BF16)
    lo = (v - hi.astype(F32)).astype(BF16)
    return hi, lo


def _dot3(a, b, dims):
    ah, al = _split2(a)
    bh, bl = _split2(b)
    dot = lambda p, q: lax.dot_general(p, q, dims, preferred_element_type=F32)
    return dot(ah, bh) + dot(ah, bl) + dot(al, bh)


def _split3(v):
    hi = v.astype(BF16)
    r = v - hi.astype(F32)
    mid = r.astype(BF16)
    lo = (r - mid.astype(F32)).astype(BF16)
    return hi, mid, lo


def _mm_kernel(x_ref, w_ref, o_ref, xb_ref):
    @pl.when(pl.program_id(1) == 0)
    def _():
        xb_ref[...] = x_ref[...].astype(BF16)

    o_ref[...] = jnp.dot(xb_ref[...], w_ref[...],
                         preferred_element_type=F32).astype(o_ref.dtype)


def _matmul(x, w, out_dtype, bm, bn):
    m, k = x.shape
    n = w.shape[1]
    return pl.pallas_call(
        _mm_kernel,
        grid=(m // bm, n // bn),
        in_specs=[pl.BlockSpec((bm, k), lambda i, j: (i, 0)),
                  pl.BlockSpec((k, bn), lambda i, j: (0, j))],
        out_specs=pl.BlockSpec((bm, bn), lambda i, j: (i, j)),
        out_shape=jax.ShapeDtypeStruct((m, n), out_dtype),
        scratch_shapes=[pltpu.VMEM((bm, k), BF16)],
        compiler_params=_params(("parallel", "arbitrary")),
        name="in_proj",
    )(x, w)


def _mm_nt_kernel(x_ref, w_ref, o_ref, wt_ref):
    @pl.when(pl.program_id(0) == 0)
    def _():
        wt_ref[...] = w_ref[...].T.astype(BF16)

    res = lax.dot_general(wt_ref[...], x_ref[...].astype(BF16), (((1,), (1,)), ((), ())),
                          preferred_element_type=F32)
    tile = o_ref.shape[2]
    for c in range(o_ref.shape[0]):
        o_ref[c] = res[:, c * tile:(c + 1) * tile].astype(o_ref.dtype)


def _matmul_nt(x, w, tile, bm=1024):
    m, k = x.shape
    n = w.shape[1]
    return pl.pallas_call(
        _mm_nt_kernel,
        grid=(m // bm,),
        in_specs=[pl.BlockSpec((bm, k), lambda i: (i, 0)),
                  pl.BlockSpec((k, n), lambda i: (0, 0))],
        out_specs=pl.BlockSpec((bm // tile, n, tile), lambda i: (i, 0, 0)),
        out_shape=jax.ShapeDtypeStruct((m // tile, n, tile), BF16),
        scratch_shapes=[pltpu.VMEM((n, k), BF16)],
        compiler_params=_params(("arbitrary",)),
        name="v_proj_t",
    )(x, w)


def _softplus(x):
    return jnp.maximum(x, 0.0) + jnp.log(1.0 + jnp.exp(-jnp.abs(x)))


def _dt_kernel(x_ref, w_ref, b_ref, dt_ref, dtt_ref):
    x = x_ref[...]
    tm = x.shape[0]
    raw = _dot3(x, w_ref[...], (((1,), (0,)), ((), ())))
    dt = _softplus(raw + b_ref[...])
    dt_t = dt.T
    r = SSD_HEADS_PER_GROUP
    for g in range(SSD_GROUPS):
        dt_ref[g] = dt[:, g * r:(g + 1) * r]
        for j in range(tm // CHUNK):
            dtt_ref[g, j] = dt_t[g * r:(g + 1) * r, j * CHUNK:(j + 1) * CHUNK]


def _dt_call(h, w_dt, dt_bias, tm=512):
    t = h.shape[0]
    r = SSD_HEADS_PER_GROUP
    lanes = 128
    w_pad = jnp.pad(w_dt, ((0, 0), (0, lanes - SSD_HEADS)))
    b_pad = jnp.pad(dt_bias, (0, lanes - SSD_HEADS)).reshape(1, lanes)
    return pl.pallas_call(
        _dt_kernel,
        grid=(t // tm,),
        in_specs=[pl.BlockSpec((tm, D_MODEL), lambda i: (i, 0)),
                  pl.BlockSpec((D_MODEL, lanes), lambda i: (0, 0)),
                  pl.BlockSpec((1, lanes), lambda i: (0, 0))],
        out_specs=[pl.BlockSpec((SSD_GROUPS, tm, r), lambda i: (0, i, 0)),
                   pl.BlockSpec((SSD_GROUPS, tm // CHUNK, r, CHUNK), lambda i: (0, i, 0, 0))],
        out_shape=[jax.ShapeDtypeStruct((SSD_GROUPS, t, r), F32),
                   jax.ShapeDtypeStruct((SSD_GROUPS, t // CHUNK, r, CHUNK), F32)],
        compiler_params=_params(("parallel",)),
        name="dt_proj",
    )(h, w_pad, b_pad)


def _ssd_kernel(x_ref, b_ref, c_ref, wx_ref, wb_ref, wc_ref, bx_ref, bb_ref, bc_ref,
                dt_ref, dtt_ref, alr_ref, alc_ref, dsk_ref, bd_ref, trit_ref, e_ref,
                y_ref,
                state_ref, ux_ref, ub_ref, uc_ref, xc_ref, bcv_ref, ccv_ref, xdt_ref,
                wst_ref, eacs_ref, acs_ref, acst_ref):
    s_idx = pl.program_id(2)
    ts = x_ref.shape[0]
    nc = ts // CHUNK
    r = SSD_HEADS_PER_GROUP
    gw = SSD_GROUP_WIDTH
    n = SSD_STATE
    ng = dt_ref.shape[0]

    @pl.when(s_idx == 0)
    def _():
        state_ref[...] = jnp.zeros_like(state_ref)
        ux_ref[0:8, :] = jnp.zeros((8, ux_ref.shape[1]), F32)
        ub_ref[0:8, :] = jnp.zeros((8, ub_ref.shape[1]), F32)
        uc_ref[0:8, :] = jnp.zeros((8, uc_ref.shape[1]), F32)

    def conv_silu(raw_ref, u_ref, w_ref, bias_ref):
        u_ref[8:8 + ts, :] = raw_ref[...]
        u = u_ref[...]
        u2 = pltpu.roll(u, 2, axis=0)
        even = w_ref[3:4, :] * u + w_ref[1:2, :] * u2
        odd = w_ref[2:3, :] * u + w_ref[0:1, :] * u2
        acc = (even + pltpu.roll(odd, 1, axis=0))[8:8 + ts] + bias_ref[...]
        u_ref[0:8, :] = u_ref[ts:ts + 8, :]
        return acc * _sigmoid(acc)

    xc_ref[...] = conv_silu(x_ref, ux_ref, wx_ref, bx_ref)
    bcv_ref[...] = conv_silu(b_ref, ub_ref, wb_ref, bb_ref).astype(BF16)
    ccv_ref[...] = conv_silu(c_ref, uc_ref, wc_ref, bc_ref).astype(BF16)

    bd = bd_ref[...]
    trit = trit_ref[...]
    e_mat = e_ref[...]

    def expand(v):
        return jnp.dot(jnp.concatenate(_split2(v), axis=1), e_mat, preferred_element_type=F32)

    for g in range(ng):
        gl = slice(g * gw, (g + 1) * gw)
        a_row = -jnp.exp(alr_ref[g])
        a_col = -jnp.exp(alc_ref[g])
        dt = dt_ref[g]
        a = dt * a_row
        acs = jnp.zeros((ts, r), F32)
        for part in _split3(a):
            acs = acs + jnp.dot(bd, part, preferred_element_type=F32)
        acs_ref[g] = acs
        a_t = dtt_ref[g].reshape(nc * r, CHUNK) * jnp.concatenate([a_col] * nc, axis=0)
        acs_t = jnp.zeros((nc * r, CHUNK), F32)
        for part in _split3(a_t):
            acs_t = acs_t + jnp.dot(part, trit, preferred_element_type=F32)
        acst_ref[g] = acs_t
        a_last = jnp.concatenate(
            [jnp.broadcast_to(acs[c * CHUNK + CHUNK - 1:c * CHUNK + CHUNK, :], (CHUNK, r))
             for c in range(nc)], axis=0)
        xdt = xc_ref[:, gl] * expand(dt)
        xdt_ref[:, gl] = xdt.astype(BF16)
        wst_ref[:, gl] = (xdt * expand(jnp.exp(a_last - acs))).astype(BF16)
        eacs_ref[:, gl] = expand(jnp.exp(acs))

    row_i = lax.broadcasted_iota(jnp.int32, (CHUNK, CHUNK), 0)
    col_i = lax.broadcasted_iota(jnp.int32, (CHUNK, CHUNK), 1)
    tril = col_i <= row_i
    left_half = lax.broadcasted_iota(jnp.int32, (CHUNK, 128), 1) < SSD_HEAD_DIM

    for c in range(nc):
        rows = slice(c * CHUNK, (c + 1) * CHUNK)
        for g in range(ng):
            cc = ccv_ref[rows, g * n:(g + 1) * n]
            bc = bcv_ref[rows, g * n:(g + 1) * n]
            cb = lax.dot_general(cc, bc, (((1,), (1,)), ((), ())), preferred_element_type=F32)
            acs_c = acs_ref[g, rows, :]
            acs_tc = acst_ref[g, c * r:(c + 1) * r, :]
            for p in range(r // 2):
                lo = g * gw + p * 128
                xp = xdt_ref[rows, lo:lo + 128]
                halves = []
                for hh in (2 * p, 2 * p + 1):
                    diff = acs_c[:, hh:hh + 1] - acs_tc[hh:hh + 1, :]
                    decay = jnp.exp(jnp.where(tril, diff, -jnp.inf))
                    halves.append(jnp.dot((cb * decay).astype(BF16), xp,
                                          preferred_element_type=F32))
                y_diag = jnp.where(left_half, halves[0], halves[1])
                y_ref[rows, lo:lo + 128] = (y_diag
                                            + dsk_ref[:, lo:lo + 128] * xc_ref[rows, lo:lo + 128])

    def chunk_body(c, carry):
        rows = pl.ds(pl.multiple_of(c * CHUNK, CHUNK), CHUNK)
        for g in range(ng):
            gl = slice(g * gw, (g + 1) * gw)
            cc = ccv_ref[rows, g * n:(g + 1) * n]
            bc = bcv_ref[rows, g * n:(g + 1) * n]
            state = state_ref[g]
            eacs = eacs_ref[rows, gl]
            y_off = jnp.dot(cc, state.astype(BF16), preferred_element_type=F32) * eacs
            y_ref[rows, gl] = y_ref[rows, gl] + y_off
            upd = lax.dot_general(bc, wst_ref[rows, gl], (((0,), (0,)), ((), ())),
                                  preferred_element_type=F32)
            state_ref[g] = state * eacs[CHUNK - 1:CHUNK, :] + upd
        return carry

    lax.fori_loop(0, nc, chunk_body, 0)


def _ssd_call(px, conv_w, conv_b, dt, dt_t, a_log, d_skip, batch, seq):
    t = batch * seq
    ts = SSD_TILE
    ns = seq // ts
    nc = ts // CHUNK
    r = SSD_HEADS_PER_GROUP
    ng = SSD_GROUPS_PER_STEP
    gw = ng * SSD_GROUP_WIDTH
    n = ng * SSD_STATE
    wb0 = SSD_D_INNER // n
    wc0 = wb0 + SSD_GROUPS // ng

    li = jnp.arange(ts)
    bd = ((li[None, :] <= li[:, None]) & (li[None, :] // CHUNK == li[:, None] // CHUNK)).astype(BF16)
    lc = jnp.arange(CHUNK)
    trit = (lc[:, None] <= lc[None, :]).astype(BF16)
    e_mat = (jnp.arange(SSD_GROUP_WIDTH)[None, :] // SSD_HEAD_DIM
             == jnp.arange(2 * r)[:, None] % r).astype(BF16)
    conv_b2 = conv_b.reshape(1, SSD_CONV_DIM)
    alr = a_log.reshape(SSD_GROUPS, 1, r)
    alc = a_log.reshape(SSD_GROUPS, r, 1)
    dsk = jnp.repeat(d_skip, SSD_HEAD_DIM).reshape(1, SSD_D_INNER)

    row = lambda b, g, s: b * ns + s
    in_specs = [
        pl.BlockSpec((ts, gw), lambda b, g, s: (row(b, g, s), g)),
        pl.BlockSpec((ts, n), lambda b, g, s: (row(b, g, s), wb0 + g)),
        pl.BlockSpec((ts, n), lambda b, g, s: (row(b, g, s), wc0 + g)),
        pl.BlockSpec((SSD_CONV, gw), lambda b, g, s: (0, g)),
        pl.BlockSpec((SSD_CONV, n), lambda b, g, s: (0, wb0 + g)),
        pl.BlockSpec((SSD_CONV, n), lambda b, g, s: (0, wc0 + g)),
        pl.BlockSpec((1, gw), lambda b, g, s: (0, g)),
        pl.BlockSpec((1, n), lambda b, g, s: (0, wb0 + g)),
        pl.BlockSpec((1, n), lambda b, g, s: (0, wc0 + g)),
        pl.BlockSpec((ng, ts, r), lambda b, g, s: (g, row(b, g, s), 0)),
        pl.BlockSpec((ng, nc, r, CHUNK), lambda b, g, s: (g, row(b, g, s), 0, 0)),
        pl.BlockSpec((ng, 1, r), lambda b, g, s: (g, 0, 0)),
        pl.BlockSpec((ng, r, 1), lambda b, g, s: (g, 0, 0)),
        pl.BlockSpec((1, gw), lambda b, g, s: (0, g)),
        pl.BlockSpec((ts, ts), lambda b, g, s: (0, 0)),
        pl.BlockSpec((CHUNK, CHUNK), lambda b, g, s: (0, 0)),
        pl.BlockSpec((2 * r, SSD_GROUP_WIDTH), lambda b, g, s: (0, 0)),
    ]
    scratch = [
        pltpu.VMEM((ng, SSD_STATE, SSD_GROUP_WIDTH), F32),
        pltpu.VMEM((ts + 8, gw), F32),
        pltpu.VMEM((ts + 8, n), F32),
        pltpu.VMEM((ts + 8, n), F32),
        pltpu.VMEM((ts, gw), F32),
        pltpu.VMEM((ts, n), BF16),
        pltpu.VMEM((ts, n), BF16),
        pltpu.VMEM((ts, gw), BF16),
        pltpu.VMEM((ts, gw), BF16),
        pltpu.VMEM((ts, gw), F32),
        pltpu.VMEM((ng, ts, r), F32),
        pltpu.VMEM((ng, nc * r, CHUNK), F32),
    ]
    return pl.pallas_call(
        _ssd_kernel,
        grid=(batch, SSD_GROUPS // ng, ns),
        in_specs=in_specs,
        out_specs=pl.BlockSpec((ts, gw), lambda b, g, s: (row(b, g, s), g)),
        out_shape=jax.ShapeDtypeStruct((t, SSD_D_INNER), F32),
        scratch_shapes=scratch,
        compiler_params=_params(("parallel", "parallel", "arbitrary")),
        name="ssd_scan",
    )(px, px, px, conv_w, conv_w, conv_w, conv_b2, conv_b2, conv_b2,
      dt, dt_t, alr, alc, dsk, bd, trit, e_mat)


def _attn_kernel(q_ref, k_ref, vt_ref, bias_ref, far_ref, lam_ref, g_ref, o_ref, sa_ref, sb_ref,
                 *, lam_init):
    i = pl.program_id(2)
    tq = q_ref.shape[0]
    dh = DIFF_HEAD_DIM
    lv = lam_ref[...]
    lam = (jnp.exp(jnp.sum(lv[0:1] * lv[1:2], axis=1, keepdims=True))
           - jnp.exp(jnp.sum(lv[2:3] * lv[3:4], axis=1, keepdims=True)) + lam_init)

    q = q_ref[...]
    hp = q.shape[1] // (2 * dh)
    ns = 2 * hp
    qs = [q[:, s * dh:(s + 1) * dh] for s in range(ns)]
    nt = (((1,), (1,)), ((), ()))

    n_tiles = bias_ref.shape[1] - 1
    h0 = pl.program_id(1) * hp
    ones = jnp.ones((16, tq), BF16)
    dv = 2 * dh

    def near_tile(t):
        valid = t <= i
        return jnp.where(valid, i - t, 0), jnp.where(valid, t, n_tiles)

    def far_tile(t):
        valid = t <= i
        j = jnp.where(valid, t - 2, 0)
        d = jnp.where(valid, i - j, 0)
        return j, [jnp.where(valid, far_ref[h0 + hh, d], NEG_BIG) for hh in range(hp)]

    def scores(j, s_ref, biases):
        kj = k_ref[pl.ds(pl.multiple_of(j * tq, tq), tq), :]
        for s in range(ns):
            sc = lax.dot_general(kj[:, s * dh:(s + 1) * dh], qs[s], nt,
                                 preferred_element_type=F32)
            s_ref[s] = sc if biases is None else sc + biases[s // 2]

    def update(j, s_ref, shifts, carry):
        vtj = vt_ref[j]
        out = []
        for s in range(ns):
            hh = s // 2
            vth = jnp.concatenate([vtj[hh * dv:(hh + 1) * dv], ones], axis=0)
            mx, acc = carry[2 * s:2 * s + 2]
            sc = s_ref[s]
            mx_new = jnp.maximum(mx, jnp.max(sc, axis=0, keepdims=True) + shifts[hh])
            p = jnp.exp2(sc - (mx_new - shifts[hh]))
            out += [mx_new, jnp.exp2(mx - mx_new) * acc
                    + jnp.dot(vth, p.astype(BF16), preferred_element_type=F32)]
        return tuple(out)

    def scores_near(t, s_ref):
        j, d = near_tile(t)
        scores(j, s_ref, [bias_ref[hh, d] for hh in range(hp)])

    def scores_far(t, s_ref):
        scores(far_tile(t)[0], s_ref, None)

    def update_near(t, s_ref, carry):
        return update(near_tile(t)[0], s_ref, [0.0] * hp, carry)

    def update_far(t, s_ref, carry):
        j, shifts = far_tile(t)
        return update(j, s_ref, shifts, carry)

    def body(u, carry):
        scores_far(2 * u + 1, sb_ref)
        carry = update_far(2 * u, sa_ref, carry)
        scores_far(2 * u + 2, sa_ref)
        return update_far(2 * u + 1, sb_ref, carry)

    carry = []
    for _ in range(ns):
        carry += [jnp.full((1, tq), NEG_BIG, F32), jnp.zeros((dv + 16, tq), F32)]
    scores_near(0, sa_ref)
    scores_near(1, sb_ref)
    carry = update_near(0, sa_ref, tuple(carry))
    scores_far(2, sa_ref)
    carry = update_near(1, sb_ref, carry)
    carry = lax.fori_loop(1, (i + 2) // 2, body, carry)
    for hh in range(hp):
        acc1, acc2 = carry[4 * hh + 1], carry[4 * hh + 3]
        o = acc1[:dv] / acc1[dv:dv + 1] - lam * (acc2[:dv] / acc2[dv:dv + 1])
        ms = jnp.mean(o * o, axis=0, keepdims=True)
        o = o * lax.rsqrt(ms + RMS_EPS) * g_ref[...] * (1.0 - lam_init)
        o_ref[:, hh * dv:(hh + 1) * dv] = o.T.astype(o_ref.dtype)


def _rel_bucket(rel):
    half = REL_BUCKETS // 2
    max_exact = half // 2
    ret = jnp.where(rel > 0, half, 0)
    n = jnp.abs(rel)
    nf = jnp.maximum(n, 1).astype(F32)
    large = max_exact + (jnp.log(nf / max_exact) / math.log(REL_MAX_DIST / max_exact)
                         * (half - max_exact)).astype(jnp.int32)
    large = jnp.minimum(large, half - 1)
    return ret + jnp.where(n < max_exact, n, large)


def _bias_lookup(rel_bias, rel):
    bucket = _rel_bucket(rel)[None]
    table = rel_bias.astype(F32)
    lead = (slice(None),) + (None,) * rel.ndim
    bias = jnp.zeros((DIFF_HEADS,) + rel.shape, F32)
    for b in range(REL_BUCKETS):
        bias = jnp.where(bucket == b, table[b][lead], bias)
    return bias * LOG2_E


def _bias_tiles(rel_bias, seq):
    tq = ATTN_TILE
    kk = jnp.arange(tq)[None, :, None]
    qq = jnp.arange(tq)[None, None, :]
    d = jnp.arange(2)[:, None, None]
    near = _bias_lookup(rel_bias, kk - qq - d * tq)
    allowed = (d > 0) | ((kk // CHUNK) <= (qq // CHUNK))
    near = jnp.where(allowed[None], near, NEG_BIG)
    masked = jnp.full((DIFF_HEADS, 1, tq, tq), NEG_BIG, F32)
    far = _bias_lookup(rel_bias, -tq * jnp.arange(seq // tq + 1))
    return jnp.concatenate([near, masked], axis=1), far


def _attn_call(pb, vt, bias_near, bias_far, lam_vecs, norm_g, layer_idx, batch, seq):
    t = batch * seq
    tq = ATTN_TILE
    nq = seq // tq
    hp = ATTN_HEADS_PER_STEP
    w = 2 * DIFF_HEAD_DIM
    wb = hp * w
    qcol0 = PB_Q // wb
    kcol0 = PB_K // wb
    lam_init = 0.8 - 0.6 * math.exp(-0.3 * layer_idx)
    return pl.pallas_call(
        functools.partial(_attn_kernel, lam_init=lam_init),
        grid=(batch, DIFF_HEADS // hp, nq),
        in_specs=[pl.BlockSpec((tq, wb), lambda b, h, i: (b * nq + i, qcol0 + h)),
                  pl.BlockSpec((seq, wb), lambda b, h, i: (b, kcol0 + h)),
                  pl.BlockSpec((nq, wb, tq), lambda b, h, i: (b, h, 0)),
                  pl.BlockSpec((hp, 3, tq, tq), lambda b, h, i: (h, 0, 0, 0)),
                  pl.BlockSpec(memory_space=pltpu.SMEM),
                  pl.BlockSpec((4, DIFF_HEAD_DIM), lambda b, h, i: (0, 0)),
                  pl.BlockSpec((w, 1), lambda b, h, i: (0, 0))],
        out_specs=pl.BlockSpec((tq, wb), lambda b, h, i: (b * nq + i, h)),
        out_shape=jax.ShapeDtypeStruct((t, DIFF_WIDTH), BF16),
        scratch_shapes=[pltpu.VMEM((2 * hp, tq, tq), F32), pltpu.VMEM((2 * hp, tq, tq), F32)],
        compiler_params=_params(("parallel", "parallel", "arbitrary")),
        name="diff_attn",
    )(pb, pb, vt, bias_near, bias_far, lam_vecs, norm_g.reshape(w, 1))


def _layer_norm(x, g, b):
    mu = jnp.mean(x, axis=1, keepdims=True)
    xc = x - mu
    var = jnp.mean(xc * xc, axis=1, keepdims=True)
    return xc * lax.rsqrt(var + LN_EPS) * g + b


def _mix_kernel(y_ref, z_ref, ao_ref, g0_ref, g1_ref, h_ref, ng_ref, wso_ref, wao_ref,
                gb_ref, wo_ref, lg_ref, lb_ref, o_ref, ot_ref):
    z = z_ref[...].astype(F32)
    yg = y_ref[...] * (z * _sigmoid(z))
    ms = jnp.mean(yg * yg, axis=1, keepdims=True)
    yn = (yg * lax.rsqrt(ms + RMS_EPS) * ng_ref[...]).astype(BF16)
    y_ssd = jnp.dot(yn, wso_ref[...], preferred_element_type=F32)
    y_att = jnp.dot(ao_ref[...], wao_ref[...], preferred_element_type=F32)
    gb = gb_ref[...]
    gate0 = _sigmoid(g0_ref[...].astype(F32) + gb[:, :D_MODEL])
    gate1 = _sigmoid(g1_ref[...].astype(F32) + gb[:, D_MODEL:])
    mixed = (gate0 * y_ssd + gate1 * y_att).astype(BF16)
    mix = jnp.dot(mixed, wo_ref[...], preferred_element_type=F32)
    out = _layer_norm(DN_ALPHA * h_ref[...] + mix, lg_ref[...], lb_ref[...])
    o_ref[...] = out
    _store_row_tiles(ot_ref, out)


ROW_TILE = (8, 128)


def _store_row_tiles(ref, rows):
    sub, lanes = ROW_TILE
    m = rows.shape[0]
    for s in range(sub):
        ref[pl.ds(s, m, stride=sub), :] = rows[:, s * lanes:(s + 1) * lanes]


def _load_row_tiles(ref):
    sub, lanes = ROW_TILE
    m = ref.shape[0] // sub
    return jnp.concatenate([ref[pl.ds(s, m, stride=sub), :] for s in range(sub)], axis=1)


def _mix_call(y, pb, ao, h, norm_g, w_ssd_out, w_attn_out, gate_b, w_o, ln_g, ln_b, tm=256):
    t = h.shape[0]
    d = D_MODEL
    gcol0 = PB_GATES // d
    const = lambda i: (0, 0)
    return pl.pallas_call(
        _mix_kernel,
        grid=(t // tm,),
        in_specs=[pl.BlockSpec((tm, SSD_D_INNER), lambda i: (i, 0)),
                  pl.BlockSpec((tm, SSD_D_INNER), lambda i: (i, 0)),
                  pl.BlockSpec((tm, DIFF_WIDTH), lambda i: (i, 0)),
                  pl.BlockSpec((tm, d), lambda i: (i, gcol0)),
                  pl.BlockSpec((tm, d), lambda i: (i, gcol0 + 1)),
                  pl.BlockSpec((tm, d), lambda i: (i, 0)),
                  pl.BlockSpec((1, SSD_D_INNER), const),
                  pl.BlockSpec((SSD_D_INNER, d), const),
                  pl.BlockSpec((DIFF_WIDTH, d), const),
                  pl.BlockSpec((1, 2 * d), const),
                  pl.BlockSpec((d, d), const),
                  pl.BlockSpec((1, d), const),
                  pl.BlockSpec((1, d), const)],
        out_specs=[pl.BlockSpec((tm, d), lambda i: (i, 0)),
                   pl.BlockSpec((tm * ROW_TILE[0], ROW_TILE[1]), lambda i: (i, 0))],
        out_shape=[jax.ShapeDtypeStruct((t, d), F32),
                   jax.ShapeDtypeStruct((t * ROW_TILE[0], ROW_TILE[1]), F32)],
        compiler_params=_params(("parallel",)),
        name="mix_ln",
    )(y, pb, ao, pb, pb, h, norm_g.reshape(1, -1), w_ssd_out.astype(BF16),
      w_attn_out.astype(BF16), gate_b.reshape(1, -1), w_o.astype(BF16),
      ln_g.reshape(1, -1), ln_b.reshape(1, -1))


def _router_kernel(h_ref, w_ref, b_ref, tri_ref, idx_ref, wt_ref, rank_ref, cnt_ref, run_ref):
    @pl.when(pl.program_id(0) == 0)
    def _():
        run_ref[...] = jnp.zeros_like(run_ref)

    tm = h_ref.shape[0]
    ne = N_EXPERTS
    logits = _dot3(h_ref[...], w_ref[...], (((1,), (0,)), ((), ()))) + b_ref[...]
    lane = lax.broadcasted_iota(jnp.int32, (tm, ne), 1).astype(F32)
    work = logits
    sel, vals = [], []
    for _ in range(TOP_K):
        mx = jnp.max(work, axis=1, keepdims=True)
        first = jnp.min(jnp.where(work == mx, lane, float(ne)), axis=1, keepdims=True)
        hit = lane == first
        sel.append((first, hit))
        vals.append(mx)
        work = jnp.where(hit, -jnp.inf, work)
    exps = [jnp.exp(v - vals[0]) for v in vals]
    denom = exps[0] + exps[1] + exps[2] + exps[3]

    onehot = jnp.zeros((tm, ne), F32)
    for _, hit in sel:
        onehot = onehot + hit.astype(F32)
    before = jnp.dot(tri_ref[...], onehot.astype(BF16), preferred_element_type=F32)
    before = before + run_ref[...]

    out_lane = lax.broadcasted_iota(jnp.int32, (tm, 128), 1)
    idx_out = jnp.zeros((tm, 128), F32)
    wt_out = jnp.zeros((tm, 128), F32)
    rank_out = jnp.zeros((tm, 128), F32)
    for k, (first, hit) in enumerate(sel):
        rank = jnp.sum(jnp.where(hit, before, 0.0), axis=1, keepdims=True)
        idx_out = jnp.where(out_lane == k, first, idx_out)
        wt_out = jnp.where(out_lane == k, exps[k] / denom, wt_out)
        rank_out = jnp.where(out_lane == k, rank, rank_out)
    idx_ref[...] = idx_out.T[0:8].astype(jnp.int32)
    wt_ref[...] = wt_out
    rank_ref[...] = rank_out.T[0:8].astype(jnp.int32)
    total = run_ref[...] + jnp.sum(onehot, axis=0, keepdims=True)
    run_ref[...] = total
    cnt_ref[...] = total


def _router_call(h, w_router, b_router, tm=512):
    t = h.shape[0]
    li = jnp.arange(tm)
    tri = (li[None, :] < li[:, None]).astype(BF16)
    const = lambda i: (0, 0)
    return pl.pallas_call(
        _router_kernel,
        grid=(t // tm,),
        in_specs=[pl.BlockSpec((tm, D_MODEL), lambda i: (i, 0)),
                  pl.BlockSpec((D_MODEL, N_EXPERTS), const),
                  pl.BlockSpec((1, N_EXPERTS), const),
                  pl.BlockSpec((tm, tm), const)],
        out_specs=[pl.BlockSpec((8, tm), lambda i: (0, i)),
                   pl.BlockSpec((tm, 128), lambda i: (i, 0)),
                   pl.BlockSpec((8, tm), lambda i: (0, i)),
                   pl.BlockSpec((1, N_EXPERTS), const)],
        out_shape=[jax.ShapeDtypeStruct((8, t), jnp.int32),
                   jax.ShapeDtypeStruct((t, 128), F32),
                   jax.ShapeDtypeStruct((8, t), jnp.int32),
                   jax.ShapeDtypeStruct((1, N_EXPERTS), F32)],
        scratch_shapes=[pltpu.VMEM((1, N_EXPERTS), F32)],
        compiler_params=_params(("arbitrary",)),
        name="router",
    )(h, w_router, b_router.reshape(1, N_EXPERTS), tri)


def _dispatch_kernel(pad_end_ref, slot_ref, x_ref, xs_out, zero_ref, sem, zero_sem):
    sub = ROW_TILE[0]
    tt = x_ref.shape[0] // sub
    n = tt * TOP_K
    zb = zero_ref.shape[0] // sub

    def row(ref, r, count=1):
        return ref.at[pl.ds(pl.multiple_of(r * sub, sub), count * sub)]

    @pl.when(pl.program_id(0) == 0)
    def _():
        zero_ref[...] = jnp.zeros_like(zero_ref)

        def clear(first):
            return pltpu.make_async_copy(zero_ref, row(xs_out, first, zb), zero_sem)

        for e in range(N_EXPERTS):
            clear(jnp.maximum(pad_end_ref[e] - zb, 0)).start()
        used = pad_end_ref[N_EXPERTS - 1] // zb
        total = xs_out.shape[0] // (zb * sub)

        def clear_tail(b, c):
            clear(b * zb).start()
            return c

        def wait_one(b, c):
            clear(0).wait()
            return c

        lax.fori_loop(used, total, clear_tail, 0)
        lax.fori_loop(used - N_EXPERTS, total, wait_one, 0)

    def start(tok, c):
        for k in range(TOP_K):
            slot = slot_ref[0, 0, k * tt + tok]
            pltpu.make_async_copy(row(x_ref, tok), row(xs_out, slot), sem).start(priority=k % 2)
        return c

    lax.fori_loop(0, tt, start, 0, unroll=8)
    pltpu.make_async_copy(row(xs_out, 0, n), row(xs_out, 0, n), sem).wait()


def _dispatch_call(xt, slots, pad_end, cap):
    sub, lanes = ROW_TILE
    t = xt.shape[0] // sub
    tt = DISPATCH_TILE
    nt = t // tt
    n = tt * TOP_K
    assert cap >= N_EXPERTS * EXPERT_BLOCK
    grid_spec = pltpu.PrefetchScalarGridSpec(
        num_scalar_prefetch=1,
        grid=(nt,),
        in_specs=[pl.BlockSpec((1, 1, n), lambda i, pe: (i, 0, 0), memory_space=pltpu.SMEM),
                  pl.BlockSpec((tt * sub, lanes), lambda i, pe: (i, 0))],
        out_specs=pl.BlockSpec(memory_space=pl.ANY),
        scratch_shapes=[pltpu.VMEM((EXPERT_BLOCK * sub, lanes), xt.dtype),
                        pltpu.SemaphoreType.DMA, pltpu.SemaphoreType.DMA],
    )
    return pl.pallas_call(
        _dispatch_kernel,
        grid_spec=grid_spec,
        out_shape=jax.ShapeDtypeStruct((cap * sub, lanes), xt.dtype),
        compiler_params=_params(("arbitrary",), disable_bounds_checks=True),
        name="moe_dispatch",
    )(pad_end, _tile_slots(slots, tt), xt)


def _expert_kernel(be_ref, nb_ref, x_ref, wg_ref, bg_ref, wu_ref, bu_ref, wd_ref, bd_ref,
                   o_ref, wgb_ref, wub_ref, wdb_ref):
    i = pl.program_id(0)
    prev = be_ref[jnp.maximum(i - 1, 0)]
    changed = jnp.logical_or(i == 0, be_ref[i] != prev)

    @pl.when(changed)
    def _():
        wgb_ref[...] = wg_ref[0].astype(BF16)
        wub_ref[...] = wu_ref[0].astype(BF16)
        wdb_ref[...] = wd_ref[0].astype(BF16)

    @pl.when(i < nb_ref[0])
    def _():
        xb = _load_row_tiles(x_ref).astype(BF16)
        g = jnp.dot(xb, wgb_ref[...], preferred_element_type=F32) + bg_ref[0]
        u = jnp.dot(xb, wub_ref[...], preferred_element_type=F32) + bu_ref[0]
        g = jnp.minimum(g, SWIGLU_LIMIT)
        u = jnp.clip(u, -SWIGLU_LIMIT, SWIGLU_LIMIT)
        act = g * _sigmoid(SWIGLU_ALPHA * g) * (u + 1.0)
        _store_row_tiles(o_ref, jnp.dot(act.astype(BF16), wdb_ref[...],
                                        preferred_element_type=F32) + bd_ref[0])

    @pl.when(i >= nb_ref[0])
    def _():
        o_ref[...] = jnp.zeros_like(o_ref)


def _expert_call(xs, blk_expert, n_used, w_gate, b_gate, w_up, b_up, w_down, b_down):
    sub, lanes = ROW_TILE
    cap = xs.shape[0] // sub
    d = sub * lanes
    bm = EXPERT_BLOCK
    nb = cap // bm
    wspec = lambda shape: pl.BlockSpec(shape, lambda i, be, nu: (be[i], 0, 0))
    grid_spec = pltpu.PrefetchScalarGridSpec(
        num_scalar_prefetch=2,
        grid=(nb,),
        in_specs=[pl.BlockSpec((bm * sub, lanes),
                               lambda i, be, nu: (jnp.minimum(i, nu[0] - 1), 0)),
                  wspec((1, d, D_FF)), wspec((1, 1, D_FF)),
                  wspec((1, d, D_FF)), wspec((1, 1, D_FF)),
                  wspec((1, D_FF, d)), wspec((1, 1, d))],
        out_specs=pl.BlockSpec((bm * sub, lanes), lambda i, be, nu: (i, 0)),
        scratch_shapes=[pltpu.VMEM((d, D_FF), BF16), pltpu.VMEM((d, D_FF), BF16),
                        pltpu.VMEM((D_FF, d), BF16)],
    )
    ne = w_gate.shape[0] * w_gate.shape[1]
    return pl.pallas_call(
        _expert_kernel,
        grid_spec=grid_spec,
        out_shape=jax.ShapeDtypeStruct((cap * sub, lanes), F32),
        compiler_params=_params(("arbitrary",)),
        name="moe_experts",
    )(blk_expert, n_used, xs, w_gate.reshape(ne, d, D_FF), b_gate.reshape(ne, 1, D_FF),
      w_up.reshape(ne, d, D_FF), b_up.reshape(ne, 1, D_FF), w_down.reshape(ne, D_FF, d),
      b_down.reshape(ne, 1, d))


def _combine_kernel(slot_ref, next_slot_ref, ys_hbm, wt_ref, h_ref, lg_ref, lb_ref, o_ref,
                    buf_ref, sems):
    i = pl.program_id(0)
    tt = h_ref.shape[0]
    n = tt * TOP_K
    sub = ROW_TILE[0]

    def issue(s_ref, buf):
        def start(r, c):
            for j in range(8):
                a = r * 8 + j
                src = pl.ds(pl.multiple_of(s_ref[0, 0, a] * sub, sub), sub)
                dst = pl.ds(pl.multiple_of(a * sub, sub), sub)
                pltpu.make_async_copy(ys_hbm.at[src], buf_ref.at[buf, dst],
                                      sems.at[buf]).start(priority=j % 2)
            return c

        lax.fori_loop(0, n // 8, start, 0)

    @pl.when(i == 0)
    def _():
        issue(slot_ref, 0)

    @pl.when(i + 1 < pl.num_programs(0))
    def _():
        issue(next_slot_ref, (i + 1) % 2)

    cur = i % 2
    pltpu.make_async_copy(buf_ref.at[cur], buf_ref.at[cur], sems.at[cur]).wait()

    wt = wt_ref[...]
    rows = lambda k: _load_row_tiles(buf_ref.at[cur, pl.ds(k * tt * sub, tt * sub)])
    ff = wt[:, 0:1] * rows(0)
    for k in range(1, TOP_K):
        ff = ff + wt[:, k:k + 1] * rows(k)
    o_ref[...] = _layer_norm(DN_ALPHA * h_ref[...] + ff, lg_ref[...], lb_ref[...])


def _combine_call(ys, slots, wts, h, ln_g, ln_b):
    t, d = h.shape
    tt = COMBINE_TILE
    nt = t // tt
    n = tt * TOP_K
    slots_km = _tile_slots(slots, tt)
    const = lambda i: (0, 0)
    return pl.pallas_call(
        _combine_kernel,
        grid=(nt,),
        in_specs=[pl.BlockSpec((1, 1, n), lambda i: (i, 0, 0), memory_space=pltpu.SMEM),
                  pl.BlockSpec((1, 1, n), lambda i: (jnp.minimum(i + 1, nt - 1), 0, 0),
                               memory_space=pltpu.SMEM),
                  pl.BlockSpec(memory_space=pl.ANY),
                  pl.BlockSpec((tt, 128), lambda i: (i, 0)),
                  pl.BlockSpec((tt, d), lambda i: (i, 0)),
                  pl.BlockSpec((1, d), const),
                  pl.BlockSpec((1, d), const)],
        out_specs=pl.BlockSpec((tt, d), lambda i: (i, 0)),
        out_shape=jax.ShapeDtypeStruct((t, d), F32),
        scratch_shapes=[pltpu.VMEM((2, n * ROW_TILE[0], ROW_TILE[1]), F32),
                        pltpu.SemaphoreType.DMA((2,))],
        compiler_params=_params(("arbitrary",), disable_bounds_checks=True),
        name="moe_combine_ln",
    )(slots_km, slots_km, ys, wts, h, ln_g.reshape(1, d), ln_b.reshape(1, d))


def _tile_slots(slots, tt):
    t = slots.shape[1]
    return slots.reshape(TOP_K, t // tt, tt).transpose(1, 0, 2).reshape(t // tt, 1, TOP_K * tt)


def _moe_layout(idx, rank, counts, n_blocks):
    bm = EXPERT_BLOCK
    counts = counts.reshape(N_EXPERTS).astype(jnp.int32)
    padded = (counts + bm - 1) // bm * bm
    pad_end = jnp.cumsum(padded)
    pad_start = pad_end - padded
    slots = rank
    for e in range(N_EXPERTS):
        slots = slots + jnp.where(idx == e, pad_start[e], 0)
    blk_start = jnp.arange(n_blocks, dtype=jnp.int32) * bm
    blk_expert = jnp.sum((pad_end[None, :] <= blk_start[:, None]).astype(jnp.int32), axis=1)
    blk_expert = jnp.minimum(blk_expert, N_EXPERTS - 1)
    n_used = (pad_end[-1] // bm).reshape(1)
    return slots.astype(jnp.int32), blk_expert, n_used, pad_end.astype(jnp.int32)


def kernel(x, rel_bias, w_in, conv_w, conv_b, dt_bias, a_log, d_skip, ssd_norm_g, w_ssd_out, diff_lambda, diff_norm_g, w_attn_out, gate_b, w_o, ln1_g, ln1_b, w_router, b_router, w_gate, b_gate, w_up, b_up, w_down, b_down, ln2_g, ln2_b):
    batch, seq, d = x.shape
    t = batch * seq
    n_assign = t * TOP_K
    n_blocks = (n_assign + N_EXPERTS * (EXPERT_BLOCK - 1) + EXPERT_BLOCK - 1) // EXPERT_BLOCK
    cap = n_blocks * EXPERT_BLOCK
    bias_near, bias_far = _bias_tiles(rel_bias, seq)
    qk_scale = jnp.concatenate([jnp.full((DIFF_WIDTH,), DIFF_HEAD_DIM ** -0.5 * LOG2_E, F32),
                                jnp.ones((DIFF_WIDTH,), F32)])[None, :]

    h = x.reshape(t, d)
    for l in range(DEPTH):
        w_l = w_in[l]
        w_x = w_l[:, OFF_XBC:OFF_DT].astype(BF16)
        w_b = jnp.concatenate([w_l[:, :OFF_XBC], w_l[:, OFF_G:],
                               w_l[:, OFF_Q:OFF_V] * qk_scale], axis=1).astype(BF16)
        w_v = w_l[:, OFF_V:OFF_G]
        px = _matmul(h, w_x, F32, 1024, 1024)
        pb = _matmul(h, w_b, BF16, 1024, 1024)
        vt = _matmul_nt(h, w_v, ATTN_TILE)
        dt, dt_t = _dt_call(h, w_l[:, OFF_DT:OFF_Q], dt_bias[l])
        y = _ssd_call(px, conv_w[l], conv_b[l], dt, dt_t, a_log[l], d_skip[l], batch, seq)
        ao = _attn_call(pb, vt, bias_near, bias_far, diff_lambda[l], diff_norm_g[l], l, batch,
                        seq)
        h1, h1_tiles = _mix_call(y, pb, ao, h, ssd_norm_g[l], w_ssd_out[l], w_attn_out[l], gate_b[l],
                       w_o[l], ln1_g[l], ln1_b[l])
        idx, wts, rank, counts = _router_call(h1, w_router[l], b_router[l])
        slots, blk_expert, n_used, pad_end = _moe_layout(idx[:TOP_K], rank[:TOP_K], counts,
                                                         n_blocks)
        xs = _dispatch_call(h1_tiles, slots, pad_end, cap)
        ys = _expert_call(xs, blk_expert + l * N_EXPERTS, n_used, w_gate, b_gate, w_up, b_up,
                          w_down, b_down)
        h = _combine_call(ys, slots, wts, h1, ln2_g[l], ln2_b[l])
    return h.reshape(batch, seq, d)
```

```python
import functools
import math

import jax
import jax.numpy as jnp
from jax import lax
from jax.experimental import pallas as pl
from jax.experimental.pallas import tpu as pltpu

F32 = jnp.float32
BF16 = jnp.bfloat16

D_MODEL = 1024
DEPTH = 2
CHUNK = 64

SSD_D_INNER = 2048
SSD_HEAD_DIM = 64
SSD_HEADS = 32
SSD_GROUPS = 4
SSD_HEADS_PER_GROUP = 8
SSD_STATE = 128
SSD_CONV = 4
SSD_CONV_DIM = 3072
SSD_GROUP_WIDTH = SSD_HEADS_PER_GROUP * SSD_HEAD_DIM

DIFF_HEAD_DIM = 64
DIFF_HEADS = 8
DIFF_WIDTH = 1024

REL_BUCKETS = 32
REL_MAX_DIST = 128

N_EXPERTS = 32
TOP_K = 4
D_FF = 1024
SWIGLU_ALPHA = 1.702
SWIGLU_LIMIT = 7.0

DN_ALPHA = (2 * DEPTH) ** 0.25
LN_EPS = 1e-5
RMS_EPS = 1e-5

OFF_XBC = 2048
OFF_DT = 5120
OFF_Q = 5152
OFF_V = 7200
OFF_G = 8224
PB_GATES = 2048
PB_Q = 4096
PB_K = 5120

VMEM_LIMIT_BYTES = 56 * 1024 * 1024

ATTN_TILE = 256
ATTN_HEADS_PER_STEP = 4
SSD_TILE = 256
SSD_GROUPS_PER_STEP = 4
EXPERT_BLOCK = 256
DISPATCH_TILE = 1024
COMBINE_TILE = 256
NEG_BIG = -1e30
LOG2_E = math.log2(math.e)


def _params(semantics, **kwargs):
    return pltpu.CompilerParams(dimension_semantics=semantics,
                                vmem_limit_bytes=VMEM_LIMIT_BYTES, **kwargs)


def _sigmoid(x):
    return 1.0 / (1.0 + jnp.exp(-x))


def _split2(v):
    hi = v.astype(BF16)
    lo = (v - hi.astype(F32)).astype(BF16)
    return hi, lo


def _dot3(a, b, dims):
    ah, al = _split2(a)
    bh, bl = _split2(b)
    dot = lambda p, q: lax.dot_general(p, q, dims, preferred_element_type=F32)
    return dot(ah, bh) + dot(ah, bl) + dot(al, bh)


def _split3(v):
    hi = v.astype(BF16)
    r = v - hi.astype(F32)
    mid = r.astype(BF16)
    lo = (r - mid.astype(F32)).astype(BF16)
    return hi, mid, lo


def _mm_kernel(x_ref, w_ref, o_ref, xb_ref):
    @pl.when(pl.program_id(1) == 0)
    def _():
        xb_ref[...] = x_ref[...].astype(BF16)

    o_ref[...] = jnp.dot(xb_ref[...], w_ref[...],
                         preferred_element_type=F32).astype(o_ref.dtype)


def _matmul(x, w, out_dtype, bm, bn):
    m, k = x.shape
    n = w.shape[1]
    return pl.pallas_call(
        _mm_kernel,
        grid=(m // bm, n // bn),
        in_specs=[pl.BlockSpec((bm, k), lambda i, j: (i, 0)),
                  pl.BlockSpec((k, bn), lambda i, j: (0, j))],
        out_specs=pl.BlockSpec((bm, bn), lambda i, j: (i, j)),
        out_shape=jax.ShapeDtypeStruct((m, n), out_dtype),
        scratch_shapes=[pltpu.VMEM((bm, k), BF16)],
        compiler_params=_params(("parallel", "arbitrary")),
        name="in_proj",
    )(x, w)


def _mm_nt_kernel(x_ref, w_ref, o_ref, wt_ref):
    @pl.when(pl.program_id(0) == 0)
    def _():
        wt_ref[...] = w_ref[...].T.astype(BF16)

    res = lax.dot_general(wt_ref[...], x_ref[...].astype(BF16), (((1,), (1,)), ((), ())),
                          preferred_element_type=F32)
    tile = o_ref.shape[2]
    for c in range(o_ref.shape[0]):
        o_ref[c] = res[:, c * tile:(c + 1) * tile].astype(o_ref.dtype)


def _matmul_nt(x, w, tile, bm=1024):
    m, k = x.shape
    n = w.shape[1]
    return pl.pallas_call(
        _mm_nt_kernel,
        grid=(m // bm,),
        in_specs=[pl.BlockSpec((bm, k), lambda i: (i, 0)),
                  pl.BlockSpec((k, n), lambda i: (0, 0))],
        out_specs=pl.BlockSpec((bm // tile, n, tile), lambda i: (i, 0, 0)),
        out_shape=jax.ShapeDtypeStruct((m // tile, n, tile), BF16),
        scratch_shapes=[pltpu.VMEM((n, k), BF16)],
        compiler_params=_params(("arbitrary",)),
        name="v_proj_t",
    )(x, w)


def _softplus(x):
    return jnp.maximum(x, 0.0) + jnp.log(1.0 + jnp.exp(-jnp.abs(x)))


def _dt_kernel(x_ref, w_ref, b_ref, dt_ref, dtt_ref):
    x = x_ref[...]
    tm = x.shape[0]
    raw = _dot3(x, w_ref[...], (((1,), (0,)), ((), ())))
    dt = _softplus(raw + b_ref[...])
    dt_t = dt.T
    r = SSD_HEADS_PER_GROUP
    for g in range(SSD_GROUPS):
        dt_ref[g] = dt[:, g * r:(g + 1) * r]
        for j in range(tm // CHUNK):
            dtt_ref[g, j] = dt_t[g * r:(g + 1) * r, j * CHUNK:(j + 1) * CHUNK]


def _dt_call(h, w_dt, dt_bias, tm=512):
    t = h.shape[0]
    r = SSD_HEADS_PER_GROUP
    lanes = 128
    w_pad = jnp.pad(w_dt, ((0, 0), (0, lanes - SSD_HEADS)))
    b_pad = jnp.pad(dt_bias, (0, lanes - SSD_HEADS)).reshape(1, lanes)
    return pl.pallas_call(
        _dt_kernel,
        grid=(t // tm,),
        in_specs=[pl.BlockSpec((tm, D_MODEL), lambda i: (i, 0)),
                  pl.BlockSpec((D_MODEL, lanes), lambda i: (0, 0)),
                  pl.BlockSpec((1, lanes), lambda i: (0, 0))],
        out_specs=[pl.BlockSpec((SSD_GROUPS, tm, r), lambda i: (0, i, 0)),
                   pl.BlockSpec((SSD_GROUPS, tm // CHUNK, r, CHUNK), lambda i: (0, i, 0, 0))],
        out_shape=[jax.ShapeDtypeStruct((SSD_GROUPS, t, r), F32),
                   jax.ShapeDtypeStruct((SSD_GROUPS, t // CHUNK, r, CHUNK), F32)],
        compiler_params=_params(("parallel",)),
        name="dt_proj",
    )(h, w_pad, b_pad)


def _ssd_kernel(x_ref, b_ref, c_ref, wx_ref, wb_ref, wc_ref, bx_ref, bb_ref, bc_ref,
                dt_ref, dtt_ref, alr_ref, alc_ref, dsk_ref, bd_ref, trit_ref, e_ref,
                y_ref,
                state_ref, ux_ref, ub_ref, uc_ref, xc_ref, bcv_ref, ccv_ref, xdt_ref,
                wst_ref, eacs_ref, acs_ref, acst_ref):
    s_idx = pl.program_id(2)
    ts = x_ref.shape[0]
    nc = ts // CHUNK
    r = SSD_HEADS_PER_GROUP
    gw = SSD_GROUP_WIDTH
    n = SSD_STATE
    ng = dt_ref.shape[0]

    @pl.when(s_idx == 0)
    def _():
        state_ref[...] = jnp.zeros_like(state_ref)
        ux_ref[0:8, :] = jnp.zeros((8, ux_ref.shape[1]), F32)
        ub_ref[0:8, :] = jnp.zeros((8, ub_ref.shape[1]), F32)
        uc_ref[0:8, :] = jnp.zeros((8, uc_ref.shape[1]), F32)

    def conv_silu(raw_ref, u_ref, w_ref, bias_ref):
        u_ref[8:8 + ts, :] = raw_ref[...]
        u = u_ref[...]
        u2 = pltpu.roll(u, 2, axis=0)
        even = w_ref[3:4, :] * u + w_ref[1:2, :] * u2
        odd = w_ref[2:3, :] * u + w_ref[0:1, :] * u2
        acc = (even + pltpu.roll(odd, 1, axis=0))[8:8 + ts] + bias_ref[...]
        u_ref[0:8, :] = u_ref[ts:ts + 8, :]
        return acc * _sigmoid(acc)

    xc_ref[...] = conv_silu(x_ref, ux_ref, wx_ref, bx_ref)
    bcv_ref[...] = conv_silu(b_ref, ub_ref, wb_ref, bb_ref).astype(BF16)
    ccv_ref[...] = conv_silu(c_ref, uc_ref, wc_ref, bc_ref).astype(BF16)

    bd = bd_ref[...]
    trit = trit_ref[...]
    e_mat = e_ref[...]

    def expand(v):
        return jnp.dot(jnp.concatenate(_split2(v), axis=1), e_mat, preferred_element_type=F32)

    for g in range(ng):
        gl = slice(g * gw, (g + 1) * gw)
        a_row = -jnp.exp(alr_ref[g])
        a_col = -jnp.exp(alc_ref[g])
        dt = dt_ref[g]
        a = dt * a_row
        acs = jnp.zeros((ts, r), F32)
        for part in _split3(a):
            acs = acs + jnp.dot(bd, part, preferred_element_type=F32)
        acs_ref[g] = acs
        a_t = dtt_ref[g].reshape(nc * r, CHUNK) * jnp.concatenate([a_col] * nc, axis=0)
        acs_t = jnp.zeros((nc * r, CHUNK), F32)
        for part in _split3(a_t):
            acs_t = acs_t + jnp.dot(part, trit, preferred_element_type=F32)
        acst_ref[g] = acs_t
        a_last = jnp.concatenate(
            [jnp.broadcast_to(acs[c * CHUNK + CHUNK - 1:c * CHUNK + CHUNK, :], (CHUNK, r))
             for c in range(nc)], axis=0)
        xdt = xc_ref[:, gl] * expand(dt)
        xdt_ref[:, gl] = xdt.astype(BF16)
        wst_ref[:, gl] = (xdt * expand(jnp.exp(a_last - acs))).astype(BF16)
        eacs_ref[:, gl] = expand(jnp.exp(acs))

    row_i = lax.broadcasted_iota(jnp.int32, (CHUNK, CHUNK), 0)
    col_i = lax.broadcasted_iota(jnp.int32, (CHUNK, CHUNK), 1)
    tril = col_i <= row_i
    left_half = lax.broadcasted_iota(jnp.int32, (CHUNK, 128), 1) < SSD_HEAD_DIM

    for c in range(nc):
        rows = slice(c * CHUNK, (c + 1) * CHUNK)
        for g in range(ng):
            cc = ccv_ref[rows, g * n:(g + 1) * n]
            bc = bcv_ref[rows, g * n:(g + 1) * n]
            cb = lax.dot_general(cc, bc, (((1,), (1,)), ((), ())), preferred_element_type=F32)
            acs_c = acs_ref[g, rows, :]
            acs_tc = acst_ref[g, c * r:(c + 1) * r, :]
            for p in range(r // 2):
                lo = g * gw + p * 128
                xp = xdt_ref[rows, lo:lo + 128]
                halves = []
                for hh in (2 * p, 2 * p + 1):
                    diff = acs_c[:, hh:hh + 1] - acs_tc[hh:hh + 1, :]
                    decay = jnp.exp(jnp.where(tril, diff, -jnp.inf))
                    halves.append(jnp.dot((cb * decay).astype(BF16), xp,
                                          preferred_element_type=F32))
                y_diag = jnp.where(left_half, halves[0], halves[1])
                y_ref[rows, lo:lo + 128] = (y_diag
                                            + dsk_ref[:, lo:lo + 128] * xc_ref[rows, lo:lo + 128])

    def chunk_body(c, carry):
        rows = pl.ds(pl.multiple_of(c * CHUNK, CHUNK), CHUNK)
        for g in range(ng):
            gl = slice(g * gw, (g + 1) * gw)
            cc = ccv_ref[rows, g * n:(g + 1) * n]
            bc = bcv_ref[rows, g * n:(g + 1) * n]
            state = state_ref[g]
            eacs = eacs_ref[rows, gl]
            y_off = jnp.dot(cc, state.astype(BF16), preferred_element_type=F32) * eacs
            y_ref[rows, gl] = y_ref[rows, gl] + y_off
            upd = lax.dot_general(bc, wst_ref[rows, gl], (((0,), (0,)), ((), ())),
                                  preferred_element_type=F32)
            state_ref[g] = state * eacs[CHUNK - 1:CHUNK, :] + upd
        return carry

    lax.fori_loop(0, nc, chunk_body, 0)


def _ssd_call(px, conv_w, conv_b, dt, dt_t, a_log, d_skip, batch, seq):
    t = batch * seq
    ts = SSD_TILE
    ns = seq // ts
    nc = ts // CHUNK
    r = SSD_HEADS_PER_GROUP
    ng = SSD_GROUPS_PER_STEP
    gw = ng * SSD_GROUP_WIDTH
    n = ng * SSD_STATE
    wb0 = SSD_D_INNER // n
    wc0 = wb0 + SSD_GROUPS // ng

    li = jnp.arange(ts)
    bd = ((li[None, :] <= li[:, None]) & (li[None, :] // CHUNK == li[:, None] // CHUNK)).astype(BF16)
    lc = jnp.arange(CHUNK)
    trit = (lc[:, None] <= lc[None, :]).astype(BF16)
    e_mat = (jnp.arange(SSD_GROUP_WIDTH)[None, :] // SSD_HEAD_DIM
             == jnp.arange(2 * r)[:, None] % r).astype(BF16)
    conv_b2 = conv_b.reshape(1, SSD_CONV_DIM)
    alr = a_log.reshape(SSD_GROUPS, 1, r)
    alc = a_log.reshape(SSD_GROUPS, r, 1)
    dsk = jnp.repeat(d_skip, SSD_HEAD_DIM).reshape(1, SSD_D_INNER)

    row = lambda b, g, s: b * ns + s
    in_specs = [
        pl.BlockSpec((ts, gw), lambda b, g, s: (row(b, g, s), g)),
        pl.BlockSpec((ts, n), lambda b, g, s: (row(b, g, s), wb0 + g)),
        pl.BlockSpec((ts, n), lambda b, g, s: (row(b, g, s), wc0 + g)),
        pl.BlockSpec((SSD_CONV, gw), lambda b, g, s: (0, g)),
        pl.BlockSpec((SSD_CONV, n), lambda b, g, s: (0, wb0 + g)),
        pl.BlockSpec((SSD_CONV, n), lambda b, g, s: (0, wc0 + g)),
        pl.BlockSpec((1, gw), lambda b, g, s: (0, g)),
        pl.BlockSpec((1, n), lambda b, g, s: (0, wb0 + g)),
        pl.BlockSpec((1, n), lambda b, g, s: (0, wc0 + g)),
        pl.BlockSpec((ng, ts, r), lambda b, g, s: (g, row(b, g, s), 0)),
        pl.BlockSpec((ng, nc, r, CHUNK), lambda b, g, s: (g, row(b, g, s), 0, 0)),
        pl.BlockSpec((ng, 1, r), lambda b, g, s: (g, 0, 0)),
        pl.BlockSpec((ng, r, 1), lambda b, g, s: (g, 0, 0)),
        pl.BlockSpec((1, gw), lambda b, g, s: (0, g)),
        pl.BlockSpec((ts, ts), lambda b, g, s: (0, 0)),
        pl.BlockSpec((CHUNK, CHUNK), lambda b, g, s: (0, 0)),
        pl.BlockSpec((2 * r, SSD_GROUP_WIDTH), lambda b, g, s: (0, 0)),
    ]
    scratch = [
        pltpu.VMEM((ng, SSD_STATE, SSD_GROUP_WIDTH), F32),
        pltpu.VMEM((ts + 8, gw), F32),
        pltpu.VMEM((ts + 8, n), F32),
        pltpu.VMEM((ts + 8, n), F32),
        pltpu.VMEM((ts, gw), F32),
        pltpu.VMEM((ts, n), BF16),
        pltpu.VMEM((ts, n), BF16),
        pltpu.VMEM((ts, gw), BF16),
        pltpu.VMEM((ts, gw), BF16),
        pltpu.VMEM((ts, gw), F32),
        pltpu.VMEM((ng, ts, r), F32),
        pltpu.VMEM((ng, nc * r, CHUNK), F32),
    ]
    return pl.pallas_call(
        _ssd_kernel,
        grid=(batch, SSD_GROUPS // ng, ns),
        in_specs=in_specs,
        out_specs=pl.BlockSpec((ts, gw), lambda b, g, s: (row(b, g, s), g)),
        out_shape=jax.ShapeDtypeStruct((t, SSD_D_INNER), F32),
        scratch_shapes=scratch,
        compiler_params=_params(("parallel", "parallel", "arbitrary")),
        name="ssd_scan",
    )(px, px, px, conv_w, conv_w, conv_w, conv_b2, conv_b2, conv_b2,
      dt, dt_t, alr, alc, dsk, bd, trit, e_mat)


def _attn_kernel(q_ref, k_ref, vt_ref, bias_ref, far_ref, lam_ref, g_ref, o_ref, sa_ref, sb_ref,
                 *, lam_init):
    i = pl.program_id(2)
    tq = q_ref.shape[0]
    dh = DIFF_HEAD_DIM
    lv = lam_ref[...]
    lam = (jnp.exp(jnp.sum(lv[0:1] * lv[1:2], axis=1, keepdims=True))
           - jnp.exp(jnp.sum(lv[2:3] * lv[3:4], axis=1, keepdims=True)) + lam_init)

    q = q_ref[...]
    hp = q.shape[1] // (2 * dh)
    ns = 2 * hp
    qs = [q[:, s * dh:(s + 1) * dh] for s in range(ns)]
    nt = (((1,), (1,)), ((), ()))

    n_tiles = bias_ref.shape[1] - 1
    h0 = pl.program_id(1) * hp
    ones = jnp.ones((16, tq), BF16)
    dv = 2 * dh

    def near_tile(t):
        valid = t <= i
        return jnp.where(valid, i - t, 0), jnp.where(valid, t, n_tiles)

    def far_tile(t):
        valid = t <= i
        j = jnp.where(valid, t - 2, 0)
        d = jnp.where(valid, i - j, 0)
        return j, [jnp.where(valid, far_ref[h0 + hh, d], NEG_BIG) for hh in range(hp)]

    def scores(j, s_ref, biases):
        kj = k_ref[pl.ds(pl.multiple_of(j * tq, tq), tq), :]
        for s in range(ns):
            sc = lax.dot_general(kj[:, s * dh:(s + 1) * dh], qs[s], nt,
                                 preferred_element_type=F32)
            s_ref[s] = sc if biases is None else sc + biases[s // 2]

    def update(j, s_ref, shifts, carry):
        vtj = vt_ref[j]
        out = []
        for s in range(ns):
            hh = s // 2
            vth = jnp.concatenate([vtj[hh * dv:(hh + 1) * dv], ones], axis=0)
            mx, acc = carry[2 * s:2 * s + 2]
            sc = s_ref[s]
            mx_new = jnp.maximum(mx, jnp.max(sc, axis=0, keepdims=True) + shifts[hh])
            p = jnp.exp2(sc - (mx_new - shifts[hh]))
            out += [mx_new, jnp.exp2(mx - mx_new) * acc
                    + jnp.dot(vth, p.astype(BF16), preferred_element_type=F32)]
        return tuple(out)

    def scores_near(t, s_ref):
        j, d = near_tile(t)
        scores(j, s_ref, [bias_ref[hh, d] for hh in range(hp)])

    def scores_far(t, s_ref):
        scores(far_tile(t)[0], s_ref, None)

    def update_near(t, s_ref, carry):
        return update(near_tile(t)[0], s_ref, [0.0] * hp, carry)

    def update_far(t, s_ref, carry):
        j, shifts = far_tile(t)
        return update(j, s_ref, shifts, carry)

    def body(u, carry):
        scores_far(2 * u + 1, sb_ref)
        carry = update_far(2 * u, sa_ref, carry)
        scores_far(2 * u + 2, sa_ref)
        return update_far(2 * u + 1, sb_ref, carry)

    carry = []
    for _ in range(ns):
        carry += [jnp.full((1, tq), NEG_BIG, F32), jnp.zeros((dv + 16, tq), F32)]
    scores_near(0, sa_ref)
    scores_near(1, sb_ref)
    carry = update_near(0, sa_ref, tuple(carry))
    scores_far(2, sa_ref)
    carry = update_near(1, sb_ref, carry)
    carry = lax.fori_loop(1, (i + 2) // 2, body, carry)
    for hh in range(hp):
        acc1, acc2 = carry[4 * hh + 1], carry[4 * hh + 3]
        o = acc1[:dv] / acc1[dv:dv + 1] - lam * (acc2[:dv] / acc2[dv:dv + 1])
        ms = jnp.mean(o * o, axis=0, keepdims=True)
        o = o * lax.rsqrt(ms + RMS_EPS) * g_ref[...] * (1.0 - lam_init)
        o_ref[:, hh * dv:(hh + 1) * dv] = o.T.astype(o_ref.dtype)


def _rel_bucket(rel):
    half = REL_BUCKETS // 2
    max_exact = half // 2
    ret = jnp.where(rel > 0, half, 0)
    n = jnp.abs(rel)
    nf = jnp.maximum(n, 1).astype(F32)
    large = max_exact + (jnp.log(nf / max_exact) / math.log(REL_MAX_DIST / max_exact)
                         * (half - max_exact)).astype(jnp.int32)
    large = jnp.minimum(large, half - 1)
    return ret + jnp.where(n < max_exact, n, large)


def _bias_lookup(rel_bias, rel):
    bucket = _rel_bucket(rel)[None]
    table = rel_bias.astype(F32)
    lead = (slice(None),) + (None,) * rel.ndim
    bias = jnp.zeros((DIFF_HEADS,) + rel.shape, F32)
    for b in range(REL_BUCKETS):
        bias = jnp.where(bucket == b, table[b][lead], bias)
    return bias * LOG2_E


def _bias_tiles(rel_bias, seq):
    tq = ATTN_TILE
    kk = jnp.arange(tq)[None, :, None]
    qq = jnp.arange(tq)[None, None, :]
    d = jnp.arange(2)[:, None, None]
    near = _bias_lookup(rel_bias, kk - qq - d * tq)
    allowed = (d > 0) | ((kk // CHUNK) <= (qq // CHUNK))
    near = jnp.where(allowed[None], near, NEG_BIG)
    masked = jnp.full((DIFF_HEADS, 1, tq, tq), NEG_BIG, F32)
    far = _bias_lookup(rel_bias, -tq * jnp.arange(seq // tq + 1))
    return jnp.concatenate([near, masked], axis=1), far


def _attn_call(pb, vt, bias_near, bias_far, lam_vecs, norm_g, layer_idx, batch, seq):
    t = batch * seq
    tq = ATTN_TILE
    nq = seq // tq
    hp = ATTN_HEADS_PER_STEP
    w = 2 * DIFF_HEAD_DIM
    wb = hp * w
    qcol0 = PB_Q // wb
    kcol0 = PB_K // wb
    lam_init = 0.8 - 0.6 * math.exp(-0.3 * layer_idx)
    return pl.pallas_call(
        functools.partial(_attn_kernel, lam_init=lam_init),
        grid=(batch, DIFF_HEADS // hp, nq),
        in_specs=[pl.BlockSpec((tq, wb), lambda b, h, i: (b * nq + i, qcol0 + h)),
                  pl.BlockSpec((seq, wb), lambda b, h, i: (b, kcol0 + h)),
                  pl.BlockSpec((nq, wb, tq), lambda b, h, i: (b, h, 0)),
                  pl.BlockSpec((hp, 3, tq, tq), lambda b, h, i: (h, 0, 0, 0)),
                  pl.BlockSpec(memory_space=pltpu.SMEM),
                  pl.BlockSpec((4, DIFF_HEAD_DIM), lambda b, h, i: (0, 0)),
                  pl.BlockSpec((w, 1), lambda b, h, i: (0, 0))],
        out_specs=pl.BlockSpec((tq, wb), lambda b, h, i: (b * nq + i, h)),
        out_shape=jax.ShapeDtypeStruct((t, DIFF_WIDTH), BF16),
        scratch_shapes=[pltpu.VMEM((2 * hp, tq, tq), F32), pltpu.VMEM((2 * hp, tq, tq), F32)],
        compiler_params=_params(("parallel", "parallel", "arbitrary")),
        name="diff_attn",
    )(pb, pb, vt, bias_near, bias_far, lam_vecs, norm_g.reshape(w, 1))


def _layer_norm(x, g, b):
    mu = jnp.mean(x, axis=1, keepdims=True)
    xc = x - mu
    var = jnp.mean(xc * xc, axis=1, keepdims=True)
    return xc * lax.rsqrt(var + LN_EPS) * g + b


def _mix_kernel(y_ref, z_ref, ao_ref, g0_ref, g1_ref, h_ref, ng_ref, wso_ref, wao_ref,
                gb_ref, wo_ref, lg_ref, lb_ref, o_ref, ot_ref):
    z = z_ref[...].astype(F32)
    yg = y_ref[...] * (z * _sigmoid(z))
    ms = jnp.mean(yg * yg, axis=1, keepdims=True)
    yn = (yg * lax.rsqrt(ms + RMS_EPS) * ng_ref[...]).astype(BF16)
    y_ssd = jnp.dot(yn, wso_ref[...], preferred_element_type=F32)
    y_att = jnp.dot(ao_ref[...], wao_ref[...], preferred_element_type=F32)
    gb = gb_ref[...]
    gate0 = _sigmoid(g0_ref[...].astype(F32) + gb[:, :D_MODEL])
    gate1 = _sigmoid(g1_ref[...].astype(F32) + gb[:, D_MODEL:])
    mixed = (gate0 * y_ssd + gate1 * y_att).astype(BF16)
    mix = jnp.dot(mixed, wo_ref[...], preferred_element_type=F32)
    out = _layer_norm(DN_ALPHA * h_ref[...] + mix, lg_ref[...], lb_ref[...])
    o_ref[...] = out
    _store_row_tiles(ot_ref, out)


ROW_TILE = (8, 128)


def _store_row_tiles(ref, rows):
    sub, lanes = ROW_TILE
    m = rows.shape[0]
    for s in range(sub):
        ref[pl.ds(s, m, stride=sub), :] = rows[:, s * lanes:(s + 1) * lanes]


def _load_row_tiles(ref):
    sub, lanes = ROW_TILE
    m = ref.shape[0] // sub
    return jnp.concatenate([ref[pl.ds(s, m, stride=sub), :] for s in range(sub)], axis=1)


def _mix_call(y, pb, ao, h, norm_g, w_ssd_out, w_attn_out, gate_b, w_o, ln_g, ln_b, tm=256):
    t = h.shape[0]
    d = D_MODEL
    gcol0 = PB_GATES // d
    const = lambda i: (0, 0)
    return pl.pallas_call(
        _mix_kernel,
        grid=(t // tm,),
        in_specs=[pl.BlockSpec((tm, SSD_D_INNER), lambda i: (i, 0)),
                  pl.BlockSpec((tm, SSD_D_INNER), lambda i: (i, 0)),
                  pl.BlockSpec((tm, DIFF_WIDTH), lambda i: (i, 0)),
                  pl.BlockSpec((tm, d), lambda i: (i, gcol0)),
                  pl.BlockSpec((tm, d), lambda i: (i, gcol0 + 1)),
                  pl.BlockSpec((tm, d), lambda i: (i, 0)),
                  pl.BlockSpec((1, SSD_D_INNER), const),
                  pl.BlockSpec((SSD_D_INNER, d), const),
                  pl.BlockSpec((DIFF_WIDTH, d), const),
                  pl.BlockSpec((1, 2 * d), const),
                  pl.BlockSpec((d, d), const),
                  pl.BlockSpec((1, d), const),
                  pl.BlockSpec((1, d), const)],
        out_specs=[pl.BlockSpec((tm, d), lambda i: (i, 0)),
                   pl.BlockSpec((tm * ROW_TILE[0], ROW_TILE[1]), lambda i: (i, 0))],
        out_shape=[jax.ShapeDtypeStruct((t, d), F32),
                   jax.ShapeDtypeStruct((t * ROW_TILE[0], ROW_TILE[1]), F32)],
        compiler_params=_params(("parallel",)),
        name="mix_ln",
    )(y, pb, ao, pb, pb, h, norm_g.reshape(1, -1), w_ssd_out.astype(BF16),
      w_attn_out.astype(BF16), gate_b.reshape(1, -1), w_o.astype(BF16),
      ln_g.reshape(1, -1), ln_b.reshape(1, -1))


def _router_kernel(h_ref, w_ref, b_ref, tri_ref, idx_ref, wt_ref, rank_ref, cnt_ref, run_ref):
    @pl.when(pl.program_id(0) == 0)
    def _():
        run_ref[...] = jnp.zeros_like(run_ref)

    tm = h_ref.shape[0]
    ne = N_EXPERTS
    logits = _dot3(h_ref[...], w_ref[...], (((1,), (0,)), ((), ()))) + b_ref[...]
    lane = lax.broadcasted_iota(jnp.int32, (tm, ne), 1).astype(F32)
    work = logits
    sel, vals = [], []
    for _ in range(TOP_K):
        mx = jnp.max(work, axis=1, keepdims=True)
        first = jnp.min(jnp.where(work == mx, lane, float(ne)), axis=1, keepdims=True)
        hit = lane == first
        sel.append((first, hit))
        vals.append(mx)
        work = jnp.where(hit, -jnp.inf, work)
    exps = [jnp.exp(v - vals[0]) for v in vals]
    denom = exps[0] + exps[1] + exps[2] + exps[3]

    onehot = jnp.zeros((tm, ne), F32)
    for _, hit in sel:
        onehot = onehot + hit.astype(F32)
    before = jnp.dot(tri_ref[...], onehot.astype(BF16), preferred_element_type=F32)
    before = before + run_ref[...]

    out_lane = lax.broadcasted_iota(jnp.int32, (tm, 128), 1)
    idx_out = jnp.zeros((tm, 128), F32)
    wt_out = jnp.zeros((tm, 128), F32)
    rank_out = jnp.zeros((tm, 128), F32)
    for k, (first, hit) in enumerate(sel):
        rank = jnp.sum(jnp.where(hit, before, 0.0), axis=1, keepdims=True)
        idx_out = jnp.where(out_lane == k, first, idx_out)
        wt_out = jnp.where(out_lane == k, exps[k] / denom, wt_out)
        rank_out = jnp.where(out_lane == k, rank, rank_out)
    idx_ref[...] = idx_out.T[0:8].astype(jnp.int32)
    wt_ref[...] = wt_out
    rank_ref[...] = rank_out.T[0:8].astype(jnp.int32)
    total = run_ref[...] + jnp.sum(onehot, axis=0, keepdims=True)
    run_ref[...] = total
    cnt_ref[...] = total


def _router_call(h, w_router, b_router, tm=512):
    t = h.shape[0]
    li = jnp.arange(tm)
    tri = (li[None, :] < li[:, None]).astype(BF16)
    const = lambda i: (0, 0)
    return pl.pallas_call(
        _router_kernel,
        grid=(t // tm,),
        in_specs=[pl.BlockSpec((tm, D_MODEL), lambda i: (i, 0)),
                  pl.BlockSpec((D_MODEL, N_EXPERTS), const),
                  pl.BlockSpec((1, N_EXPERTS), const),
                  pl.BlockSpec((tm, tm), const)],
        out_specs=[pl.BlockSpec((8, tm), lambda i: (0, i)),
                   pl.BlockSpec((tm, 128), lambda i: (i, 0)),
                   pl.BlockSpec((8, tm), lambda i: (0, i)),
                   pl.BlockSpec((1, N_EXPERTS), const)],
        out_shape=[jax.ShapeDtypeStruct((8, t), jnp.int32),
                   jax.ShapeDtypeStruct((t, 128), F32),
                   jax.ShapeDtypeStruct((8, t), jnp.int32),
                   jax.ShapeDtypeStruct((1, N_EXPERTS), F32)],
        scratch_shapes=[pltpu.VMEM((1, N_EXPERTS), F32)],
        compiler_params=_params(("arbitrary",)),
        name="router",
    )(h, w_router, b_router.reshape(1, N_EXPERTS), tri)


def _dispatch_kernel(pad_end_ref, slot_ref, x_ref, xs_out, zero_ref, sem, zero_sem):
    sub = ROW_TILE[0]
    tt = x_ref.shape[0] // sub
    n = tt * TOP_K
    zb = zero_ref.shape[0] // sub

    def row(ref, r, count=1):
        return ref.at[pl.ds(pl.multiple_of(r * sub, sub), count * sub)]

    @pl.when(pl.program_id(0) == 0)
    def _():
        zero_ref[...] = jnp.zeros_like(zero_ref)

        def clear(first):
            return pltpu.make_async_copy(zero_ref, row(xs_out, first, zb), zero_sem)

        for e in range(N_EXPERTS):
            clear(jnp.maximum(pad_end_ref[e] - zb, 0)).start()
        used = pad_end_ref[N_EXPERTS - 1] // zb
        total = xs_out.shape[0] // (zb * sub)

        def clear_tail(b, c):
            clear(b * zb).start()
            return c

        def wait_one(b, c):
            clear(0).wait()
            return c

        lax.fori_loop(used, total, clear_tail, 0)
        lax.fori_loop(used - N_EXPERTS, total, wait_one, 0)

    def start(tok, c):
        for k in range(TOP_K):
            slot = slot_ref[0, 0, k * tt + tok]
            pltpu.make_async_copy(row(x_ref, tok), row(xs_out, slot), sem).start(priority=k % 2)
        return c

    lax.fori_loop(0, tt, start, 0, unroll=8)
    pltpu.make_async_copy(row(xs_out, 0, n), row(xs_out, 0, n), sem).wait()


def _dispatch_call(xt, slots, pad_end, cap):
    sub, lanes = ROW_TILE
    t = xt.shape[0] // sub
    tt = DISPATCH_TILE
    nt = t // tt
    n = tt * TOP_K
    assert cap >= N_EXPERTS * EXPERT_BLOCK
    grid_spec = pltpu.PrefetchScalarGridSpec(
        num_scalar_prefetch=1,
        grid=(nt,),
        in_specs=[pl.BlockSpec((1, 1, n), lambda i, pe: (i, 0, 0), memory_space=pltpu.SMEM),
                  pl.BlockSpec((tt * sub, lanes), lambda i, pe: (i, 0))],
        out_specs=pl.BlockSpec(memory_space=pl.ANY),
        scratch_shapes=[pltpu.VMEM((EXPERT_BLOCK * sub, lanes), xt.dtype),
                        pltpu.SemaphoreType.DMA, pltpu.SemaphoreType.DMA],
    )
    return pl.pallas_call(
        _dispatch_kernel,
        grid_spec=grid_spec,
        out_shape=jax.ShapeDtypeStruct((cap * sub, lanes), xt.dtype),
        compiler_params=_params(("arbitrary",), disable_bounds_checks=True),
        name="moe_dispatch",
    )(pad_end, _tile_slots(slots, tt), xt)


def _expert_kernel(be_ref, nb_ref, x_ref, wg_ref, bg_ref, wu_ref, bu_ref, wd_ref, bd_ref,
                   o_ref, wgb_ref, wub_ref, wdb_ref):
    i = pl.program_id(0)
    prev = be_ref[jnp.maximum(i - 1, 0)]
    changed = jnp.logical_or(i == 0, be_ref[i] != prev)

    @pl.when(changed)
    def _():
        wgb_ref[...] = wg_ref[0].astype(BF16)
        wub_ref[...] = wu_ref[0].astype(BF16)
        wdb_ref[...] = wd_ref[0].astype(BF16)

    @pl.when(i < nb_ref[0])
    def _():
        xb = _load_row_tiles(x_ref).astype(BF16)
        g = jnp.dot(xb, wgb_ref[...], preferred_element_type=F32) + bg_ref[0]
        u = jnp.dot(xb, wub_ref[...], preferred_element_type=F32) + bu_ref[0]
        g = jnp.minimum(g, SWIGLU_LIMIT)
        u = jnp.clip(u, -SWIGLU_LIMIT, SWIGLU_LIMIT)
        act = g * _sigmoid(SWIGLU_ALPHA * g) * (u + 1.0)
        _store_row_tiles(o_ref, jnp.dot(act.astype(BF16), wdb_ref[...],
                                        preferred_element_type=F32) + bd_ref[0])

    @pl.when(i >= nb_ref[0])
    def _():
        o_ref[...] = jnp.zeros_like(o_ref)


def _expert_call(xs, blk_expert, n_used, w_gate, b_gate, w_up, b_up, w_down, b_down):
    sub, lanes = ROW_TILE
    cap = xs.shape[0] // sub
    d = sub * lanes
    bm = EXPERT_BLOCK
    nb = cap // bm
    wspec = lambda shape: pl.BlockSpec(shape, lambda i, be, nu: (be[i], 0, 0))
    grid_spec = pltpu.PrefetchScalarGridSpec(
        num_scalar_prefetch=2,
        grid=(nb,),
        in_specs=[pl.BlockSpec((bm * sub, lanes),
                               lambda i, be, nu: (jnp.minimum(i, nu[0] - 1), 0)),
                  wspec((1, d, D_FF)), wspec((1, 1, D_FF)),
                  wspec((1, d, D_FF)), wspec((1, 1, D_FF)),
                  wspec((1, D_FF, d)), wspec((1, 1, d))],
        out_specs=pl.BlockSpec((bm * sub, lanes), lambda i, be, nu: (i, 0)),
        scratch_shapes=[pltpu.VMEM((d, D_FF), BF16), pltpu.VMEM((d, D_FF), BF16),
                        pltpu.VMEM((D_FF, d), BF16)],
    )
    ne = w_gate.shape[0] * w_gate.shape[1]
    return pl.pallas_call(
        _expert_kernel,
        grid_spec=grid_spec,
        out_shape=jax.ShapeDtypeStruct((cap * sub, lanes), F32),
        compiler_params=_params(("arbitrary",)),
        name="moe_experts",
    )(blk_expert, n_used, xs, w_gate.reshape(ne, d, D_FF), b_gate.reshape(ne, 1, D_FF),
      w_up.reshape(ne, d, D_FF), b_up.reshape(ne, 1, D_FF), w_down.reshape(ne, D_FF, d),
      b_down.reshape(ne, 1, d))


def _combine_kernel(slot_ref, next_slot_ref, ys_hbm, wt_ref, h_ref, lg_ref, lb_ref, o_ref,
                    buf_ref, sems):
    i = pl.program_id(0)
    tt = h_ref.shape[0]
    n = tt * TOP_K
    sub = ROW_TILE[0]

    def issue(s_ref, buf):
        def start(r, c):
            for j in range(8):
                a = r * 8 + j
                src = pl.ds(pl.multiple_of(s_ref[0, 0, a] * sub, sub), sub)
                dst = pl.ds(pl.multiple_of(a * sub, sub), sub)
                pltpu.make_async_copy(ys_hbm.at[src], buf_ref.at[buf, dst],
                                      sems.at[buf]).start(priority=j % 2)
            return c

        lax.fori_loop(0, n // 8, start, 0)

    @pl.when(i == 0)
    def _():
        issue(slot_ref, 0)

    @pl.when(i + 1 < pl.num_programs(0))
    def _():
        issue(next_slot_ref, (i + 1) % 2)

    cur = i % 2
    pltpu.make_async_copy(buf_ref.at[cur], buf_ref.at[cur], sems.at[cur]).wait()

    wt = wt_ref[...]
    rows = lambda k: _load_row_tiles(buf_ref.at[cur, pl.ds(k * tt * sub, tt * sub)])
    ff = wt[:, 0:1] * rows(0)
    for k in range(1, TOP_K):
        ff = ff + wt[:, k:k + 1] * rows(k)
    o_ref[...] = _layer_norm(DN_ALPHA * h_ref[...] + ff, lg_ref[...], lb_ref[...])


def _combine_call(ys, slots, wts, h, ln_g, ln_b):
    t, d = h.shape
    tt = COMBINE_TILE
    nt = t // tt
    n = tt * TOP_K
    slots_km = _tile_slots(slots, tt)
    const = lambda i: (0, 0)
    return pl.pallas_call(
        _combine_kernel,
        grid=(nt,),
        in_specs=[pl.BlockSpec((1, 1, n), lambda i: (i, 0, 0), memory_space=pltpu.SMEM),
                  pl.BlockSpec((1, 1, n), lambda i: (jnp.minimum(i + 1, nt - 1), 0, 0),
                               memory_space=pltpu.SMEM),
                  pl.BlockSpec(memory_space=pl.ANY),
                  pl.BlockSpec((tt, 128), lambda i: (i, 0)),
                  pl.BlockSpec((tt, d), lambda i: (i, 0)),
                  pl.BlockSpec((1, d), const),
                  pl.BlockSpec((1, d), const)],
        out_specs=pl.BlockSpec((tt, d), lambda i: (i, 0)),
        out_shape=jax.ShapeDtypeStruct((t, d), F32),
        scratch_shapes=[pltpu.VMEM((2, n * ROW_TILE[0], ROW_TILE[1]), F32),
                        pltpu.SemaphoreType.DMA((2,))],
        compiler_params=_params(("arbitrary",), disable_bounds_checks=True),
        name="moe_combine_ln",
    )(slots_km, slots_km, ys, wts, h, ln_g.reshape(1, d), ln_b.reshape(1, d))


def _tile_slots(slots, tt):
    t = slots.shape[1]
    return slots.reshape(TOP_K, t // tt, tt).transpose(1, 0, 2).reshape(t // tt, 1, TOP_K * tt)


def _moe_layout(idx, rank, counts, n_blocks):
    bm = EXPERT_BLOCK
    counts = counts.reshape(N_EXPERTS).astype(jnp.int32)
    padded = (counts + bm - 1) // bm * bm
    pad_end = jnp.cumsum(padded)
    pad_start = pad_end - padded
    slots = rank
    for e in range(N_EXPERTS):
        slots = slots + jnp.where(idx == e, pad_start[e], 0)
    blk_start = jnp.arange(n_blocks, dtype=jnp.int32) * bm
    blk_expert = jnp.sum((pad_end[None, :] <= blk_start[:, None]).astype(jnp.int32), axis=1)
    blk_expert = jnp.minimum(blk_expert, N_EXPERTS - 1)
    n_used = (pad_end[-1] // bm).reshape(1)
    return slots.astype(jnp.int32), blk_expert, n_used, pad_end.astype(jnp.int32)


def kernel(x, rel_bias, w_in, conv_w, conv_b, dt_bias, a_log, d_skip, ssd_norm_g, w_ssd_out, diff_lambda, diff_norm_g, w_attn_out, gate_b, w_o, ln1_g, ln1_b, w_router, b_router, w_gate, b_gate, w_up, b_up, w_down, b_down, ln2_g, ln2_b):
    batch, seq, d = x.shape
    t = batch * seq
    n_assign = t * TOP_K
    n_blocks = (n_assign + N_EXPERTS * (EXPERT_BLOCK - 1) + EXPERT_BLOCK - 1) // EXPERT_BLOCK
    cap = n_blocks * EXPERT_BLOCK
    bias_near, bias_far = _bias_tiles(rel_bias, seq)
    qk_scale = jnp.concatenate([jnp.full((DIFF_WIDTH,), DIFF_HEAD_DIM ** -0.5 * LOG2_E, F32),
                                jnp.ones((DIFF_WIDTH,), F32)])[None, :]

    h = x.reshape(t, d)
    for l in range(DEPTH):
        w_l = w_in[l]
        w_x = w_l[:, OFF_XBC:OFF_DT].astype(BF16)
        w_b = jnp.concatenate([w_l[:, :OFF_XBC], w_l[:, OFF_G:],
                               w_l[:, OFF_Q:OFF_V] * qk_scale], axis=1).astype(BF16)
        w_v = w_l[:, OFF_V:OFF_G]
        px = _matmul(h, w_x, F32, 1024, 1024)
        pb = _matmul(h, w_b, BF16, 1024, 1024)
        vt = _matmul_nt(h, w_v, ATTN_TILE)
        dt, dt_t = _dt_call(h, w_l[:, OFF_DT:OFF_Q], dt_bias[l])
        y = _ssd_call(px, conv_w[l], conv_b[l], dt, dt_t, a_log[l], d_skip[l], batch, seq)
        ao = _attn_call(pb, vt, bias_near, bias_far, diff_lambda[l], diff_norm_g[l], l, batch,
                        seq)
        h1, h1_tiles = _mix_call(y, pb, ao, h, ssd_norm_g[l], w_ssd_out[l], w_attn_out[l], gate_b[l],
                       w_o[l], ln1_g[l], ln1_b[l])
        idx, wts, rank, counts = _router_call(h1, w_router[l], b_router[l])
        slots, blk_expert, n_used, pad_end = _moe_layout(idx[:TOP_K], rank[:TOP_K], counts,
                                                         n_blocks)
        xs = _dispatch_call(h1_tiles, slots, pad_end, cap)
        ys = _expert_call(xs, blk_expert + l * N_EXPERTS, n_used, w_gate, b_gate, w_up, b_up,
                          w_down, b_down)
        h = _combine_call(ys, slots, wts, h1, ln2_g[l], ln2_b[l])
    return h.reshape(batch, seq, d)
```

```python
import functools
import math

import jax
import jax.numpy as jnp
from jax import lax
from jax.experimental import pallas as pl
from jax.experimental.pallas import tpu as pltpu

F32 = jnp.float32
BF16 = jnp.bfloat16

D_MODEL = 1024
DEPTH = 2
CHUNK = 64

SSD_D_INNER = 2048
SSD_HEAD_DIM = 64
SSD_HEADS = 32
SSD_GROUPS = 4
SSD_HEADS_PER_GROUP = 8
SSD_STATE = 128
SSD_CONV = 4
SSD_CONV_DIM = 3072
SSD_GROUP_WIDTH = SSD_HEADS_PER_GROUP * SSD_HEAD_DIM

DIFF_HEAD_DIM = 64
DIFF_HEADS = 8
DIFF_WIDTH = 1024

REL_BUCKETS = 32
REL_MAX_DIST = 128

N_EXPERTS = 32
TOP_K = 4
D_FF = 1024
SWIGLU_ALPHA = 1.702
SWIGLU_LIMIT = 7.0

DN_ALPHA = (2 * DEPTH) ** 0.25
LN_EPS = 1e-5
RMS_EPS = 1e-5

OFF_XBC = 2048
OFF_DT = 5120
OFF_Q = 5152
OFF_V = 7200
OFF_G = 8224
MISALIGN = OFF_Q % 128
PB_GATES = 2048
PB_Q = 4096
PB_K = 5120

VMEM_LIMIT_BYTES = 56 * 1024 * 1024

ATTN_TILE = 256
ATTN_HEADS_PER_STEP = 4
SSD_TILE = 256
SSD_GROUPS_PER_STEP = 4
EXPERT_BLOCK = 256
DISPATCH_TILE = 1024
COMBINE_TILE = 256
NEG_BIG = -1e30
LOG2_E = math.log2(math.e)


def _params(semantics, **kwargs):
    return pltpu.CompilerParams(dimension_semantics=semantics,
                                vmem_limit_bytes=VMEM_LIMIT_BYTES, **kwargs)


def _sigmoid(x):
    return 1.0 / (1.0 + jnp.exp(-x))


def _split2(v):
    hi = v.astype(BF16)
    lo = (v - hi.astype(F32)).astype(BF16)
    return hi, lo


def _dot3(a, b, dims):
    ah, al = _split2(a)
    bh, bl = _split2(b)
    dot = lambda p, q: lax.dot_general(p, q, dims, preferred_element_type=F32)
    return dot(ah, bh) + dot(ah, bl) + dot(al, bh)


def _split3(v):
    hi = v.astype(BF16)
    r = v - hi.astype(F32)
    mid = r.astype(BF16)
    lo = (r - mid.astype(F32)).astype(BF16)
    return hi, mid, lo


def _mm_kernel(x_ref, w_ref, o_ref, xb_ref):
    @pl.when(pl.program_id(1) == 0)
    def _():
        xb_ref[...] = x_ref[...].astype(BF16)

    o_ref[...] = jnp.dot(xb_ref[...], w_ref[...],
                         preferred_element_type=F32).astype(o_ref.dtype)


def _matmul(x, w, out_dtype, bm, bn):
    m, k = x.shape
    n = w.shape[1]
    return pl.pallas_call(
        _mm_kernel,
        grid=(m // bm, n // bn),
        in_specs=[pl.BlockSpec((bm, k), lambda i, j: (i, 0)),
                  pl.BlockSpec((k, bn), lambda i, j: (0, j))],
        out_specs=pl.BlockSpec((bm, bn), lambda i, j: (i, j)),
        out_shape=jax.ShapeDtypeStruct((m, n), out_dtype),
        scratch_shapes=[pltpu.VMEM((bm, k), BF16)],
        compiler_params=_params(("parallel", "arbitrary")),
        name="in_proj",
    )(x, w)


def _regroup_kernel(src_ref, shift_ref, scale_ref, a_ref, b_ref, o_ref, *, q_scale):
    j = pl.program_id(0)
    a = a_ref[0]
    b = b_ref[0]
    tail = 128 - MISALIGN
    lane = lax.broadcasted_iota(jnp.int32, a.shape, 1)
    shifted = jnp.where(lane < tail, pltpu.roll(a, tail, axis=1), pltpu.roll(b, tail, axis=1))
    out = jnp.where(shift_ref[j] == 1, shifted, a)
    out = jnp.where(scale_ref[j] == 1, out * q_scale, out)
    o_ref[...] = out.astype(o_ref.dtype)


def _regroup_call(w_in, layer, starts, scaled, out_dtype, q_scale):
    depth, k, cols = w_in.shape
    src = jnp.asarray([s // 128 for s in starts], jnp.int32)
    shift = jnp.asarray([int(s % 128 != 0) for s in starts], jnp.int32)
    assert all(s % 128 in (0, MISALIGN) for s in starts)
    last = (cols - 1) // 128
    grid_spec = pltpu.PrefetchScalarGridSpec(
        num_scalar_prefetch=3,
        grid=(len(starts),),
        in_specs=[pl.BlockSpec((1, k, 128), lambda j, src, sh, sc: (layer, 0, src[j])),
                  pl.BlockSpec((1, k, 128),
                               lambda j, src, sh, sc: (layer, 0, jnp.minimum(src[j] + 1, last)))],
        out_specs=pl.BlockSpec((k, 128), lambda j, src, sh, sc: (0, j)),
    )
    return pl.pallas_call(
        functools.partial(_regroup_kernel, q_scale=q_scale),
        grid_spec=grid_spec,
        out_shape=jax.ShapeDtypeStruct((k, 128 * len(starts)), out_dtype),
        compiler_params=_params(("arbitrary",)),
        name="w_regroup",
    )(src, shift, jnp.asarray(scaled, jnp.int32), w_in, w_in)


def _mm_nt_kernel(x_ref, w_ref, o_ref, wt_ref):
    @pl.when(pl.program_id(0) == 0)
    def _():
        wt_ref[...] = w_ref[...].T.astype(BF16)

    res = lax.dot_general(wt_ref[...], x_ref[...].astype(BF16), (((1,), (1,)), ((), ())),
                          preferred_element_type=F32)
    tile = o_ref.shape[2]
    for c in range(o_ref.shape[0]):
        o_ref[c] = res[:, c * tile:(c + 1) * tile].astype(o_ref.dtype)


def _matmul_nt(x, w, tile, bm=1024):
    m, k = x.shape
    n = w.shape[1]
    return pl.pallas_call(
        _mm_nt_kernel,
        grid=(m // bm,),
        in_specs=[pl.BlockSpec((bm, k), lambda i: (i, 0)),
                  pl.BlockSpec((k, n), lambda i: (0, 0))],
        out_specs=pl.BlockSpec((bm // tile, n, tile), lambda i: (i, 0, 0)),
        out_shape=jax.ShapeDtypeStruct((m // tile, n, tile), BF16),
        scratch_shapes=[pltpu.VMEM((n, k), BF16)],
        compiler_params=_params(("arbitrary",)),
        name="v_proj_t",
    )(x, w)


def _softplus(x):
    return jnp.maximum(x, 0.0) + jnp.log(1.0 + jnp.exp(-jnp.abs(x)))


def _dt_kernel(x_ref, w_ref, b_ref, dt_ref, dtt_ref):
    x = x_ref[...]
    tm = x.shape[0]
    raw = _dot3(x, w_ref[...], (((1,), (0,)), ((), ())))
    dt = _softplus(raw + b_ref[...])
    dt_t = dt.T
    r = SSD_HEADS_PER_GROUP
    for g in range(SSD_GROUPS):
        dt_ref[g] = dt[:, g * r:(g + 1) * r]
        for j in range(tm // CHUNK):
            dtt_ref[g, j] = dt_t[g * r:(g + 1) * r, j * CHUNK:(j + 1) * CHUNK]


def _dt_call(h, w_dt, dt_bias, tm=512):
    t = h.shape[0]
    r = SSD_HEADS_PER_GROUP
    lanes = 128
    w_pad = jnp.pad(w_dt, ((0, 0), (0, lanes - SSD_HEADS)))
    b_pad = jnp.pad(dt_bias, (0, lanes - SSD_HEADS)).reshape(1, lanes)
    return pl.pallas_call(
        _dt_kernel,
        grid=(t // tm,),
        in_specs=[pl.BlockSpec((tm, D_MODEL), lambda i: (i, 0)),
                  pl.BlockSpec((D_MODEL, lanes), lambda i: (0, 0)),
                  pl.BlockSpec((1, lanes), lambda i: (0, 0))],
        out_specs=[pl.BlockSpec((SSD_GROUPS, tm, r), lambda i: (0, i, 0)),
                   pl.BlockSpec((SSD_GROUPS, tm // CHUNK, r, CHUNK), lambda i: (0, i, 0, 0))],
        out_shape=[jax.ShapeDtypeStruct((SSD_GROUPS, t, r), F32),
                   jax.ShapeDtypeStruct((SSD_GROUPS, t // CHUNK, r, CHUNK), F32)],
        compiler_params=_params(("parallel",)),
        name="dt_proj",
    )(h, w_pad, b_pad)


def _ssd_kernel(x_ref, b_ref, c_ref, wx_ref, wb_ref, wc_ref, bx_ref, bb_ref, bc_ref,
                dt_ref, dtt_ref, alr_ref, alc_ref, dsk_ref, bd_ref, trit_ref, e_ref,
                y_ref,
                state_ref, ux_ref, ub_ref, uc_ref, xc_ref, bcv_ref, ccv_ref, xdt_ref,
                wst_ref, eacs_ref, acs_ref, acst_ref):
    s_idx = pl.program_id(2)
    ts = x_ref.shape[0]
    nc = ts // CHUNK
    r = SSD_HEADS_PER_GROUP
    gw = SSD_GROUP_WIDTH
    n = SSD_STATE
    ng = dt_ref.shape[0]

    @pl.when(s_idx == 0)
    def _():
        state_ref[...] = jnp.zeros_like(state_ref)
        ux_ref[0:8, :] = jnp.zeros((8, ux_ref.shape[1]), F32)
        ub_ref[0:8, :] = jnp.zeros((8, ub_ref.shape[1]), F32)
        uc_ref[0:8, :] = jnp.zeros((8, uc_ref.shape[1]), F32)

    def conv_silu(raw_ref, u_ref, w_ref, bias_ref):
        u_ref[8:8 + ts, :] = raw_ref[...]
        u = u_ref[...]
        u2 = pltpu.roll(u, 2, axis=0)
        even = w_ref[3:4, :] * u + w_ref[1:2, :] * u2
        odd = w_ref[2:3, :] * u + w_ref[0:1, :] * u2
        acc = (even + pltpu.roll(odd, 1, axis=0))[8:8 + ts] + bias_ref[...]
        u_ref[0:8, :] = u_ref[ts:ts + 8, :]
        return acc * _sigmoid(acc)

    xc_ref[...] = conv_silu(x_ref, ux_ref, wx_ref, bx_ref)
    bcv_ref[...] = conv_silu(b_ref, ub_ref, wb_ref, bb_ref).astype(BF16)
    ccv_ref[...] = conv_silu(c_ref, uc_ref, wc_ref, bc_ref).astype(BF16)

    bd = bd_ref[...]
    trit = trit_ref[...]
    e_mat = e_ref[...]

    def expand(v):
        return jnp.dot(jnp.concatenate(_split2(v), axis=1), e_mat, preferred_element_type=F32)

    for g in range(ng):
        gl = slice(g * gw, (g + 1) * gw)
        a_row = -jnp.exp(alr_ref[g])
        a_col = -jnp.exp(alc_ref[g])
        dt = dt_ref[g]
        a = dt * a_row
        acs = jnp.zeros((ts, r), F32)
        for part in _split3(a):
            acs = acs + jnp.dot(bd, part, preferred_element_type=F32)
        acs_ref[g] = acs
        a_t = dtt_ref[g].reshape(nc * r, CHUNK) * jnp.concatenate([a_col] * nc, axis=0)
        acs_t = jnp.zeros((nc * r, CHUNK), F32)
        for part in _split3(a_t):
            acs_t = acs_t + jnp.dot(part, trit, preferred_element_type=F32)
        acst_ref[g] = acs_t
        a_last = jnp.concatenate(
            [jnp.broadcast_to(acs[c * CHUNK + CHUNK - 1:c * CHUNK + CHUNK, :], (CHUNK, r))
             for c in range(nc)], axis=0)
        xdt = xc_ref[:, gl] * expand(dt)
        xdt_ref[:, gl] = xdt.astype(BF16)
        wst_ref[:, gl] = (xdt * expand(jnp.exp(a_last - acs))).astype(BF16)
        eacs_ref[:, gl] = expand(jnp.exp(acs))

    row_i = lax.broadcasted_iota(jnp.int32, (CHUNK, CHUNK), 0)
    col_i = lax.broadcasted_iota(jnp.int32, (CHUNK, CHUNK), 1)
    tril = col_i <= row_i
    left_half = lax.broadcasted_iota(jnp.int32, (CHUNK, 128), 1) < SSD_HEAD_DIM

    for c in range(nc):
        rows = slice(c * CHUNK, (c + 1) * CHUNK)
        for g in range(ng):
            cc = ccv_ref[rows, g * n:(g + 1) * n]
            bc = bcv_ref[rows, g * n:(g + 1) * n]
            cb = lax.dot_general(cc, bc, (((1,), (1,)), ((), ())), preferred_element_type=F32)
            acs_c = acs_ref[g, rows, :]
            acs_tc = acst_ref[g, c * r:(c + 1) * r, :]
            for p in range(r // 2):
                lo = g * gw + p * 128
                xp = xdt_ref[rows, lo:lo + 128]
                halves = []
                for hh in (2 * p, 2 * p + 1):
                    diff = acs_c[:, hh:hh + 1] - acs_tc[hh:hh + 1, :]
                    decay = jnp.exp(jnp.where(tril, diff, -jnp.inf))
                    halves.append(jnp.dot((cb * decay).astype(BF16), xp,
                                          preferred_element_type=F32))
                y_diag = jnp.where(left_half, halves[0], halves[1])
                y_ref[rows, lo:lo + 128] = (y_diag
                                            + dsk_ref[:, lo:lo + 128] * xc_ref[rows, lo:lo + 128])

    def chunk_body(c, carry):
        rows = pl.ds(pl.multiple_of(c * CHUNK, CHUNK), CHUNK)
        for g in range(ng):
            gl = slice(g * gw, (g + 1) * gw)
            cc = ccv_ref[rows, g * n:(g + 1) * n]
            bc = bcv_ref[rows, g * n:(g + 1) * n]
            state = state_ref[g]
            eacs = eacs_ref[rows, gl]
            y_off = jnp.dot(cc, state.astype(BF16), preferred_element_type=F32) * eacs
            y_ref[rows, gl] = y_ref[rows, gl] + y_off
            upd = lax.dot_general(bc, wst_ref[rows, gl], (((0,), (0,)), ((), ())),
                                  preferred_element_type=F32)
            state_ref[g] = state * eacs[CHUNK - 1:CHUNK, :] + upd
        return carry

    lax.fori_loop(0, nc, chunk_body, 0)


def _ssd_call(px, conv_w, conv_b, dt, dt_t, a_log, d_skip, batch, seq):
    t = batch * seq
    ts = SSD_TILE
    ns = seq // ts
    nc = ts // CHUNK
    r = SSD_HEADS_PER_GROUP
    ng = SSD_GROUPS_PER_STEP
    gw = ng * SSD_GROUP_WIDTH
    n = ng * SSD_STATE
    wb0 = SSD_D_INNER // n
    wc0 = wb0 + SSD_GROUPS // ng

    li = jnp.arange(ts)
    bd = ((li[None, :] <= li[:, None]) & (li[None, :] // CHUNK == li[:, None] // CHUNK)).astype(BF16)
    lc = jnp.arange(CHUNK)
    trit = (lc[:, None] <= lc[None, :]).astype(BF16)
    e_mat = (jnp.arange(SSD_GROUP_WIDTH)[None, :] // SSD_HEAD_DIM
             == jnp.arange(2 * r)[:, None] % r).astype(BF16)
    conv_b2 = conv_b.reshape(1, SSD_CONV_DIM)
    alr = a_log.reshape(SSD_GROUPS, 1, r)
    alc = a_log.reshape(SSD_GROUPS, r, 1)
    dsk = jnp.repeat(d_skip, SSD_HEAD_DIM).reshape(1, SSD_D_INNER)

    row = lambda b, g, s: b * ns + s
    in_specs = [
        pl.BlockSpec((ts, gw), lambda b, g, s: (row(b, g, s), g)),
        pl.BlockSpec((ts, n), lambda b, g, s: (row(b, g, s), wb0 + g)),
        pl.BlockSpec((ts, n), lambda b, g, s: (row(b, g, s), wc0 + g)),
        pl.BlockSpec((SSD_CONV, gw), lambda b, g, s: (0, g)),
        pl.BlockSpec((SSD_CONV, n), lambda b, g, s: (0, wb0 + g)),
        pl.BlockSpec((SSD_CONV, n), lambda b, g, s: (0, wc0 + g)),
        pl.BlockSpec((1, gw), lambda b, g, s: (0, g)),
        pl.BlockSpec((1, n), lambda b, g, s: (0, wb0 + g)),
        pl.BlockSpec((1, n), lambda b, g, s: (0, wc0 + g)),
        pl.BlockSpec((ng, ts, r), lambda b, g, s: (g, row(b, g, s), 0)),
        pl.BlockSpec((ng, nc, r, CHUNK), lambda b, g, s: (g, row(b, g, s), 0, 0)),
        pl.BlockSpec((ng, 1, r), lambda b, g, s: (g, 0, 0)),
        pl.BlockSpec((ng, r, 1), lambda b, g, s: (g, 0, 0)),
        pl.BlockSpec((1, gw), lambda b, g, s: (0, g)),
        pl.BlockSpec((ts, ts), lambda b, g, s: (0, 0)),
        pl.BlockSpec((CHUNK, CHUNK), lambda b, g, s: (0, 0)),
        pl.BlockSpec((2 * r, SSD_GROUP_WIDTH), lambda b, g, s: (0, 0)),
    ]
    scratch = [
        pltpu.VMEM((ng, SSD_STATE, SSD_GROUP_WIDTH), F32),
        pltpu.VMEM((ts + 8, gw), F32),
        pltpu.VMEM((ts + 8, n), F32),
        pltpu.VMEM((ts + 8, n), F32),
        pltpu.VMEM((ts, gw), F32),
        pltpu.VMEM((ts, n), BF16),
        pltpu.VMEM((ts, n), BF16),
        pltpu.VMEM((ts, gw), BF16),
        pltpu.VMEM((ts, gw), BF16),
        pltpu.VMEM((ts, gw), F32),
        pltpu.VMEM((ng, ts, r), F32),
        pltpu.VMEM((ng, nc * r, CHUNK), F32),
    ]
    return pl.pallas_call(
        _ssd_kernel,
        grid=(batch, SSD_GROUPS // ng, ns),
        in_specs=in_specs,
        out_specs=pl.BlockSpec((ts, gw), lambda b, g, s: (row(b, g, s), g)),
        out_shape=jax.ShapeDtypeStruct((t, SSD_D_INNER), F32),
        scratch_shapes=scratch,
        compiler_params=_params(("parallel", "parallel", "arbitrary")),
        name="ssd_scan",
    )(px, px, px, conv_w, conv_w, conv_w, conv_b2, conv_b2, conv_b2,
      dt, dt_t, alr, alc, dsk, bd, trit, e_mat)


def _attn_kernel(q_ref, k_ref, vt_ref, bias_ref, far_ref, lam_ref, g_ref, o_ref, sa_ref, sb_ref,
                 *, lam_init):
    i = pl.program_id(2)
    tq = q_ref.shape[0]
    dh = DIFF_HEAD_DIM
    lv = lam_ref[...]
    lam = (jnp.exp(jnp.sum(lv[0:1] * lv[1:2], axis=1, keepdims=True))
           - jnp.exp(jnp.sum(lv[2:3] * lv[3:4], axis=1, keepdims=True)) + lam_init)

    q = q_ref[...]
    hp = q.shape[1] // (2 * dh)
    ns = 2 * hp
    qs = [q[:, s * dh:(s + 1) * dh] for s in range(ns)]
    nt = (((1,), (1,)), ((), ()))

    n_tiles = bias_ref.shape[1] - 1
    h0 = pl.program_id(1) * hp
    ones = jnp.ones((16, tq), BF16)
    dv = 2 * dh

    def near_tile(t):
        valid = t <= i
        return jnp.where(valid, i - t, 0), jnp.where(valid, t, n_tiles)

    def far_tile(t):
        valid = t <= i
        j = jnp.where(valid, t - 2, 0)
        d = jnp.where(valid, i - j, 0)
        return j, [jnp.where(valid, far_ref[h0 + hh, d], NEG_BIG) for hh in range(hp)]

    def scores(j, s_ref, biases):
        kj = k_ref[pl.ds(pl.multiple_of(j * tq, tq), tq), :]
        for s in range(ns):
            sc = lax.dot_general(kj[:, s * dh:(s + 1) * dh], qs[s], nt,
                                 preferred_element_type=F32)
            s_ref[s] = sc if biases is None else sc + biases[s // 2]

    def update(j, s_ref, shifts, carry):
        vtj = vt_ref[j]
        out = []
        for s in range(ns):
            hh = s // 2
            vth = jnp.concatenate([vtj[hh * dv:(hh + 1) * dv], ones], axis=0)
            mx, acc = carry[2 * s:2 * s + 2]
            sc = s_ref[s]
            mx_new = jnp.maximum(mx, jnp.max(sc, axis=0, keepdims=True) + shifts[hh])
            p = jnp.exp2(sc - (mx_new - shifts[hh]))
            out += [mx_new, jnp.exp2(mx - mx_new) * acc
                    + jnp.dot(vth, p.astype(BF16), preferred_element_type=F32)]
        return tuple(out)

    def scores_near(t, s_ref):
        j, d = near_tile(t)
        scores(j, s_ref, [bias_ref[hh, d] for hh in range(hp)])

    def scores_far(t, s_ref):
        scores(far_tile(t)[0], s_ref, None)

    def update_near(t, s_ref, carry):
        return update(near_tile(t)[0], s_ref, [0.0] * hp, carry)

    def update_far(t, s_ref, carry):
        j, shifts = far_tile(t)
        return update(j, s_ref, shifts, carry)

    def body(u, carry):
        scores_far(2 * u + 1, sb_ref)
        carry = update_far(2 * u, sa_ref, carry)
        scores_far(2 * u + 2, sa_ref)
        return update_far(2 * u + 1, sb_ref, carry)

    carry = []
    for _ in range(ns):
        carry += [jnp.full((1, tq), NEG_BIG, F32), jnp.zeros((dv + 16, tq), F32)]
    scores_near(0, sa_ref)
    scores_near(1, sb_ref)
    carry = update_near(0, sa_ref, tuple(carry))
    scores_far(2, sa_ref)
    carry = update_near(1, sb_ref, carry)
    carry = lax.fori_loop(1, (i + 2) // 2, body, carry)
    for hh in range(hp):
        acc1, acc2 = carry[4 * hh + 1], carry[4 * hh + 3]
        o = acc1[:dv] / acc1[dv:dv + 1] - lam * (acc2[:dv] / acc2[dv:dv + 1])
        ms = jnp.mean(o * o, axis=0, keepdims=True)
        o = o * lax.rsqrt(ms + RMS_EPS) * g_ref[...] * (1.0 - lam_init)
        o_ref[:, hh * dv:(hh + 1) * dv] = o.T.astype(o_ref.dtype)


def _rel_bucket(rel):
    half = REL_BUCKETS // 2
    max_exact = half // 2
    ret = jnp.where(rel > 0, half, 0)
    n = jnp.abs(rel)
    nf = jnp.maximum(n, 1).astype(F32)
    large = max_exact + (jnp.log(nf / max_exact) / math.log(REL_MAX_DIST / max_exact)
                         * (half - max_exact)).astype(jnp.int32)
    large = jnp.minimum(large, half - 1)
    return ret + jnp.where(n < max_exact, n, large)


def _bias_lookup(rel_bias, rel):
    bucket = _rel_bucket(rel)[None]
    table = rel_bias.astype(F32)
    lead = (slice(None),) + (None,) * rel.ndim
    bias = jnp.zeros((DIFF_HEADS,) + rel.shape, F32)
    for b in range(REL_BUCKETS):
        bias = jnp.where(bucket == b, table[b][lead], bias)
    return bias * LOG2_E


def _bias_tiles(rel_bias, seq):
    tq = ATTN_TILE
    kk = jnp.arange(tq)[None, :, None]
    qq = jnp.arange(tq)[None, None, :]
    d = jnp.arange(2)[:, None, None]
    near = _bias_lookup(rel_bias, kk - qq - d * tq)
    allowed = (d > 0) | ((kk // CHUNK) <= (qq // CHUNK))
    near = jnp.where(allowed[None], near, NEG_BIG)
    masked = jnp.full((DIFF_HEADS, 1, tq, tq), NEG_BIG, F32)
    far = _bias_lookup(rel_bias, -tq * jnp.arange(seq // tq + 1))
    return jnp.concatenate([near, masked], axis=1), far


def _attn_call(pb, vt, bias_near, bias_far, lam_vecs, norm_g, layer_idx, batch, seq):
    t = batch * seq
    tq = ATTN_TILE
    nq = seq // tq
    hp = ATTN_HEADS_PER_STEP
    w = 2 * DIFF_HEAD_DIM
    wb = hp * w
    qcol0 = PB_Q // wb
    kcol0 = PB_K // wb
    lam_init = 0.8 - 0.6 * math.exp(-0.3 * layer_idx)
    return pl.pallas_call(
        functools.partial(_attn_kernel, lam_init=lam_init),
        grid=(batch, DIFF_HEADS // hp, nq),
        in_specs=[pl.BlockSpec((tq, wb), lambda b, h, i: (b * nq + i, qcol0 + h)),
                  pl.BlockSpec((seq, wb), lambda b, h, i: (b, kcol0 + h)),
                  pl.BlockSpec((nq, wb, tq), lambda b, h, i: (b, h, 0)),
                  pl.BlockSpec((hp, 3, tq, tq), lambda b, h, i: (h, 0, 0, 0)),
                  pl.BlockSpec(memory_space=pltpu.SMEM),
                  pl.BlockSpec((4, DIFF_HEAD_DIM), lambda b, h, i: (0, 0)),
                  pl.BlockSpec((w, 1), lambda b, h, i: (0, 0))],
        out_specs=pl.BlockSpec((tq, wb), lambda b, h, i: (b * nq + i, h)),
        out_shape=jax.ShapeDtypeStruct((t, DIFF_WIDTH), BF16),
        scratch_shapes=[pltpu.VMEM((2 * hp, tq, tq), F32), pltpu.VMEM((2 * hp, tq, tq), F32)],
        compiler_params=_params(("parallel", "parallel", "arbitrary")),
        name="diff_attn",
    )(pb, pb, vt, bias_near, bias_far, lam_vecs, norm_g.reshape(w, 1))


def _layer_norm(x, g, b):
    mu = jnp.mean(x, axis=1, keepdims=True)
    xc = x - mu
    var = jnp.mean(xc * xc, axis=1, keepdims=True)
    return xc * lax.rsqrt(var + LN_EPS) * g + b


def _mix_kernel(y_ref, z_ref, ao_ref, g0_ref, g1_ref, h_ref, ng_ref, wso_ref, wao_ref,
                gb_ref, wo_ref, lg_ref, lb_ref, o_ref, ot_ref):
    z = z_ref[...].astype(F32)
    yg = y_ref[...] * (z * _sigmoid(z))
    ms = jnp.mean(yg * yg, axis=1, keepdims=True)
    yn = (yg * lax.rsqrt(ms + RMS_EPS) * ng_ref[...]).astype(BF16)
    y_ssd = jnp.dot(yn, wso_ref[...], preferred_element_type=F32)
    y_att = jnp.dot(ao_ref[...], wao_ref[...], preferred_element_type=F32)
    gb = gb_ref[...]
    gate0 = _sigmoid(g0_ref[...].astype(F32) + gb[:, :D_MODEL])
    gate1 = _sigmoid(g1_ref[...].astype(F32) + gb[:, D_MODEL:])
    mixed = (gate0 * y_ssd + gate1 * y_att).astype(BF16)
    mix = jnp.dot(mixed, wo_ref[...], preferred_element_type=F32)
    out = _layer_norm(DN_ALPHA * h_ref[...] + mix, lg_ref[...], lb_ref[...])
    o_ref[...] = out
    _store_row_tiles(ot_ref, out)


ROW_TILE = (8, 128)


def _store_row_tiles(ref, rows):
    sub, lanes = ROW_TILE
    m = rows.shape[0]
    for s in range(sub):
        ref[pl.ds(s, m, stride=sub), :] = rows[:, s * lanes:(s + 1) * lanes]


def _load_row_tiles(ref):
    sub, lanes = ROW_TILE
    m = ref.shape[0] // sub
    return jnp.concatenate([ref[pl.ds(s, m, stride=sub), :] for s in range(sub)], axis=1)


def _mix_call(y, pb, ao, h, norm_g, w_ssd_out, w_attn_out, gate_b, w_o, ln_g, ln_b, tm=256):
    t = h.shape[0]
    d = D_MODEL
    gcol0 = PB_GATES // d
    const = lambda i: (0, 0)
    return pl.pallas_call(
        _mix_kernel,
        grid=(t // tm,),
        in_specs=[pl.BlockSpec((tm, SSD_D_INNER), lambda i: (i, 0)),
                  pl.BlockSpec((tm, SSD_D_INNER), lambda i: (i, 0)),
                  pl.BlockSpec((tm, DIFF_WIDTH), lambda i: (i, 0)),
                  pl.BlockSpec((tm, d), lambda i: (i, gcol0)),
                  pl.BlockSpec((tm, d), lambda i: (i, gcol0 + 1)),
                  pl.BlockSpec((tm, d), lambda i: (i, 0)),
                  pl.BlockSpec((1, SSD_D_INNER), const),
                  pl.BlockSpec((SSD_D_INNER, d), const),
                  pl.BlockSpec((DIFF_WIDTH, d), const),
                  pl.BlockSpec((1, 2 * d), const),
                  pl.BlockSpec((d, d), const),
                  pl.BlockSpec((1, d), const),
                  pl.BlockSpec((1, d), const)],
        out_specs=[pl.BlockSpec((tm, d), lambda i: (i, 0)),
                   pl.BlockSpec((tm * ROW_TILE[0], ROW_TILE[1]), lambda i: (i, 0))],
        out_shape=[jax.ShapeDtypeStruct((t, d), F32),
                   jax.ShapeDtypeStruct((t * ROW_TILE[0], ROW_TILE[1]), F32)],
        compiler_params=_params(("parallel",)),
        name="mix_ln",
    )(y, pb, ao, pb, pb, h, norm_g.reshape(1, -1), w_ssd_out.astype(BF16),
      w_attn_out.astype(BF16), gate_b.reshape(1, -1), w_o.astype(BF16),
      ln_g.reshape(1, -1), ln_b.reshape(1, -1))


def _router_kernel(h_ref, w_ref, b_ref, tri_ref, idx_ref, wt_ref, rank_ref, cnt_ref, run_ref):
    @pl.when(pl.program_id(0) == 0)
    def _():
        run_ref[...] = jnp.zeros_like(run_ref)

    tm = h_ref.shape[0]
    ne = N_EXPERTS
    logits = _dot3(h_ref[...], w_ref[...], (((1,), (0,)), ((), ()))) + b_ref[...]
    lane = lax.broadcasted_iota(jnp.int32, (tm, ne), 1).astype(F32)
    work = logits
    sel, vals = [], []
    for _ in range(TOP_K):
        mx = jnp.max(work, axis=1, keepdims=True)
        first = jnp.min(jnp.where(work == mx, lane, float(ne)), axis=1, keepdims=True)
        hit = lane == first
        sel.append((first, hit))
        vals.append(mx)
        work = jnp.where(hit, -jnp.inf, work)
    exps = [jnp.exp(v - vals[0]) for v in vals]
    denom = exps[0] + exps[1] + exps[2] + exps[3]

    onehot = jnp.zeros((tm, ne), F32)
    for _, hit in sel:
        onehot = onehot + hit.astype(F32)
    before = jnp.dot(tri_ref[...], onehot.astype(BF16), preferred_element_type=F32)
    before = before + run_ref[...]

    out_lane = lax.broadcasted_iota(jnp.int32, (tm, 128), 1)
    idx_out = jnp.zeros((tm, 128), F32)
    wt_out = jnp.zeros((tm, 128), F32)
    rank_out = jnp.zeros((tm, 128), F32)
    for k, (first, hit) in enumerate(sel):
        rank = jnp.sum(jnp.where(hit, before, 0.0), axis=1, keepdims=True)
        idx_out = jnp.where(out_lane == k, first, idx_out)
        wt_out = jnp.where(out_lane == k, exps[k] / denom, wt_out)
        rank_out = jnp.where(out_lane == k, rank, rank_out)
    idx_ref[...] = idx_out.T[0:8].astype(jnp.int32)
    wt_ref[...] = wt_out
    rank_ref[...] = rank_out.T[0:8].astype(jnp.int32)
    total = run_ref[...] + jnp.sum(onehot, axis=0, keepdims=True)
    run_ref[...] = total
    cnt_ref[...] = total


def _router_call(h, w_router, b_router, tm=512):
    t = h.shape[0]
    li = jnp.arange(tm)
    tri = (li[None, :] < li[:, None]).astype(BF16)
    const = lambda i: (0, 0)
    return pl.pallas_call(
        _router_kernel,
        grid=(t // tm,),
        in_specs=[pl.BlockSpec((tm, D_MODEL), lambda i: (i, 0)),
                  pl.BlockSpec((D_MODEL, N_EXPERTS), const),
                  pl.BlockSpec((1, N_EXPERTS), const),
                  pl.BlockSpec((tm, tm), const)],
        out_specs=[pl.BlockSpec((8, tm), lambda i: (0, i)),
                   pl.BlockSpec((tm, 128), lambda i: (i, 0)),
                   pl.BlockSpec((8, tm), lambda i: (0, i)),
                   pl.BlockSpec((1, N_EXPERTS), const)],
        out_shape=[jax.ShapeDtypeStruct((8, t), jnp.int32),
                   jax.ShapeDtypeStruct((t, 128), F32),
                   jax.ShapeDtypeStruct((8, t), jnp.int32),
                   jax.ShapeDtypeStruct((1, N_EXPERTS), F32)],
        scratch_shapes=[pltpu.VMEM((1, N_EXPERTS), F32)],
        compiler_params=_params(("arbitrary",)),
        name="router",
    )(h, w_router, b_router.reshape(1, N_EXPERTS), tri)


def _dispatch_kernel(pad_end_ref, slot_ref, x_ref, xs_out, zero_ref, sem, zero_sem):
    sub = ROW_TILE[0]
    tt = x_ref.shape[0] // sub
    n = tt * TOP_K
    zb = zero_ref.shape[0] // sub

    def row(ref, r, count=1):
        return ref.at[pl.ds(pl.multiple_of(r * sub, sub), count * sub)]

    @pl.when(pl.program_id(0) == 0)
    def _():
        zero_ref[...] = jnp.zeros_like(zero_ref)

        def clear(first):
            return pltpu.make_async_copy(zero_ref, row(xs_out, first, zb), zero_sem)

        for e in range(N_EXPERTS):
            clear(jnp.maximum(pad_end_ref[e] - zb, 0)).start()
        used = pad_end_ref[N_EXPERTS - 1] // zb
        total = xs_out.shape[0] // (zb * sub)

        def clear_tail(b, c):
            clear(b * zb).start()
            return c

        def wait_one(b, c):
            clear(0).wait()
            return c

        lax.fori_loop(used, total, clear_tail, 0)
        lax.fori_loop(used - N_EXPERTS, total, wait_one, 0)

    def start(tok, c):
        for k in range(TOP_K):
            slot = slot_ref[0, 0, k * tt + tok]
            pltpu.make_async_copy(row(x_ref, tok), row(xs_out, slot), sem).start(priority=k % 2)
        return c

    lax.fori_loop(0, tt, start, 0, unroll=8)
    pltpu.make_async_copy(row(xs_out, 0, n), row(xs_out, 0, n), sem).wait()


def _dispatch_call(xt, slots, pad_end, cap):
    sub, lanes = ROW_TILE
    t = xt.shape[0] // sub
    tt = DISPATCH_TILE
    nt = t // tt
    n = tt * TOP_K
    assert cap >= N_EXPERTS * EXPERT_BLOCK
    grid_spec = pltpu.PrefetchScalarGridSpec(
        num_scalar_prefetch=1,
        grid=(nt,),
        in_specs=[pl.BlockSpec((1, 1, n), lambda i, pe: (i, 0, 0), memory_space=pltpu.SMEM),
                  pl.BlockSpec((tt * sub, lanes), lambda i, pe: (i, 0))],
        out_specs=pl.BlockSpec(memory_space=pl.ANY),
        scratch_shapes=[pltpu.VMEM((EXPERT_BLOCK * sub, lanes), xt.dtype),
                        pltpu.SemaphoreType.DMA, pltpu.SemaphoreType.DMA],
    )
    return pl.pallas_call(
        _dispatch_kernel,
        grid_spec=grid_spec,
        out_shape=jax.ShapeDtypeStruct((cap * sub, lanes), xt.dtype),
        compiler_params=_params(("arbitrary",), disable_bounds_checks=True),
        name="moe_dispatch",
    )(pad_end, _tile_slots(slots, tt), xt)


def _expert_kernel(be_ref, nb_ref, x_ref, wg_ref, bg_ref, wu_ref, bu_ref, wd_ref, bd_ref,
                   o_ref, wgb_ref, wub_ref, wdb_ref):
    i = pl.program_id(0)
    prev = be_ref[jnp.maximum(i - 1, 0)]
    changed = jnp.logical_or(i == 0, be_ref[i] != prev)

    @pl.when(changed)
    def _():
        wgb_ref[...] = wg_ref[0].astype(BF16)
        wub_ref[...] = wu_ref[0].astype(BF16)
        wdb_ref[...] = wd_ref[0].astype(BF16)

    @pl.when(i < nb_ref[0])
    def _():
        xb = _load_row_tiles(x_ref).astype(BF16)
        g = jnp.dot(xb, wgb_ref[...], preferred_element_type=F32) + bg_ref[0]
        u = jnp.dot(xb, wub_ref[...], preferred_element_type=F32) + bu_ref[0]
        g = jnp.minimum(g, SWIGLU_LIMIT)
        u = jnp.clip(u, -SWIGLU_LIMIT, SWIGLU_LIMIT)
        act = g * _sigmoid(SWIGLU_ALPHA * g) * (u + 1.0)
        _store_row_tiles(o_ref, jnp.dot(act.astype(BF16), wdb_ref[...],
                                        preferred_element_type=F32) + bd_ref[0])

    @pl.when(i >= nb_ref[0])
    def _():
        o_ref[...] = jnp.zeros_like(o_ref)


def _expert_call(xs, blk_expert, n_used, w_gate, b_gate, w_up, b_up, w_down, b_down):
    sub, lanes = ROW_TILE
    cap = xs.shape[0] // sub
    d = sub * lanes
    bm = EXPERT_BLOCK
    nb = cap // bm
    wspec = lambda shape: pl.BlockSpec(shape, lambda i, be, nu: (be[i], 0, 0))
    grid_spec = pltpu.PrefetchScalarGridSpec(
        num_scalar_prefetch=2,
        grid=(nb,),
        in_specs=[pl.BlockSpec((bm * sub, lanes),
                               lambda i, be, nu: (jnp.minimum(i, nu[0] - 1), 0)),
                  wspec((1, d, D_FF)), wspec((1, 1, D_FF)),
                  wspec((1, d, D_FF)), wspec((1, 1, D_FF)),
                  wspec((1, D_FF, d)), wspec((1, 1, d))],
        out_specs=pl.BlockSpec((bm * sub, lanes), lambda i, be, nu: (i, 0)),
        scratch_shapes=[pltpu.VMEM((d, D_FF), BF16), pltpu.VMEM((d, D_FF), BF16),
                        pltpu.VMEM((D_FF, d), BF16)],
    )
    ne = w_gate.shape[0] * w_gate.shape[1]
    return pl.pallas_call(
        _expert_kernel,
        grid_spec=grid_spec,
        out_shape=jax.ShapeDtypeStruct((cap * sub, lanes), F32),
        compiler_params=_params(("arbitrary",)),
        name="moe_experts",
    )(blk_expert, n_used, xs, w_gate.reshape(ne, d, D_FF), b_gate.reshape(ne, 1, D_FF),
      w_up.reshape(ne, d, D_FF), b_up.reshape(ne, 1, D_FF), w_down.reshape(ne, D_FF, d),
      b_down.reshape(ne, 1, d))


def _combine_kernel(slot_ref, next_slot_ref, ys_hbm, wt_ref, h_ref, lg_ref, lb_ref, o_ref,
                    buf_ref, sems):
    i = pl.program_id(0)
    tt = h_ref.shape[0]
    n = tt * TOP_K
    sub = ROW_TILE[0]

    def issue(s_ref, buf):
        def start(r, c):
            for j in range(8):
                a = r * 8 + j
                src = pl.ds(pl.multiple_of(s_ref[0, 0, a] * sub, sub), sub)
                dst = pl.ds(pl.multiple_of(a * sub, sub), sub)
                pltpu.make_async_copy(ys_hbm.at[src], buf_ref.at[buf, dst],
                                      sems.at[buf]).start(priority=j % 2)
            return c

        lax.fori_loop(0, n // 8, start, 0)

    @pl.when(i == 0)
    def _():
        issue(slot_ref, 0)

    @pl.when(i + 1 < pl.num_programs(0))
    def _():
        issue(next_slot_ref, (i + 1) % 2)

    cur = i % 2
    pltpu.make_async_copy(buf_ref.at[cur], buf_ref.at[cur], sems.at[cur]).wait()

    wt = wt_ref[...]
    rows = lambda k: _load_row_tiles(buf_ref.at[cur, pl.ds(k * tt * sub, tt * sub)])
    ff = wt[:, 0:1] * rows(0)
    for k in range(1, TOP_K):
        ff = ff + wt[:, k:k + 1] * rows(k)
    o_ref[...] = _layer_norm(DN_ALPHA * h_ref[...] + ff, lg_ref[...], lb_ref[...])


def _combine_call(ys, slots, wts, h, ln_g, ln_b):
    t, d = h.shape
    tt = COMBINE_TILE
    nt = t // tt
    n = tt * TOP_K
    slots_km = _tile_slots(slots, tt)
    const = lambda i: (0, 0)
    return pl.pallas_call(
        _combine_kernel,
        grid=(nt,),
        in_specs=[pl.BlockSpec((1, 1, n), lambda i: (i, 0, 0), memory_space=pltpu.SMEM),
                  pl.BlockSpec((1, 1, n), lambda i: (jnp.minimum(i + 1, nt - 1), 0, 0),
                               memory_space=pltpu.SMEM),
                  pl.BlockSpec(memory_space=pl.ANY),
                  pl.BlockSpec((tt, 128), lambda i: (i, 0)),
                  pl.BlockSpec((tt, d), lambda i: (i, 0)),
                  pl.BlockSpec((1, d), const),
                  pl.BlockSpec((1, d), const)],
        out_specs=pl.BlockSpec((tt, d), lambda i: (i, 0)),
        out_shape=jax.ShapeDtypeStruct((t, d), F32),
        scratch_shapes=[pltpu.VMEM((2, n * ROW_TILE[0], ROW_TILE[1]), F32),
                        pltpu.SemaphoreType.DMA((2,))],
        compiler_params=_params(("arbitrary",), disable_bounds_checks=True),
        name="moe_combine_ln",
    )(slots_km, slots_km, ys, wts, h, ln_g.reshape(1, d), ln_b.reshape(1, d))


def _tile_slots(slots, tt):
    t = slots.shape[1]
    return slots.reshape(TOP_K, t // tt, tt).transpose(1, 0, 2).reshape(t // tt, 1, TOP_K * tt)


def _moe_layout(idx, rank, counts, n_blocks):
    bm = EXPERT_BLOCK
    counts = counts.reshape(N_EXPERTS).astype(jnp.int32)
    padded = (counts + bm - 1) // bm * bm
    pad_end = jnp.cumsum(padded)
    pad_start = pad_end - padded
    slots = rank
    for e in range(N_EXPERTS):
        slots = slots + jnp.where(idx == e, pad_start[e], 0)
    blk_start = jnp.arange(n_blocks, dtype=jnp.int32) * bm
    blk_expert = jnp.sum((pad_end[None, :] <= blk_start[:, None]).astype(jnp.int32), axis=1)
    blk_expert = jnp.minimum(blk_expert, N_EXPERTS - 1)
    n_used = (pad_end[-1] // bm).reshape(1)
    return slots.astype(jnp.int32), blk_expert, n_used, pad_end.astype(jnp.int32)


def kernel(x, rel_bias, w_in, conv_w, conv_b, dt_bias, a_log, d_skip, ssd_norm_g, w_ssd_out, diff_lambda, diff_norm_g, w_attn_out, gate_b, w_o, ln1_g, ln1_b, w_router, b_router, w_gate, b_gate, w_up, b_up, w_down, b_down, ln2_g, ln2_b):
    batch, seq, d = x.shape
    t = batch * seq
    n_assign = t * TOP_K
    n_blocks = (n_assign + N_EXPERTS * (EXPERT_BLOCK - 1) + EXPERT_BLOCK - 1) // EXPERT_BLOCK
    cap = n_blocks * EXPERT_BLOCK
    bias_near, bias_far = _bias_tiles(rel_bias, seq)
    cols = lambda lo, hi: list(range(lo, hi, 128))
    pb_starts = cols(0, OFF_XBC) + cols(OFF_G, OFF_G + 2 * d) + cols(OFF_Q, OFF_V)
    pb_scaled = [int(OFF_Q <= s < OFF_Q + DIFF_WIDTH) for s in pb_starts]
    v_starts = cols(OFF_V, OFF_G)
    q_scale = DIFF_HEAD_DIM ** -0.5 * LOG2_E

    h = x.reshape(t, d)
    for l in range(DEPTH):
        w_x = w_in[l, :, OFF_XBC:OFF_DT].astype(BF16)
        w_b = _regroup_call(w_in, l, pb_starts, pb_scaled, BF16, q_scale)
        w_v = _regroup_call(w_in, l, v_starts, [0] * len(v_starts), F32, q_scale)
        px = _matmul(h, w_x, F32, 1024, 1024)
        pb = _matmul(h, w_b, BF16, 1024, 1024)
        vt = _matmul_nt(h, w_v, ATTN_TILE)
        dt, dt_t = _dt_call(h, w_in[l, :, OFF_DT:OFF_Q], dt_bias[l])
        y = _ssd_call(px, conv_w[l], conv_b[l], dt, dt_t, a_log[l], d_skip[l], batch, seq)
        ao = _attn_call(pb, vt, bias_near, bias_far, diff_lambda[l], diff_norm_g[l], l, batch,
                        seq)
        h1, h1_tiles = _mix_call(y, pb, ao, h, ssd_norm_g[l], w_ssd_out[l], w_attn_out[l], gate_b[l],
                       w_o[l], ln1_g[l], ln1_b[l])
        idx, wts, rank, counts = _router_call(h1, w_router[l], b_router[l])
        slots, blk_expert, n_used, pad_end = _moe_layout(idx[:TOP_K], rank[:TOP_K], counts,
                                                         n_blocks)
        xs = _dispatch_call(h1_tiles, slots, pad_end, cap)
        ys = _expert_call(xs, blk_expert + l * N_EXPERTS, n_used, w_gate, b_gate, w_up, b_up,
                          w_down, b_down)
        h = _combine_call(ys, slots, wts, h1, ln2_g[l], ln2_b[l])
    return h.reshape(batch, seq, d)
```

```python
import functools
import math

import jax
import jax.numpy as jnp
from jax import lax
from jax.experimental import pallas as pl
from jax.experimental.pallas import tpu as pltpu

F32 = jnp.float32
BF16 = jnp.bfloat16

D_MODEL = 1024
DEPTH = 2
CHUNK = 64

SSD_D_INNER = 2048
SSD_HEAD_DIM = 64
SSD_HEADS = 32
SSD_GROUPS = 4
SSD_HEADS_PER_GROUP = 8
SSD_STATE = 128
SSD_CONV = 4
SSD_CONV_DIM = 3072
SSD_GROUP_WIDTH = SSD_HEADS_PER_GROUP * SSD_HEAD_DIM

DIFF_HEAD_DIM = 64
DIFF_HEADS = 8
DIFF_WIDTH = 1024

REL_BUCKETS = 32
REL_MAX_DIST = 128

N_EXPERTS = 32
TOP_K = 4
D_FF = 1024
SWIGLU_ALPHA = 1.702
SWIGLU_LIMIT = 7.0

DN_ALPHA = (2 * DEPTH) ** 0.25
LN_EPS = 1e-5
RMS_EPS = 1e-5

OFF_XBC = 2048
OFF_DT = 5120
OFF_Q = 5152
OFF_V = 7200
OFF_G = 8224
PB_GATES = 2048
PB_Q = 4096
PB_K = 5120

VMEM_LIMIT_BYTES = 56 * 1024 * 1024

ATTN_TILE = 256
ATTN_HEADS_PER_STEP = 4
SSD_TILE = 256
SSD_GROUPS_PER_STEP = 4
EXPERT_BLOCK = 256
DISPATCH_TILE = 1024
COMBINE_TILE = 512
NEG_BIG = -1e30
LOG2_E = math.log2(math.e)


def _params(semantics, **kwargs):
    return pltpu.CompilerParams(dimension_semantics=semantics,
                                vmem_limit_bytes=VMEM_LIMIT_BYTES, **kwargs)


def _sigmoid(x):
    return 1.0 / (1.0 + jnp.exp(-x))


def _split2(v):
    hi = v.astype(BF16)
    lo = (v - hi.astype(F32)).astype(BF16)
    return hi, lo


def _dot3(a, b, dims):
    ah, al = _split2(a)
    bh, bl = _split2(b)
    dot = lambda p, q: lax.dot_general(p, q, dims, preferred_element_type=F32)
    return dot(ah, bh) + dot(ah, bl) + dot(al, bh)


def _split3(v):
    hi = v.astype(BF16)
    r = v - hi.astype(F32)
    mid = r.astype(BF16)
    lo = (r - mid.astype(F32)).astype(BF16)
    return hi, mid, lo


def _mm_kernel(x_ref, w_ref, o_ref, xb_ref):
    @pl.when(pl.program_id(1) == 0)
    def _():
        xb_ref[...] = x_ref[...].astype(BF16)

    o_ref[...] = jnp.dot(xb_ref[...], w_ref[...],
                         preferred_element_type=F32).astype(o_ref.dtype)


def _matmul(x, w, out_dtype, bm, bn):
    m, k = x.shape
    n = w.shape[1]
    return pl.pallas_call(
        _mm_kernel,
        grid=(m // bm, n // bn),
        in_specs=[pl.BlockSpec((bm, k), lambda i, j: (i, 0)),
                  pl.BlockSpec((k, bn), lambda i, j: (0, j))],
        out_specs=pl.BlockSpec((bm, bn), lambda i, j: (i, j)),
        out_shape=jax.ShapeDtypeStruct((m, n), out_dtype),
        scratch_shapes=[pltpu.VMEM((bm, k), BF16)],
        compiler_params=_params(("parallel", "arbitrary")),
        name="in_proj",
    )(x, w)


def _mm_nt_kernel(x_ref, w_ref, o_ref, wt_ref):
    @pl.when(pl.program_id(0) == 0)
    def _():
        wt_ref[...] = w_ref[...].T.astype(BF16)

    res = lax.dot_general(wt_ref[...], x_ref[...].astype(BF16), (((1,), (1,)), ((), ())),
                          preferred_element_type=F32)
    tile = o_ref.shape[2]
    for c in range(o_ref.shape[0]):
        o_ref[c] = res[:, c * tile:(c + 1) * tile].astype(o_ref.dtype)


def _matmul_nt(x, w, tile, bm=1024):
    m, k = x.shape
    n = w.shape[1]
    return pl.pallas_call(
        _mm_nt_kernel,
        grid=(m // bm,),
        in_specs=[pl.BlockSpec((bm, k), lambda i: (i, 0)),
                  pl.BlockSpec((k, n), lambda i: (0, 0))],
        out_specs=pl.BlockSpec((bm // tile, n, tile), lambda i: (i, 0, 0)),
        out_shape=jax.ShapeDtypeStruct((m // tile, n, tile), BF16),
        scratch_shapes=[pltpu.VMEM((n, k), BF16)],
        compiler_params=_params(("arbitrary",)),
        name="v_proj_t",
    )(x, w)


def _softplus(x):
    return jnp.maximum(x, 0.0) + jnp.log(1.0 + jnp.exp(-jnp.abs(x)))


def _dt_kernel(x_ref, w_ref, b_ref, dt_ref, dtt_ref):
    x = x_ref[...]
    tm = x.shape[0]
    raw = _dot3(x, w_ref[...], (((1,), (0,)), ((), ())))
    dt = _softplus(raw + b_ref[...])
    dt_t = dt.T
    r = SSD_HEADS_PER_GROUP
    for g in range(SSD_GROUPS):
        dt_ref[g] = dt[:, g * r:(g + 1) * r]
        for j in range(tm // CHUNK):
            dtt_ref[g, j] = dt_t[g * r:(g + 1) * r, j * CHUNK:(j + 1) * CHUNK]


def _dt_call(h, w_dt, dt_bias, tm=512):
    t = h.shape[0]
    r = SSD_HEADS_PER_GROUP
    lanes = 128
    w_pad = jnp.pad(w_dt, ((0, 0), (0, lanes - SSD_HEADS)))
    b_pad = jnp.pad(dt_bias, (0, lanes - SSD_HEADS)).reshape(1, lanes)
    return pl.pallas_call(
        _dt_kernel,
        grid=(t // tm,),
        in_specs=[pl.BlockSpec((tm, D_MODEL), lambda i: (i, 0)),
                  pl.BlockSpec((D_MODEL, lanes), lambda i: (0, 0)),
                  pl.BlockSpec((1, lanes), lambda i: (0, 0))],
        out_specs=[pl.BlockSpec((SSD_GROUPS, tm, r), lambda i: (0, i, 0)),
                   pl.BlockSpec((SSD_GROUPS, tm // CHUNK, r, CHUNK), lambda i: (0, i, 0, 0))],
        out_shape=[jax.ShapeDtypeStruct((SSD_GROUPS, t, r), F32),
                   jax.ShapeDtypeStruct((SSD_GROUPS, t // CHUNK, r, CHUNK), F32)],
        compiler_params=_params(("parallel",)),
        name="dt_proj",
    )(h, w_pad, b_pad)


def _ssd_kernel(x_ref, b_ref, c_ref, wx_ref, wb_ref, wc_ref, bx_ref, bb_ref, bc_ref,
                dt_ref, dtt_ref, alr_ref, alc_ref, dsk_ref, bd_ref, trit_ref, e_ref,
                y_ref,
                state_ref, ux_ref, ub_ref, uc_ref, xc_ref, bcv_ref, ccv_ref, xdt_ref,
                wst_ref, eacs_ref, acs_ref, acst_ref):
    s_idx = pl.program_id(2)
    ts = x_ref.shape[0]
    nc = ts // CHUNK
    r = SSD_HEADS_PER_GROUP
    gw = SSD_GROUP_WIDTH
    n = SSD_STATE
    ng = dt_ref.shape[0]

    @pl.when(s_idx == 0)
    def _():
        state_ref[...] = jnp.zeros_like(state_ref)
        ux_ref[0:8, :] = jnp.zeros((8, ux_ref.shape[1]), F32)
        ub_ref[0:8, :] = jnp.zeros((8, ub_ref.shape[1]), F32)
        uc_ref[0:8, :] = jnp.zeros((8, uc_ref.shape[1]), F32)

    def conv_silu(raw_ref, u_ref, w_ref, bias_ref):
        u_ref[8:8 + ts, :] = raw_ref[...]
        u = u_ref[...]
        u2 = pltpu.roll(u, 2, axis=0)
        even = w_ref[3:4, :] * u + w_ref[1:2, :] * u2
        odd = w_ref[2:3, :] * u + w_ref[0:1, :] * u2
        acc = (even + pltpu.roll(odd, 1, axis=0))[8:8 + ts] + bias_ref[...]
        u_ref[0:8, :] = u_ref[ts:ts + 8, :]
        return acc * _sigmoid(acc)

    xc_ref[...] = conv_silu(x_ref, ux_ref, wx_ref, bx_ref)
    bcv_ref[...] = conv_silu(b_ref, ub_ref, wb_ref, bb_ref).astype(BF16)
    ccv_ref[...] = conv_silu(c_ref, uc_ref, wc_ref, bc_ref).astype(BF16)

    bd = bd_ref[...]
    trit = trit_ref[...]
    e_mat = e_ref[...]

    def expand(v):
        return jnp.dot(jnp.concatenate(_split2(v), axis=1), e_mat, preferred_element_type=F32)

    for g in range(ng):
        gl = slice(g * gw, (g + 1) * gw)
        a_row = -jnp.exp(alr_ref[g])
        a_col = -jnp.exp(alc_ref[g])
        dt = dt_ref[g]
        a = dt * a_row
        acs = jnp.zeros((ts, r), F32)
        for part in _split3(a):
            acs = acs + jnp.dot(bd, part, preferred_element_type=F32)
        acs_ref[g] = acs
        a_t = dtt_ref[g].reshape(nc * r, CHUNK) * jnp.concatenate([a_col] * nc, axis=0)
        acs_t = jnp.zeros((nc * r, CHUNK), F32)
        for part in _split3(a_t):
            acs_t = acs_t + jnp.dot(part, trit, preferred_element_type=F32)
        acst_ref[g] = acs_t
        a_last = jnp.concatenate(
            [jnp.broadcast_to(acs[c * CHUNK + CHUNK - 1:c * CHUNK + CHUNK, :], (CHUNK, r))
             for c in range(nc)], axis=0)
        xdt = xc_ref[:, gl] * expand(dt)
        xdt_ref[:, gl] = xdt.astype(BF16)
        wst_ref[:, gl] = (xdt * expand(jnp.exp(a_last - acs))).astype(BF16)
        eacs_ref[:, gl] = expand(jnp.exp(acs))

    row_i = lax.broadcasted_iota(jnp.int32, (CHUNK, CHUNK), 0)
    col_i = lax.broadcasted_iota(jnp.int32, (CHUNK, CHUNK), 1)
    tril = col_i <= row_i
    left_half = lax.broadcasted_iota(jnp.int32, (CHUNK, 128), 1) < SSD_HEAD_DIM

    for c in range(nc):
        rows = slice(c * CHUNK, (c + 1) * CHUNK)
        for g in range(ng):
            cc = ccv_ref[rows, g * n:(g + 1) * n]
            bc = bcv_ref[rows, g * n:(g + 1) * n]
            cb = lax.dot_general(cc, bc, (((1,), (1,)), ((), ())), preferred_element_type=F32)
            acs_c = acs_ref[g, rows, :]
            acs_tc = acst_ref[g, c * r:(c + 1) * r, :]
            for p in range(r // 2):
                lo = g * gw + p * 128
                xp = xdt_ref[rows, lo:lo + 128]
                halves = []
                for hh in (2 * p, 2 * p + 1):
                    diff = acs_c[:, hh:hh + 1] - acs_tc[hh:hh + 1, :]
                    decay = jnp.exp(jnp.where(tril, diff, -jnp.inf))
                    halves.append(jnp.dot((cb * decay).astype(BF16), xp,
                                          preferred_element_type=F32))
                y_diag = jnp.where(left_half, halves[0], halves[1])
                y_ref[rows, lo:lo + 128] = (y_diag
                                            + dsk_ref[:, lo:lo + 128] * xc_ref[rows, lo:lo + 128])

    def chunk_body(c, carry):
        rows = pl.ds(pl.multiple_of(c * CHUNK, CHUNK), CHUNK)
        for g in range(ng):
            gl = slice(g * gw, (g + 1) * gw)
            cc = ccv_ref[rows, g * n:(g + 1) * n]
            bc = bcv_ref[rows, g * n:(g + 1) * n]
            state = state_ref[g]
            eacs = eacs_ref[rows, gl]
            y_off = jnp.dot(cc, state.astype(BF16), preferred_element_type=F32) * eacs
            y_ref[rows, gl] = y_ref[rows, gl] + y_off
            upd = lax.dot_general(bc, wst_ref[rows, gl], (((0,), (0,)), ((), ())),
                                  preferred_element_type=F32)
            state_ref[g] = state * eacs[CHUNK - 1:CHUNK, :] + upd
        return carry

    lax.fori_loop(0, nc, chunk_body, 0)


def _ssd_call(px, conv_w, conv_b, dt, dt_t, a_log, d_skip, batch, seq):
    t = batch * seq
    ts = SSD_TILE
    ns = seq // ts
    nc = ts // CHUNK
    r = SSD_HEADS_PER_GROUP
    ng = SSD_GROUPS_PER_STEP
    gw = ng * SSD_GROUP_WIDTH
    n = ng * SSD_STATE
    wb0 = SSD_D_INNER // n
    wc0 = wb0 + SSD_GROUPS // ng

    li = jnp.arange(ts)
    bd = ((li[None, :] <= li[:, None]) & (li[None, :] // CHUNK == li[:, None] // CHUNK)).astype(BF16)
    lc = jnp.arange(CHUNK)
    trit = (lc[:, None] <= lc[None, :]).astype(BF16)
    e_mat = (jnp.arange(SSD_GROUP_WIDTH)[None, :] // SSD_HEAD_DIM
             == jnp.arange(2 * r)[:, None] % r).astype(BF16)
    conv_b2 = conv_b.reshape(1, SSD_CONV_DIM)
    alr = a_log.reshape(SSD_GROUPS, 1, r)
    alc = a_log.reshape(SSD_GROUPS, r, 1)
    dsk = jnp.repeat(d_skip, SSD_HEAD_DIM).reshape(1, SSD_D_INNER)

    row = lambda b, g, s: b * ns + s
    in_specs = [
        pl.BlockSpec((ts, gw), lambda b, g, s: (row(b, g, s), g)),
        pl.BlockSpec((ts, n), lambda b, g, s: (row(b, g, s), wb0 + g)),
        pl.BlockSpec((ts, n), lambda b, g, s: (row(b, g, s), wc0 + g)),
        pl.BlockSpec((SSD_CONV, gw), lambda b, g, s: (0, g)),
        pl.BlockSpec((SSD_CONV, n), lambda b, g, s: (0, wb0 + g)),
        pl.BlockSpec((SSD_CONV, n), lambda b, g, s: (0, wc0 + g)),
        pl.BlockSpec((1, gw), lambda b, g, s: (0, g)),
        pl.BlockSpec((1, n), lambda b, g, s: (0, wb0 + g)),
        pl.BlockSpec((1, n), lambda b, g, s: (0, wc0 + g)),
        pl.BlockSpec((ng, ts, r), lambda b, g, s: (g, row(b, g, s), 0)),
        pl.BlockSpec((ng, nc, r, CHUNK), lambda b, g, s: (g, row(b, g, s), 0, 0)),
        pl.BlockSpec((ng, 1, r), lambda b, g, s: (g, 0, 0)),
        pl.BlockSpec((ng, r, 1), lambda b, g, s: (g, 0, 0)),
        pl.BlockSpec((1, gw), lambda b, g, s: (0, g)),
        pl.BlockSpec((ts, ts), lambda b, g, s: (0, 0)),
        pl.BlockSpec((CHUNK, CHUNK), lambda b, g, s: (0, 0)),
        pl.BlockSpec((2 * r, SSD_GROUP_WIDTH), lambda b, g, s: (0, 0)),
    ]
    scratch = [
        pltpu.VMEM((ng, SSD_STATE, SSD_GROUP_WIDTH), F32),
        pltpu.VMEM((ts + 8, gw), F32),
        pltpu.VMEM((ts + 8, n), F32),
        pltpu.VMEM((ts + 8, n), F32),
        pltpu.VMEM((ts, gw), F32),
        pltpu.VMEM((ts, n), BF16),
        pltpu.VMEM((ts, n), BF16),
        pltpu.VMEM((ts, gw), BF16),
        pltpu.VMEM((ts, gw), BF16),
        pltpu.VMEM((ts, gw), F32),
        pltpu.VMEM((ng, ts, r), F32),
        pltpu.VMEM((ng, nc * r, CHUNK), F32),
    ]
    return pl.pallas_call(
        _ssd_kernel,
        grid=(batch, SSD_GROUPS // ng, ns),
        in_specs=in_specs,
        out_specs=pl.BlockSpec((ts, gw), lambda b, g, s: (row(b, g, s), g)),
        out_shape=jax.ShapeDtypeStruct((t, SSD_D_INNER), F32),
        scratch_shapes=scratch,
        compiler_params=_params(("parallel", "parallel", "arbitrary")),
        name="ssd_scan",
    )(px, px, px, conv_w, conv_w, conv_w, conv_b2, conv_b2, conv_b2,
      dt, dt_t, alr, alc, dsk, bd, trit, e_mat)


def _attn_kernel(q_ref, k_ref, vt_ref, bias_ref, far_ref, lam_ref, g_ref, o_ref, sa_ref, sb_ref,
                 *, lam_init):
    i = pl.program_id(2)
    tq = q_ref.shape[0]
    dh = DIFF_HEAD_DIM
    lv = lam_ref[...]
    lam = (jnp.exp(jnp.sum(lv[0:1] * lv[1:2], axis=1, keepdims=True))
           - jnp.exp(jnp.sum(lv[2:3] * lv[3:4], axis=1, keepdims=True)) + lam_init)

    q = q_ref[...]
    hp = q.shape[1] // (2 * dh)
    ns = 2 * hp
    qs = [q[:, s * dh:(s + 1) * dh] for s in range(ns)]
    nt = (((1,), (1,)), ((), ()))

    n_tiles = bias_ref.shape[1] - 1
    h0 = pl.program_id(1) * hp
    ones = jnp.ones((16, tq), BF16)
    dv = 2 * dh

    def near_tile(t):
        valid = t <= i
        return jnp.where(valid, i - t, 0), jnp.where(valid, t, n_tiles)

    def far_tile(t):
        valid = t <= i
        j = jnp.where(valid, t - 2, 0)
        d = jnp.where(valid, i - j, 0)
        return j, [jnp.where(valid, far_ref[h0 + hh, d], NEG_BIG) for hh in range(hp)]

    def scores(j, s_ref, biases):
        kj = k_ref[pl.ds(pl.multiple_of(j * tq, tq), tq), :]
        for s in range(ns):
            sc = lax.dot_general(kj[:, s * dh:(s + 1) * dh], qs[s], nt,
                                 preferred_element_type=F32)
            s_ref[s] = sc if biases is None else sc + biases[s // 2]

    def update(j, s_ref, shifts, carry):
        vtj = vt_ref[j]
        out = []
        for s in range(ns):
            hh = s // 2
            vth = jnp.concatenate([vtj[hh * dv:(hh + 1) * dv], ones], axis=0)
            mx, acc = carry[2 * s:2 * s + 2]
            sc = s_ref[s]
            mx_new = jnp.maximum(mx, jnp.max(sc, axis=0, keepdims=True) + shifts[hh])
            p = jnp.exp2(sc - (mx_new - shifts[hh]))
            out += [mx_new, jnp.exp2(mx - mx_new) * acc
                    + jnp.dot(vth, p.astype(BF16), preferred_element_type=F32)]
        return tuple(out)

    def scores_near(t, s_ref):
        j, d = near_tile(t)
        scores(j, s_ref, [bias_ref[hh, d] for hh in range(hp)])

    def scores_far(t, s_ref):
        scores(far_tile(t)[0], s_ref, None)

    def update_near(t, s_ref, carry):
        return update(near_tile(t)[0], s_ref, [0.0] * hp, carry)

    def update_far(t, s_ref, carry):
        j, shifts = far_tile(t)
        return update(j, s_ref, shifts, carry)

    def body(u, carry):
        scores_far(2 * u + 1, sb_ref)
        carry = update_far(2 * u, sa_ref, carry)
        scores_far(2 * u + 2, sa_ref)
        return update_far(2 * u + 1, sb_ref, carry)

    carry = []
    for _ in range(ns):
        carry += [jnp.full((1, tq), NEG_BIG, F32), jnp.zeros((dv + 16, tq), F32)]
    scores_near(0, sa_ref)
    scores_near(1, sb_ref)
    carry = update_near(0, sa_ref, tuple(carry))
    scores_far(2, sa_ref)
    carry = update_near(1, sb_ref, carry)
    carry = lax.fori_loop(1, (i + 2) // 2, body, carry)
    for hh in range(hp):
        acc1, acc2 = carry[4 * hh + 1], carry[4 * hh + 3]
        o = acc1[:dv] / acc1[dv:dv + 1] - lam * (acc2[:dv] / acc2[dv:dv + 1])
        ms = jnp.mean(o * o, axis=0, keepdims=True)
        o = o * lax.rsqrt(ms + RMS_EPS) * g_ref[...] * (1.0 - lam_init)
        o_ref[:, hh * dv:(hh + 1) * dv] = o.T.astype(o_ref.dtype)


def _rel_bucket(rel):
    half = REL_BUCKETS // 2
    max_exact = half // 2
    ret = jnp.where(rel > 0, half, 0)
    n = jnp.abs(rel)
    nf = jnp.maximum(n, 1).astype(F32)
    large = max_exact + (jnp.log(nf / max_exact) / math.log(REL_MAX_DIST / max_exact)
                         * (half - max_exact)).astype(jnp.int32)
    large = jnp.minimum(large, half - 1)
    return ret + jnp.where(n < max_exact, n, large)


def _bias_lookup(rel_bias, rel):
    bucket = _rel_bucket(rel)[None]
    table = rel_bias.astype(F32)
    lead = (slice(None),) + (None,) * rel.ndim
    bias = jnp.zeros((DIFF_HEADS,) + rel.shape, F32)
    for b in range(REL_BUCKETS):
        bias = jnp.where(bucket == b, table[b][lead], bias)
    return bias * LOG2_E


def _bias_tiles(rel_bias, seq):
    tq = ATTN_TILE
    kk = jnp.arange(tq)[None, :, None]
    qq = jnp.arange(tq)[None, None, :]
    d = jnp.arange(2)[:, None, None]
    near = _bias_lookup(rel_bias, kk - qq - d * tq)
    allowed = (d > 0) | ((kk // CHUNK) <= (qq // CHUNK))
    near = jnp.where(allowed[None], near, NEG_BIG)
    masked = jnp.full((DIFF_HEADS, 1, tq, tq), NEG_BIG, F32)
    far = _bias_lookup(rel_bias, -tq * jnp.arange(seq // tq + 1))
    return jnp.concatenate([near, masked], axis=1), far


def _attn_call(pb, vt, bias_near, bias_far, lam_vecs, norm_g, layer_idx, batch, seq):
    t = batch * seq
    tq = ATTN_TILE
    nq = seq // tq
    hp = ATTN_HEADS_PER_STEP
    w = 2 * DIFF_HEAD_DIM
    wb = hp * w
    qcol0 = PB_Q // wb
    kcol0 = PB_K // wb
    lam_init = 0.8 - 0.6 * math.exp(-0.3 * layer_idx)
    return pl.pallas_call(
        functools.partial(_attn_kernel, lam_init=lam_init),
        grid=(batch, DIFF_HEADS // hp, nq),
        in_specs=[pl.BlockSpec((tq, wb), lambda b, h, i: (b * nq + i, qcol0 + h)),
                  pl.BlockSpec((seq, wb), lambda b, h, i: (b, kcol0 + h)),
                  pl.BlockSpec((nq, wb, tq), lambda b, h, i: (b, h, 0)),
                  pl.BlockSpec((hp, 3, tq, tq), lambda b, h, i: (h, 0, 0, 0)),
                  pl.BlockSpec(memory_space=pltpu.SMEM),
                  pl.BlockSpec((4, DIFF_HEAD_DIM), lambda b, h, i: (0, 0)),
                  pl.BlockSpec((w, 1), lambda b, h, i: (0, 0))],
        out_specs=pl.BlockSpec((tq, wb), lambda b, h, i: (b * nq + i, h)),
        out_shape=jax.ShapeDtypeStruct((t, DIFF_WIDTH), BF16),
        scratch_shapes=[pltpu.VMEM((2 * hp, tq, tq), F32), pltpu.VMEM((2 * hp, tq, tq), F32)],
        compiler_params=_params(("parallel", "parallel", "arbitrary")),
        name="diff_attn",
    )(pb, pb, vt, bias_near, bias_far, lam_vecs, norm_g.reshape(w, 1))


def _layer_norm(x, g, b):
    mu = jnp.mean(x, axis=1, keepdims=True)
    xc = x - mu
    var = jnp.mean(xc * xc, axis=1, keepdims=True)
    return xc * lax.rsqrt(var + LN_EPS) * g + b


def _mix_kernel(y_ref, z_ref, ao_ref, g0_ref, g1_ref, h_ref, ng_ref, wso_ref, wao_ref,
                gb_ref, wo_ref, lg_ref, lb_ref, o_ref, ot_ref):
    z = z_ref[...].astype(F32)
    yg = y_ref[...] * (z * _sigmoid(z))
    ms = jnp.mean(yg * yg, axis=1, keepdims=True)
    yn = (yg * lax.rsqrt(ms + RMS_EPS) * ng_ref[...]).astype(BF16)
    y_ssd = jnp.dot(yn, wso_ref[...], preferred_element_type=F32)
    y_att = jnp.dot(ao_ref[...], wao_ref[...], preferred_element_type=F32)
    gb = gb_ref[...]
    gate0 = _sigmoid(g0_ref[...].astype(F32) + gb[:, :D_MODEL])
    gate1 = _sigmoid(g1_ref[...].astype(F32) + gb[:, D_MODEL:])
    mixed = (gate0 * y_ssd + gate1 * y_att).astype(BF16)
    mix = jnp.dot(mixed, wo_ref[...], preferred_element_type=F32)
    out = _layer_norm(DN_ALPHA * h_ref[...] + mix, lg_ref[...], lb_ref[...])
    o_ref[...] = out
    _store_row_tiles(ot_ref, out)


ROW_TILE = (8, 128)


def _store_row_tiles(ref, rows):
    sub, lanes = ROW_TILE
    m = rows.shape[0]
    for s in range(sub):
        ref[pl.ds(s, m, stride=sub), :] = rows[:, s * lanes:(s + 1) * lanes]


def _load_row_tiles(ref):
    sub, lanes = ROW_TILE
    m = ref.shape[0] // sub
    return jnp.concatenate([ref[pl.ds(s, m, stride=sub), :] for s in range(sub)], axis=1)


def _mix_call(y, pb, ao, h, norm_g, w_ssd_out, w_attn_out, gate_b, w_o, ln_g, ln_b, tm=512):
    t = h.shape[0]
    d = D_MODEL
    gcol0 = PB_GATES // d
    const = lambda i: (0, 0)
    return pl.pallas_call(
        _mix_kernel,
        grid=(t // tm,),
        in_specs=[pl.BlockSpec((tm, SSD_D_INNER), lambda i: (i, 0)),
                  pl.BlockSpec((tm, SSD_D_INNER), lambda i: (i, 0)),
                  pl.BlockSpec((tm, DIFF_WIDTH), lambda i: (i, 0)),
                  pl.BlockSpec((tm, d), lambda i: (i, gcol0)),
                  pl.BlockSpec((tm, d), lambda i: (i, gcol0 + 1)),
                  pl.BlockSpec((tm, d), lambda i: (i, 0)),
                  pl.BlockSpec((1, SSD_D_INNER), const),
                  pl.BlockSpec((SSD_D_INNER, d), const),
                  pl.BlockSpec((DIFF_WIDTH, d), const),
                  pl.BlockSpec((1, 2 * d), const),
                  pl.BlockSpec((d, d), const),
                  pl.BlockSpec((1, d), const),
                  pl.BlockSpec((1, d), const)],
        out_specs=[pl.BlockSpec((tm, d), lambda i: (i, 0)),
                   pl.BlockSpec((tm * ROW_TILE[0], ROW_TILE[1]), lambda i: (i, 0))],
        out_shape=[jax.ShapeDtypeStruct((t, d), F32),
                   jax.ShapeDtypeStruct((t * ROW_TILE[0], ROW_TILE[1]), F32)],
        compiler_params=_params(("parallel",)),
        name="mix_ln",
    )(y, pb, ao, pb, pb, h, norm_g.reshape(1, -1), w_ssd_out.astype(BF16),
      w_attn_out.astype(BF16), gate_b.reshape(1, -1), w_o.astype(BF16),
      ln_g.reshape(1, -1), ln_b.reshape(1, -1))


def _router_kernel(h_ref, w_ref, b_ref, tri_ref, idx_ref, wt_ref, rank_ref, cnt_ref, run_ref):
    @pl.when(pl.program_id(0) == 0)
    def _():
        run_ref[...] = jnp.zeros_like(run_ref)

    tm = h_ref.shape[0]
    ne = N_EXPERTS
    logits = _dot3(h_ref[...], w_ref[...], (((1,), (0,)), ((), ()))) + b_ref[...]
    lane = lax.broadcasted_iota(jnp.int32, (tm, ne), 1).astype(F32)
    work = logits
    sel, vals = [], []
    for _ in range(TOP_K):
        mx = jnp.max(work, axis=1, keepdims=True)
        first = jnp.min(jnp.where(work == mx, lane, float(ne)), axis=1, keepdims=True)
        hit = lane == first
        sel.append((first, hit))
        vals.append(mx)
        work = jnp.where(hit, -jnp.inf, work)
    exps = [jnp.exp(v - vals[0]) for v in vals]
    denom = exps[0] + exps[1] + exps[2] + exps[3]

    onehot = jnp.zeros((tm, ne), F32)
    for _, hit in sel:
        onehot = onehot + hit.astype(F32)
    before = jnp.dot(tri_ref[...], onehot.astype(BF16), preferred_element_type=F32)
    before = before + run_ref[...]

    out_lane = lax.broadcasted_iota(jnp.int32, (tm, 128), 1)
    idx_out = jnp.zeros((tm, 128), F32)
    wt_out = jnp.zeros((tm, 128), F32)
    rank_out = jnp.zeros((tm, 128), F32)
    for k, (first, hit) in enumerate(sel):
        rank = jnp.sum(jnp.where(hit, before, 0.0), axis=1, keepdims=True)
        idx_out = jnp.where(out_lane == k, first, idx_out)
        wt_out = jnp.where(out_lane == k, exps[k] / denom, wt_out)
        rank_out = jnp.where(out_lane == k, rank, rank_out)
    idx_ref[...] = idx_out.T[0:8].astype(jnp.int32)
    wt_ref[...] = wt_out
    rank_ref[...] = rank_out.T[0:8].astype(jnp.int32)
    total = run_ref[...] + jnp.sum(onehot, axis=0, keepdims=True)
    run_ref[...] = total
    cnt_ref[...] = total


def _router_call(h, w_router, b_router, tm=512):
    t = h.shape[0]
    li = jnp.arange(tm)
    tri = (li[None, :] < li[:, None]).astype(BF16)
    const = lambda i: (0, 0)
    return pl.pallas_call(
        _router_kernel,
        grid=(t // tm,),
        in_specs=[pl.BlockSpec((tm, D_MODEL), lambda i: (i, 0)),
                  pl.BlockSpec((D_MODEL, N_EXPERTS), const),
                  pl.BlockSpec((1, N_EXPERTS), const),
                  pl.BlockSpec((tm, tm), const)],
        out_specs=[pl.BlockSpec((8, tm), lambda i: (0, i)),
                   pl.BlockSpec((tm, 128), lambda i: (i, 0)),
                   pl.BlockSpec((8, tm), lambda i: (0, i)),
                   pl.BlockSpec((1, N_EXPERTS), const)],
        out_shape=[jax.ShapeDtypeStruct((8, t), jnp.int32),
                   jax.ShapeDtypeStruct((t, 128), F32),
                   jax.ShapeDtypeStruct((8, t), jnp.int32),
                   jax.ShapeDtypeStruct((1, N_EXPERTS), F32)],
        scratch_shapes=[pltpu.VMEM((1, N_EXPERTS), F32)],
        compiler_params=_params(("arbitrary",)),
        name="router",
    )(h, w_router, b_router.reshape(1, N_EXPERTS), tri)


def _dispatch_kernel(pad_end_ref, slot_ref, x_ref, xs_out, zero_ref, sem, zero_sem):
    sub = ROW_TILE[0]
    tt = x_ref.shape[0] // sub
    n = tt * TOP_K
    zb = zero_ref.shape[0] // sub

    def row(ref, r, count=1):
        return ref.at[pl.ds(pl.multiple_of(r * sub, sub), count * sub)]

    @pl.when(pl.program_id(0) == 0)
    def _():
        zero_ref[...] = jnp.zeros_like(zero_ref)

        def clear(first):
            return pltpu.make_async_copy(zero_ref, row(xs_out, first, zb), zero_sem)

        for e in range(N_EXPERTS):
            clear(jnp.maximum(pad_end_ref[e] - zb, 0)).start()
        used = pad_end_ref[N_EXPERTS - 1] // zb
        total = xs_out.shape[0] // (zb * sub)

        def clear_tail(b, c):
            clear(b * zb).start()
            return c

        def wait_one(b, c):
            clear(0).wait()
            return c

        lax.fori_loop(used, total, clear_tail, 0)
        lax.fori_loop(used - N_EXPERTS, total, wait_one, 0)

    def start(tok, c):
        for k in range(TOP_K):
            slot = slot_ref[0, 0, k * tt + tok]
            pltpu.make_async_copy(row(x_ref, tok), row(xs_out, slot), sem).start(priority=k % 2)
        return c

    lax.fori_loop(0, tt, start, 0, unroll=8)
    pltpu.make_async_copy(row(xs_out, 0, n), row(xs_out, 0, n), sem).wait()


def _dispatch_call(xt, slots, pad_end, cap):
    sub, lanes = ROW_TILE
    t = xt.shape[0] // sub
    tt = DISPATCH_TILE
    nt = t // tt
    n = tt * TOP_K
    assert cap >= N_EXPERTS * EXPERT_BLOCK
    grid_spec = pltpu.PrefetchScalarGridSpec(
        num_scalar_prefetch=1,
        grid=(nt,),
        in_specs=[pl.BlockSpec((1, 1, n), lambda i, pe: (i, 0, 0), memory_space=pltpu.SMEM),
                  pl.BlockSpec((tt * sub, lanes), lambda i, pe: (i, 0))],
        out_specs=pl.BlockSpec(memory_space=pl.ANY),
        scratch_shapes=[pltpu.VMEM((EXPERT_BLOCK * sub, lanes), xt.dtype),
                        pltpu.SemaphoreType.DMA, pltpu.SemaphoreType.DMA],
    )
    return pl.pallas_call(
        _dispatch_kernel,
        grid_spec=grid_spec,
        out_shape=jax.ShapeDtypeStruct((cap * sub, lanes), xt.dtype),
        compiler_params=_params(("arbitrary",), disable_bounds_checks=True),
        name="moe_dispatch",
    )(pad_end, _tile_slots(slots, tt), xt)


def _expert_kernel(be_ref, nb_ref, x_ref, wg_ref, bg_ref, wu_ref, bu_ref, wd_ref, bd_ref,
                   o_ref, wgb_ref, wub_ref, wdb_ref):
    i = pl.program_id(0)
    prev = be_ref[jnp.maximum(i - 1, 0)]
    changed = jnp.logical_or(i == 0, be_ref[i] != prev)

    @pl.when(changed)
    def _():
        wgb_ref[...] = wg_ref[0].astype(BF16)
        wub_ref[...] = wu_ref[0].astype(BF16)
        wdb_ref[...] = wd_ref[0].astype(BF16)

    @pl.when(i < nb_ref[0])
    def _():
        xb = _load_row_tiles(x_ref).astype(BF16)
        g = jnp.dot(xb, wgb_ref[...], preferred_element_type=F32) + bg_ref[0]
        u = jnp.dot(xb, wub_ref[...], preferred_element_type=F32) + bu_ref[0]
        g = jnp.minimum(g, SWIGLU_LIMIT)
        u = jnp.clip(u, -SWIGLU_LIMIT, SWIGLU_LIMIT)
        act = g * _sigmoid(SWIGLU_ALPHA * g) * (u + 1.0)
        _store_row_tiles(o_ref, jnp.dot(act.astype(BF16), wdb_ref[...],
                                        preferred_element_type=F32) + bd_ref[0])

    @pl.when(i >= nb_ref[0])
    def _():
        o_ref[...] = jnp.zeros_like(o_ref)


def _expert_call(xs, blk_expert, n_used, w_gate, b_gate, w_up, b_up, w_down, b_down):
    sub, lanes = ROW_TILE
    cap = xs.shape[0] // sub
    d = sub * lanes
    bm = EXPERT_BLOCK
    nb = cap // bm
    wspec = lambda shape: pl.BlockSpec(shape, lambda i, be, nu: (be[i], 0, 0))
    grid_spec = pltpu.PrefetchScalarGridSpec(
        num_scalar_prefetch=2,
        grid=(nb,),
        in_specs=[pl.BlockSpec((bm * sub, lanes),
                               lambda i, be, nu: (jnp.minimum(i, nu[0] - 1), 0)),
                  wspec((1, d, D_FF)), wspec((1, 1, D_FF)),
                  wspec((1, d, D_FF)), wspec((1, 1, D_FF)),
                  wspec((1, D_FF, d)), wspec((1, 1, d))],
        out_specs=pl.BlockSpec((bm * sub, lanes), lambda i, be, nu: (i, 0)),
        scratch_shapes=[pltpu.VMEM((d, D_FF), BF16), pltpu.VMEM((d, D_FF), BF16),
                        pltpu.VMEM((D_FF, d), BF16)],
    )
    ne = w_gate.shape[0] * w_gate.shape[1]
    return pl.pallas_call(
        _expert_kernel,
        grid_spec=grid_spec,
        out_shape=jax.ShapeDtypeStruct((cap * sub, lanes), F32),
        compiler_params=_params(("arbitrary",)),
        name="moe_experts",
    )(blk_expert, n_used, xs, w_gate.reshape(ne, d, D_FF), b_gate.reshape(ne, 1, D_FF),
      w_up.reshape(ne, d, D_FF), b_up.reshape(ne, 1, D_FF), w_down.reshape(ne, D_FF, d),
      b_down.reshape(ne, 1, d))


def _combine_kernel(slot_ref, next_slot_ref, ys_hbm, wt_ref, h_ref, lg_ref, lb_ref, o_ref,
                    buf_ref, sems):
    i = pl.program_id(0)
    tt = h_ref.shape[0]
    n = tt * TOP_K
    sub = ROW_TILE[0]

    def issue(s_ref, buf):
        def start(r, c):
            for j in range(8):
                a = r * 8 + j
                src = pl.ds(pl.multiple_of(s_ref[0, 0, a] * sub, sub), sub)
                dst = pl.ds(pl.multiple_of(a * sub, sub), sub)
                pltpu.make_async_copy(ys_hbm.at[src], buf_ref.at[buf, dst],
                                      sems.at[buf]).start(priority=j % 2)
            return c

        lax.fori_loop(0, n // 8, start, 0)

    @pl.when(i == 0)
    def _():
        issue(slot_ref, 0)

    @pl.when(i + 1 < pl.num_programs(0))
    def _():
        issue(next_slot_ref, (i + 1) % 2)

    cur = i % 2
    pltpu.make_async_copy(buf_ref.at[cur], buf_ref.at[cur], sems.at[cur]).wait()

    wt = wt_ref[...]
    rows = lambda k: _load_row_tiles(buf_ref.at[cur, pl.ds(k * tt * sub, tt * sub)])
    ff = wt[:, 0:1] * rows(0)
    for k in range(1, TOP_K):
        ff = ff + wt[:, k:k + 1] * rows(k)
    o_ref[...] = _layer_norm(DN_ALPHA * h_ref[...] + ff, lg_ref[...], lb_ref[...])


def _combine_call(ys, slots, wts, h, ln_g, ln_b):
    t, d = h.shape
    tt = COMBINE_TILE
    nt = t // tt
    n = tt * TOP_K
    slots_km = _tile_slots(slots, tt)
    const = lambda i: (0, 0)
    return pl.pallas_call(
        _combine_kernel,
        grid=(nt,),
        in_specs=[pl.BlockSpec((1, 1, n), lambda i: (i, 0, 0), memory_space=pltpu.SMEM),
                  pl.BlockSpec((1, 1, n), lambda i: (jnp.minimum(i + 1, nt - 1), 0, 0),
                               memory_space=pltpu.SMEM),
                  pl.BlockSpec(memory_space=pl.ANY),
                  pl.BlockSpec((tt, 128), lambda i: (i, 0)),
                  pl.BlockSpec((tt, d), lambda i: (i, 0)),
                  pl.BlockSpec((1, d), const),
                  pl.BlockSpec((1, d), const)],
        out_specs=pl.BlockSpec((tt, d), lambda i: (i, 0)),
        out_shape=jax.ShapeDtypeStruct((t, d), F32),
        scratch_shapes=[pltpu.VMEM((2, n * ROW_TILE[0], ROW_TILE[1]), F32),
                        pltpu.SemaphoreType.DMA((2,))],
        compiler_params=_params(("arbitrary",), disable_bounds_checks=True),
        name="moe_combine_ln",
    )(slots_km, slots_km, ys, wts, h, ln_g.reshape(1, d), ln_b.reshape(1, d))


def _tile_slots(slots, tt):
    t = slots.shape[1]
    return slots.reshape(TOP_K, t // tt, tt).transpose(1, 0, 2).reshape(t // tt, 1, TOP_K * tt)


def _moe_layout(idx, rank, counts, n_blocks):
    bm = EXPERT_BLOCK
    counts = counts.reshape(N_EXPERTS).astype(jnp.int32)
    padded = (counts + bm - 1) // bm * bm
    pad_end = jnp.cumsum(padded)
    pad_start = pad_end - padded
    slots = rank
    for e in range(N_EXPERTS):
        slots = slots + jnp.where(idx == e, pad_start[e], 0)
    blk_start = jnp.arange(n_blocks, dtype=jnp.int32) * bm
    blk_expert = jnp.sum((pad_end[None, :] <= blk_start[:, None]).astype(jnp.int32), axis=1)
    blk_expert = jnp.minimum(blk_expert, N_EXPERTS - 1)
    n_used = (pad_end[-1] // bm).reshape(1)
    return slots.astype(jnp.int32), blk_expert, n_used, pad_end.astype(jnp.int32)


def kernel(x, rel_bias, w_in, conv_w, conv_b, dt_bias, a_log, d_skip, ssd_norm_g, w_ssd_out, diff_lambda, diff_norm_g, w_attn_out, gate_b, w_o, ln1_g, ln1_b, w_router, b_router, w_gate, b_gate, w_up, b_up, w_down, b_down, ln2_g, ln2_b):
    batch, seq, d = x.shape
    t = batch * seq
    n_assign = t * TOP_K
    n_blocks = (n_assign + N_EXPERTS * (EXPERT_BLOCK - 1) + EXPERT_BLOCK - 1) // EXPERT_BLOCK
    cap = n_blocks * EXPERT_BLOCK
    bias_near, bias_far = _bias_tiles(rel_bias, seq)
    qk_scale = jnp.concatenate([jnp.full((DIFF_WIDTH,), DIFF_HEAD_DIM ** -0.5 * LOG2_E, F32),
                                jnp.ones((DIFF_WIDTH,), F32)])[None, :]

    h = x.reshape(t, d)
    for l in range(DEPTH):
        w_l = w_in[l]
        w_x = w_l[:, OFF_XBC:OFF_DT].astype(BF16)
        w_b = jnp.concatenate([w_l[:, :OFF_XBC], w_l[:, OFF_G:],
                               w_l[:, OFF_Q:OFF_V] * qk_scale], axis=1).astype(BF16)
        w_v = w_l[:, OFF_V:OFF_G]
        px = _matmul(h, w_x, F32, 1024, 1024)
        pb = _matmul(h, w_b, BF16, 1024, 1024)
        vt = _matmul_nt(h, w_v, ATTN_TILE)
        dt, dt_t = _dt_call(h, w_l[:, OFF_DT:OFF_Q], dt_bias[l])
        y = _ssd_call(px, conv_w[l], conv_b[l], dt, dt_t, a_log[l], d_skip[l], batch, seq)
        ao = _attn_call(pb, vt, bias_near, bias_far, diff_lambda[l], diff_norm_g[l], l, batch,
                        seq)
        h1, h1_tiles = _mix_call(y, pb, ao, h, ssd_norm_g[l], w_ssd_out[l], w_attn_out[l], gate_b[l],
                       w_o[l], ln1_g[l], ln1_b[l])
        idx, wts, rank, counts = _router_call(h1, w_router[l], b_router[l])
        slots, blk_expert, n_used, pad_end = _moe_layout(idx[:TOP_K], rank[:TOP_K], counts,
                                                         n_blocks)
        xs = _dispatch_call(h1_tiles, slots, pad_end, cap)
        ys = _expert_call(xs, blk_expert + l * N_EXPERTS, n_used, w_gate, b_gate, w_up, b_up,
                          w_down, b_down)
        h = _combine_call(ys, slots, wts, h1, ln2_g[l], ln2_b[l])
    return h.reshape(batch, seq, d)
```

```python
import functools
import math

import jax
import jax.numpy as jnp
from jax import lax
from jax.experimental import pallas as pl
from jax.experimental.pallas import tpu as pltpu

F32 = jnp.float32
BF16 = jnp.bfloat16

D_MODEL = 1024
DEPTH = 2
CHUNK = 64

SSD_D_INNER = 2048
SSD_HEAD_DIM = 64
SSD_HEADS = 32
SSD_GROUPS = 4
SSD_HEADS_PER_GROUP = 8
SSD_STATE = 128
SSD_CONV = 4
SSD_CONV_DIM = 3072
SSD_GROUP_WIDTH = SSD_HEADS_PER_GROUP * SSD_HEAD_DIM

DIFF_HEAD_DIM = 64
DIFF_HEADS = 8
DIFF_WIDTH = 1024

REL_BUCKETS = 32
REL_MAX_DIST = 128

N_EXPERTS = 32
TOP_K = 4
D_FF = 1024
SWIGLU_ALPHA = 1.702
SWIGLU_LIMIT = 7.0

DN_ALPHA = (2 * DEPTH) ** 0.25
LN_EPS = 1e-5
RMS_EPS = 1e-5

OFF_XBC = 2048
OFF_DT = 5120
OFF_Q = 5152
OFF_V = 7200
OFF_G = 8224
PB_GATES = 2048
PB_Q = 4096
PB_K = 5120

VMEM_LIMIT_BYTES = 56 * 1024 * 1024

ATTN_TILE = 256
ATTN_HEADS_PER_STEP = 4
SSD_TILE = 256
SSD_GROUPS_PER_STEP = 4
EXPERT_BLOCK = 256
DISPATCH_TILE = 1024
COMBINE_TILE = 512
NEG_BIG = -1e30
LOG2_E = math.log2(math.e)


def _params(semantics, **kwargs):
    return pltpu.CompilerParams(dimension_semantics=semantics,
                                vmem_limit_bytes=VMEM_LIMIT_BYTES, **kwargs)


def _sigmoid(x):
    return 1.0 / (1.0 + jnp.exp(-x))


def _split2(v):
    hi = v.astype(BF16)
    lo = (v - hi.astype(F32)).astype(BF16)
    return hi, lo


def _dot3(a, b, dims):
    ah, al = _split2(a)
    bh, bl = _split2(b)
    dot = lambda p, q: lax.dot_general(p, q, dims, preferred_element_type=F32)
    return dot(ah, bh) + dot(ah, bl) + dot(al, bh)


def _split3(v):
    hi = v.astype(BF16)
    r = v - hi.astype(F32)
    mid = r.astype(BF16)
    lo = (r - mid.astype(F32)).astype(BF16)
    return hi, mid, lo


def _mm_kernel(x_ref, w_ref, o_ref, xb_ref):
    @pl.when(pl.program_id(1) == 0)
    def _():
        xb_ref[...] = x_ref[...].astype(BF16)

    o_ref[...] = jnp.dot(xb_ref[...], w_ref[...],
                         preferred_element_type=F32).astype(o_ref.dtype)


def _matmul(x, w, out_dtype, bm, bn):
    m, k = x.shape
    n = w.shape[1]
    return pl.pallas_call(
        _mm_kernel,
        grid=(m // bm, n // bn),
        in_specs=[pl.BlockSpec((bm, k), lambda i, j: (i, 0)),
                  pl.BlockSpec((k, bn), lambda i, j: (0, j))],
        out_specs=pl.BlockSpec((bm, bn), lambda i, j: (i, j)),
        out_shape=jax.ShapeDtypeStruct((m, n), out_dtype),
        scratch_shapes=[pltpu.VMEM((bm, k), BF16)],
        compiler_params=_params(("parallel", "arbitrary")),
        name="in_proj",
    )(x, w)


def _mm_nt_kernel(x_ref, w_ref, o_ref, wt_ref):
    @pl.when(pl.program_id(0) == 0)
    def _():
        wt_ref[...] = w_ref[...].T.astype(BF16)

    res = lax.dot_general(wt_ref[...], x_ref[...].astype(BF16), (((1,), (1,)), ((), ())),
                          preferred_element_type=F32)
    tile = o_ref.shape[2]
    for c in range(o_ref.shape[0]):
        o_ref[c] = res[:, c * tile:(c + 1) * tile].astype(o_ref.dtype)


def _matmul_nt(x, w, tile, bm=1024):
    m, k = x.shape
    n = w.shape[1]
    return pl.pallas_call(
        _mm_nt_kernel,
        grid=(m // bm,),
        in_specs=[pl.BlockSpec((bm, k), lambda i: (i, 0)),
                  pl.BlockSpec((k, n), lambda i: (0, 0))],
        out_specs=pl.BlockSpec((bm // tile, n, tile), lambda i: (i, 0, 0)),
        out_shape=jax.ShapeDtypeStruct((m // tile, n, tile), BF16),
        scratch_shapes=[pltpu.VMEM((n, k), BF16)],
        compiler_params=_params(("arbitrary",)),
        name="v_proj_t",
    )(x, w)


def _softplus(x):
    return jnp.maximum(x, 0.0) + jnp.log(1.0 + jnp.exp(-jnp.abs(x)))


def _dt_kernel(x_ref, w_ref, b_ref, dt_ref, dtt_ref):
    x = x_ref[...]
    tm = x.shape[0]
    raw = _dot3(x, w_ref[...], (((1,), (0,)), ((), ())))
    dt = _softplus(raw + b_ref[...])
    dt_t = dt.T
    r = SSD_HEADS_PER_GROUP
    for g in range(SSD_GROUPS):
        dt_ref[g] = dt[:, g * r:(g + 1) * r]
        for j in range(tm // CHUNK):
            dtt_ref[g, j] = dt_t[g * r:(g + 1) * r, j * CHUNK:(j + 1) * CHUNK]


def _dt_call(h, w_dt, dt_bias, tm=512):
    t = h.shape[0]
    r = SSD_HEADS_PER_GROUP
    lanes = 128
    w_pad = jnp.pad(w_dt, ((0, 0), (0, lanes - SSD_HEADS)))
    b_pad = jnp.pad(dt_bias, (0, lanes - SSD_HEADS)).reshape(1, lanes)
    return pl.pallas_call(
        _dt_kernel,
        grid=(t // tm,),
        in_specs=[pl.BlockSpec((tm, D_MODEL), lambda i: (i, 0)),
                  pl.BlockSpec((D_MODEL, lanes), lambda i: (0, 0)),
                  pl.BlockSpec((1, lanes), lambda i: (0, 0))],
        out_specs=[pl.BlockSpec((SSD_GROUPS, tm, r), lambda i: (0, i, 0)),
                   pl.BlockSpec((SSD_GROUPS, tm // CHUNK, r, CHUNK), lambda i: (0, i, 0, 0))],
        out_shape=[jax.ShapeDtypeStruct((SSD_GROUPS, t, r), F32),
                   jax.ShapeDtypeStruct((SSD_GROUPS, t // CHUNK, r, CHUNK), F32)],
        compiler_params=_params(("parallel",)),
        name="dt_proj",
    )(h, w_pad, b_pad)


def _ssd_kernel(x_ref, b_ref, c_ref, wx_ref, wb_ref, wc_ref, bx_ref, bb_ref, bc_ref,
                dt_ref, dtt_ref, alr_ref, alc_ref, dsk_ref, bd_ref, trit_ref, e_ref,
                y_ref,
                state_ref, ux_ref, ub_ref, uc_ref, xc_ref, bcv_ref, ccv_ref, xdt_ref,
                wst_ref, eacs_ref, acs_ref, acst_ref):
    s_idx = pl.program_id(2)
    ts = x_ref.shape[0]
    nc = ts // CHUNK
    r = SSD_HEADS_PER_GROUP
    gw = SSD_GROUP_WIDTH
    n = SSD_STATE
    ng = dt_ref.shape[0]

    @pl.when(s_idx == 0)
    def _():
        state_ref[...] = jnp.zeros_like(state_ref)
        ux_ref[0:8, :] = jnp.zeros((8, ux_ref.shape[1]), F32)
        ub_ref[0:8, :] = jnp.zeros((8, ub_ref.shape[1]), F32)
        uc_ref[0:8, :] = jnp.zeros((8, uc_ref.shape[1]), F32)

    def conv_silu(raw_ref, u_ref, w_ref, bias_ref):
        u_ref[8:8 + ts, :] = raw_ref[...]
        u = u_ref[...]
        u2 = pltpu.roll(u, 2, axis=0)
        even = w_ref[3:4, :] * u + w_ref[1:2, :] * u2
        odd = w_ref[2:3, :] * u + w_ref[0:1, :] * u2
        acc = (even + pltpu.roll(odd, 1, axis=0))[8:8 + ts] + bias_ref[...]
        u_ref[0:8, :] = u_ref[ts:ts + 8, :]
        return acc * _sigmoid(acc)

    xc_ref[...] = conv_silu(x_ref, ux_ref, wx_ref, bx_ref)
    bcv_ref[...] = conv_silu(b_ref, ub_ref, wb_ref, bb_ref).astype(BF16)
    ccv_ref[...] = conv_silu(c_ref, uc_ref, wc_ref, bc_ref).astype(BF16)

    bd = bd_ref[...]
    trit = trit_ref[...]
    e_mat = e_ref[...]

    def expand(v):
        return jnp.dot(jnp.concatenate(_split2(v), axis=1), e_mat, preferred_element_type=F32)

    for g in range(ng):
        gl = slice(g * gw, (g + 1) * gw)
        a_row = -jnp.exp(alr_ref[g])
        a_col = -jnp.exp(alc_ref[g])
        dt = dt_ref[g]
        a = dt * a_row
        acs = jnp.zeros((ts, r), F32)
        for part in _split3(a):
            acs = acs + jnp.dot(bd, part, preferred_element_type=F32)
        acs_ref[g] = acs
        a_t = dtt_ref[g].reshape(nc * r, CHUNK) * jnp.concatenate([a_col] * nc, axis=0)
        acs_t = jnp.zeros((nc * r, CHUNK), F32)
        for part in _split3(a_t):
            acs_t = acs_t + jnp.dot(part, trit, preferred_element_type=F32)
        acst_ref[g] = acs_t
        a_last = jnp.concatenate(
            [jnp.broadcast_to(acs[c * CHUNK + CHUNK - 1:c * CHUNK + CHUNK, :], (CHUNK, r))
             for c in range(nc)], axis=0)
        xdt = xc_ref[:, gl] * expand(dt)
        xdt_ref[:, gl] = xdt.astype(BF16)
        wst_ref[:, gl] = (xdt * expand(jnp.exp(a_last - acs))).astype(BF16)
        eacs_ref[:, gl] = expand(jnp.exp(acs))

    row_i = lax.broadcasted_iota(jnp.int32, (CHUNK, CHUNK), 0)
    col_i = lax.broadcasted_iota(jnp.int32, (CHUNK, CHUNK), 1)
    tril = col_i <= row_i
    left_half = lax.broadcasted_iota(jnp.int32, (CHUNK, 128), 1) < SSD_HEAD_DIM

    for c in range(nc):
        rows = slice(c * CHUNK, (c + 1) * CHUNK)
        for g in range(ng):
            cc = ccv_ref[rows, g * n:(g + 1) * n]
            bc = bcv_ref[rows, g * n:(g + 1) * n]
            cb = lax.dot_general(cc, bc, (((1,), (1,)), ((), ())), preferred_element_type=F32)
            acs_c = acs_ref[g, rows, :]
            acs_tc = acst_ref[g, c * r:(c + 1) * r, :]
            for p in range(r // 2):
                lo = g * gw + p * 128
                xp = xdt_ref[rows, lo:lo + 128]
                halves = []
                for hh in (2 * p, 2 * p + 1):
                    diff = acs_c[:, hh:hh + 1] - acs_tc[hh:hh + 1, :]
                    decay = jnp.exp(jnp.where(tril, diff, -jnp.inf))
                    halves.append(jnp.dot((cb * decay).astype(BF16), xp,
                                          preferred_element_type=F32))
                y_diag = jnp.where(left_half, halves[0], halves[1])
                y_ref[rows, lo:lo + 128] = (y_diag
                                            + dsk_ref[:, lo:lo + 128] * xc_ref[rows, lo:lo + 128])

    def chunk_body(c, carry):
        rows = pl.ds(pl.multiple_of(c * CHUNK, CHUNK), CHUNK)
        for g in range(ng):
            gl = slice(g * gw, (g + 1) * gw)
            cc = ccv_ref[rows, g * n:(g + 1) * n]
            bc = bcv_ref[rows, g * n:(g + 1) * n]
            state = state_ref[g]
            eacs = eacs_ref[rows, gl]
            y_off = jnp.dot(cc, state.astype(BF16), preferred_element_type=F32) * eacs
            y_ref[rows, gl] = y_ref[rows, gl] + y_off
            upd = lax.dot_general(bc, wst_ref[rows, gl], (((0,), (0,)), ((), ())),
                                  preferred_element_type=F32)
            state_ref[g] = state * eacs[CHUNK - 1:CHUNK, :] + upd
        return carry

    lax.fori_loop(0, nc, chunk_body, 0, unroll=True)


def _ssd_call(px, conv_w, conv_b, dt, dt_t, a_log, d_skip, batch, seq):
    t = batch * seq
    ts = SSD_TILE
    ns = seq // ts
    nc = ts // CHUNK
    r = SSD_HEADS_PER_GROUP
    ng = SSD_GROUPS_PER_STEP
    gw = ng * SSD_GROUP_WIDTH
    n = ng * SSD_STATE
    wb0 = SSD_D_INNER // n
    wc0 = wb0 + SSD_GROUPS // ng

    li = jnp.arange(ts)
    bd = ((li[None, :] <= li[:, None]) & (li[None, :] // CHUNK == li[:, None] // CHUNK)).astype(BF16)
    lc = jnp.arange(CHUNK)
    trit = (lc[:, None] <= lc[None, :]).astype(BF16)
    e_mat = (jnp.arange(SSD_GROUP_WIDTH)[None, :] // SSD_HEAD_DIM
             == jnp.arange(2 * r)[:, None] % r).astype(BF16)
    conv_b2 = conv_b.reshape(1, SSD_CONV_DIM)
    alr = a_log.reshape(SSD_GROUPS, 1, r)
    alc = a_log.reshape(SSD_GROUPS, r, 1)
    dsk = jnp.repeat(d_skip, SSD_HEAD_DIM).reshape(1, SSD_D_INNER)

    row = lambda b, g, s: b * ns + s
    in_specs = [
        pl.BlockSpec((ts, gw), lambda b, g, s: (row(b, g, s), g)),
        pl.BlockSpec((ts, n), lambda b, g, s: (row(b, g, s), wb0 + g)),
        pl.BlockSpec((ts, n), lambda b, g, s: (row(b, g, s), wc0 + g)),
        pl.BlockSpec((SSD_CONV, gw), lambda b, g, s: (0, g)),
        pl.BlockSpec((SSD_CONV, n), lambda b, g, s: (0, wb0 + g)),
        pl.BlockSpec((SSD_CONV, n), lambda b, g, s: (0, wc0 + g)),
        pl.BlockSpec((1, gw), lambda b, g, s: (0, g)),
        pl.BlockSpec((1, n), lambda b, g, s: (0, wb0 + g)),
        pl.BlockSpec((1, n), lambda b, g, s: (0, wc0 + g)),
        pl.BlockSpec((ng, ts, r), lambda b, g, s: (g, row(b, g, s), 0)),
        pl.BlockSpec((ng, nc, r, CHUNK), lambda b, g, s: (g, row(b, g, s), 0, 0)),
        pl.BlockSpec((ng, 1, r), lambda b, g, s: (g, 0, 0)),
        pl.BlockSpec((ng, r, 1), lambda b, g, s: (g, 0, 0)),
        pl.BlockSpec((1, gw), lambda b, g, s: (0, g)),
        pl.BlockSpec((ts, ts), lambda b, g, s: (0, 0)),
        pl.BlockSpec((CHUNK, CHUNK), lambda b, g, s: (0, 0)),
        pl.BlockSpec((2 * r, SSD_GROUP_WIDTH), lambda b, g, s: (0, 0)),
    ]
    scratch = [
        pltpu.VMEM((ng, SSD_STATE, SSD_GROUP_WIDTH), F32),
        pltpu.VMEM((ts + 8, gw), F32),
        pltpu.VMEM((ts + 8, n), F32),
        pltpu.VMEM((ts + 8, n), F32),
        pltpu.VMEM((ts, gw), F32),
        pltpu.VMEM((ts, n), BF16),
        pltpu.VMEM((ts, n), BF16),
        pltpu.VMEM((ts, gw), BF16),
        pltpu.VMEM((ts, gw), BF16),
        pltpu.VMEM((ts, gw), F32),
        pltpu.VMEM((ng, ts, r), F32),
        pltpu.VMEM((ng, nc * r, CHUNK), F32),
    ]
    return pl.pallas_call(
        _ssd_kernel,
        grid=(batch, SSD_GROUPS // ng, ns),
        in_specs=in_specs,
        out_specs=pl.BlockSpec((ts, gw), lambda b, g, s: (row(b, g, s), g)),
        out_shape=jax.ShapeDtypeStruct((t, SSD_D_INNER), F32),
        scratch_shapes=scratch,
        compiler_params=_params(("parallel", "parallel", "arbitrary")),
        name="ssd_scan",
    )(px, px, px, conv_w, conv_w, conv_w, conv_b2, conv_b2, conv_b2,
      dt, dt_t, alr, alc, dsk, bd, trit, e_mat)


def _attn_kernel(q_ref, k_ref, vt_ref, bias_ref, far_ref, lam_ref, g_ref, o_ref, sa_ref, sb_ref,
                 *, lam_init):
    i = pl.program_id(2)
    tq = q_ref.shape[0]
    dh = DIFF_HEAD_DIM
    lv = lam_ref[...]
    lam = (jnp.exp(jnp.sum(lv[0:1] * lv[1:2], axis=1, keepdims=True))
           - jnp.exp(jnp.sum(lv[2:3] * lv[3:4], axis=1, keepdims=True)) + lam_init)

    q = q_ref[...]
    hp = q.shape[1] // (2 * dh)
    ns = 2 * hp
    qs = [q[:, s * dh:(s + 1) * dh] for s in range(ns)]
    nt = (((1,), (1,)), ((), ()))

    n_tiles = bias_ref.shape[1] - 1
    h0 = pl.program_id(1) * hp
    ones = jnp.ones((16, tq), BF16)
    dv = 2 * dh

    def near_tile(t):
        valid = t <= i
        return jnp.where(valid, i - t, 0), jnp.where(valid, t, n_tiles)

    def far_tile(t):
        valid = t <= i
        j = jnp.where(valid, t - 2, 0)
        d = jnp.where(valid, i - j, 0)
        return j, [jnp.where(valid, far_ref[h0 + hh, d], NEG_BIG) for hh in range(hp)]

    def scores(j, s_ref, biases):
        kj = k_ref[pl.ds(pl.multiple_of(j * tq, tq), tq), :]
        for s in range(ns):
            sc = lax.dot_general(kj[:, s * dh:(s + 1) * dh], qs[s], nt,
                                 preferred_element_type=F32)
            s_ref[s] = sc if biases is None else sc + biases[s // 2]

    def update(j, s_ref, shifts, carry):
        vtj = vt_ref[j]
        out = []
        for s in range(ns):
            hh = s // 2
            vth = jnp.concatenate([vtj[hh * dv:(hh + 1) * dv], ones], axis=0)
            mx, acc = carry[2 * s:2 * s + 2]
            sc = s_ref[s]
            mx_new = jnp.maximum(mx, jnp.max(sc, axis=0, keepdims=True) + shifts[hh])
            p = jnp.exp2(sc - (mx_new - shifts[hh]))
            out += [mx_new, jnp.exp2(mx - mx_new) * acc
                    + jnp.dot(vth, p.astype(BF16), preferred_element_type=F32)]
        return tuple(out)

    def scores_near(t, s_ref):
        j, d = near_tile(t)
        scores(j, s_ref, [bias_ref[hh, d] for hh in range(hp)])

    def scores_far(t, s_ref):
        scores(far_tile(t)[0], s_ref, None)

    def update_near(t, s_ref, carry):
        return update(near_tile(t)[0], s_ref, [0.0] * hp, carry)

    def update_far(t, s_ref, carry):
        j, shifts = far_tile(t)
        return update(j, s_ref, shifts, carry)

    def body(u, carry):
        scores_far(2 * u + 1, sb_ref)
        carry = update_far(2 * u, sa_ref, carry)
        scores_far(2 * u + 2, sa_ref)
        return update_far(2 * u + 1, sb_ref, carry)

    carry = []
    for _ in range(ns):
        carry += [jnp.full((1, tq), NEG_BIG, F32), jnp.zeros((dv + 16, tq), F32)]
    scores_near(0, sa_ref)
    scores_near(1, sb_ref)
    carry = update_near(0, sa_ref, tuple(carry))
    scores_far(2, sa_ref)
    carry = update_near(1, sb_ref, carry)
    carry = lax.fori_loop(1, (i + 2) // 2, body, carry)
    for hh in range(hp):
        acc1, acc2 = carry[4 * hh + 1], carry[4 * hh + 3]
        o = acc1[:dv] / acc1[dv:dv + 1] - lam * (acc2[:dv] / acc2[dv:dv + 1])
        ms = jnp.mean(o * o, axis=0, keepdims=True)
        o = o * lax.rsqrt(ms + RMS_EPS) * g_ref[...] * (1.0 - lam_init)
        o_ref[:, hh * dv:(hh + 1) * dv] = o.T.astype(o_ref.dtype)


def _rel_bucket(rel):
    half = REL_BUCKETS // 2
    max_exact = half // 2
    ret = jnp.where(rel > 0, half, 0)
    n = jnp.abs(rel)
    nf = jnp.maximum(n, 1).astype(F32)
    large = max_exact + (jnp.log(nf / max_exact) / math.log(REL_MAX_DIST / max_exact)
                         * (half - max_exact)).astype(jnp.int32)
    large = jnp.minimum(large, half - 1)
    return ret + jnp.where(n < max_exact, n, large)


def _bias_lookup(rel_bias, rel):
    bucket = _rel_bucket(rel)[None]
    table = rel_bias.astype(F32)
    lead = (slice(None),) + (None,) * rel.ndim
    bias = jnp.zeros((DIFF_HEADS,) + rel.shape, F32)
    for b in range(REL_BUCKETS):
        bias = jnp.where(bucket == b, table[b][lead], bias)
    return bias * LOG2_E


def _bias_tiles(rel_bias, seq):
    tq = ATTN_TILE
    kk = jnp.arange(tq)[None, :, None]
    qq = jnp.arange(tq)[None, None, :]
    d = jnp.arange(2)[:, None, None]
    near = _bias_lookup(rel_bias, kk - qq - d * tq)
    allowed = (d > 0) | ((kk // CHUNK) <= (qq // CHUNK))
    near = jnp.where(allowed[None], near, NEG_BIG)
    masked = jnp.full((DIFF_HEADS, 1, tq, tq), NEG_BIG, F32)
    far = _bias_lookup(rel_bias, -tq * jnp.arange(seq // tq + 1))
    return jnp.concatenate([near, masked], axis=1), far


def _attn_call(pb, vt, bias_near, bias_far, lam_vecs, norm_g, layer_idx, batch, seq):
    t = batch * seq
    tq = ATTN_TILE
    nq = seq // tq
    hp = ATTN_HEADS_PER_STEP
    w = 2 * DIFF_HEAD_DIM
    wb = hp * w
    qcol0 = PB_Q // wb
    kcol0 = PB_K // wb
    lam_init = 0.8 - 0.6 * math.exp(-0.3 * layer_idx)
    return pl.pallas_call(
        functools.partial(_attn_kernel, lam_init=lam_init),
        grid=(batch, DIFF_HEADS // hp, nq),
        in_specs=[pl.BlockSpec((tq, wb), lambda b, h, i: (b * nq + i, qcol0 + h)),
                  pl.BlockSpec((seq, wb), lambda b, h, i: (b, kcol0 + h)),
                  pl.BlockSpec((nq, wb, tq), lambda b, h, i: (b, h, 0)),
                  pl.BlockSpec((hp, 3, tq, tq), lambda b, h, i: (h, 0, 0, 0)),
                  pl.BlockSpec(memory_space=pltpu.SMEM),
                  pl.BlockSpec((4, DIFF_HEAD_DIM), lambda b, h, i: (0, 0)),
                  pl.BlockSpec((w, 1), lambda b, h, i: (0, 0))],
        out_specs=pl.BlockSpec((tq, wb), lambda b, h, i: (b * nq + i, h)),
        out_shape=jax.ShapeDtypeStruct((t, DIFF_WIDTH), BF16),
        scratch_shapes=[pltpu.VMEM((2 * hp, tq, tq), F32), pltpu.VMEM((2 * hp, tq, tq), F32)],
        compiler_params=_params(("parallel", "parallel", "arbitrary")),
        name="diff_attn",
    )(pb, pb, vt, bias_near, bias_far, lam_vecs, norm_g.reshape(w, 1))


def _layer_norm(x, g, b):
    mu = jnp.mean(x, axis=1, keepdims=True)
    xc = x - mu
    var = jnp.mean(xc * xc, axis=1, keepdims=True)
    return xc * lax.rsqrt(var + LN_EPS) * g + b


def _mix_kernel(y_ref, z_ref, ao_ref, g0_ref, g1_ref, h_ref, ng_ref, wso_ref, wao_ref,
                gb_ref, wo_ref, lg_ref, lb_ref, o_ref, ot_ref):
    z = z_ref[...].astype(F32)
    yg = y_ref[...] * (z * _sigmoid(z))
    ms = jnp.mean(yg * yg, axis=1, keepdims=True)
    yn = (yg * lax.rsqrt(ms + RMS_EPS) * ng_ref[...]).astype(BF16)
    y_ssd = jnp.dot(yn, wso_ref[...], preferred_element_type=F32)
    y_att = jnp.dot(ao_ref[...], wao_ref[...], preferred_element_type=F32)
    gb = gb_ref[...]
    gate0 = _sigmoid(g0_ref[...].astype(F32) + gb[:, :D_MODEL])
    gate1 = _sigmoid(g1_ref[...].astype(F32) + gb[:, D_MODEL:])
    mixed = (gate0 * y_ssd + gate1 * y_att).astype(BF16)
    mix = jnp.dot(mixed, wo_ref[...], preferred_element_type=F32)
    out = _layer_norm(DN_ALPHA * h_ref[...] + mix, lg_ref[...], lb_ref[...])
    o_ref[...] = out
    _store_row_tiles(ot_ref, out)


ROW_TILE = (8, 128)


def _store_row_tiles(ref, rows):
    sub, lanes = ROW_TILE
    m = rows.shape[0]
    for s in range(sub):
        ref[pl.ds(s, m, stride=sub), :] = rows[:, s * lanes:(s + 1) * lanes]


def _load_row_tiles(ref):
    sub, lanes = ROW_TILE
    m = ref.shape[0] // sub
    return jnp.concatenate([ref[pl.ds(s, m, stride=sub), :] for s in range(sub)], axis=1)


def _mix_call(y, pb, ao, h, norm_g, w_ssd_out, w_attn_out, gate_b, w_o, ln_g, ln_b, tm=512):
    t = h.shape[0]
    d = D_MODEL
    gcol0 = PB_GATES // d
    const = lambda i: (0, 0)
    return pl.pallas_call(
        _mix_kernel,
        grid=(t // tm,),
        in_specs=[pl.BlockSpec((tm, SSD_D_INNER), lambda i: (i, 0)),
                  pl.BlockSpec((tm, SSD_D_INNER), lambda i: (i, 0)),
                  pl.BlockSpec((tm, DIFF_WIDTH), lambda i: (i, 0)),
                  pl.BlockSpec((tm, d), lambda i: (i, gcol0)),
                  pl.BlockSpec((tm, d), lambda i: (i, gcol0 + 1)),
                  pl.BlockSpec((tm, d), lambda i: (i, 0)),
                  pl.BlockSpec((1, SSD_D_INNER), const),
                  pl.BlockSpec((SSD_D_INNER, d), const),
                  pl.BlockSpec((DIFF_WIDTH, d), const),
                  pl.BlockSpec((1, 2 * d), const),
                  pl.BlockSpec((d, d), const),
                  pl.BlockSpec((1, d), const),
                  pl.BlockSpec((1, d), const)],
        out_specs=[pl.BlockSpec((tm, d), lambda i: (i, 0)),
                   pl.BlockSpec((tm * ROW_TILE[0], ROW_TILE[1]), lambda i: (i, 0))],
        out_shape=[jax.ShapeDtypeStruct((t, d), F32),
                   jax.ShapeDtypeStruct((t * ROW_TILE[0], ROW_TILE[1]), F32)],
        compiler_params=_params(("parallel",)),
        name="mix_ln",
    )(y, pb, ao, pb, pb, h, norm_g.reshape(1, -1), w_ssd_out.astype(BF16),
      w_attn_out.astype(BF16), gate_b.reshape(1, -1), w_o.astype(BF16),
      ln_g.reshape(1, -1), ln_b.reshape(1, -1))


def _router_kernel(h_ref, w_ref, b_ref, tri_ref, idx_ref, wt_ref, rank_ref, cnt_ref, run_ref):
    @pl.when(pl.program_id(0) == 0)
    def _():
        run_ref[...] = jnp.zeros_like(run_ref)

    tm = h_ref.shape[0]
    ne = N_EXPERTS
    logits = _dot3(h_ref[...], w_ref[...], (((1,), (0,)), ((), ()))) + b_ref[...]
    lane = lax.broadcasted_iota(jnp.int32, (tm, ne), 1).astype(F32)
    work = logits
    sel, vals = [], []
    for _ in range(TOP_K):
        mx = jnp.max(work, axis=1, keepdims=True)
        first = jnp.min(jnp.where(work == mx, lane, float(ne)), axis=1, keepdims=True)
        hit = lane == first
        sel.append((first, hit))
        vals.append(mx)
        work = jnp.where(hit, -jnp.inf, work)
    exps = [jnp.exp(v - vals[0]) for v in vals]
    denom = exps[0] + exps[1] + exps[2] + exps[3]

    onehot = jnp.zeros((tm, ne), F32)
    for _, hit in sel:
        onehot = onehot + hit.astype(F32)
    before = jnp.dot(tri_ref[...], onehot.astype(BF16), preferred_element_type=F32)
    before = before + run_ref[...]

    out_lane = lax.broadcasted_iota(jnp.int32, (tm, 128), 1)
    idx_out = jnp.zeros((tm, 128), F32)
    wt_out = jnp.zeros((tm, 128), F32)
    rank_out = jnp.zeros((tm, 128), F32)
    for k, (first, hit) in enumerate(sel):
        rank = jnp.sum(jnp.where(hit, before, 0.0), axis=1, keepdims=True)
        idx_out = jnp.where(out_lane == k, first, idx_out)
        wt_out = jnp.where(out_lane == k, exps[k] / denom, wt_out)
        rank_out = jnp.where(out_lane == k, rank, rank_out)
    idx_ref[...] = idx_out.T[0:8].astype(jnp.int32)
    wt_ref[...] = wt_out
    rank_ref[...] = rank_out.T[0:8].astype(jnp.int32)
    total = run_ref[...] + jnp.sum(onehot, axis=0, keepdims=True)
    run_ref[...] = total
    cnt_ref[...] = total


def _router_call(h, w_router, b_router, tm=512):
    t = h.shape[0]
    li = jnp.arange(tm)
    tri = (li[None, :] < li[:, None]).astype(BF16)
    const = lambda i: (0, 0)
    return pl.pallas_call(
        _router_kernel,
        grid=(t // tm,),
        in_specs=[pl.BlockSpec((tm, D_MODEL), lambda i: (i, 0)),
                  pl.BlockSpec((D_MODEL, N_EXPERTS), const),
                  pl.BlockSpec((1, N_EXPERTS), const),
                  pl.BlockSpec((tm, tm), const)],
        out_specs=[pl.BlockSpec((8, tm), lambda i: (0, i)),
                   pl.BlockSpec((tm, 128), lambda i: (i, 0)),
                   pl.BlockSpec((8, tm), lambda i: (0, i)),
                   pl.BlockSpec((1, N_EXPERTS), const)],
        out_shape=[jax.ShapeDtypeStruct((8, t), jnp.int32),
                   jax.ShapeDtypeStruct((t, 128), F32),
                   jax.ShapeDtypeStruct((8, t), jnp.int32),
                   jax.ShapeDtypeStruct((1, N_EXPERTS), F32)],
        scratch_shapes=[pltpu.VMEM((1, N_EXPERTS), F32)],
        compiler_params=_params(("arbitrary",)),
        name="router",
    )(h, w_router, b_router.reshape(1, N_EXPERTS), tri)


def _dispatch_kernel(pad_end_ref, slot_ref, x_ref, xs_out, zero_ref, sem, zero_sem):
    sub = ROW_TILE[0]
    tt = x_ref.shape[0] // sub
    n = tt * TOP_K
    zb = zero_ref.shape[0] // sub

    def row(ref, r, count=1):
        return ref.at[pl.ds(pl.multiple_of(r * sub, sub), count * sub)]

    @pl.when(pl.program_id(0) == 0)
    def _():
        zero_ref[...] = jnp.zeros_like(zero_ref)

        def clear(first):
            return pltpu.make_async_copy(zero_ref, row(xs_out, first, zb), zero_sem)

        for e in range(N_EXPERTS):
            clear(jnp.maximum(pad_end_ref[e] - zb, 0)).start()
        used = pad_end_ref[N_EXPERTS - 1] // zb
        total = xs_out.shape[0] // (zb * sub)

        def clear_tail(b, c):
            clear(b * zb).start()
            return c

        def wait_one(b, c):
            clear(0).wait()
            return c

        lax.fori_loop(used, total, clear_tail, 0)
        lax.fori_loop(used - N_EXPERTS, total, wait_one, 0)

    def start(tok, c):
        for k in range(TOP_K):
            slot = slot_ref[0, 0, k * tt + tok]
            pltpu.make_async_copy(row(x_ref, tok), row(xs_out, slot), sem).start(priority=k % 2)
        return c

    lax.fori_loop(0, tt, start, 0, unroll=8)
    pltpu.make_async_copy(row(xs_out, 0, n), row(xs_out, 0, n), sem).wait()


def _dispatch_call(xt, slots, pad_end, cap):
    sub, lanes = ROW_TILE
    t = xt.shape[0] // sub
    tt = DISPATCH_TILE
    nt = t // tt
    n = tt * TOP_K
    assert cap >= N_EXPERTS * EXPERT_BLOCK
    grid_spec = pltpu.PrefetchScalarGridSpec(
        num_scalar_prefetch=1,
        grid=(nt,),
        in_specs=[pl.BlockSpec((1, 1, n), lambda i, pe: (i, 0, 0), memory_space=pltpu.SMEM),
                  pl.BlockSpec((tt * sub, lanes), lambda i, pe: (i, 0))],
        out_specs=pl.BlockSpec(memory_space=pl.ANY),
        scratch_shapes=[pltpu.VMEM((EXPERT_BLOCK * sub, lanes), xt.dtype),
                        pltpu.SemaphoreType.DMA, pltpu.SemaphoreType.DMA],
    )
    return pl.pallas_call(
        _dispatch_kernel,
        grid_spec=grid_spec,
        out_shape=jax.ShapeDtypeStruct((cap * sub, lanes), xt.dtype),
        compiler_params=_params(("arbitrary",), disable_bounds_checks=True),
        name="moe_dispatch",
    )(pad_end, _tile_slots(slots, tt), xt)


def _expert_kernel(be_ref, nb_ref, x_ref, wg_ref, bg_ref, wu_ref, bu_ref, wd_ref, bd_ref,
                   o_ref, wgb_ref, wub_ref, wdb_ref):
    i = pl.program_id(0)
    prev = be_ref[jnp.maximum(i - 1, 0)]
    changed = jnp.logical_or(i == 0, be_ref[i] != prev)

    @pl.when(changed)
    def _():
        wgb_ref[...] = wg_ref[0].astype(BF16)
        wub_ref[...] = wu_ref[0].astype(BF16)
        wdb_ref[...] = wd_ref[0].astype(BF16)

    @pl.when(i < nb_ref[0])
    def _():
        xb = _load_row_tiles(x_ref).astype(BF16)
        g = jnp.dot(xb, wgb_ref[...], preferred_element_type=F32) + bg_ref[0]
        u = jnp.dot(xb, wub_ref[...], preferred_element_type=F32) + bu_ref[0]
        g = jnp.minimum(g, SWIGLU_LIMIT)
        u = jnp.clip(u, -SWIGLU_LIMIT, SWIGLU_LIMIT)
        act = g * _sigmoid(SWIGLU_ALPHA * g) * (u + 1.0)
        _store_row_tiles(o_ref, jnp.dot(act.astype(BF16), wdb_ref[...],
                                        preferred_element_type=F32) + bd_ref[0])

    @pl.when(i >= nb_ref[0])
    def _():
        o_ref[...] = jnp.zeros_like(o_ref)


def _expert_call(xs, blk_expert, n_used, w_gate, b_gate, w_up, b_up, w_down, b_down):
    sub, lanes = ROW_TILE
    cap = xs.shape[0] // sub
    d = sub * lanes
    bm = EXPERT_BLOCK
    nb = cap // bm
    wspec = lambda shape: pl.BlockSpec(shape, lambda i, be, nu: (be[i], 0, 0))
    grid_spec = pltpu.PrefetchScalarGridSpec(
        num_scalar_prefetch=2,
        grid=(nb,),
        in_specs=[pl.BlockSpec((bm * sub, lanes),
                               lambda i, be, nu: (jnp.minimum(i, nu[0] - 1), 0)),
                  wspec((1, d, D_FF)), wspec((1, 1, D_FF)),
                  wspec((1, d, D_FF)), wspec((1, 1, D_FF)),
                  wspec((1, D_FF, d)), wspec((1, 1, d))],
        out_specs=pl.BlockSpec((bm * sub, lanes), lambda i, be, nu: (i, 0)),
        scratch_shapes=[pltpu.VMEM((d, D_FF), BF16), pltpu.VMEM((d, D_FF), BF16),
                        pltpu.VMEM((D_FF, d), BF16)],
    )
    ne = w_gate.shape[0] * w_gate.shape[1]
    return pl.pallas_call(
        _expert_kernel,
        grid_spec=grid_spec,
        out_shape=jax.ShapeDtypeStruct((cap * sub, lanes), F32),
        compiler_params=_params(("arbitrary",)),
        name="moe_experts",
    )(blk_expert, n_used, xs, w_gate.reshape(ne, d, D_FF), b_gate.reshape(ne, 1, D_FF),
      w_up.reshape(ne, d, D_FF), b_up.reshape(ne, 1, D_FF), w_down.reshape(ne, D_FF, d),
      b_down.reshape(ne, 1, d))


def _combine_kernel(slot_ref, next_slot_ref, ys_hbm, wt_ref, h_ref, lg_ref, lb_ref, o_ref,
                    buf_ref, sems):
    i = pl.program_id(0)
    tt = h_ref.shape[0]
    n = tt * TOP_K
    sub = ROW_TILE[0]

    def issue(s_ref, buf):
        def start(r, c):
            for j in range(8):
                a = r * 8 + j
                src = pl.ds(pl.multiple_of(s_ref[0, 0, a] * sub, sub), sub)
                dst = pl.ds(pl.multiple_of(a * sub, sub), sub)
                pltpu.make_async_copy(ys_hbm.at[src], buf_ref.at[buf, dst],
                                      sems.at[buf]).start(priority=j % 2)
            return c

        lax.fori_loop(0, n // 8, start, 0)

    @pl.when(i == 0)
    def _():
        issue(slot_ref, 0)

    @pl.when(i + 1 < pl.num_programs(0))
    def _():
        issue(next_slot_ref, (i + 1) % 2)

    cur = i % 2
    pltpu.make_async_copy(buf_ref.at[cur], buf_ref.at[cur], sems.at[cur]).wait()

    wt = wt_ref[...]
    rows = lambda k: _load_row_tiles(buf_ref.at[cur, pl.ds(k * tt * sub, tt * sub)])
    ff = wt[:, 0:1] * rows(0)
    for k in range(1, TOP_K):
        ff = ff + wt[:, k:k + 1] * rows(k)
    o_ref[...] = _layer_norm(DN_ALPHA * h_ref[...] + ff, lg_ref[...], lb_ref[...])


def _combine_call(ys, slots, wts, h, ln_g, ln_b):
    t, d = h.shape
    tt = COMBINE_TILE
    nt = t // tt
    n = tt * TOP_K
    slots_km = _tile_slots(slots, tt)
    const = lambda i: (0, 0)
    return pl.pallas_call(
        _combine_kernel,
        grid=(nt,),
        in_specs=[pl.BlockSpec((1, 1, n), lambda i: (i, 0, 0), memory_space=pltpu.SMEM),
                  pl.BlockSpec((1, 1, n), lambda i: (jnp.minimum(i + 1, nt - 1), 0, 0),
                               memory_space=pltpu.SMEM),
                  pl.BlockSpec(memory_space=pl.ANY),
                  pl.BlockSpec((tt, 128), lambda i: (i, 0)),
                  pl.BlockSpec((tt, d), lambda i: (i, 0)),
                  pl.BlockSpec((1, d), const),
                  pl.BlockSpec((1, d), const)],
        out_specs=pl.BlockSpec((tt, d), lambda i: (i, 0)),
        out_shape=jax.ShapeDtypeStruct((t, d), F32),
        scratch_shapes=[pltpu.VMEM((2, n * ROW_TILE[0], ROW_TILE[1]), F32),
                        pltpu.SemaphoreType.DMA((2,))],
        compiler_params=_params(("arbitrary",), disable_bounds_checks=True),
        name="moe_combine_ln",
    )(slots_km, slots_km, ys, wts, h, ln_g.reshape(1, d), ln_b.reshape(1, d))


def _tile_slots(slots, tt):
    t = slots.shape[1]
    return slots.reshape(TOP_K, t // tt, tt).transpose(1, 0, 2).reshape(t // tt, 1, TOP_K * tt)


def _moe_layout(idx, rank, counts, n_blocks):
    bm = EXPERT_BLOCK
    counts = counts.reshape(N_EXPERTS).astype(jnp.int32)
    padded = (counts + bm - 1) // bm * bm
    pad_end = jnp.cumsum(padded)
    pad_start = pad_end - padded
    slots = rank
    for e in range(N_EXPERTS):
        slots = slots + jnp.where(idx == e, pad_start[e], 0)
    blk_start = jnp.arange(n_blocks, dtype=jnp.int32) * bm
    blk_expert = jnp.sum((pad_end[None, :] <= blk_start[:, None]).astype(jnp.int32), axis=1)
    blk_expert = jnp.minimum(blk_expert, N_EXPERTS - 1)
    n_used = (pad_end[-1] // bm).reshape(1)
    return slots.astype(jnp.int32), blk_expert, n_used, pad_end.astype(jnp.int32)


def kernel(x, rel_bias, w_in, conv_w, conv_b, dt_bias, a_log, d_skip, ssd_norm_g, w_ssd_out, diff_lambda, diff_norm_g, w_attn_out, gate_b, w_o, ln1_g, ln1_b, w_router, b_router, w_gate, b_gate, w_up, b_up, w_down, b_down, ln2_g, ln2_b):
    batch, seq, d = x.shape
    t = batch * seq
    n_assign = t * TOP_K
    n_blocks = (n_assign + N_EXPERTS * (EXPERT_BLOCK - 1) + EXPERT_BLOCK - 1) // EXPERT_BLOCK
    cap = n_blocks * EXPERT_BLOCK
    bias_near, bias_far = _bias_tiles(rel_bias, seq)
    qk_scale = jnp.concatenate([jnp.full((DIFF_WIDTH,), DIFF_HEAD_DIM ** -0.5 * LOG2_E, F32),
                                jnp.ones((DIFF_WIDTH,), F32)])[None, :]

    h = x.reshape(t, d)
    for l in range(DEPTH):
        w_l = w_in[l]
        w_x = w_l[:, OFF_XBC:OFF_DT].astype(BF16)
        w_b = jnp.concatenate([w_l[:, :OFF_XBC], w_l[:, OFF_G:],
                               w_l[:, OFF_Q:OFF_V] * qk_scale], axis=1).astype(BF16)
        w_v = w_l[:, OFF_V:OFF_G]
        px = _matmul(h, w_x, F32, 1024, 1024)
        pb = _matmul(h, w_b, BF16, 1024, 1024)
        vt = _matmul_nt(h, w_v, ATTN_TILE)
        dt, dt_t = _dt_call(h, w_l[:, OFF_DT:OFF_Q], dt_bias[l])
        y = _ssd_call(px, conv_w[l], conv_b[l], dt, dt_t, a_log[l], d_skip[l], batch, seq)
        ao = _attn_call(pb, vt, bias_near, bias_far, diff_lambda[l], diff_norm_g[l], l, batch,
                        seq)
        h1, h1_tiles = _mix_call(y, pb, ao, h, ssd_norm_g[l], w_ssd_out[l], w_attn_out[l], gate_b[l],
                       w_o[l], ln1_g[l], ln1_b[l])
        idx, wts, rank, counts = _router_call(h1, w_router[l], b_router[l])
        slots, blk_expert, n_used, pad_end = _moe_layout(idx[:TOP_K], rank[:TOP_K], counts,
                                                         n_blocks)
        xs = _dispatch_call(h1_tiles, slots, pad_end, cap)
        ys = _expert_call(xs, blk_expert + l * N_EXPERTS, n_used, w_gate, b_gate, w_up, b_up,
                          w_down, b_down)
        h = _combine_call(ys, slots, wts, h1, ln2_g[l], ln2_b[l])
    return h.reshape(batch, seq, d)
```

```python
import functools
import math

import jax
import jax.numpy as jnp
from jax import lax
from jax.experimental import pallas as pl
from jax.experimental.pallas import tpu as pltpu

F32 = jnp.float32
BF16 = jnp.bfloat16

D_MODEL = 1024
DEPTH = 2
CHUNK = 64

SSD_D_INNER = 2048
SSD_HEAD_DIM = 64
SSD_HEADS = 32
SSD_GROUPS = 4
SSD_HEADS_PER_GROUP = 8
SSD_STATE = 128
SSD_CONV = 4
SSD_CONV_DIM = 3072
SSD_GROUP_WIDTH = SSD_HEADS_PER_GROUP * SSD_HEAD_DIM

DIFF_HEAD_DIM = 64
DIFF_HEADS = 8
DIFF_WIDTH = 1024

REL_BUCKETS = 32
REL_MAX_DIST = 128

N_EXPERTS = 32
TOP_K = 4
D_FF = 1024
SWIGLU_ALPHA = 1.702
SWIGLU_LIMIT = 7.0

DN_ALPHA = (2 * DEPTH) ** 0.25
LN_EPS = 1e-5
RMS_EPS = 1e-5

OFF_XBC = 2048
OFF_DT = 5120
OFF_Q = 5152
OFF_V = 7200
OFF_G = 8224
PB_GATES = 2048
PB_Q = 4096
PB_K = 5120

VMEM_LIMIT_BYTES = 56 * 1024 * 1024

ATTN_TILE = 256
ATTN_HEADS_PER_STEP = 4
SSD_TILE = 512
SSD_GROUPS_PER_STEP = 4
EXPERT_BLOCK = 256
DISPATCH_TILE = 1024
COMBINE_TILE = 512
NEG_BIG = -1e30
LOG2_E = math.log2(math.e)


def _params(semantics, **kwargs):
    return pltpu.CompilerParams(dimension_semantics=semantics,
                                vmem_limit_bytes=VMEM_LIMIT_BYTES, **kwargs)


def _sigmoid(x):
    return 1.0 / (1.0 + jnp.exp(-x))


def _split2(v):
    hi = v.astype(BF16)
    lo = (v - hi.astype(F32)).astype(BF16)
    return hi, lo


def _dot3(a, b, dims):
    ah, al = _split2(a)
    bh, bl = _split2(b)
    dot = lambda p, q: lax.dot_general(p, q, dims, preferred_element_type=F32)
    return dot(ah, bh) + dot(ah, bl) + dot(al, bh)


def _split3(v):
    hi = v.astype(BF16)
    r = v - hi.astype(F32)
    mid = r.astype(BF16)
    lo = (r - mid.astype(F32)).astype(BF16)
    return hi, mid, lo


def _mm_kernel(x_ref, w_ref, o_ref, xb_ref):
    @pl.when(pl.program_id(1) == 0)
    def _():
        xb_ref[...] = x_ref[...].astype(BF16)

    o_ref[...] = jnp.dot(xb_ref[...], w_ref[...],
                         preferred_element_type=F32).astype(o_ref.dtype)


def _matmul(x, w, out_dtype, bm, bn):
    m, k = x.shape
    n = w.shape[1]
    return pl.pallas_call(
        _mm_kernel,
        grid=(m // bm, n // bn),
        in_specs=[pl.BlockSpec((bm, k), lambda i, j: (i, 0)),
                  pl.BlockSpec((k, bn), lambda i, j: (0, j))],
        out_specs=pl.BlockSpec((bm, bn), lambda i, j: (i, j)),
        out_shape=jax.ShapeDtypeStruct((m, n), out_dtype),
        scratch_shapes=[pltpu.VMEM((bm, k), BF16)],
        compiler_params=_params(("parallel", "arbitrary")),
        name="in_proj",
    )(x, w)


def _mm_nt_kernel(x_ref, w_ref, o_ref, wt_ref):
    @pl.when(pl.program_id(0) == 0)
    def _():
        wt_ref[...] = w_ref[...].T.astype(BF16)

    res = lax.dot_general(wt_ref[...], x_ref[...].astype(BF16), (((1,), (1,)), ((), ())),
                          preferred_element_type=F32)
    tile = o_ref.shape[2]
    for c in range(o_ref.shape[0]):
        o_ref[c] = res[:, c * tile:(c + 1) * tile].astype(o_ref.dtype)


def _matmul_nt(x, w, tile, bm=1024):
    m, k = x.shape
    n = w.shape[1]
    return pl.pallas_call(
        _mm_nt_kernel,
        grid=(m // bm,),
        in_specs=[pl.BlockSpec((bm, k), lambda i: (i, 0)),
                  pl.BlockSpec((k, n), lambda i: (0, 0))],
        out_specs=pl.BlockSpec((bm // tile, n, tile), lambda i: (i, 0, 0)),
        out_shape=jax.ShapeDtypeStruct((m // tile, n, tile), BF16),
        scratch_shapes=[pltpu.VMEM((n, k), BF16)],
        compiler_params=_params(("arbitrary",)),
        name="v_proj_t",
    )(x, w)


def _softplus(x):
    return jnp.maximum(x, 0.0) + jnp.log(1.0 + jnp.exp(-jnp.abs(x)))


def _dt_kernel(x_ref, w_ref, b_ref, dt_ref, dtt_ref):
    x = x_ref[...]
    tm = x.shape[0]
    raw = _dot3(x, w_ref[...], (((1,), (0,)), ((), ())))
    dt = _softplus(raw + b_ref[...])
    dt_t = dt.T
    r = SSD_HEADS_PER_GROUP
    for g in range(SSD_GROUPS):
        dt_ref[g] = dt[:, g * r:(g + 1) * r]
        for j in range(tm // CHUNK):
            dtt_ref[g, j] = dt_t[g * r:(g + 1) * r, j * CHUNK:(j + 1) * CHUNK]


def _dt_call(h, w_dt, dt_bias, tm=512):
    t = h.shape[0]
    r = SSD_HEADS_PER_GROUP
    lanes = 128
    w_pad = jnp.pad(w_dt, ((0, 0), (0, lanes - SSD_HEADS)))
    b_pad = jnp.pad(dt_bias, (0, lanes - SSD_HEADS)).reshape(1, lanes)
    return pl.pallas_call(
        _dt_kernel,
        grid=(t // tm,),
        in_specs=[pl.BlockSpec((tm, D_MODEL), lambda i: (i, 0)),
                  pl.BlockSpec((D_MODEL, lanes), lambda i: (0, 0)),
                  pl.BlockSpec((1, lanes), lambda i: (0, 0))],
        out_specs=[pl.BlockSpec((SSD_GROUPS, tm, r), lambda i: (0, i, 0)),
                   pl.BlockSpec((SSD_GROUPS, tm // CHUNK, r, CHUNK), lambda i: (0, i, 0, 0))],
        out_shape=[jax.ShapeDtypeStruct((SSD_GROUPS, t, r), F32),
                   jax.ShapeDtypeStruct((SSD_GROUPS, t // CHUNK, r, CHUNK), F32)],
        compiler_params=_params(("parallel",)),
        name="dt_proj",
    )(h, w_pad, b_pad)


def _ssd_kernel(x_ref, b_ref, c_ref, wx_ref, wb_ref, wc_ref, bx_ref, bb_ref, bc_ref,
                dt_ref, dtt_ref, alr_ref, alc_ref, dsk_ref, bd_ref, trit_ref, e_ref,
                y_ref,
                state_ref, ux_ref, ub_ref, uc_ref, xc_ref, bcv_ref, ccv_ref, xdt_ref,
                wst_ref, eacs_ref, acs_ref, acst_ref):
    s_idx = pl.program_id(2)
    ts = x_ref.shape[0]
    nc = ts // CHUNK
    r = SSD_HEADS_PER_GROUP
    gw = SSD_GROUP_WIDTH
    n = SSD_STATE
    ng = dt_ref.shape[0]

    @pl.when(s_idx == 0)
    def _():
        state_ref[...] = jnp.zeros_like(state_ref)
        ux_ref[0:8, :] = jnp.zeros((8, ux_ref.shape[1]), F32)
        ub_ref[0:8, :] = jnp.zeros((8, ub_ref.shape[1]), F32)
        uc_ref[0:8, :] = jnp.zeros((8, uc_ref.shape[1]), F32)

    def conv_silu(raw_ref, u_ref, w_ref, bias_ref):
        u_ref[8:8 + ts, :] = raw_ref[...]
        u = u_ref[...]
        u2 = pltpu.roll(u, 2, axis=0)
        even = w_ref[3:4, :] * u + w_ref[1:2, :] * u2
        odd = w_ref[2:3, :] * u + w_ref[0:1, :] * u2
        acc = (even + pltpu.roll(odd, 1, axis=0))[8:8 + ts] + bias_ref[...]
        u_ref[0:8, :] = u_ref[ts:ts + 8, :]
        return acc * _sigmoid(acc)

    xc_ref[...] = conv_silu(x_ref, ux_ref, wx_ref, bx_ref)
    bcv_ref[...] = conv_silu(b_ref, ub_ref, wb_ref, bb_ref).astype(BF16)
    ccv_ref[...] = conv_silu(c_ref, uc_ref, wc_ref, bc_ref).astype(BF16)

    bd = bd_ref[...]
    trit = trit_ref[...]
    e_mat = e_ref[...]

    def expand(v):
        return jnp.dot(jnp.concatenate(_split2(v), axis=1), e_mat, preferred_element_type=F32)

    for g in range(ng):
        gl = slice(g * gw, (g + 1) * gw)
        a_row = -jnp.exp(alr_ref[g])
        a_col = -jnp.exp(alc_ref[g])
        dt = dt_ref[g]
        a = dt * a_row
        acs = jnp.zeros((ts, r), F32)
        for part in _split3(a):
            acs = acs + jnp.dot(bd, part, preferred_element_type=F32)
        acs_ref[g] = acs
        a_t = dtt_ref[g].reshape(nc * r, CHUNK) * jnp.concatenate([a_col] * nc, axis=0)
        acs_t = jnp.zeros((nc * r, CHUNK), F32)
        for part in _split3(a_t):
            acs_t = acs_t + jnp.dot(part, trit, preferred_element_type=F32)
        acst_ref[g] = acs_t
        a_last = jnp.concatenate(
            [jnp.broadcast_to(acs[c * CHUNK + CHUNK - 1:c * CHUNK + CHUNK, :], (CHUNK, r))
             for c in range(nc)], axis=0)
        xdt = xc_ref[:, gl] * expand(dt)
        xdt_ref[:, gl] = xdt.astype(BF16)
        wst_ref[:, gl] = (xdt * expand(jnp.exp(a_last - acs))).astype(BF16)
        eacs_ref[:, gl] = expand(jnp.exp(acs))

    row_i = lax.broadcasted_iota(jnp.int32, (CHUNK, CHUNK), 0)
    col_i = lax.broadcasted_iota(jnp.int32, (CHUNK, CHUNK), 1)
    tril = col_i <= row_i
    left_half = lax.broadcasted_iota(jnp.int32, (CHUNK, 128), 1) < SSD_HEAD_DIM

    for c in range(nc):
        rows = slice(c * CHUNK, (c + 1) * CHUNK)
        for g in range(ng):
            cc = ccv_ref[rows, g * n:(g + 1) * n]
            bc = bcv_ref[rows, g * n:(g + 1) * n]
            cb = lax.dot_general(cc, bc, (((1,), (1,)), ((), ())), preferred_element_type=F32)
            acs_c = acs_ref[g, rows, :]
            acs_tc = acst_ref[g, c * r:(c + 1) * r, :]
            for p in range(r // 2):
                lo = g * gw + p * 128
                xp = xdt_ref[rows, lo:lo + 128]
                halves = []
                for hh in (2 * p, 2 * p + 1):
                    diff = acs_c[:, hh:hh + 1] - acs_tc[hh:hh + 1, :]
                    decay = jnp.exp(jnp.where(tril, diff, -jnp.inf))
                    halves.append(jnp.dot((cb * decay).astype(BF16), xp,
                                          preferred_element_type=F32))
                y_diag = jnp.where(left_half, halves[0], halves[1])
                y_ref[rows, lo:lo + 128] = (y_diag
                                            + dsk_ref[:, lo:lo + 128] * xc_ref[rows, lo:lo + 128])

    def chunk_body(c, carry):
        rows = pl.ds(pl.multiple_of(c * CHUNK, CHUNK), CHUNK)
        for g in range(ng):
            gl = slice(g * gw, (g + 1) * gw)
            cc = ccv_ref[rows, g * n:(g + 1) * n]
            bc = bcv_ref[rows, g * n:(g + 1) * n]
            state = state_ref[g]
            eacs = eacs_ref[rows, gl]
            y_off = jnp.dot(cc, state.astype(BF16), preferred_element_type=F32) * eacs
            y_ref[rows, gl] = y_ref[rows, gl] + y_off
            upd = lax.dot_general(bc, wst_ref[rows, gl], (((0,), (0,)), ((), ())),
                                  preferred_element_type=F32)
            state_ref[g] = state * eacs[CHUNK - 1:CHUNK, :] + upd
        return carry

    lax.fori_loop(0, nc, chunk_body, 0, unroll=True)


def _ssd_call(px, conv_w, conv_b, dt, dt_t, a_log, d_skip, batch, seq):
    t = batch * seq
    ts = SSD_TILE
    ns = seq // ts
    nc = ts // CHUNK
    r = SSD_HEADS_PER_GROUP
    ng = SSD_GROUPS_PER_STEP
    gw = ng * SSD_GROUP_WIDTH
    n = ng * SSD_STATE
    wb0 = SSD_D_INNER // n
    wc0 = wb0 + SSD_GROUPS // ng

    li = jnp.arange(ts)
    bd = ((li[None, :] <= li[:, None]) & (li[None, :] // CHUNK == li[:, None] // CHUNK)).astype(BF16)
    lc = jnp.arange(CHUNK)
    trit = (lc[:, None] <= lc[None, :]).astype(BF16)
    e_mat = (jnp.arange(SSD_GROUP_WIDTH)[None, :] // SSD_HEAD_DIM
             == jnp.arange(2 * r)[:, None] % r).astype(BF16)
    conv_b2 = conv_b.reshape(1, SSD_CONV_DIM)
    alr = a_log.reshape(SSD_GROUPS, 1, r)
    alc = a_log.reshape(SSD_GROUPS, r, 1)
    dsk = jnp.repeat(d_skip, SSD_HEAD_DIM).reshape(1, SSD_D_INNER)

    row = lambda b, g, s: b * ns + s
    in_specs = [
        pl.BlockSpec((ts, gw), lambda b, g, s: (row(b, g, s), g)),
        pl.BlockSpec((ts, n), lambda b, g, s: (row(b, g, s), wb0 + g)),
        pl.BlockSpec((ts, n), lambda b, g, s: (row(b, g, s), wc0 + g)),
        pl.BlockSpec((SSD_CONV, gw), lambda b, g, s: (0, g)),
        pl.BlockSpec((SSD_CONV, n), lambda b, g, s: (0, wb0 + g)),
        pl.BlockSpec((SSD_CONV, n), lambda b, g, s: (0, wc0 + g)),
        pl.BlockSpec((1, gw), lambda b, g, s: (0, g)),
        pl.BlockSpec((1, n), lambda b, g, s: (0, wb0 + g)),
        pl.BlockSpec((1, n), lambda b, g, s: (0, wc0 + g)),
        pl.BlockSpec((ng, ts, r), lambda b, g, s: (g, row(b, g, s), 0)),
        pl.BlockSpec((ng, nc, r, CHUNK), lambda b, g, s: (g, row(b, g, s), 0, 0)),
        pl.BlockSpec((ng, 1, r), lambda b, g, s: (g, 0, 0)),
        pl.BlockSpec((ng, r, 1), lambda b, g, s: (g, 0, 0)),
        pl.BlockSpec((1, gw), lambda b, g, s: (0, g)),
        pl.BlockSpec((ts, ts), lambda b, g, s: (0, 0)),
        pl.BlockSpec((CHUNK, CHUNK), lambda b, g, s: (0, 0)),
        pl.BlockSpec((2 * r, SSD_GROUP_WIDTH), lambda b, g, s: (0, 0)),
    ]
    scratch = [
        pltpu.VMEM((ng, SSD_STATE, SSD_GROUP_WIDTH), F32),
        pltpu.VMEM((ts + 8, gw), F32),
        pltpu.VMEM((ts + 8, n), F32),
        pltpu.VMEM((ts + 8, n), F32),
        pltpu.VMEM((ts, gw), F32),
        pltpu.VMEM((ts, n), BF16),
        pltpu.VMEM((ts, n), BF16),
        pltpu.VMEM((ts, gw), BF16),
        pltpu.VMEM((ts, gw), BF16),
        pltpu.VMEM((ts, gw), F32),
        pltpu.VMEM((ng, ts, r), F32),
        pltpu.VMEM((ng, nc * r, CHUNK), F32),
    ]
    return pl.pallas_call(
        _ssd_kernel,
        grid=(batch, SSD_GROUPS // ng, ns),
        in_specs=in_specs,
        out_specs=pl.BlockSpec((ts, gw), lambda b, g, s: (row(b, g, s), g)),
        out_shape=jax.ShapeDtypeStruct((t, SSD_D_INNER), F32),
        scratch_shapes=scratch,
        compiler_params=_params(("parallel", "parallel", "arbitrary")),
        name="ssd_scan",
    )(px, px, px, conv_w, conv_w, conv_w, conv_b2, conv_b2, conv_b2,
      dt, dt_t, alr, alc, dsk, bd, trit, e_mat)


def _attn_kernel(q_ref, k_ref, vt_ref, bias_ref, far_ref, lam_ref, g_ref, o_ref, sa_ref, sb_ref,
                 *, lam_init):
    i = pl.program_id(2)
    tq = q_ref.shape[0]
    dh = DIFF_HEAD_DIM
    lv = lam_ref[...]
    lam = (jnp.exp(jnp.sum(lv[0:1] * lv[1:2], axis=1, keepdims=True))
           - jnp.exp(jnp.sum(lv[2:3] * lv[3:4], axis=1, keepdims=True)) + lam_init)

    q = q_ref[...]
    hp = q.shape[1] // (2 * dh)
    ns = 2 * hp
    qs = [q[:, s * dh:(s + 1) * dh] for s in range(ns)]
    nt = (((1,), (1,)), ((), ()))

    n_tiles = bias_ref.shape[1] - 1
    h0 = pl.program_id(1) * hp
    ones = jnp.ones((16, tq), BF16)
    dv = 2 * dh

    def near_tile(t):
        valid = t <= i
        return jnp.where(valid, i - t, 0), jnp.where(valid, t, n_tiles)

    def far_tile(t):
        valid = t <= i
        j = jnp.where(valid, t - 2, 0)
        d = jnp.where(valid, i - j, 0)
        return j, [jnp.where(valid, far_ref[h0 + hh, d], NEG_BIG) for hh in range(hp)]

    def scores(j, s_ref, biases):
        kj = k_ref[pl.ds(pl.multiple_of(j * tq, tq), tq), :]
        for s in range(ns):
            sc = lax.dot_general(kj[:, s * dh:(s + 1) * dh], qs[s], nt,
                                 preferred_element_type=F32)
            s_ref[s] = sc if biases is None else sc + biases[s // 2]

    def update(j, s_ref, shifts, carry):
        vtj = vt_ref[j]
        out = []
        for s in range(ns):
            hh = s // 2
            vth = jnp.concatenate([vtj[hh * dv:(hh + 1) * dv], ones], axis=0)
            mx, acc = carry[2 * s:2 * s + 2]
            sc = s_ref[s]
            mx_new = jnp.maximum(mx, jnp.max(sc, axis=0, keepdims=True) + shifts[hh])
            p = jnp.exp2(sc - (mx_new - shifts[hh]))
            out += [mx_new, jnp.exp2(mx - mx_new) * acc
                    + jnp.dot(vth, p.astype(BF16), preferred_element_type=F32)]
        return tuple(out)

    def scores_near(t, s_ref):
        j, d = near_tile(t)
        scores(j, s_ref, [bias_ref[hh, d] for hh in range(hp)])

    def scores_far(t, s_ref):
        scores(far_tile(t)[0], s_ref, None)

    def update_near(t, s_ref, carry):
        return update(near_tile(t)[0], s_ref, [0.0] * hp, carry)

    def update_far(t, s_ref, carry):
        j, shifts = far_tile(t)
        return update(j, s_ref, shifts, carry)

    def body(u, carry):
        scores_far(2 * u + 1, sb_ref)
        carry = update_far(2 * u, sa_ref, carry)
        scores_far(2 * u + 2, sa_ref)
        return update_far(2 * u + 1, sb_ref, carry)

    carry = []
    for _ in range(ns):
        carry += [jnp.full((1, tq), NEG_BIG, F32), jnp.zeros((dv + 16, tq), F32)]
    scores_near(0, sa_ref)
    scores_near(1, sb_ref)
    carry = update_near(0, sa_ref, tuple(carry))
    scores_far(2, sa_ref)
    carry = update_near(1, sb_ref, carry)
    carry = lax.fori_loop(1, (i + 2) // 2, body, carry)
    for hh in range(hp):
        acc1, acc2 = carry[4 * hh + 1], carry[4 * hh + 3]
        o = acc1[:dv] / acc1[dv:dv + 1] - lam * (acc2[:dv] / acc2[dv:dv + 1])
        ms = jnp.mean(o * o, axis=0, keepdims=True)
        o = o * lax.rsqrt(ms + RMS_EPS) * g_ref[...] * (1.0 - lam_init)
        o_ref[:, hh * dv:(hh + 1) * dv] = o.T.astype(o_ref.dtype)


def _rel_bucket(rel):
    half = REL_BUCKETS // 2
    max_exact = half // 2
    ret = jnp.where(rel > 0, half, 0)
    n = jnp.abs(rel)
    nf = jnp.maximum(n, 1).astype(F32)
    large = max_exact + (jnp.log(nf / max_exact) / math.log(REL_MAX_DIST / max_exact)
                         * (half - max_exact)).astype(jnp.int32)
    large = jnp.minimum(large, half - 1)
    return ret + jnp.where(n < max_exact, n, large)


def _bias_lookup(rel_bias, rel):
    bucket = _rel_bucket(rel)[None]
    table = rel_bias.astype(F32)
    lead = (slice(None),) + (None,) * rel.ndim
    bias = jnp.zeros((DIFF_HEADS,) + rel.shape, F32)
    for b in range(REL_BUCKETS):
        bias = jnp.where(bucket == b, table[b][lead], bias)
    return bias * LOG2_E


def _bias_tiles(rel_bias, seq):
    tq = ATTN_TILE
    kk = jnp.arange(tq)[None, :, None]
    qq = jnp.arange(tq)[None, None, :]
    d = jnp.arange(2)[:, None, None]
    near = _bias_lookup(rel_bias, kk - qq - d * tq)
    allowed = (d > 0) | ((kk // CHUNK) <= (qq // CHUNK))
    near = jnp.where(allowed[None], near, NEG_BIG)
    masked = jnp.full((DIFF_HEADS, 1, tq, tq), NEG_BIG, F32)
    far = _bias_lookup(rel_bias, -tq * jnp.arange(seq // tq + 1))
    return jnp.concatenate([near, masked], axis=1), far


def _attn_call(pb, vt, bias_near, bias_far, lam_vecs, norm_g, layer_idx, batch, seq):
    t = batch * seq
    tq = ATTN_TILE
    nq = seq // tq
    hp = ATTN_HEADS_PER_STEP
    w = 2 * DIFF_HEAD_DIM
    wb = hp * w
    qcol0 = PB_Q // wb
    kcol0 = PB_K // wb
    lam_init = 0.8 - 0.6 * math.exp(-0.3 * layer_idx)
    return pl.pallas_call(
        functools.partial(_attn_kernel, lam_init=lam_init),
        grid=(batch, DIFF_HEADS // hp, nq),
        in_specs=[pl.BlockSpec((tq, wb), lambda b, h, i: (b * nq + i, qcol0 + h)),
                  pl.BlockSpec((seq, wb), lambda b, h, i: (b, kcol0 + h)),
                  pl.BlockSpec((nq, wb, tq), lambda b, h, i: (b, h, 0)),
                  pl.BlockSpec((hp, 3, tq, tq), lambda b, h, i: (h, 0, 0, 0)),
                  pl.BlockSpec(memory_space=pltpu.SMEM),
                  pl.BlockSpec((4, DIFF_HEAD_DIM), lambda b, h, i: (0, 0)),
                  pl.BlockSpec((w, 1), lambda b, h, i: (0, 0))],
        out_specs=pl.BlockSpec((tq, wb), lambda b, h, i: (b * nq + i, h)),
        out_shape=jax.ShapeDtypeStruct((t, DIFF_WIDTH), BF16),
        scratch_shapes=[pltpu.VMEM((2 * hp, tq, tq), F32), pltpu.VMEM((2 * hp, tq, tq), F32)],
        compiler_params=_params(("parallel", "parallel", "arbitrary")),
        name="diff_attn",
    )(pb, pb, vt, bias_near, bias_far, lam_vecs, norm_g.reshape(w, 1))


def _layer_norm(x, g, b):
    mu = jnp.mean(x, axis=1, keepdims=True)
    xc = x - mu
    var = jnp.mean(xc * xc, axis=1, keepdims=True)
    return xc * lax.rsqrt(var + LN_EPS) * g + b


def _mix_kernel(y_ref, z_ref, ao_ref, g0_ref, g1_ref, h_ref, ng_ref, wso_ref, wao_ref,
                gb_ref, wo_ref, lg_ref, lb_ref, o_ref, ot_ref):
    z = z_ref[...].astype(F32)
    yg = y_ref[...] * (z * _sigmoid(z))
    ms = jnp.mean(yg * yg, axis=1, keepdims=True)
    yn = (yg * lax.rsqrt(ms + RMS_EPS) * ng_ref[...]).astype(BF16)
    y_ssd = jnp.dot(yn, wso_ref[...], preferred_element_type=F32)
    y_att = jnp.dot(ao_ref[...], wao_ref[...], preferred_element_type=F32)
    gb = gb_ref[...]
    gate0 = _sigmoid(g0_ref[...].astype(F32) + gb[:, :D_MODEL])
    gate1 = _sigmoid(g1_ref[...].astype(F32) + gb[:, D_MODEL:])
    mixed = (gate0 * y_ssd + gate1 * y_att).astype(BF16)
    mix = jnp.dot(mixed, wo_ref[...], preferred_element_type=F32)
    out = _layer_norm(DN_ALPHA * h_ref[...] + mix, lg_ref[...], lb_ref[...])
    o_ref[...] = out
    _store_row_tiles(ot_ref, out)


ROW_TILE = (8, 128)


def _store_row_tiles(ref, rows):
    sub, lanes = ROW_TILE
    m = rows.shape[0]
    for s in range(sub):
        ref[pl.ds(s, m, stride=sub), :] = rows[:, s * lanes:(s + 1) * lanes]


def _load_row_tiles(ref):
    sub, lanes = ROW_TILE
    m = ref.shape[0] // sub
    return jnp.concatenate([ref[pl.ds(s, m, stride=sub), :] for s in range(sub)], axis=1)


def _mix_call(y, pb, ao, h, norm_g, w_ssd_out, w_attn_out, gate_b, w_o, ln_g, ln_b, tm=512):
    t = h.shape[0]
    d = D_MODEL
    gcol0 = PB_GATES // d
    const = lambda i: (0, 0)
    return pl.pallas_call(
        _mix_kernel,
        grid=(t // tm,),
        in_specs=[pl.BlockSpec((tm, SSD_D_INNER), lambda i: (i, 0)),
                  pl.BlockSpec((tm, SSD_D_INNER), lambda i: (i, 0)),
                  pl.BlockSpec((tm, DIFF_WIDTH), lambda i: (i, 0)),
                  pl.BlockSpec((tm, d), lambda i: (i, gcol0)),
                  pl.BlockSpec((tm, d), lambda i: (i, gcol0 + 1)),
                  pl.BlockSpec((tm, d), lambda i: (i, 0)),
                  pl.BlockSpec((1, SSD_D_INNER), const),
                  pl.BlockSpec((SSD_D_INNER, d), const),
                  pl.BlockSpec((DIFF_WIDTH, d), const),
                  pl.BlockSpec((1, 2 * d), const),
                  pl.BlockSpec((d, d), const),
                  pl.BlockSpec((1, d), const),
                  pl.BlockSpec((1, d), const)],
        out_specs=[pl.BlockSpec((tm, d), lambda i: (i, 0)),
                   pl.BlockSpec((tm * ROW_TILE[0], ROW_TILE[1]), lambda i: (i, 0))],
        out_shape=[jax.ShapeDtypeStruct((t, d), F32),
                   jax.ShapeDtypeStruct((t * ROW_TILE[0], ROW_TILE[1]), F32)],
        compiler_params=_params(("parallel",)),
        name="mix_ln",
    )(y, pb, ao, pb, pb, h, norm_g.reshape(1, -1), w_ssd_out.astype(BF16),
      w_attn_out.astype(BF16), gate_b.reshape(1, -1), w_o.astype(BF16),
      ln_g.reshape(1, -1), ln_b.reshape(1, -1))


def _router_kernel(h_ref, w_ref, b_ref, tri_ref, idx_ref, wt_ref, rank_ref, cnt_ref, run_ref):
    @pl.when(pl.program_id(0) == 0)
    def _():
        run_ref[...] = jnp.zeros_like(run_ref)

    tm = h_ref.shape[0]
    ne = N_EXPERTS
    logits = _dot3(h_ref[...], w_ref[...], (((1,), (0,)), ((), ()))) + b_ref[...]
    lane = lax.broadcasted_iota(jnp.int32, (tm, ne), 1).astype(F32)
    work = logits
    sel, vals = [], []
    for _ in range(TOP_K):
        mx = jnp.max(work, axis=1, keepdims=True)
        first = jnp.min(jnp.where(work == mx, lane, float(ne)), axis=1, keepdims=True)
        hit = lane == first
        sel.append((first, hit))
        vals.append(mx)
        work = jnp.where(hit, -jnp.inf, work)
    exps = [jnp.exp(v - vals[0]) for v in vals]
    denom = exps[0] + exps[1] + exps[2] + exps[3]

    onehot = jnp.zeros((tm, ne), F32)
    for _, hit in sel:
        onehot = onehot + hit.astype(F32)
    before = jnp.dot(tri_ref[...], onehot.astype(BF16), preferred_element_type=F32)
    before = before + run_ref[...]

    out_lane = lax.broadcasted_iota(jnp.int32, (tm, 128), 1)
    idx_out = jnp.zeros((tm, 128), F32)
    wt_out = jnp.zeros((tm, 128), F32)
    rank_out = jnp.zeros((tm, 128), F32)
    for k, (first, hit) in enumerate(sel):
        rank = jnp.sum(jnp.where(hit, before, 0.0), axis=1, keepdims=True)
        idx_out = jnp.where(out_lane == k, first, idx_out)
        wt_out = jnp.where(out_lane == k, exps[k] / denom, wt_out)
        rank_out = jnp.where(out_lane == k, rank, rank_out)
    idx_ref[...] = idx_out.T[0:8].astype(jnp.int32)
    wt_ref[...] = wt_out
    rank_ref[...] = rank_out.T[0:8].astype(jnp.int32)
    total = run_ref[...] + jnp.sum(onehot, axis=0, keepdims=True)
    run_ref[...] = total
    cnt_ref[...] = total


def _router_call(h, w_router, b_router, tm=512):
    t = h.shape[0]
    li = jnp.arange(tm)
    tri = (li[None, :] < li[:, None]).astype(BF16)
    const = lambda i: (0, 0)
    return pl.pallas_call(
        _router_kernel,
        grid=(t // tm,),
        in_specs=[pl.BlockSpec((tm, D_MODEL), lambda i: (i, 0)),
                  pl.BlockSpec((D_MODEL, N_EXPERTS), const),
                  pl.BlockSpec((1, N_EXPERTS), const),
                  pl.BlockSpec((tm, tm), const)],
        out_specs=[pl.BlockSpec((8, tm), lambda i: (0, i)),
                   pl.BlockSpec((tm, 128), lambda i: (i, 0)),
                   pl.BlockSpec((8, tm), lambda i: (0, i)),
                   pl.BlockSpec((1, N_EXPERTS), const)],
        out_shape=[jax.ShapeDtypeStruct((8, t), jnp.int32),
                   jax.ShapeDtypeStruct((t, 128), F32),
                   jax.ShapeDtypeStruct((8, t), jnp.int32),
                   jax.ShapeDtypeStruct((1, N_EXPERTS), F32)],
        scratch_shapes=[pltpu.VMEM((1, N_EXPERTS), F32)],
        compiler_params=_params(("arbitrary",)),
        name="router",
    )(h, w_router, b_router.reshape(1, N_EXPERTS), tri)


def _dispatch_kernel(pad_end_ref, slot_ref, x_ref, xs_out, zero_ref, sem, zero_sem):
    sub = ROW_TILE[0]
    tt = x_ref.shape[0] // sub
    n = tt * TOP_K
    zb = zero_ref.shape[0] // sub

    def row(ref, r, count=1):
        return ref.at[pl.ds(pl.multiple_of(r * sub, sub), count * sub)]

    @pl.when(pl.program_id(0) == 0)
    def _():
        zero_ref[...] = jnp.zeros_like(zero_ref)

        def clear(first):
            return pltpu.make_async_copy(zero_ref, row(xs_out, first, zb), zero_sem)

        for e in range(N_EXPERTS):
            clear(jnp.maximum(pad_end_ref[e] - zb, 0)).start()
        used = pad_end_ref[N_EXPERTS - 1] // zb
        total = xs_out.shape[0] // (zb * sub)

        def clear_tail(b, c):
            clear(b * zb).start()
            return c

        def wait_one(b, c):
            clear(0).wait()
            return c

        lax.fori_loop(used, total, clear_tail, 0)
        lax.fori_loop(used - N_EXPERTS, total, wait_one, 0)

    def start(tok, c):
        for k in range(TOP_K):
            slot = slot_ref[0, 0, k * tt + tok]
            pltpu.make_async_copy(row(x_ref, tok), row(xs_out, slot), sem).start(priority=k % 2)
        return c

    lax.fori_loop(0, tt, start, 0, unroll=8)
    pltpu.make_async_copy(row(xs_out, 0, n), row(xs_out, 0, n), sem).wait()


def _dispatch_call(xt, slots, pad_end, cap):
    sub, lanes = ROW_TILE
    t = xt.shape[0] // sub
    tt = DISPATCH_TILE
    nt = t // tt
    n = tt * TOP_K
    assert cap >= N_EXPERTS * EXPERT_BLOCK
    grid_spec = pltpu.PrefetchScalarGridSpec(
        num_scalar_prefetch=1,
        grid=(nt,),
        in_specs=[pl.BlockSpec((1, 1, n), lambda i, pe: (i, 0, 0), memory_space=pltpu.SMEM),
                  pl.BlockSpec((tt * sub, lanes), lambda i, pe: (i, 0))],
        out_specs=pl.BlockSpec(memory_space=pl.ANY),
        scratch_shapes=[pltpu.VMEM((EXPERT_BLOCK * sub, lanes), xt.dtype),
                        pltpu.SemaphoreType.DMA, pltpu.SemaphoreType.DMA],
    )
    return pl.pallas_call(
        _dispatch_kernel,
        grid_spec=grid_spec,
        out_shape=jax.ShapeDtypeStruct((cap * sub, lanes), xt.dtype),
        compiler_params=_params(("arbitrary",), disable_bounds_checks=True),
        name="moe_dispatch",
    )(pad_end, _tile_slots(slots, tt), xt)


def _expert_kernel(be_ref, nb_ref, x_ref, wg_ref, bg_ref, wu_ref, bu_ref, wd_ref, bd_ref,
                   o_ref, wgb_ref, wub_ref, wdb_ref):
    i = pl.program_id(0)
    prev = be_ref[jnp.maximum(i - 1, 0)]
    changed = jnp.logical_or(i == 0, be_ref[i] != prev)

    @pl.when(changed)
    def _():
        wgb_ref[...] = wg_ref[0].astype(BF16)
        wub_ref[...] = wu_ref[0].astype(BF16)
        wdb_ref[...] = wd_ref[0].astype(BF16)

    @pl.when(i < nb_ref[0])
    def _():
        xb = _load_row_tiles(x_ref).astype(BF16)
        g = jnp.dot(xb, wgb_ref[...], preferred_element_type=F32) + bg_ref[0]
        u = jnp.dot(xb, wub_ref[...], preferred_element_type=F32) + bu_ref[0]
        g = jnp.minimum(g, SWIGLU_LIMIT)
        u = jnp.clip(u, -SWIGLU_LIMIT, SWIGLU_LIMIT)
        act = g * _sigmoid(SWIGLU_ALPHA * g) * (u + 1.0)
        _store_row_tiles(o_ref, jnp.dot(act.astype(BF16), wdb_ref[...],
                                        preferred_element_type=F32) + bd_ref[0])

    @pl.when(i >= nb_ref[0])
    def _():
        o_ref[...] = jnp.zeros_like(o_ref)


def _expert_call(xs, blk_expert, n_used, w_gate, b_gate, w_up, b_up, w_down, b_down):
    sub, lanes = ROW_TILE
    cap = xs.shape[0] // sub
    d = sub * lanes
    bm = EXPERT_BLOCK
    nb = cap // bm
    wspec = lambda shape: pl.BlockSpec(shape, lambda i, be, nu: (be[i], 0, 0))
    grid_spec = pltpu.PrefetchScalarGridSpec(
        num_scalar_prefetch=2,
        grid=(nb,),
        in_specs=[pl.BlockSpec((bm * sub, lanes),
                               lambda i, be, nu: (jnp.minimum(i, nu[0] - 1), 0)),
                  wspec((1, d, D_FF)), wspec((1, 1, D_FF)),
                  wspec((1, d, D_FF)), wspec((1, 1, D_FF)),
                  wspec((1, D_FF, d)), wspec((1, 1, d))],
        out_specs=pl.BlockSpec((bm * sub, lanes), lambda i, be, nu: (i, 0)),
        scratch_shapes=[pltpu.VMEM((d, D_FF), BF16), pltpu.VMEM((d, D_FF), BF16),
                        pltpu.VMEM((D_FF, d), BF16)],
    )
    ne = w_gate.shape[0] * w_gate.shape[1]
    return pl.pallas_call(
        _expert_kernel,
        grid_spec=grid_spec,
        out_shape=jax.ShapeDtypeStruct((cap * sub, lanes), F32),
        compiler_params=_params(("arbitrary",)),
        name="moe_experts",
    )(blk_expert, n_used, xs, w_gate.reshape(ne, d, D_FF), b_gate.reshape(ne, 1, D_FF),
      w_up.reshape(ne, d, D_FF), b_up.reshape(ne, 1, D_FF), w_down.reshape(ne, D_FF, d),
      b_down.reshape(ne, 1, d))


def _combine_kernel(slot_ref, next_slot_ref, ys_hbm, wt_ref, h_ref, lg_ref, lb_ref, o_ref,
                    buf_ref, sems):
    i = pl.program_id(0)
    tt = h_ref.shape[0]
    n = tt * TOP_K
    sub = ROW_TILE[0]

    def issue(s_ref, buf):
        def start(r, c):
            for j in range(8):
                a = r * 8 + j
                src = pl.ds(pl.multiple_of(s_ref[0, 0, a] * sub, sub), sub)
                dst = pl.ds(pl.multiple_of(a * sub, sub), sub)
                pltpu.make_async_copy(ys_hbm.at[src], buf_ref.at[buf, dst],
                                      sems.at[buf]).start(priority=j % 2)
            return c

        lax.fori_loop(0, n // 8, start, 0)

    @pl.when(i == 0)
    def _():
        issue(slot_ref, 0)

    @pl.when(i + 1 < pl.num_programs(0))
    def _():
        issue(next_slot_ref, (i + 1) % 2)

    cur = i % 2
    pltpu.make_async_copy(buf_ref.at[cur], buf_ref.at[cur], sems.at[cur]).wait()

    wt = wt_ref[...]
    rows = lambda k: _load_row_tiles(buf_ref.at[cur, pl.ds(k * tt * sub, tt * sub)])
    ff = wt[:, 0:1] * rows(0)
    for k in range(1, TOP_K):
        ff = ff + wt[:, k:k + 1] * rows(k)
    o_ref[...] = _layer_norm(DN_ALPHA * h_ref[...] + ff, lg_ref[...], lb_ref[...])


def _combine_call(ys, slots, wts, h, ln_g, ln_b):
    t, d = h.shape
    tt = COMBINE_TILE
    nt = t // tt
    n = tt * TOP_K
    slots_km = _tile_slots(slots, tt)
    const = lambda i: (0, 0)
    return pl.pallas_call(
        _combine_kernel,
        grid=(nt,),
        in_specs=[pl.BlockSpec((1, 1, n), lambda i: (i, 0, 0), memory_space=pltpu.SMEM),
                  pl.BlockSpec((1, 1, n), lambda i: (jnp.minimum(i + 1, nt - 1), 0, 0),
                               memory_space=pltpu.SMEM),
                  pl.BlockSpec(memory_space=pl.ANY),
                  pl.BlockSpec((tt, 128), lambda i: (i, 0)),
                  pl.BlockSpec((tt, d), lambda i: (i, 0)),
                  pl.BlockSpec((1, d), const),
                  pl.BlockSpec((1, d), const)],
        out_specs=pl.BlockSpec((tt, d), lambda i: (i, 0)),
        out_shape=jax.ShapeDtypeStruct((t, d), F32),
        scratch_shapes=[pltpu.VMEM((2, n * ROW_TILE[0], ROW_TILE[1]), F32),
                        pltpu.SemaphoreType.DMA((2,))],
        compiler_params=_params(("arbitrary",), disable_bounds_checks=True),
        name="moe_combine_ln",
    )(slots_km, slots_km, ys, wts, h, ln_g.reshape(1, d), ln_b.reshape(1, d))


def _tile_slots(slots, tt):
    t = slots.shape[1]
    return slots.reshape(TOP_K, t // tt, tt).transpose(1, 0, 2).reshape(t // tt, 1, TOP_K * tt)


def _moe_layout(idx, rank, counts, n_blocks):
    bm = EXPERT_BLOCK
    counts = counts.reshape(N_EXPERTS).astype(jnp.int32)
    padded = (counts + bm - 1) // bm * bm
    pad_end = jnp.cumsum(padded)
    pad_start = pad_end - padded
    slots = rank
    for e in range(N_EXPERTS):
        slots = slots + jnp.where(idx == e, pad_start[e], 0)
    blk_start = jnp.arange(n_blocks, dtype=jnp.int32) * bm
    blk_expert = jnp.sum((pad_end[None, :] <= blk_start[:, None]).astype(jnp.int32), axis=1)
    blk_expert = jnp.minimum(blk_expert, N_EXPERTS - 1)
    n_used = (pad_end[-1] // bm).reshape(1)
    return slots.astype(jnp.int32), blk_expert, n_used, pad_end.astype(jnp.int32)


def kernel(x, rel_bias, w_in, conv_w, conv_b, dt_bias, a_log, d_skip, ssd_norm_g, w_ssd_out, diff_lambda, diff_norm_g, w_attn_out, gate_b, w_o, ln1_g, ln1_b, w_router, b_router, w_gate, b_gate, w_up, b_up, w_down, b_down, ln2_g, ln2_b):
    batch, seq, d = x.shape
    t = batch * seq
    n_assign = t * TOP_K
    n_blocks = (n_assign + N_EXPERTS * (EXPERT_BLOCK - 1) + EXPERT_BLOCK - 1) // EXPERT_BLOCK
    cap = n_blocks * EXPERT_BLOCK
    bias_near, bias_far = _bias_tiles(rel_bias, seq)
    qk_scale = jnp.concatenate([jnp.full((DIFF_WIDTH,), DIFF_HEAD_DIM ** -0.5 * LOG2_E, F32),
                                jnp.ones((DIFF_WIDTH,), F32)])[None, :]

    h = x.reshape(t, d)
    for l in range(DEPTH):
        w_l = w_in[l]
        w_x = w_l[:, OFF_XBC:OFF_DT].astype(BF16)
        w_b = jnp.concatenate([w_l[:, :OFF_XBC], w_l[:, OFF_G:],
                               w_l[:, OFF_Q:OFF_V] * qk_scale], axis=1).astype(BF16)
        w_v = w_l[:, OFF_V:OFF_G]
        px = _matmul(h, w_x, F32, 1024, 1024)
        pb = _matmul(h, w_b, BF16, 1024, 1024)
        vt = _matmul_nt(h, w_v, ATTN_TILE)
        dt, dt_t = _dt_call(h, w_l[:, OFF_DT:OFF_Q], dt_bias[l])
        y = _ssd_call(px, conv_w[l], conv_b[l], dt, dt_t, a_log[l], d_skip[l], batch, seq)
        ao = _attn_call(pb, vt, bias_near, bias_far, diff_lambda[l], diff_norm_g[l], l, batch,
                        seq)
        h1, h1_tiles = _mix_call(y, pb, ao, h, ssd_norm_g[l], w_ssd_out[l], w_attn_out[l], gate_b[l],
                       w_o[l], ln1_g[l], ln1_b[l])
        idx, wts, rank, counts = _router_call(h1, w_router[l], b_router[l])
        slots, blk_expert, n_used, pad_end = _moe_layout(idx[:TOP_K], rank[:TOP_K], counts,
                                                         n_blocks)
        xs = _dispatch_call(h1_tiles, slots, pad_end, cap)
        ys = _expert_call(xs, blk_expert + l * N_EXPERTS, n_used, w_gate, b_gate, w_up, b_up,
                          w_down, b_down)
        h = _combine_call(ys, slots, wts, h1, ln2_g[l], ln2_b[l])
    return h.reshape(batch, seq, d)
```
